```python
import jax, jax.numpy as jnp
from jax import lax
import numpy as np

D_MODEL = 1024
BATCH = 1
SEQ = 16384
DEPTH = 1
DEC_BATCH = 128
DEC_SEQ = 8
PAST_LEN = 16384
PAGE_SIZE = 128

HEAD_DIM = 64
ATT_HEADS = 8
ATT_KV_HEADS = 2
ATT_GROUP = ATT_HEADS // ATT_KV_HEADS
ATT_W = ATT_HEADS * HEAD_DIM
KV_W = ATT_KV_HEADS * HEAD_DIM
WINDOW = 128
ATT_BLOCK = WINDOW
ROPE_THETA = 10000.0
RET_HEADS = 4
RET_DK = 64
RET_DV = 128
RET_QK_W = RET_HEADS * RET_DK
RET_W = RET_HEADS * RET_DV
RET_CHUNK = 128
MIX_W = ATT_W + RET_W
IN_SIZES = (ATT_W, KV_W, KV_W, RET_QK_W, RET_QK_W, RET_W, RET_W)
IN_W = ATT_W + 2 * KV_W + 2 * RET_QK_W + 2 * RET_W
N_EXPERTS = 32
TOP_K = 4
D_FF = 1024
SWIGLU_LIMIT = 7.0
SWIGLU_ALPHA = 1.702
EPS = 1e-6
CACHE_W = min(WINDOW, PAST_LEN)

kernel_name = 'hymba_swa_sink_retnet_moe_step'


def _rms(x):
    xf = x.astype(jnp.float32)
    return xf * lax.rsqrt(jnp.mean(xf * xf, axis=-1, keepdims=True) + EPS)


def rms_norm(x, g):
    return (_rms(x) * g.astype(jnp.float32)).astype(x.dtype)


def rope_inv_freq():
    return 1.0 / (ROPE_THETA ** (jnp.arange(0, HEAD_DIM, 2, dtype=jnp.float32) / HEAD_DIM))


def retention_inv_freq():
    return 1.0 / (ROPE_THETA ** jnp.linspace(0.0, 1.0, RET_DK // 2, dtype=jnp.float32))


def rotate(x, pos, inv_freq):
    ang = pos.astype(jnp.float32)[:, None] * inv_freq[None, :]
    cos = jnp.cos(ang)[:, None, :]
    sin = jnp.sin(ang)[:, None, :]
    xf = x.astype(jnp.float32)
    half = x.shape[-1] // 2
    x1, x2 = xf[..., :half], xf[..., half:]
    return jnp.concatenate([x1 * cos - x2 * sin, x2 * cos + x1 * sin], axis=-1).astype(x.dtype)


def retention_log_gamma():
    return jnp.log1p(-jnp.exp2(-5.0 - jnp.arange(RET_HEADS, dtype=jnp.float32)))


def adaln(c, w_ada, b_ada):
    mod = (jax.nn.silu(c) @ w_ada + b_ada).reshape(c.shape[0], 6, 1, D_MODEL)
    return [mod[:, i] for i in range(6)]


def modulate(x, shift, scale):
    return x * (1 + scale) + shift


def project_heads(h, pos, w_in, q_gain, k_gain):
    N, S, _ = h.shape
    z = h @ w_in
    splits = [int(s) for s in np.cumsum(IN_SIZES)[:-1]]
    q, k, v, rq, rk, rv, rg = jnp.split(z, splits, axis=-1)
    q = rotate(rms_norm(q.reshape(N, S, ATT_HEADS, HEAD_DIM), q_gain), pos, rope_inv_freq())
    k = rotate(rms_norm(k.reshape(N, S, ATT_KV_HEADS, HEAD_DIM), k_gain), pos, rope_inv_freq())
    v = v.reshape(N, S, ATT_KV_HEADS, HEAD_DIM)
    rq = rotate(rq.reshape(N, S, RET_HEADS, RET_DK), pos, retention_inv_freq())
    rk = rotate(rk.reshape(N, S, RET_HEADS, RET_DK), pos, retention_inv_freq()) * (RET_DK ** -0.5)
    rv = rv.reshape(N, S, RET_HEADS, RET_DV)
    return q, k, v, rq, rk, rv, rg


def window_attention(q, k, v, q_pos, k_pos, sinks):
    qg = q.reshape(q.shape[:-2] + (ATT_KV_HEADS, ATT_GROUP, HEAD_DIM))
    s = jnp.einsum('...qhgd,...khd->...hgqk', qg, k,
                   preferred_element_type=jnp.float32) * (HEAD_DIM ** -0.5)
    delta = q_pos[..., :, None] - k_pos[..., None, :]
    mask = (delta >= 0) & (delta < WINDOW) & (k_pos[..., None, :] >= 0)
    s = jnp.where(mask[..., None, None, :, :], s, -jnp.inf)
    sink = sinks.astype(jnp.float32).reshape(ATT_KV_HEADS, ATT_GROUP, 1)
    m = jnp.maximum(jnp.max(s, axis=-1), sink)
    p = jnp.exp(s - m[..., None])
    denom = jnp.sum(p, axis=-1) + jnp.exp(sink - m)
    p = p / denom[..., None]
    o = jnp.einsum('...hgqk,...khd->...qhgd', p.astype(v.dtype), v)
    return o.reshape(q.shape[:-2] + (ATT_W,))


def prompt_attention(q, k, v, sinks):
    B, S = q.shape[:2]
    nb = S // ATT_BLOCK
    qb = q.reshape(B, nb, ATT_BLOCK, ATT_HEADS, HEAD_DIM)

    def band(t):
        tb = t.reshape(B, nb, ATT_BLOCK, ATT_KV_HEADS, HEAD_DIM)
        prev = jnp.concatenate([jnp.zeros_like(tb[:, :1]), tb[:, :-1]], axis=1)
        return jnp.concatenate([prev, tb], axis=2)

    starts = jnp.arange(nb, dtype=jnp.int32)[:, None] * ATT_BLOCK
    q_pos = starts + jnp.arange(ATT_BLOCK, dtype=jnp.int32)[None, :]
    k_pos = starts - ATT_BLOCK + jnp.arange(2 * ATT_BLOCK, dtype=jnp.int32)[None, :]
    o = window_attention(qb, band(k), band(v), q_pos, k_pos, sinks)
    return o.reshape(B, S, ATT_W)


def retention_chunk(R, q, k, v, log_gamma):
    C = q.shape[1]
    idx = jnp.arange(C, dtype=jnp.float32)
    diff = idx[:, None] - idx[None, :]
    decay = jnp.where(diff[None] >= 0,
                      jnp.exp(jnp.maximum(diff, 0.0)[None] * log_gamma[:, None, None]), 0.0)
    qf, kf, vf = q.astype(jnp.float32), k.astype(jnp.float32), v.astype(jnp.float32)
    s = jnp.einsum('nihd,njhd->nhij', qf, kf) * decay[None]
    intra = jnp.einsum('nhij,njhe->nihe', s, vf)
    q_decay = jnp.exp((idx + 1.0)[:, None] * log_gamma[None, :])
    inter = jnp.einsum('nihd,nhde->nihe', qf, R) * q_decay[None, :, :, None]
    k_decay = jnp.exp((C - 1.0 - idx)[:, None] * log_gamma[None, :])
    R_new = (jnp.exp(C * log_gamma)[None, :, None, None] * R
             + jnp.einsum('njhd,njhe->nhde', kf * k_decay[None, :, :, None], vf))
    return R_new, intra + inter


def retention_prompt(rq, rk, rv, log_gamma):
    B, S, H, _ = rq.shape
    nc = S // RET_CHUNK

    def chunks(t):
        return jnp.moveaxis(t.reshape((B, nc, RET_CHUNK) + t.shape[2:]), 1, 0)

    def step(R, xs):
        return retention_chunk(R, xs[0], xs[1], xs[2], log_gamma)

    R0 = jnp.zeros((B, H, RET_DK, RET_DV), jnp.float32)
    R_end, out = lax.scan(step, R0, (chunks(rq), chunks(rk), chunks(rv)))
    return R_end, jnp.moveaxis(out, 0, 1).reshape(B, S, H, RET_DV)


def merge_heads(att, ret, rg, w_out):
    N, S, _ = att.shape
    y_ret = jax.nn.silu(rg.astype(jnp.float32)) * _rms(ret).reshape(N, S, RET_W)
    mix = jnp.concatenate([att.astype(jnp.float32), y_ret], axis=-1).astype(rg.dtype)
    return mix @ w_out


def moe(h, w_router, b_router, w_gate_up, b_gate_up, w_down, b_down):
    N, S, D = h.shape
    t = h.reshape(N * S, D)
    logits = (t @ w_router + b_router).astype(jnp.float32)
    top_val, top_idx = lax.top_k(logits, TOP_K)
    wts = jax.nn.softmax(top_val, axis=-1)
    gates = jnp.sum(jax.nn.one_hot(top_idx, N_EXPERTS, dtype=jnp.float32) * wts[..., None], axis=1)
    out = jnp.zeros((N * S, D), jnp.float32)
    for e in range(N_EXPERTS):
        gu = t @ w_gate_up[e] + b_gate_up[e]
        glu = jnp.minimum(gu[:, :D_FF], SWIGLU_LIMIT)
        lin = jnp.clip(gu[:, D_FF:], -SWIGLU_LIMIT, SWIGLU_LIMIT)
        act = glu * jax.nn.sigmoid(SWIGLU_ALPHA * glu) * (lin + 1)
        out = out + gates[:, e:e + 1] * (act @ w_down[e] + b_down[e]).astype(jnp.float32)
    return out.astype(h.dtype).reshape(N, S, D)


def setup_inputs(seed: int = 0) -> dict:
    key = jax.random.key(seed)
    ks = jax.random.split(key, 24)
    f32 = jnp.float32
    nrm = lambda k, shape, s: jax.random.normal(k, shape, f32) * s
    return {
        'x_prompt': nrm(ks[0], (BATCH, SEQ, D_MODEL), 1.0),
        'x_sample': nrm(ks[1], (DEC_BATCH, DEC_SEQ, D_MODEL), 1.0),
        'cache_k': nrm(ks[2], (DEPTH, DEC_BATCH, CACHE_W, ATT_KV_HEADS, HEAD_DIM), 1.0),
        'cache_v': nrm(ks[3], (DEPTH, DEC_BATCH, CACHE_W, ATT_KV_HEADS, HEAD_DIM), 1.0),
        'state_ret': nrm(ks[4], (DEPTH, DEC_BATCH, RET_HEADS, RET_DK, RET_DV), 1.0),
        'c_prompt': nrm(ks[5], (BATCH, D_MODEL), 1.0),
        'c_sample': nrm(ks[6], (DEC_BATCH, D_MODEL), 1.0),
        'w_ada': nrm(ks[7], (DEPTH, D_MODEL, 6 * D_MODEL), D_MODEL ** -0.5),
        'b_ada': nrm(ks[8], (DEPTH, 6 * D_MODEL), 0.02),
        'g_mix': 1.0 + nrm(ks[9], (DEPTH, D_MODEL), 0.02),
        'w_in': nrm(ks[10], (DEPTH, D_MODEL, IN_W), D_MODEL ** -0.5),
        'q_gain': 1.0 + nrm(ks[11], (DEPTH, HEAD_DIM), 0.02),
        'k_gain': 1.0 + nrm(ks[12], (DEPTH, HEAD_DIM), 0.02),
        'sinks': nrm(ks[13], (DEPTH, ATT_HEADS), 1.0),
        'w_out': nrm(ks[14], (DEPTH, MIX_W, D_MODEL), MIX_W ** -0.5),
        'g_ffn': 1.0 + nrm(ks[15], (DEPTH, D_MODEL), 0.02),
        'w_router': nrm(ks[16], (DEPTH, D_MODEL, N_EXPERTS), D_MODEL ** -0.5),
        'b_router': nrm(ks[17], (DEPTH, N_EXPERTS), 0.01),
        'w_gate_up': nrm(ks[18], (DEPTH, N_EXPERTS, D_MODEL, 2 * D_FF), D_MODEL ** -0.5),
        'b_gate_up': nrm(ks[19], (DEPTH, N_EXPERTS, 2 * D_FF), 0.02),
        'w_down': nrm(ks[20], (DEPTH, N_EXPERTS, D_FF, D_MODEL), D_FF ** -0.5),
        'b_down': nrm(ks[21], (DEPTH, N_EXPERTS, D_MODEL), 0.02),
    }


def reference(x_prompt, x_sample, cache_k, cache_v, state_ret, c_prompt, c_sample,
              w_ada, b_ada, g_mix, w_in, q_gain, k_gain, sinks, w_out, g_ffn,
              w_router, b_router, w_gate_up, b_gate_up, w_down, b_down):
    pos_p = jnp.arange(SEQ, dtype=jnp.int32)
    pos_s = PAST_LEN + jnp.arange(DEC_SEQ, dtype=jnp.int32)
    k_pos_s = jnp.concatenate([PAST_LEN - CACHE_W + jnp.arange(CACHE_W, dtype=jnp.int32), pos_s])
    log_gamma = retention_log_gamma()
    xp, xs = x_prompt, x_sample
    kp_l, vp_l, rp_l, ks_l, vs_l, rs_l = [], [], [], [], [], []
    for l in range(DEPTH):
        sh_ap, sc_ap, gt_ap, sh_fp, sc_fp, gt_fp = adaln(c_prompt, w_ada[l], b_ada[l])
        sh_as, sc_as, gt_as, sh_fs, sc_fs, gt_fs = adaln(c_sample, w_ada[l], b_ada[l])

        hp = modulate(rms_norm(xp, g_mix[l]), sh_ap, sc_ap)
        q, k, v, rq, rk, rv, rg = project_heads(hp, pos_p, w_in[l], q_gain[l], k_gain[l])
        att = prompt_attention(q, k, v, sinks[l])
        R_p, ret = retention_prompt(rq, rk, rv, log_gamma)
        xp = xp + gt_ap * merge_heads(att, ret, rg, w_out[l])
        kp_l.append(k[:, SEQ - CACHE_W:])
        vp_l.append(v[:, SEQ - CACHE_W:])
        rp_l.append(R_p.astype(x_prompt.dtype))

        hs = modulate(rms_norm(xs, g_mix[l]), sh_as, sc_as)
        q, k, v, rq, rk, rv, rg = project_heads(hs, pos_s, w_in[l], q_gain[l], k_gain[l])
        k_all = jnp.concatenate([cache_k[l].astype(k.dtype), k], axis=1)
        v_all = jnp.concatenate([cache_v[l].astype(v.dtype), v], axis=1)
        att = window_attention(q, k_all, v_all, pos_s, k_pos_s, sinks[l])
        R_s, ret = retention_chunk(state_ret[l].astype(jnp.float32), rq, rk, rv, log_gamma)
        xs = xs + gt_as * merge_heads(att, ret, rg, w_out[l])
        ks_l.append(k_all[:, -CACHE_W:])
        vs_l.append(v_all[:, -CACHE_W:])
        rs_l.append(R_s.astype(state_ret.dtype))

        xp = xp + gt_fp * moe(modulate(rms_norm(xp, g_ffn[l]), sh_fp, sc_fp), w_router[l], b_router[l],
                              w_gate_up[l], b_gate_up[l], w_down[l], b_down[l])
        xs = xs + gt_fs * moe(modulate(rms_norm(xs, g_ffn[l]), sh_fs, sc_fs), w_router[l], b_router[l],
                              w_gate_up[l], b_gate_up[l], w_down[l], b_down[l])

    k_win_prompt = jnp.stack(kp_l)
    v_win_prompt = jnp.stack(vp_l)
    state_ret_prompt = jnp.stack(rp_l)
    k_win_sample = jnp.stack(ks_l)
    v_win_sample = jnp.stack(vs_l)
    state_ret_sample = jnp.stack(rs_l)
    return (xp, xs, k_win_prompt, v_win_prompt, state_ret_prompt, k_win_sample, v_win_sample, state_ret_sample)
```

```python
import functools

import jax
import jax.numpy as jnp
import numpy as np
from jax import lax
from jax.experimental import pallas as pl
from jax.experimental.pallas import tpu as pltpu

HEAD_DIM = 64
ATT_HEADS = 8
ATT_KV_HEADS = 2
ATT_W = ATT_HEADS * HEAD_DIM
KV_W = ATT_KV_HEADS * HEAD_DIM
WINDOW = 128
PAST_LEN = 16384
ROPE_THETA = 10000.0
RET_HEADS = 4
RET_DK = 64
RET_DV = 128
RET_QK_W = RET_HEADS * RET_DK
RET_W = RET_HEADS * RET_DV
RET_CHUNK = 128
N_EXPERTS = 32
TOP_K = 4
SWIGLU_LIMIT = 7.0
SWIGLU_ALPHA = 1.702
EPS = 1e-6

_Q0, _K0, _V0 = 0, ATT_W, ATT_W + KV_W
_RQ0 = ATT_W + 2 * KV_W
_RK0 = _RQ0 + RET_QK_W
_RV0 = _RK0 + RET_QK_W
_RG0 = _RV0 + RET_W
IN_W = _RG0 + RET_W

LANES = 128
SUBLANES = 8
VMEM_LIMIT_BYTES = 56 * 1024 * 1024
NEG_BIG = -1e30

PROMPT_BLOCK = 512
SAMPLE_SEQS = 16
ROUTE_BLOCK = 512
EXPERT_TILE = 512

_NT = (((1,), (1,)), ((), ()))
_TN = (((0,), (0,)), ((), ()))


def _bf(x):
    return x.astype(jnp.bfloat16)


def _dot(a, b):
    return jnp.dot(_bf(a), _bf(b), preferred_element_type=jnp.float32)


def _dot_nt(a, b):
    return lax.dot_general(_bf(a), _bf(b), _NT, preferred_element_type=jnp.float32)


def _dot_tn(a, b):
    return lax.dot_general(_bf(a), _bf(b), _TN, preferred_element_type=jnp.float32)


def _lane_lo(shape):
    lane = lax.broadcasted_iota(jnp.int32, shape, len(shape) - 1)
    return (lane % LANES) < HEAD_DIM


def _swap_halves(x):
    lane = lax.broadcasted_iota(jnp.int32, x.shape, 1)
    first = (lane % HEAD_DIM) < (HEAD_DIM // 2)
    return jnp.where(first, pltpu.roll(x, LANES - HEAD_DIM // 2, axis=1),
                     pltpu.roll(x, HEAD_DIM // 2, axis=1))


def _rotate(x, cos, sin_signed):
    outs = []
    for j in range(x.shape[1] // LANES):
        xs = x[:, j * LANES:(j + 1) * LANES]
        outs.append(xs * cos + _swap_halves(xs) * sin_signed)
    return outs[0] if len(outs) == 1 else jnp.concatenate(outs, axis=1)


def _rms_rows(x):
    return x * lax.rsqrt(jnp.mean(x * x, axis=-1, keepdims=True) + EPS)


def _pre_norm(x, gain, shift, scale):
    return _rms_rows(x) * gain * (1.0 + scale) + shift


def _head_norm(x, gain_tiled, blockdiag):
    ms = jnp.dot(_bf(x * x), blockdiag, preferred_element_type=jnp.float32)
    return x * lax.rsqrt(ms + EPS) * gain_tiled


def _project(h, w_in_ref, qg, kg, bd_q, bd_k, cos_a, sin_a, cos_r, sin_r):
    z = jnp.dot(_bf(h), w_in_ref[...], preferred_element_type=jnp.float32)
    q = _rotate(_head_norm(z[:, _Q0:_K0], qg, bd_q), cos_a, sin_a) * (HEAD_DIM ** -0.5)
    k = _rotate(_head_norm(z[:, _K0:_V0], kg, bd_k), cos_a, sin_a)
    v = z[:, _V0:_RQ0]
    rq = _rotate(z[:, _RQ0:_RK0], cos_r, sin_r)
    rk = _rotate(z[:, _RK0:_RV0], cos_r, sin_r) * (RET_DK ** -0.5)
    rv = z[:, _RV0:_RG0]
    rg = z[:, _RG0:IN_W]
    return q, k, v, rq, rk, rv, rg


_NAT_HEADS = (0, 2, 5, 7)
_ROL_HEADS = (1, 3, 4, 6)


def _stack_heads(q, heads):
    lo = _lane_lo((q.shape[0], LANES))
    parts = []
    for j in heads:
        slab = q[:, (j // 2) * LANES:(j // 2 + 1) * LANES]
        parts.append(jnp.where(lo if j % 2 == 0 else ~lo, slab, 0.0))
    return jnp.concatenate(parts, axis=0)


def _sink_column(sinks_ref, heads, rows):
    r = lax.broadcasted_iota(jnp.int32, (len(heads) * rows, 1), 0)
    col = jnp.full((len(heads) * rows, 1), sinks_ref[heads[-1]], jnp.float32)
    for n in range(len(heads) - 2, -1, -1):
        col = jnp.where(r < (n + 1) * rows, sinks_ref[heads[n]], col)
    return col


def _softmax_pv(s, mask, sink, v):
    s = jnp.where(mask, s, NEG_BIG)
    m = jnp.maximum(jnp.max(s, axis=-1, keepdims=True), sink)
    p = jnp.exp(s - m)
    denom = jnp.sum(p, axis=-1, keepdims=True) + jnp.exp(sink - m)
    return _dot(p, v) / denom


def _attention(q, k_all, v_all, mask, sinks_ref):
    rows = q.shape[0]
    k_rol = pltpu.roll(k_all, HEAD_DIM, axis=1)
    v_rol = pltpu.roll(v_all, HEAD_DIM, axis=1)
    mask4 = jnp.concatenate([mask] * 4, axis=0)
    o_nat = _softmax_pv(_dot_nt(_stack_heads(q, _NAT_HEADS), k_all), mask4,
                        _sink_column(sinks_ref, _NAT_HEADS, rows), v_all)
    o_rol = _softmax_pv(_dot_nt(_stack_heads(q, _ROL_HEADS), k_rol), mask4,
                        _sink_column(sinks_ref, _ROL_HEADS, rows), v_rol)
    lo = _lane_lo((rows, LANES))
    blk = lambda o, n: o[n * rows:(n + 1) * rows]
    return jnp.concatenate([
        jnp.where(lo, blk(o_nat, 0), blk(o_rol, 0)),
        jnp.where(lo, blk(o_nat, 1), blk(o_rol, 1)),
        jnp.where(lo, blk(o_rol, 2), blk(o_nat, 2)),
        jnp.where(lo, blk(o_rol, 3), blk(o_nat, 3)),
    ], axis=1)


def _retention(rq, rk, rv, rg, state, decay_ref, qdec, kdec, gpow):
    rows = rq.shape[0]
    lo = _lane_lo((rows, LANES))
    rkd = rk * kdec
    outs, new_state = [], []
    for m in range(RET_HEADS // 2):
        sl = slice(m * LANES, (m + 1) * LANES)
        upd = state[m] * gpow[:, m:m + 1]
        for p in range(2):
            h = 2 * m + p
            half = lo if p == 0 else ~lo
            rqm = jnp.where(half, rq[:, sl], 0.0)
            rv_h = rv[:, h * RET_DV:(h + 1) * RET_DV]
            s = _dot_nt(rqm, rk[:, sl]) * decay_ref[h]
            ret = _dot(s, rv_h) + _dot(rqm, state[m]) * qdec[:, h * RET_DV:(h + 1) * RET_DV]
            upd = upd + _dot_tn(jnp.where(half, rkd[:, sl], 0.0), rv_h)
            rg_h = rg[:, h * RET_DV:(h + 1) * RET_DV]
            outs.append(rg_h * jax.nn.sigmoid(rg_h) * _rms_rows(ret))
        new_state.append(upd)
    return jnp.concatenate(outs, axis=1), new_state


def _route(h2, wr_hi_ref, wr_lo_ref, br):
    h_hi = _bf(h2)
    h_lo = _bf(h2 - h_hi.astype(jnp.float32))
    logits = (jnp.dot(h_hi, wr_hi_ref[...], preferred_element_type=jnp.float32)
              + jnp.dot(h_hi, wr_lo_ref[...], preferred_element_type=jnp.float32)
              + jnp.dot(h_lo, wr_hi_ref[...], preferred_element_type=jnp.float32)) + br
    lane = lax.broadcasted_iota(jnp.int32, logits.shape, 1)
    idx_out = jnp.zeros(logits.shape, jnp.int32)
    val_out = jnp.zeros(logits.shape, jnp.float32)
    vals = []
    work = logits
    for k in range(TOP_K):
        mx = jnp.max(work, axis=-1, keepdims=True)
        ix = jnp.min(jnp.where(work == mx, lane, LANES), axis=-1, keepdims=True)
        work = jnp.where(lane == ix, NEG_BIG, work)
        idx_out = jnp.where(lane == k, ix, idx_out)
        vals.append(mx)
    exps = [jnp.exp(v - vals[0]) for v in vals]
    total = exps[0] + exps[1] + exps[2] + exps[3]
    for k in range(TOP_K):
        val_out = jnp.where(lane == k, exps[k] / total, val_out)
    return idx_out, val_out


def _adaln_kernel(c_ref, w_ref, b_ref, o_ref):
    c = c_ref[...]
    o_ref[...] = _dot(c * jax.nn.sigmoid(c), w_ref[...]) + b_ref[...]


def _adaln(c_all, w_ada, b_ada):
    rows, d = c_all.shape
    n = w_ada.shape[1]
    bn = 1536
    return pl.pallas_call(
        _adaln_kernel,
        grid=(n // bn,),
        in_specs=[pl.BlockSpec((rows, d), lambda j: (0, 0)),
                  pl.BlockSpec((d, bn), lambda j: (0, j)),
                  pl.BlockSpec((1, bn), lambda j: (0, j))],
        out_specs=pl.BlockSpec((rows, bn), lambda j: (0, j)),
        out_shape=jax.ShapeDtypeStruct((rows, n), jnp.float32),
        compiler_params=pltpu.CompilerParams(vmem_limit_bytes=VMEM_LIMIT_BYTES),
    )(c_all, w_ada, b_ada.reshape(1, n))


def _prompt_mix_kernel(sinks_ref, x_ref, mod_ref, gmix_ref, gffn_ref, w_in_ref, qg_ref, kg_ref,
                       bdq_ref, bdk_ref, cosa_ref, sina_ref, cosr_ref, sinr_ref,
                       decay_ref, qdec_ref, kdec_ref, gpow_ref, w_out_ref,
                       wrh_ref, wrl_ref, br_ref,
                       x1_ref, h2_ref, topi_ref, topw_ref, kwin_ref, vwin_ref, rend_ref,
                       kprev, vprev, state, mix):
    step = pl.program_id(0)
    tb = x_ref.shape[0]

    @pl.when(step == 0)
    def _():
        kprev[...] = jnp.zeros_like(kprev)
        vprev[...] = jnp.zeros_like(vprev)
        state[...] = jnp.zeros_like(state)

    x = x_ref[...]
    h = _pre_norm(x, gmix_ref[...], mod_ref[0:1, :], mod_ref[1:2, :])
    q, k, v, rq, rk, rv, rg = _project(
        h, w_in_ref, qg_ref[...], kg_ref[...], bdq_ref[...], bdk_ref[...],
        cosa_ref[...], sina_ref[...], cosr_ref[...], sinr_ref[...])

    qi = lax.broadcasted_iota(jnp.int32, (WINDOW, 2 * WINDOW), 0)
    ci = lax.broadcasted_iota(jnp.int32, (WINDOW, 2 * WINDOW), 1)
    band = (ci > qi) & (ci <= qi + WINDOW)
    qdec, kdec, gpow = qdec_ref[...], kdec_ref[...], gpow_ref[...]

    for sb in range(tb // WINDOW):
        rs = slice(sb * WINDOW, (sb + 1) * WINDOW)
        k_sb, v_sb = k[rs], v[rs]
        k_all = jnp.concatenate([kprev[...], k_sb], axis=0)
        v_all = jnp.concatenate([vprev[...], v_sb], axis=0)
        has_prev = jnp.logical_or(step > 0, sb > 0) if sb == 0 else True
        mask = band & ((ci >= WINDOW) | has_prev)
        att = _attention(q[rs], k_all, v_all, mask, sinks_ref)
        kprev[...] = k_sb
        vprev[...] = v_sb
        yret, new_state = _retention(rq[rs], rk[rs], rv[rs], rg[rs], [state[0], state[1]],
                                     decay_ref, qdec, kdec, gpow)
        state[0] = new_state[0]
        state[1] = new_state[1]
        mix[rs, 0:ATT_W] = _bf(att)
        mix[rs, ATT_W:ATT_W + RET_W] = _bf(yret)

    x1 = x + mod_ref[2:3, :] * jnp.dot(mix[...], w_out_ref[...],
                                       preferred_element_type=jnp.float32)
    x1_ref[...] = x1
    h2 = _pre_norm(x1, gffn_ref[...], mod_ref[3:4, :], mod_ref[4:5, :])
    h2_ref[...] = h2
    topi, topw = _route(h2, wrh_ref, wrl_ref, br_ref[...])
    topi_ref[...] = topi
    topw_ref[...] = topw

    @pl.when(step == pl.num_programs(0) - 1)
    def _():
        kwin_ref[...] = kprev[...]
        vwin_ref[...] = vprev[...]
        rend_ref[...] = state[...]


def _const_spec(shape):
    nd = len(shape)
    return pl.BlockSpec(shape, lambda *_: (0,) * nd)


def _prompt_mix(x, mod, sinks, gmix, gffn, w_in, qg, kg, bdq, bdk, tabs, ret_tabs, w_out,
                wrh, wrl, br):
    t, d = x.shape
    tb = PROMPT_BLOCK
    cosa, sina, cosr, sinr = tabs
    decay, qdec, kdec, gpow = ret_tabs
    row_spec = lambda w: pl.BlockSpec((tb, w), lambda i, *_: (i, 0))
    consts = [mod, gmix, gffn, w_in, qg, kg, bdq, bdk]
    consts2 = [decay, qdec, kdec, gpow, w_out, wrh, wrl, br]
    grid_spec = pltpu.PrefetchScalarGridSpec(
        num_scalar_prefetch=1,
        grid=(t // tb,),
        in_specs=([row_spec(d)] + [_const_spec(a.shape) for a in consts]
                  + [row_spec(LANES)] * 4 + [_const_spec(a.shape) for a in consts2]),
        out_specs=[row_spec(d), row_spec(d), row_spec(LANES), row_spec(LANES),
                   _const_spec((WINDOW, KV_W)), _const_spec((WINDOW, KV_W)),
                   _const_spec((2, LANES, RET_DV))],
        scratch_shapes=[pltpu.VMEM((WINDOW, KV_W), jnp.float32),
                        pltpu.VMEM((WINDOW, KV_W), jnp.float32),
                        pltpu.VMEM((2, LANES, RET_DV), jnp.float32),
                        pltpu.VMEM((tb, ATT_W + RET_W), jnp.bfloat16)],
    )
    return pl.pallas_call(
        _prompt_mix_kernel,
        grid_spec=grid_spec,
        out_shape=[jax.ShapeDtypeStruct((t, d), jnp.float32),
                   jax.ShapeDtypeStruct((t, d), jnp.float32),
                   jax.ShapeDtypeStruct((t, LANES), jnp.int32),
                   jax.ShapeDtypeStruct((t, LANES), jnp.float32),
                   jax.ShapeDtypeStruct((WINDOW, KV_W), jnp.float32),
                   jax.ShapeDtypeStruct((WINDOW, KV_W), jnp.float32),
                   jax.ShapeDtypeStruct((2, LANES, RET_DV), jnp.float32)],
        compiler_params=pltpu.CompilerParams(
            dimension_semantics=("arbitrary",), vmem_limit_bytes=VMEM_LIMIT_BYTES),
    )(sinks, x, *consts, cosa, sina, cosr, sinr, *consts2)


def _sample_mix_kernel(sinks_ref, x_ref, mod_ref, gmix_ref, gffn_ref, w_in_ref, qg_ref, kg_ref,
                       bdq_ref, bdk_ref, cosa_ref, sina_ref, cosr_ref, sinr_ref,
                       decay_ref, qdec_ref, kdec_ref, gpow_ref, w_out_ref,
                       wrh_ref, wrl_ref, br_ref, ck_ref, cv_ref, st_ref,
                       x1_ref, h2_ref, topi_ref, topw_ref, kwin_ref, vwin_ref, rend_ref,
                       mix):
    nseq, cache_w = ck_ref.shape[0], ck_ref.shape[1]
    ds = x_ref.shape[0] // nseq
    d = x_ref.shape[1]
    x = x_ref[...]
    mod = lambda i: mod_ref[:, i * d:(i + 1) * d]
    h = _pre_norm(x, gmix_ref[...], mod(0), mod(1))
    q, k, v, rq, rk, rv, rg = _project(
        h, w_in_ref, qg_ref[...], kg_ref[...], bdq_ref[...], bdk_ref[...],
        cosa_ref[...], sina_ref[...], cosr_ref[...], sinr_ref[...])

    qi = lax.broadcasted_iota(jnp.int32, (ds, cache_w + ds), 0)
    ci = lax.broadcasted_iota(jnp.int32, (ds, cache_w + ds), 1)
    delta = qi + cache_w - ci
    mask = (delta >= 0) & (delta < WINDOW)
    qdec, kdec, gpow = qdec_ref[...], kdec_ref[...], gpow_ref[...]

    for s in range(nseq):
        rs = slice(s * ds, (s + 1) * ds)
        k_all = jnp.concatenate([ck_ref[s], k[rs]], axis=0)
        v_all = jnp.concatenate([cv_ref[s], v[rs]], axis=0)
        att = _attention(q[rs], k_all, v_all, mask, sinks_ref)
        kwin_ref[s] = k_all[ds:]
        vwin_ref[s] = v_all[ds:]
        yret, new_state = _retention(rq[rs], rk[rs], rv[rs], rg[rs],
                                     [st_ref[s, 0], st_ref[s, 1]],
                                     decay_ref, qdec, kdec, gpow)
        rend_ref[s, 0] = new_state[0]
        rend_ref[s, 1] = new_state[1]
        mix[rs, 0:ATT_W] = _bf(att)
        mix[rs, ATT_W:ATT_W + RET_W] = _bf(yret)

    x1 = x + mod(2) * jnp.dot(mix[...], w_out_ref[...], preferred_element_type=jnp.float32)
    x1_ref[...] = x1
    h2 = _pre_norm(x1, gffn_ref[...], mod(3), mod(4))
    h2_ref[...] = h2
    topi, topw = _route(h2, wrh_ref, wrl_ref, br_ref[...])
    topi_ref[...] = topi
    topw_ref[...] = topw


def _sample_mix(x, mod_tok, sinks, gmix, gffn, w_in, qg, kg, bdq, bdk, tabs, ret_tabs, w_out,
                wrh, wrl, br, cache_k, cache_v, state):
    t, d = x.shape
    nb, cache_w = cache_k.shape[0], cache_k.shape[1]
    ds = t // nb
    g = SAMPLE_SEQS
    tb = g * ds
    cosa, sina, cosr, sinr = tabs
    decay, qdec, kdec, gpow = ret_tabs
    row_spec = lambda w: pl.BlockSpec((tb, w), lambda i, *_: (i, 0))
    seq3 = lambda a: pl.BlockSpec((g,) + a.shape[1:], lambda i, *_: (i,) + (0,) * (a.ndim - 1))
    consts = [gmix, gffn, w_in, qg, kg, bdq, bdk, cosa, sina, cosr, sinr,
              decay, qdec, kdec, gpow, w_out, wrh, wrl, br]
    grid_spec = pltpu.PrefetchScalarGridSpec(
        num_scalar_prefetch=1,
        grid=(nb // g,),
        in_specs=([row_spec(d), row_spec(mod_tok.shape[1])]
                  + [_const_spec(a.shape) for a in consts]
                  + [seq3(cache_k), seq3(cache_v), seq3(state)]),
        out_specs=[row_spec(d), row_spec(d), row_spec(LANES), row_spec(LANES),
                   seq3(cache_k), seq3(cache_v), seq3(state)],
        scratch_shapes=[pltpu.VMEM((tb, ATT_W + RET_W), jnp.bfloat16)],
    )
    return pl.pallas_call(
        _sample_mix_kernel,
        grid_spec=grid_spec,
        out_shape=[jax.ShapeDtypeStruct((t, d), jnp.float32),
                   jax.ShapeDtypeStruct((t, d), jnp.float32),
                   jax.ShapeDtypeStruct((t, LANES), jnp.int32),
                   jax.ShapeDtypeStruct((t, LANES), jnp.float32),
                   jax.ShapeDtypeStruct(cache_k.shape, jnp.float32),
                   jax.ShapeDtypeStruct(cache_v.shape, jnp.float32),
                   jax.ShapeDtypeStruct(state.shape, jnp.float32)],
        compiler_params=pltpu.CompilerParams(
            dimension_semantics=("arbitrary",), vmem_limit_bytes=VMEM_LIMIT_BYTES),
    )(sinks, x, mod_tok, *consts, cache_k, cache_v, state)


def _route_rank_kernel(topi_ref, rank_ref, count_ref, running):
    step = pl.program_id(0)

    @pl.when(step == 0)
    def _():
        running[...] = jnp.zeros_like(running)

    topi = topi_ref[...]
    rows = topi.shape[0]
    lane = lax.broadcasted_iota(jnp.int32, (rows, LANES), 1)
    picks = []
    onehot = jnp.zeros((rows, LANES), jnp.float32)
    for k in range(TOP_K):
        ix = jnp.sum(jnp.where(lane == k, topi, 0), axis=-1, keepdims=True)
        pick = lane == ix
        picks.append(pick)
        onehot = jnp.where(pick, 1.0, onehot)
    r = lax.broadcasted_iota(jnp.int32, (rows, rows), 0)
    c = lax.broadcasted_iota(jnp.int32, (rows, rows), 1)
    lower = jnp.where(c < r, 1.0, 0.0)
    before = _dot(lower, onehot) + running[...]
    rank = jnp.zeros((rows, LANES), jnp.int32)
    for k in range(TOP_K):
        rk = jnp.sum(jnp.where(picks[k], before, 0.0), axis=-1, keepdims=True)
        rank = jnp.where(lane == k, rk.astype(jnp.int32), rank)
    rank_ref[...] = rank
    running[...] = running[...] + jnp.sum(onehot, axis=0, keepdims=True)

    @pl.when(step == pl.num_programs(0) - 1)
    def _():
        count_ref[...] = running[...]


def _route_rank(topi):
    t = topi.shape[0]
    tb = ROUTE_BLOCK
    return pl.pallas_call(
        _route_rank_kernel,
        grid=(t // tb,),
        in_specs=[pl.BlockSpec((tb, LANES), lambda i: (i, 0))],
        out_specs=[pl.BlockSpec((tb, LANES), lambda i: (i, 0)),
                   pl.BlockSpec((1, LANES), lambda i: (0, 0))],
        out_shape=[jax.ShapeDtypeStruct((t, LANES), jnp.int32),
                   jax.ShapeDtypeStruct((1, LANES), jnp.float32)],
        scratch_shapes=[pltpu.VMEM((1, LANES), jnp.float32)],
        compiler_params=pltpu.CompilerParams(dimension_semantics=("arbitrary",)),
    )(topi)


def _dispatch_kernel(dest_ref, h2_ref, sorted_in_ref, sorted_ref, sem):
    del sorted_in_ref
    tb = h2_ref.shape[0]
    base = pl.program_id(0) * tb * TOP_K

    def copy(i, k):
        slot = dest_ref[base + i * TOP_K + k]
        return pltpu.make_async_copy(h2_ref.at[pl.ds(i, 1)], sorted_ref.at[pl.ds(slot, 1)], sem)

    def issue(i, carry):
        for k in range(TOP_K):
            copy(i, k).start()
        return carry

    lax.fori_loop(0, tb, issue, 0)

    def drain(i, carry):
        for k in range(TOP_K):
            copy(i, k).wait()
        return carry

    lax.fori_loop(0, tb, drain, 0)


def _dispatch(dest_flat, h2, sorted_init):
    t, d = h2.shape
    tb = ROUTE_BLOCK
    grid_spec = pltpu.PrefetchScalarGridSpec(
        num_scalar_prefetch=1,
        grid=(t // tb,),
        in_specs=[pl.BlockSpec((tb, d), lambda i, *_: (i, 0)),
                  pl.BlockSpec(memory_space=pl.ANY)],
        out_specs=pl.BlockSpec(memory_space=pl.ANY),
        scratch_shapes=[pltpu.SemaphoreType.DMA],
    )
    return pl.pallas_call(
        _dispatch_kernel,
        grid_spec=grid_spec,
        out_shape=jax.ShapeDtypeStruct(sorted_init.shape, sorted_init.dtype),
        input_output_aliases={2: 0},
        compiler_params=pltpu.CompilerParams(dimension_semantics=("arbitrary",)),
    )(dest_flat, h2, sorted_init)


def _experts_kernel(tile_expert_ref, x_ref, wgu_ref, bgu_ref, wd_ref, bd_ref, y_ref,
                    wgu_bf, wd_bf):
    step = pl.program_id(0)
    e = tile_expert_ref[step]
    prev = tile_expert_ref[jnp.maximum(step - 1, 0)]

    @pl.when(jnp.logical_or(step == 0, e != prev))
    def _():
        wgu_bf[...] = _bf(wgu_ref[0])
        wd_bf[...] = _bf(wd_ref[0])

    d_ff = wd_bf.shape[0]
    gu = jnp.dot(_bf(x_ref[...]), wgu_bf[...], preferred_element_type=jnp.float32) + bgu_ref[0]
    glu = jnp.minimum(gu[:, :d_ff], SWIGLU_LIMIT)
    lin = jnp.clip(gu[:, d_ff:], -SWIGLU_LIMIT, SWIGLU_LIMIT)
    act = glu * jax.nn.sigmoid(SWIGLU_ALPHA * glu) * (lin + 1.0)
    y_ref[...] = jnp.dot(_bf(act), wd_bf[...], preferred_element_type=jnp.float32) + bd_ref[0]


def _experts(tile_expert, x_sorted, w_gate_up, b_gate_up, w_down, b_down):
    p, d = x_sorted.shape
    n_e, _, two_ff = w_gate_up.shape
    d_ff = two_ff // 2
    tm = EXPERT_TILE
    grid_spec = pltpu.PrefetchScalarGridSpec(
        num_scalar_prefetch=1,
        grid=(p // tm,),
        in_specs=[pl.BlockSpec((tm, d), lambda i, te: (i, 0)),
                  pl.BlockSpec((1, d, two_ff), lambda i, te: (te[i], 0, 0)),
                  pl.BlockSpec((1, 1, two_ff), lambda i, te: (te[i], 0, 0)),
                  pl.BlockSpec((1, d_ff, d), lambda i, te: (te[i], 0, 0)),
                  pl.BlockSpec((1, 1, d), lambda i, te: (te[i], 0, 0))],
        out_specs=pl.BlockSpec((tm, d), lambda i, te: (i, 0)),
        scratch_shapes=[pltpu.VMEM((d, two_ff), jnp.bfloat16),
                        pltpu.VMEM((d_ff, d), jnp.bfloat16)],
    )
    return pl.pallas_call(
        _experts_kernel,
        grid_spec=grid_spec,
        out_shape=jax.ShapeDtypeStruct((p, d), jnp.float32),
        compiler_params=pltpu.CompilerParams(
            dimension_semantics=("arbitrary",), vmem_limit_bytes=VMEM_LIMIT_BYTES),
    )(tile_expert, x_sorted, w_gate_up, b_gate_up.reshape(n_e, 1, two_ff), w_down,
      b_down.reshape(n_e, 1, d))


def _combine_kernel(dest_ref, x1_ref, gate_ref, topw_ref, y_ref, out_ref, rows, sem):
    tb = x1_ref.shape[0]
    base = pl.program_id(0) * tb * TOP_K

    def copy(i, k):
        slot = dest_ref[base + i * TOP_K + k]
        return pltpu.make_async_copy(y_ref.at[pl.ds(slot, 1)], rows.at[k, pl.ds(i, 1)], sem)

    def issue(i, carry):
        for k in range(TOP_K):
            copy(i, k).start()
        return carry

    lax.fori_loop(0, tb, issue, 0)

    def drain(i, carry):
        for k in range(TOP_K):
            copy(i, k).wait()
        return carry

    lax.fori_loop(0, tb, drain, 0)

    topw = topw_ref[...]
    lane = lax.broadcasted_iota(jnp.int32, topw.shape, 1)
    acc = jnp.zeros(x1_ref.shape, jnp.float32)
    for k in range(TOP_K):
        wk = jnp.sum(jnp.where(lane == k, topw, 0.0), axis=-1, keepdims=True)
        acc = acc + wk * rows[k]
    out_ref[...] = x1_ref[...] + gate_ref[...] * acc


def _combine(dest_flat, x1, gate, topw, y_sorted):
    t, d = x1.shape
    tb = ROUTE_BLOCK
    gate_rows = gate.shape[0]
    gate_spec = (pl.BlockSpec((1, d), lambda i, *_: (0, 0)) if gate_rows == 1
                 else pl.BlockSpec((tb, d), lambda i, *_: (i, 0)))
    grid_spec = pltpu.PrefetchScalarGridSpec(
        num_scalar_prefetch=1,
        grid=(t // tb,),
        in_specs=[pl.BlockSpec((tb, d), lambda i, *_: (i, 0)),
                  gate_spec,
                  pl.BlockSpec((tb, LANES), lambda i, *_: (i, 0)),
                  pl.BlockSpec(memory_space=pl.ANY)],
        out_specs=pl.BlockSpec((tb, d), lambda i, *_: (i, 0)),
        scratch_shapes=[pltpu.VMEM((TOP_K, tb, d), jnp.float32),
                        pltpu.SemaphoreType.DMA],
    )
    return pl.pallas_call(
        _combine_kernel,
        grid_spec=grid_spec,
        out_shape=jax.ShapeDtypeStruct((t, d), jnp.float32),
        compiler_params=pltpu.CompilerParams(
            dimension_semantics=("arbitrary",), vmem_limit_bytes=VMEM_LIMIT_BYTES),
    )(dest_flat, x1, gate, topw, y_sorted)


def _rotary_tables(pos, inv_freq):
    ang = pos.astype(jnp.float32)[:, None] * inv_freq[None, :]
    cos, sin = jnp.cos(ang), jnp.sin(ang)
    cos_t = jnp.tile(cos, (1, LANES // cos.shape[1]))
    sin_t = jnp.tile(jnp.concatenate([-sin, sin], axis=1), (1, LANES // (2 * sin.shape[1])))
    return cos_t, sin_t


def _retention_tables(chunk):
    log_gamma = jnp.log1p(-jnp.exp2(-5.0 - jnp.arange(RET_HEADS, dtype=jnp.float32)))
    idx = jnp.arange(chunk, dtype=jnp.float32)
    diff = idx[:, None] - idx[None, :]
    decay = jnp.where(diff[None] >= 0,
                      jnp.exp(jnp.maximum(diff, 0.0)[None] * log_gamma[:, None, None]), 0.0)
    q_decay = jnp.exp((idx + 1.0)[:, None] * log_gamma[None, :])
    k_decay = jnp.exp((chunk - 1.0 - idx)[:, None] * log_gamma[None, :])
    qdec = jnp.repeat(q_decay, RET_DV, axis=1)
    kdec = jnp.repeat(k_decay, RET_DK, axis=1)
    g_chunk = jnp.exp(chunk * log_gamma)
    gpow = jnp.repeat(g_chunk.reshape(RET_HEADS // 2, 2), RET_DK, axis=1).T
    return decay, qdec, kdec, gpow


def _block_diag_mean(width):
    head = jnp.arange(width) // HEAD_DIM
    return jnp.where(head[:, None] == head[None, :], 1.0 / HEAD_DIM, 0.0).astype(jnp.bfloat16)


def kernel(x_prompt, x_sample, cache_k, cache_v, state_ret, c_prompt, c_sample, w_ada, b_ada,
           g_mix, w_in, q_gain, k_gain, sinks, w_out, g_ffn, w_router, b_router, w_gate_up,
           b_gate_up, w_down, b_down):
    depth = w_ada.shape[0]
    batch, seq, d = x_prompt.shape
    dec_batch, dec_seq, _ = x_sample.shape
    cache_w = cache_k.shape[2]
    assert batch == 1 and depth == 1
    assert seq % PROMPT_BLOCK == 0 and PROMPT_BLOCK % WINDOW == 0 and WINDOW == RET_CHUNK
    assert dec_batch % SAMPLE_SEQS == 0 and cache_w == WINDOW
    n_p, n_s = batch * seq, dec_batch * dec_seq
    n_tok = n_p + n_s
    assert n_p % ROUTE_BLOCK == 0 and n_s % ROUTE_BLOCK == 0

    l = 0
    f32 = jnp.float32
    rope_freq = 1.0 / (ROPE_THETA ** (jnp.arange(0, HEAD_DIM, 2, dtype=f32) / HEAD_DIM))
    ret_freq = 1.0 / (ROPE_THETA ** jnp.linspace(0.0, 1.0, RET_DK // 2, dtype=f32))
    pos_p = jnp.arange(seq, dtype=jnp.int32)
    pos_s = PAST_LEN + jnp.arange(dec_seq, dtype=jnp.int32)
    tabs_p = _rotary_tables(pos_p, rope_freq) + _rotary_tables(pos_p, ret_freq)
    tabs_s = tuple(jnp.tile(a, (SAMPLE_SEQS, 1))
                   for a in _rotary_tables(pos_s, rope_freq) + _rotary_tables(pos_s, ret_freq))
    ret_p = _retention_tables(RET_CHUNK)
    ret_s = _retention_tables(dec_seq)

    n_c = batch + dec_batch
    c_rows = -(-n_c // SUBLANES) * SUBLANES
    c_all = jnp.concatenate([c_prompt, c_sample, jnp.zeros((c_rows - n_c, d), f32)], axis=0)
    mod = _adaln(c_all, w_ada[l], b_ada[l])
    mod_p = mod[0].reshape(6, d)
    mod_s = jnp.repeat(mod[batch:n_c], dec_seq, axis=0)

    w_in_bf = _bf(w_in[l])
    w_out_bf = _bf(w_out[l])
    qg = jnp.tile(q_gain[l], ATT_HEADS).reshape(1, ATT_W)
    kg = jnp.tile(k_gain[l], ATT_KV_HEADS).reshape(1, KV_W)
    bdq, bdk = _block_diag_mean(ATT_W), _block_diag_mean(KV_W)
    gmix = g_mix[l].reshape(1, d)
    gffn = g_ffn[l].reshape(1, d)
    wr = jnp.pad(w_router[l], ((0, 0), (0, LANES - N_EXPERTS)))
    wr_hi = _bf(wr)
    wr_lo = _bf(wr - wr_hi.astype(f32))
    br = jnp.concatenate([b_router[l], jnp.full((LANES - N_EXPERTS,), NEG_BIG, f32)]).reshape(1, LANES)
    shared = (sinks[l], gmix, gffn, w_in_bf, qg, kg, bdq, bdk)

    (x1_p, h2_p, topi_p, topw_p, kwin_p, vwin_p, rend_p) = _prompt_mix(
        x_prompt.reshape(n_p, d), mod_p, *shared, tabs_p, ret_p, w_out_bf, wr_hi, wr_lo, br)

    (x1_s, h2_s, topi_s, topw_s, kwin_s, vwin_s, rend_s) = _sample_mix(
        x_sample.reshape(n_s, d), mod_s, *shared, tabs_s, ret_s, w_out_bf, wr_hi, wr_lo, br,
        cache_k[l].reshape(dec_batch, cache_w, KV_W), cache_v[l].reshape(dec_batch, cache_w, KV_W),
        state_ret[l].reshape(dec_batch, 2, LANES, RET_DV))

    h2 = jnp.concatenate([h2_p, h2_s], axis=0)
    topi = jnp.concatenate([topi_p, topi_s], axis=0)
    rank, counts = _route_rank(topi)
    counts = counts[0, :N_EXPERTS].astype(jnp.int32)
    tm = EXPERT_TILE
    padded = -(-counts // tm) * tm
    ends = jnp.cumsum(padded)
    offsets = ends - padded
    n_tiles = (n_tok * TOP_K) // tm + N_EXPERTS
    tile_expert = jnp.minimum(
        jnp.searchsorted(ends, jnp.arange(n_tiles, dtype=jnp.int32) * tm, side="right"),
        N_EXPERTS - 1).astype(jnp.int32)
    dest = (offsets[topi[:, :TOP_K]] + rank[:, :TOP_K]).reshape(-1).astype(jnp.int32)

    x_sorted = _dispatch(dest, h2, jnp.zeros((n_tiles * tm, d), f32))
    y_sorted = _experts(tile_expert, x_sorted, w_gate_up[l], b_gate_up[l], w_down[l], b_down[l])

    y_p = _combine(dest[:n_p * TOP_K], x1_p, mod_p[5:6], topw_p, y_sorted)
    y_s = _combine(dest[n_p * TOP_K:], x1_s, mod_s[:, 5 * d:6 * d], topw_s, y_sorted)

    kv5 = lambda a, n: a.reshape(1, n, cache_w, ATT_KV_HEADS, HEAD_DIM)
    st5 = lambda a, n: a.reshape(1, n, RET_HEADS, RET_DK, RET_DV)
    return (y_p.reshape(batch, seq, d), y_s.reshape(dec_batch, dec_seq, d),
            kv5(kwin_p, batch), kv5(vwin_p, batch), st5(rend_p, batch),
            kv5(kwin_s, dec_batch), kv5(vwin_s, dec_batch), st5(rend_s, dec_batch))
```

```python
import functools

import jax
import jax.numpy as jnp
import numpy as np
from jax import lax
from jax.experimental import pallas as pl
from jax.experimental.pallas import tpu as pltpu

HEAD_DIM = 64
ATT_HEADS = 8
ATT_KV_HEADS = 2
ATT_W = ATT_HEADS * HEAD_DIM
KV_W = ATT_KV_HEADS * HEAD_DIM
WINDOW = 128
PAST_LEN = 16384
ROPE_THETA = 10000.0
RET_HEADS = 4
RET_DK = 64
RET_DV = 128
RET_QK_W = RET_HEADS * RET_DK
RET_W = RET_HEADS * RET_DV
RET_CHUNK = 128
N_EXPERTS = 32
TOP_K = 4
SWIGLU_LIMIT = 7.0
SWIGLU_ALPHA = 1.702
EPS = 1e-6

_Q0, _K0, _V0 = 0, ATT_W, ATT_W + KV_W
_RQ0 = ATT_W + 2 * KV_W
_RK0 = _RQ0 + RET_QK_W
_RV0 = _RK0 + RET_QK_W
_RG0 = _RV0 + RET_W
IN_W = _RG0 + RET_W

LANES = 128
SUBLANES = 8
VMEM_LIMIT_BYTES = 56 * 1024 * 1024
NEG_BIG = -1e30

PROMPT_BLOCK = 512
SAMPLE_SEQS = 16
ROUTE_BLOCK = 512
EXPERT_TILE = 512
CHUNK_BITS = ROUTE_BLOCK.bit_length()
STAGE_ROWS = TOP_K * ROUTE_BLOCK

_NT = (((1,), (1,)), ((), ()))
_TN = (((0,), (0,)), ((), ()))


def _bf(x):
    return x.astype(jnp.bfloat16)


def _dot(a, b):
    return jnp.dot(_bf(a), _bf(b), preferred_element_type=jnp.float32)


def _dot_nt(a, b):
    return lax.dot_general(_bf(a), _bf(b), _NT, preferred_element_type=jnp.float32)


def _dot_tn(a, b):
    return lax.dot_general(_bf(a), _bf(b), _TN, preferred_element_type=jnp.float32)


def _lane_lo(shape):
    lane = lax.broadcasted_iota(jnp.int32, shape, len(shape) - 1)
    return (lane % LANES) < HEAD_DIM


def _swap_halves(x):
    lane = lax.broadcasted_iota(jnp.int32, x.shape, 1)
    first = (lane % HEAD_DIM) < (HEAD_DIM // 2)
    return jnp.where(first, pltpu.roll(x, LANES - HEAD_DIM // 2, axis=1),
                     pltpu.roll(x, HEAD_DIM // 2, axis=1))


def _rotate(x, cos, sin_signed):
    outs = []
    for j in range(x.shape[1] // LANES):
        xs = x[:, j * LANES:(j + 1) * LANES]
        outs.append(xs * cos + _swap_halves(xs) * sin_signed)
    return outs[0] if len(outs) == 1 else jnp.concatenate(outs, axis=1)


def _rms_rows(x):
    return x * lax.rsqrt(jnp.mean(x * x, axis=-1, keepdims=True) + EPS)


def _pre_norm(x, gain, shift, scale):
    return _rms_rows(x) * gain * (1.0 + scale) + shift


def _head_norm(x, gain_tiled, blockdiag):
    ms = jnp.dot(_bf(x * x), blockdiag, preferred_element_type=jnp.float32)
    return x * lax.rsqrt(ms + EPS) * gain_tiled


def _project(h, w_in_ref, qg, kg, bd_q, bd_k, cos_a, sin_a, cos_r, sin_r):
    z = jnp.dot(_bf(h), w_in_ref[...], preferred_element_type=jnp.float32)
    q = _rotate(_head_norm(z[:, _Q0:_K0], qg, bd_q), cos_a, sin_a) * (HEAD_DIM ** -0.5)
    k = _rotate(_head_norm(z[:, _K0:_V0], kg, bd_k), cos_a, sin_a)
    v = z[:, _V0:_RQ0]
    rq = _rotate(z[:, _RQ0:_RK0], cos_r, sin_r)
    rk = _rotate(z[:, _RK0:_RV0], cos_r, sin_r) * (RET_DK ** -0.5)
    rv = z[:, _RV0:_RG0]
    rg = z[:, _RG0:IN_W]
    return q, k, v, rq, rk, rv, rg


_NAT_HEADS = (0, 2, 5, 7)
_ROL_HEADS = (1, 3, 4, 6)


def _stack_heads(q, heads):
    lo = _lane_lo((q.shape[0], LANES))
    parts = []
    for j in heads:
        slab = q[:, (j // 2) * LANES:(j // 2 + 1) * LANES]
        parts.append(jnp.where(lo if j % 2 == 0 else ~lo, slab, 0.0))
    return jnp.concatenate(parts, axis=0)


def _sink_column(sinks_ref, heads, rows):
    r = lax.broadcasted_iota(jnp.int32, (len(heads) * rows, 1), 0)
    col = jnp.full((len(heads) * rows, 1), sinks_ref[heads[-1]], jnp.float32)
    for n in range(len(heads) - 2, -1, -1):
        col = jnp.where(r < (n + 1) * rows, sinks_ref[heads[n]], col)
    return col


def _softmax_pv(s, mask, sink, v):
    s = jnp.where(mask, s, NEG_BIG)
    m = jnp.maximum(jnp.max(s, axis=-1, keepdims=True), sink)
    p = jnp.exp(s - m)
    denom = jnp.sum(p, axis=-1, keepdims=True) + jnp.exp(sink - m)
    return _dot(p, v) / denom


def _attention(q, k_all, v_all, mask, sinks_ref):
    rows = q.shape[0]
    k_rol = pltpu.roll(k_all, HEAD_DIM, axis=1)
    v_rol = pltpu.roll(v_all, HEAD_DIM, axis=1)
    mask4 = jnp.concatenate([mask] * 4, axis=0)
    o_nat = _softmax_pv(_dot_nt(_stack_heads(q, _NAT_HEADS), k_all), mask4,
                        _sink_column(sinks_ref, _NAT_HEADS, rows), v_all)
    o_rol = _softmax_pv(_dot_nt(_stack_heads(q, _ROL_HEADS), k_rol), mask4,
                        _sink_column(sinks_ref, _ROL_HEADS, rows), v_rol)
    lo = _lane_lo((rows, LANES))
    blk = lambda o, n: o[n * rows:(n + 1) * rows]
    return jnp.concatenate([
        jnp.where(lo, blk(o_nat, 0), blk(o_rol, 0)),
        jnp.where(lo, blk(o_nat, 1), blk(o_rol, 1)),
        jnp.where(lo, blk(o_rol, 2), blk(o_nat, 2)),
        jnp.where(lo, blk(o_rol, 3), blk(o_nat, 3)),
    ], axis=1)


def _retention(rq, rk, rv, rg, state, decay_ref, qdec, kdec, gpow):
    rows = rq.shape[0]
    lo = _lane_lo((rows, LANES))
    rkd = rk * kdec
    outs, new_state = [], []
    for m in range(RET_HEADS // 2):
        sl = slice(m * LANES, (m + 1) * LANES)
        upd = state[m] * gpow[:, m:m + 1]
        for p in range(2):
            h = 2 * m + p
            half = lo if p == 0 else ~lo
            rqm = jnp.where(half, rq[:, sl], 0.0)
            rv_h = rv[:, h * RET_DV:(h + 1) * RET_DV]
            s = _dot_nt(rqm, rk[:, sl]) * decay_ref[h]
            ret = _dot(s, rv_h) + _dot(rqm, state[m]) * qdec[:, h * RET_DV:(h + 1) * RET_DV]
            upd = upd + _dot_tn(jnp.where(half, rkd[:, sl], 0.0), rv_h)
            rg_h = rg[:, h * RET_DV:(h + 1) * RET_DV]
            outs.append(rg_h * jax.nn.sigmoid(rg_h) * _rms_rows(ret))
        new_state.append(upd)
    return jnp.concatenate(outs, axis=1), new_state


def _route(h2, wr_hi_ref, wr_lo_ref, br):
    h_hi = _bf(h2)
    h_lo = _bf(h2 - h_hi.astype(jnp.float32))
    logits = (jnp.dot(h_hi, wr_hi_ref[...], preferred_element_type=jnp.float32)
              + jnp.dot(h_hi, wr_lo_ref[...], preferred_element_type=jnp.float32)
              + jnp.dot(h_lo, wr_hi_ref[...], preferred_element_type=jnp.float32)) + br
    lane = lax.broadcasted_iota(jnp.int32, logits.shape, 1)
    idx_out = jnp.zeros(logits.shape, jnp.int32)
    val_out = jnp.zeros(logits.shape, jnp.float32)
    onehot = jnp.zeros(logits.shape, jnp.float32)
    vals = []
    work = logits
    for k in range(TOP_K):
        mx = jnp.max(work, axis=-1, keepdims=True)
        ix = jnp.min(jnp.where(work == mx, lane, LANES), axis=-1, keepdims=True)
        work = jnp.where(lane == ix, NEG_BIG, work)
        onehot = jnp.where(lane == ix, 1.0, onehot)
        idx_out = jnp.where(lane == k, ix, idx_out)
        vals.append(mx)
    exps = [jnp.exp(v - vals[0]) for v in vals]
    total = exps[0] + exps[1] + exps[2] + exps[3]
    for k in range(TOP_K):
        val_out = jnp.where(lane == k, exps[k] / total, val_out)
    idx_t = idx_out.astype(jnp.float32).T[0:SUBLANES].astype(jnp.int32)
    counts = jnp.sum(onehot, axis=0, keepdims=True).astype(jnp.int32)
    return idx_out, val_out, idx_t, counts


def _adaln_kernel(c_ref, w_ref, b_ref, o_ref):
    c = c_ref[...]
    o_ref[...] = _dot(c * jax.nn.sigmoid(c), w_ref[...]) + b_ref[...]


def _adaln(c_all, w_ada, b_ada):
    rows, d = c_all.shape
    n = w_ada.shape[1]
    bn = 1536
    return pl.pallas_call(
        _adaln_kernel,
        grid=(n // bn,),
        in_specs=[pl.BlockSpec((rows, d), lambda j: (0, 0)),
                  pl.BlockSpec((d, bn), lambda j: (0, j)),
                  pl.BlockSpec((1, bn), lambda j: (0, j))],
        out_specs=pl.BlockSpec((rows, bn), lambda j: (0, j)),
        out_shape=jax.ShapeDtypeStruct((rows, n), jnp.float32),
        compiler_params=pltpu.CompilerParams(vmem_limit_bytes=VMEM_LIMIT_BYTES),
    )(c_all, w_ada, b_ada.reshape(1, n))


def _prompt_mix_kernel(sinks_ref, x_ref, mod_ref, gmix_ref, gffn_ref, w_in_ref, qg_ref, kg_ref,
                       bdq_ref, bdk_ref, cosa_ref, sina_ref, cosr_ref, sinr_ref,
                       decay_ref, qdec_ref, kdec_ref, gpow_ref, w_out_ref,
                       wrh_ref, wrl_ref, br_ref,
                       x1_ref, h2_ref, topi_ref, topw_ref, topit_ref, cnt_ref, kwin_ref, vwin_ref, rend_ref,
                       kprev, vprev, state, mix):
    step = pl.program_id(0)
    tb = x_ref.shape[0]

    @pl.when(step == 0)
    def _():
        kprev[...] = jnp.zeros_like(kprev)
        vprev[...] = jnp.zeros_like(vprev)
        state[...] = jnp.zeros_like(state)

    x = x_ref[...]
    h = _pre_norm(x, gmix_ref[...], mod_ref[0:1, :], mod_ref[1:2, :])
    q, k, v, rq, rk, rv, rg = _project(
        h, w_in_ref, qg_ref[...], kg_ref[...], bdq_ref[...], bdk_ref[...],
        cosa_ref[...], sina_ref[...], cosr_ref[...], sinr_ref[...])

    qi = lax.broadcasted_iota(jnp.int32, (WINDOW, 2 * WINDOW), 0)
    ci = lax.broadcasted_iota(jnp.int32, (WINDOW, 2 * WINDOW), 1)
    band = (ci > qi) & (ci <= qi + WINDOW)
    qdec, kdec, gpow = qdec_ref[...], kdec_ref[...], gpow_ref[...]

    for sb in range(tb // WINDOW):
        rs = slice(sb * WINDOW, (sb + 1) * WINDOW)
        k_sb, v_sb = k[rs], v[rs]
        k_all = jnp.concatenate([kprev[...], k_sb], axis=0)
        v_all = jnp.concatenate([vprev[...], v_sb], axis=0)
        has_prev = jnp.logical_or(step > 0, sb > 0) if sb == 0 else True
        mask = band & ((ci >= WINDOW) | has_prev)
        att = _attention(q[rs], k_all, v_all, mask, sinks_ref)
        kprev[...] = k_sb
        vprev[...] = v_sb
        yret, new_state = _retention(rq[rs], rk[rs], rv[rs], rg[rs], [state[0], state[1]],
                                     decay_ref, qdec, kdec, gpow)
        state[0] = new_state[0]
        state[1] = new_state[1]
        mix[rs, 0:ATT_W] = _bf(att)
        mix[rs, ATT_W:ATT_W + RET_W] = _bf(yret)

    x1 = x + mod_ref[2:3, :] * jnp.dot(mix[...], w_out_ref[...],
                                       preferred_element_type=jnp.float32)
    x1_ref[...] = x1
    h2 = _pre_norm(x1, gffn_ref[...], mod_ref[3:4, :], mod_ref[4:5, :])
    h2_ref[...] = h2
    topi, topw, topi_t, counts = _route(h2, wrh_ref, wrl_ref, br_ref[...])
    topi_ref[...] = topi
    topw_ref[...] = topw
    topit_ref[...] = topi_t
    cnt_ref[0] = counts

    @pl.when(step == pl.num_programs(0) - 1)
    def _():
        kwin_ref[...] = kprev[...]
        vwin_ref[...] = vprev[...]
        rend_ref[...] = state[...]


def _const_spec(shape):
    nd = len(shape)
    return pl.BlockSpec(shape, lambda *_: (0,) * nd)


def _prompt_mix(x, mod, sinks, gmix, gffn, w_in, qg, kg, bdq, bdk, tabs, ret_tabs, w_out,
                wrh, wrl, br):
    t, d = x.shape
    tb = PROMPT_BLOCK
    cosa, sina, cosr, sinr = tabs
    decay, qdec, kdec, gpow = ret_tabs
    row_spec = lambda w: pl.BlockSpec((tb, w), lambda i, *_: (i, 0))
    consts = [mod, gmix, gffn, w_in, qg, kg, bdq, bdk]
    consts2 = [decay, qdec, kdec, gpow, w_out, wrh, wrl, br]
    grid_spec = pltpu.PrefetchScalarGridSpec(
        num_scalar_prefetch=1,
        grid=(t // tb,),
        in_specs=([row_spec(d)] + [_const_spec(a.shape) for a in consts]
                  + [row_spec(LANES)] * 4 + [_const_spec(a.shape) for a in consts2]),
        out_specs=[row_spec(d), row_spec(d), row_spec(LANES), row_spec(LANES),
                   pl.BlockSpec((SUBLANES, tb), lambda i, *_: (0, i)),
                   pl.BlockSpec((1, 1, LANES), lambda i, *_: (i, 0, 0)),
                   _const_spec((WINDOW, KV_W)), _const_spec((WINDOW, KV_W)),
                   _const_spec((2, LANES, RET_DV))],
        scratch_shapes=[pltpu.VMEM((WINDOW, KV_W), jnp.float32),
                        pltpu.VMEM((WINDOW, KV_W), jnp.float32),
                        pltpu.VMEM((2, LANES, RET_DV), jnp.float32),
                        pltpu.VMEM((tb, ATT_W + RET_W), jnp.bfloat16)],
    )
    return pl.pallas_call(
        _prompt_mix_kernel,
        grid_spec=grid_spec,
        out_shape=[jax.ShapeDtypeStruct((t, d), jnp.float32),
                   jax.ShapeDtypeStruct((t, d), jnp.float32),
                   jax.ShapeDtypeStruct((t, LANES), jnp.int32),
                   jax.ShapeDtypeStruct((t, LANES), jnp.float32),
                   jax.ShapeDtypeStruct((SUBLANES, t), jnp.int32),
                   jax.ShapeDtypeStruct((t // tb, 1, LANES), jnp.int32),
                   jax.ShapeDtypeStruct((WINDOW, KV_W), jnp.float32),
                   jax.ShapeDtypeStruct((WINDOW, KV_W), jnp.float32),
                   jax.ShapeDtypeStruct((2, LANES, RET_DV), jnp.float32)],
        compiler_params=pltpu.CompilerParams(
            dimension_semantics=("arbitrary",), vmem_limit_bytes=VMEM_LIMIT_BYTES),
    )(sinks, x, *consts, cosa, sina, cosr, sinr, *consts2)


def _sample_mix_kernel(sinks_ref, x_ref, mod_ref, gmix_ref, gffn_ref, w_in_ref, qg_ref, kg_ref,
                       bdq_ref, bdk_ref, cosa_ref, sina_ref, cosr_ref, sinr_ref,
                       decay_ref, qdec_ref, kdec_ref, gpow_ref, w_out_ref,
                       wrh_ref, wrl_ref, br_ref, ck_ref, cv_ref, st_ref,
                       x1_ref, h2_ref, topi_ref, topw_ref, topit_ref, cnt_ref, kwin_ref, vwin_ref, rend_ref,
                       mix):
    nseq, cache_w = ck_ref.shape[0], ck_ref.shape[1]
    ds = x_ref.shape[0] // nseq
    d = x_ref.shape[1]
    x = x_ref[...]
    mod = lambda i: mod_ref[:, i * d:(i + 1) * d]
    h = _pre_norm(x, gmix_ref[...], mod(0), mod(1))
    q, k, v, rq, rk, rv, rg = _project(
        h, w_in_ref, qg_ref[...], kg_ref[...], bdq_ref[...], bdk_ref[...],
        cosa_ref[...], sina_ref[...], cosr_ref[...], sinr_ref[...])

    qi = lax.broadcasted_iota(jnp.int32, (ds, cache_w + ds), 0)
    ci = lax.broadcasted_iota(jnp.int32, (ds, cache_w + ds), 1)
    delta = qi + cache_w - ci
    mask = (delta >= 0) & (delta < WINDOW)
    qdec, kdec, gpow = qdec_ref[...], kdec_ref[...], gpow_ref[...]

    for s in range(nseq):
        rs = slice(s * ds, (s + 1) * ds)
        k_all = jnp.concatenate([ck_ref[s], k[rs]], axis=0)
        v_all = jnp.concatenate([cv_ref[s], v[rs]], axis=0)
        att = _attention(q[rs], k_all, v_all, mask, sinks_ref)
        kwin_ref[s] = k_all[ds:]
        vwin_ref[s] = v_all[ds:]
        yret, new_state = _retention(rq[rs], rk[rs], rv[rs], rg[rs],
                                     [st_ref[s, 0], st_ref[s, 1]],
                                     decay_ref, qdec, kdec, gpow)
        rend_ref[s, 0] = new_state[0]
        rend_ref[s, 1] = new_state[1]
        mix[rs, 0:ATT_W] = _bf(att)
        mix[rs, ATT_W:ATT_W + RET_W] = _bf(yret)

    x1 = x + mod(2) * jnp.dot(mix[...], w_out_ref[...], preferred_element_type=jnp.float32)
    x1_ref[...] = x1
    h2 = _pre_norm(x1, gffn_ref[...], mod(3), mod(4))
    h2_ref[...] = h2
    topi, topw, topi_t, counts = _route(h2, wrh_ref, wrl_ref, br_ref[...])
    topi_ref[...] = topi
    topw_ref[...] = topw
    topit_ref[...] = topi_t
    cnt_ref[0] = counts


def _sample_mix(x, mod_tok, sinks, gmix, gffn, w_in, qg, kg, bdq, bdk, tabs, ret_tabs, w_out,
                wrh, wrl, br, cache_k, cache_v, state):
    t, d = x.shape
    nb, cache_w = cache_k.shape[0], cache_k.shape[1]
    ds = t // nb
    g = SAMPLE_SEQS
    tb = g * ds
    cosa, sina, cosr, sinr = tabs
    decay, qdec, kdec, gpow = ret_tabs
    row_spec = lambda w: pl.BlockSpec((tb, w), lambda i, *_: (i, 0))
    seq3 = lambda a: pl.BlockSpec((g,) + a.shape[1:], lambda i, *_: (i,) + (0,) * (a.ndim - 1))
    consts = [gmix, gffn, w_in, qg, kg, bdq, bdk, cosa, sina, cosr, sinr,
              decay, qdec, kdec, gpow, w_out, wrh, wrl, br]
    grid_spec = pltpu.PrefetchScalarGridSpec(
        num_scalar_prefetch=1,
        grid=(nb // g,),
        in_specs=([row_spec(d), row_spec(mod_tok.shape[1])]
                  + [_const_spec(a.shape) for a in consts]
                  + [seq3(cache_k), seq3(cache_v), seq3(state)]),
        out_specs=[row_spec(d), row_spec(d), row_spec(LANES), row_spec(LANES),
                   pl.BlockSpec((SUBLANES, tb), lambda i, *_: (0, i)),
                   pl.BlockSpec((1, 1, LANES), lambda i, *_: (i, 0, 0)),
                   seq3(cache_k), seq3(cache_v), seq3(state)],
        scratch_shapes=[pltpu.VMEM((tb, ATT_W + RET_W), jnp.bfloat16)],
    )
    return pl.pallas_call(
        _sample_mix_kernel,
        grid_spec=grid_spec,
        out_shape=[jax.ShapeDtypeStruct((t, d), jnp.float32),
                   jax.ShapeDtypeStruct((t, d), jnp.float32),
                   jax.ShapeDtypeStruct((t, LANES), jnp.int32),
                   jax.ShapeDtypeStruct((t, LANES), jnp.float32),
                   jax.ShapeDtypeStruct((SUBLANES, t), jnp.int32),
                   jax.ShapeDtypeStruct((t // tb, 1, LANES), jnp.int32),
                   jax.ShapeDtypeStruct(cache_k.shape, jnp.float32),
                   jax.ShapeDtypeStruct(cache_v.shape, jnp.float32),
                   jax.ShapeDtypeStruct(state.shape, jnp.float32)],
        compiler_params=pltpu.CompilerParams(
            dimension_semantics=("arbitrary",), vmem_limit_bytes=VMEM_LIMIT_BYTES),
    )(sinks, x, mod_tok, *consts, cache_k, cache_v, state)


def _rows_to_tiles(ref, x):
    for j in range(ref.shape[1]):
        ref[:, j, :] = x[:, j * LANES:(j + 1) * LANES]


def _tiles_to_rows(ref):
    return jnp.concatenate([ref[:, j, :] for j in range(ref.shape[1])], axis=1)


def _chunk_copies(cnt, src_ref, src_start, dst_ref, dst_start, sem):
    for bit in range(CHUNK_BITS - 1, -1, -1):
        size = 1 << bit
        done = (cnt >> (bit + 1)) << (bit + 1)

        @pl.when(((cnt >> bit) & 1) == 1)
        def _():
            pltpu.make_async_copy(src_ref.at[pl.ds(src_start + done, size)],
                                  dst_ref.at[pl.ds(dst_start + done, size)], sem).start()


def _wait_rows(ref, nrows, sem):
    pltpu.make_async_copy(ref.at[pl.ds(0, nrows)], ref.at[pl.ds(0, nrows)], sem).wait()


def _dispatch_kernel(cnt_ref, gstart_ref, cb_ref, ends_ref, tot_ref,
                     topit_p_ref, topit_s_ref, h2_p_ref, h2_s_ref, cbcol_ref, xs_ref,
                     stage, zeros, sems, zsem, *, nblk_p):
    step = pl.program_id(0)
    nblk = pl.num_programs(0)
    tb = h2_p_ref.shape[0]
    slot = step % 2

    @pl.when(step == 0)
    def _():
        zeros[...] = jnp.zeros_like(zeros)
        tile = zeros.shape[0]

        def fill(e):
            return pltpu.make_async_copy(zeros, xs_ref.at[pl.ds(ends_ref[e] - tile, tile)], zsem)

        for e in range(N_EXPERTS):
            @pl.when(tot_ref[e] > 0)
            def _():
                fill(e).start()
        for e in range(N_EXPERTS):
            @pl.when(tot_ref[e] > 0)
            def _():
                fill(e).wait()

    def tail_fill(t):
        tile = zeros.shape[0]
        return pltpu.make_async_copy(zeros, xs_ref.at[pl.ds(t * tile, tile)], zsem)

    first_unused = ends_ref[N_EXPERTS - 1] // zeros.shape[0]
    n_tiles = xs_ref.shape[0] // zeros.shape[0]

    @pl.when(step == 0)
    def _():
        lax.fori_loop(first_unused, n_tiles, lambda t, c: (tail_fill(t).start(), c)[1], 0)

    @pl.when(step == nblk - 1)
    def _():
        lax.fori_loop(first_unused, n_tiles, lambda t, c: (tail_fill(t).wait(), c)[1], 0)

    is_p = step < nblk_p
    topit = jnp.where(is_p, topit_p_ref[...], topit_s_ref[...])
    h2 = jnp.where(is_p, h2_p_ref[...], h2_s_ref[...])
    cbcol = cbcol_ref[0]

    sub = lax.broadcasted_iota(jnp.int32, (LANES, tb), 0)
    picks = [sub == topit[k:k + 1, :] for k in range(TOP_K)]
    onehot = jnp.zeros((LANES, tb), jnp.float32)
    for pk in picks:
        onehot = jnp.where(pk, 1.0, onehot)
    r = lax.broadcasted_iota(jnp.int32, (tb, tb), 0)
    c = lax.broadcasted_iota(jnp.int32, (tb, tb), 1)
    earlier = jnp.where(r < c, 1.0, 0.0)
    before = _dot(onehot, earlier)
    base = before + cbcol.astype(jnp.float32)
    row = lax.broadcasted_iota(jnp.int32, (stage.shape[1], tb), 0)
    sel = jnp.zeros((stage.shape[1], tb), jnp.float32)
    for pk in picks:
        col = jnp.sum(jnp.where(pk, base, 0.0), axis=0, keepdims=True).astype(jnp.int32)
        sel = jnp.where(row == col, 1.0, sel)

    _rows_to_tiles(stage.at[slot], _dot(sel, h2))

    @pl.when(step > 0)
    def _():
        _wait_rows(xs_ref, tb * TOP_K, sems.at[1 - slot])

    def issue(e, carry):
        i = step * N_EXPERTS + e
        _chunk_copies(cnt_ref[i], stage.at[slot], cb_ref[i], xs_ref, gstart_ref[i], sems.at[slot])
        return carry

    lax.fori_loop(0, N_EXPERTS, issue, 0)

    @pl.when(step == nblk - 1)
    def _():
        _wait_rows(xs_ref, tb * TOP_K, sems.at[slot])


def _dispatch(tables, topit_p, topit_s, h2_p, h2_s, cbcol, n_rows):
    n_p, d = h2_p.shape
    n_s = h2_s.shape[0]
    tb = ROUTE_BLOCK
    nblk_p, nblk_s = n_p // tb, n_s // tb
    pidx = lambda i, *_: jnp.minimum(i, nblk_p - 1)
    sidx = lambda i, *_: jnp.maximum(i - nblk_p, 0)
    grid_spec = pltpu.PrefetchScalarGridSpec(
        num_scalar_prefetch=5,
        grid=(nblk_p + nblk_s,),
        in_specs=[pl.BlockSpec((SUBLANES, tb), lambda i, *_: (0, pidx(i))),
                  pl.BlockSpec((SUBLANES, tb), lambda i, *_: (0, sidx(i))),
                  pl.BlockSpec((tb, d), lambda i, *_: (pidx(i), 0)),
                  pl.BlockSpec((tb, d), lambda i, *_: (sidx(i), 0)),
                  pl.BlockSpec((1, LANES, 1), lambda i, *_: (i, 0, 0))],
        out_specs=pl.BlockSpec(memory_space=pl.ANY),
        scratch_shapes=[pltpu.VMEM((2, STAGE_ROWS, SUBLANES, LANES), jnp.float32),
                        pltpu.VMEM((EXPERT_TILE, SUBLANES, LANES), jnp.float32),
                        pltpu.SemaphoreType.DMA((2,)),
                        pltpu.SemaphoreType.DMA],
    )
    return pl.pallas_call(
        functools.partial(_dispatch_kernel, nblk_p=nblk_p),
        grid_spec=grid_spec,
        out_shape=jax.ShapeDtypeStruct((n_rows, SUBLANES, LANES), jnp.float32),
        compiler_params=pltpu.CompilerParams(
            dimension_semantics=("arbitrary",), vmem_limit_bytes=VMEM_LIMIT_BYTES),
    )(*tables, topit_p, topit_s, h2_p, h2_s, cbcol)


def _experts_kernel(tile_expert_ref, n_used_ref, x_ref, wgu_ref, bgu_ref, wd_ref, bd_ref, y_ref,
                    wgu_bf, wd_bf):
    step = pl.program_id(0)

    @pl.when(step < n_used_ref[0])
    def _():
        e = tile_expert_ref[step]
        prev = tile_expert_ref[jnp.maximum(step - 1, 0)]

        @pl.when(jnp.logical_or(step == 0, e != prev))
        def _():
            wgu_bf[...] = _bf(wgu_ref[0])
            wd_bf[...] = _bf(wd_ref[0])

        d_ff = wd_bf.shape[0]
        gu = jnp.dot(_bf(_tiles_to_rows(x_ref)), wgu_bf[...],
                     preferred_element_type=jnp.float32) + bgu_ref[0]
        glu = jnp.minimum(gu[:, :d_ff], SWIGLU_LIMIT)
        lin = jnp.clip(gu[:, d_ff:], -SWIGLU_LIMIT, SWIGLU_LIMIT)
        act = glu * jax.nn.sigmoid(SWIGLU_ALPHA * glu) * (lin + 1.0)
        _rows_to_tiles(y_ref, jnp.dot(_bf(act), wd_bf[...],
                                      preferred_element_type=jnp.float32) + bd_ref[0])

    @pl.when(step >= n_used_ref[0])
    def _():
        y_ref[...] = jnp.zeros_like(y_ref)


def _experts(tile_expert, n_used, x_sorted, w_gate_up, b_gate_up, w_down, b_down):
    p = x_sorted.shape[0]
    n_e, d, two_ff = w_gate_up.shape
    d_ff = two_ff // 2
    tm = EXPERT_TILE
    tile = lambda i, te, nu: jnp.minimum(i, nu[0] - 1)
    grid_spec = pltpu.PrefetchScalarGridSpec(
        num_scalar_prefetch=2,
        grid=(p // tm,),
        in_specs=[pl.BlockSpec((tm, SUBLANES, LANES), lambda i, te, nu: (tile(i, te, nu), 0, 0)),
                  pl.BlockSpec((1, d, two_ff), lambda i, te, nu: (te[i], 0, 0)),
                  pl.BlockSpec((1, 1, two_ff), lambda i, te, nu: (te[i], 0, 0)),
                  pl.BlockSpec((1, d_ff, d), lambda i, te, nu: (te[i], 0, 0)),
                  pl.BlockSpec((1, 1, d), lambda i, te, nu: (te[i], 0, 0))],
        out_specs=pl.BlockSpec((tm, SUBLANES, LANES), lambda i, te, nu: (i, 0, 0)),
        scratch_shapes=[pltpu.VMEM((d, two_ff), jnp.bfloat16),
                        pltpu.VMEM((d_ff, d), jnp.bfloat16)],
    )
    return pl.pallas_call(
        _experts_kernel,
        grid_spec=grid_spec,
        out_shape=jax.ShapeDtypeStruct(x_sorted.shape, jnp.float32),
        compiler_params=pltpu.CompilerParams(
            dimension_semantics=("arbitrary",), vmem_limit_bytes=VMEM_LIMIT_BYTES),
    )(tile_expert, n_used, x_sorted, w_gate_up, b_gate_up.reshape(n_e, 1, two_ff), w_down,
      b_down.reshape(n_e, 1, d))


def _combine_kernel(cnt_ref, gstart_ref, cb_ref,
                    topi_p_ref, topi_s_ref, topw_p_ref, topw_s_ref, x1_p_ref, x1_s_ref,
                    gate_p_ref, gate_s_ref, cbrow_ref, y_ref, out_p_ref, out_s_ref,
                    ybuf, sems, *, nblk_p):
    step = pl.program_id(0)
    nblk = pl.num_programs(0)
    tb = x1_p_ref.shape[0]
    slot = step % 2

    def fetch(blk, into):
        def issue(e, carry):
            i = blk * N_EXPERTS + e
            _chunk_copies(cnt_ref[i], y_ref, gstart_ref[i], ybuf.at[into], cb_ref[i], sems.at[into])
            return carry
        lax.fori_loop(0, N_EXPERTS, issue, 0)

    @pl.when(step == 0)
    def _():
        fetch(0, 0)

    @pl.when(step + 1 < nblk)
    def _():
        fetch(step + 1, 1 - slot)

    is_p = step < nblk_p
    topi = jnp.where(is_p, topi_p_ref[...], topi_s_ref[...])
    topw = jnp.where(is_p, topw_p_ref[...], topw_s_ref[...])
    cbrow = cbrow_ref[0]

    lane = lax.broadcasted_iota(jnp.int32, (tb, LANES), 1)
    picks = []
    onehot = jnp.zeros((tb, LANES), jnp.float32)
    for k in range(TOP_K):
        ix = jnp.sum(jnp.where(lane == k, topi, 0), axis=-1, keepdims=True)
        picks.append(lane == ix)
        onehot = jnp.where(picks[k], 1.0, onehot)
    r = lax.broadcasted_iota(jnp.int32, (tb, tb), 0)
    c = lax.broadcasted_iota(jnp.int32, (tb, tb), 1)
    earlier = jnp.where(c < r, 1.0, 0.0)
    base = _dot(earlier, onehot) + cbrow.astype(jnp.float32)
    colid = lax.broadcasted_iota(jnp.int32, (tb, ybuf.shape[1]), 1)
    weights = jnp.zeros((tb, ybuf.shape[1]), jnp.float32)
    for k in range(TOP_K):
        col = jnp.sum(jnp.where(picks[k], base, 0.0), axis=-1, keepdims=True).astype(jnp.int32)
        wk = jnp.sum(jnp.where(lane == k, topw, 0.0), axis=-1, keepdims=True)
        weights = jnp.where(colid == col, wk, weights)

    _wait_rows(y_ref, tb * TOP_K, sems.at[slot])
    moe = _dot(weights, _tiles_to_rows(ybuf.at[slot]))

    @pl.when(is_p)
    def _():
        out_p_ref[...] = x1_p_ref[...] + gate_p_ref[...] * moe

    @pl.when(jnp.logical_not(is_p))
    def _():
        out_s_ref[...] = x1_s_ref[...] + gate_s_ref[...] * moe


def _combine(tables, topi_p, topi_s, topw_p, topw_s, x1_p, x1_s, gate_p, gate_s, cbrow, y_sorted):
    n_p, d = x1_p.shape
    n_s = x1_s.shape[0]
    tb = ROUTE_BLOCK
    nblk_p, nblk_s = n_p // tb, n_s // tb
    pidx = lambda i, *_: jnp.minimum(i, nblk_p - 1)
    sidx = lambda i, *_: jnp.maximum(i - nblk_p, 0)
    prow = lambda w: pl.BlockSpec((tb, w), lambda i, *_: (pidx(i), 0))
    srow = lambda w: pl.BlockSpec((tb, w), lambda i, *_: (sidx(i), 0))
    grid_spec = pltpu.PrefetchScalarGridSpec(
        num_scalar_prefetch=3,
        grid=(nblk_p + nblk_s,),
        in_specs=[prow(LANES), srow(LANES), prow(LANES), srow(LANES), prow(d), srow(d),
                  pl.BlockSpec((1, d), lambda i, *_: (0, 0)), srow(d),
                  pl.BlockSpec((1, 1, LANES), lambda i, *_: (i, 0, 0)),
                  pl.BlockSpec(memory_space=pl.ANY)],
        out_specs=[prow(d), srow(d)],
        scratch_shapes=[pltpu.VMEM((2, STAGE_ROWS, SUBLANES, LANES), jnp.float32),
                        pltpu.SemaphoreType.DMA((2,))],
    )
    return pl.pallas_call(
        functools.partial(_combine_kernel, nblk_p=nblk_p),
        grid_spec=grid_spec,
        out_shape=[jax.ShapeDtypeStruct((n_p, d), jnp.float32),
                   jax.ShapeDtypeStruct((n_s, d), jnp.float32)],
        compiler_params=pltpu.CompilerParams(
            dimension_semantics=("arbitrary",), vmem_limit_bytes=VMEM_LIMIT_BYTES),
    )(*tables, topi_p, topi_s, topw_p, topw_s, x1_p, x1_s, gate_p, gate_s, cbrow, y_sorted)


def _rotary_tables(pos, inv_freq):
    ang = pos.astype(jnp.float32)[:, None] * inv_freq[None, :]
    cos, sin = jnp.cos(ang), jnp.sin(ang)
    cos_t = jnp.tile(cos, (1, LANES // cos.shape[1]))
    sin_t = jnp.tile(jnp.concatenate([-sin, sin], axis=1), (1, LANES // (2 * sin.shape[1])))
    return cos_t, sin_t


def _retention_tables(chunk):
    log_gamma = jnp.log1p(-jnp.exp2(-5.0 - jnp.arange(RET_HEADS, dtype=jnp.float32)))
    idx = jnp.arange(chunk, dtype=jnp.float32)
    diff = idx[:, None] - idx[None, :]
    decay = jnp.where(diff[None] >= 0,
                      jnp.exp(jnp.maximum(diff, 0.0)[None] * log_gamma[:, None, None]), 0.0)
    q_decay = jnp.exp((idx + 1.0)[:, None] * log_gamma[None, :])
    k_decay = jnp.exp((chunk - 1.0 - idx)[:, None] * log_gamma[None, :])
    qdec = jnp.repeat(q_decay, RET_DV, axis=1)
    kdec = jnp.repeat(k_decay, RET_DK, axis=1)
    g_chunk = jnp.exp(chunk * log_gamma)
    gpow = jnp.repeat(g_chunk.reshape(RET_HEADS // 2, 2), RET_DK, axis=1).T
    return decay, qdec, kdec, gpow


def _block_diag_mean(width):
    head = jnp.arange(width) // HEAD_DIM
    return jnp.where(head[:, None] == head[None, :], 1.0 / HEAD_DIM, 0.0).astype(jnp.bfloat16)


def kernel(x_prompt, x_sample, cache_k, cache_v, state_ret, c_prompt, c_sample, w_ada, b_ada,
           g_mix, w_in, q_gain, k_gain, sinks, w_out, g_ffn, w_router, b_router, w_gate_up,
           b_gate_up, w_down, b_down):
    depth = w_ada.shape[0]
    batch, seq, d = x_prompt.shape
    dec_batch, dec_seq, _ = x_sample.shape
    cache_w = cache_k.shape[2]
    assert batch == 1 and depth == 1
    assert seq % PROMPT_BLOCK == 0 and PROMPT_BLOCK % WINDOW == 0 and WINDOW == RET_CHUNK
    assert dec_batch % SAMPLE_SEQS == 0 and cache_w == WINDOW
    n_p, n_s = batch * seq, dec_batch * dec_seq
    n_tok = n_p + n_s
    assert n_p % ROUTE_BLOCK == 0 and n_s % ROUTE_BLOCK == 0
    assert ROUTE_BLOCK % (SAMPLE_SEQS * dec_seq) == 0 and d == SUBLANES * LANES

    l = 0
    f32 = jnp.float32
    rope_freq = 1.0 / (ROPE_THETA ** (jnp.arange(0, HEAD_DIM, 2, dtype=f32) / HEAD_DIM))
    ret_freq = 1.0 / (ROPE_THETA ** jnp.linspace(0.0, 1.0, RET_DK // 2, dtype=f32))
    pos_p = jnp.arange(seq, dtype=jnp.int32)
    pos_s = PAST_LEN + jnp.arange(dec_seq, dtype=jnp.int32)
    tabs_p = _rotary_tables(pos_p, rope_freq) + _rotary_tables(pos_p, ret_freq)
    tabs_s = tuple(jnp.tile(a, (SAMPLE_SEQS, 1))
                   for a in _rotary_tables(pos_s, rope_freq) + _rotary_tables(pos_s, ret_freq))
    ret_p = _retention_tables(RET_CHUNK)
    ret_s = _retention_tables(dec_seq)

    n_c = batch + dec_batch
    c_rows = -(-n_c // SUBLANES) * SUBLANES
    c_all = jnp.concatenate([c_prompt, c_sample, jnp.zeros((c_rows - n_c, d), f32)], axis=0)
    mod = _adaln(c_all, w_ada[l], b_ada[l])
    mod_p = mod[0].reshape(6, d)
    mod_s = jnp.repeat(mod[batch:n_c], dec_seq, axis=0)

    w_in_bf = _bf(w_in[l])
    w_out_bf = _bf(w_out[l])
    qg = jnp.tile(q_gain[l], ATT_HEADS).reshape(1, ATT_W)
    kg = jnp.tile(k_gain[l], ATT_KV_HEADS).reshape(1, KV_W)
    bdq, bdk = _block_diag_mean(ATT_W), _block_diag_mean(KV_W)
    gmix = g_mix[l].reshape(1, d)
    gffn = g_ffn[l].reshape(1, d)
    wr = jnp.pad(w_router[l], ((0, 0), (0, LANES - N_EXPERTS)))
    wr_hi = _bf(wr)
    wr_lo = _bf(wr - wr_hi.astype(f32))
    br = jnp.concatenate([b_router[l], jnp.full((LANES - N_EXPERTS,), NEG_BIG, f32)]).reshape(1, LANES)
    shared = (sinks[l], gmix, gffn, w_in_bf, qg, kg, bdq, bdk)

    (x1_p, h2_p, topi_p, topw_p, topit_p, cnt_p, kwin_p, vwin_p, rend_p) = _prompt_mix(
        x_prompt.reshape(n_p, d), mod_p, *shared, tabs_p, ret_p, w_out_bf, wr_hi, wr_lo, br)

    (x1_s, h2_s, topi_s, topw_s, topit_s, cnt_s, kwin_s, vwin_s, rend_s) = _sample_mix(
        x_sample.reshape(n_s, d), mod_s, *shared, tabs_s, ret_s, w_out_bf, wr_hi, wr_lo, br,
        cache_k[l].reshape(dec_batch, cache_w, KV_W), cache_v[l].reshape(dec_batch, cache_w, KV_W),
        state_ret[l].reshape(dec_batch, 2, LANES, RET_DV))

    tb, tm = ROUTE_BLOCK, EXPERT_TILE
    i32 = jnp.int32
    cnt_s = cnt_s.reshape(n_s // tb, -1, LANES).sum(axis=1)
    cnt = jnp.concatenate([cnt_p.reshape(-1, LANES), cnt_s], axis=0)
    rs = jnp.cumsum(cnt, axis=0) - cnt
    tot = jnp.sum(cnt, axis=0)[:N_EXPERTS]
    padded = -(-tot // tm) * tm
    ends = jnp.cumsum(padded)
    gstart = (ends - padded)[None, :] + rs[:, :N_EXPERTS]
    cb = jnp.cumsum(cnt, axis=1) - cnt
    n_rows = -(-(n_tok * TOP_K) // tm) * tm + N_EXPERTS * tm
    tile_start = jnp.arange(n_rows // tm, dtype=i32) * tm
    tile_expert = jnp.minimum(jnp.sum(ends[None, :] <= tile_start[:, None], axis=1),
                              N_EXPERTS - 1).astype(i32)
    n_used = (ends[-1:] // tm).astype(i32)
    flat = lambda a: a[:, :N_EXPERTS].reshape(-1).astype(i32)
    tables = (flat(cnt), gstart.reshape(-1).astype(i32), flat(cb))

    x_sorted = _dispatch(tables + (ends.astype(i32), tot.astype(i32)), topit_p, topit_s, h2_p, h2_s,
                         cb.astype(i32)[:, :, None], n_rows)
    y_sorted = _experts(tile_expert, n_used, x_sorted, w_gate_up[l], b_gate_up[l], w_down[l],
                        b_down[l])
    y_p, y_s = _combine(tables, topi_p, topi_s, topw_p, topw_s, x1_p, x1_s, mod_p[5:6],
                        mod_s[:, 5 * d:6 * d], cb.astype(i32)[:, None, :], y_sorted)

    kv5 = lambda a, n: a.reshape(1, n, cache_w, ATT_KV_HEADS, HEAD_DIM)
    st5 = lambda a, n: a.reshape(1, n, RET_HEADS, RET_DK, RET_DV)
    return (y_p.reshape(batch, seq, d), y_s.reshape(dec_batch, dec_seq, d),
            kv5(kwin_p, batch), kv5(vwin_p, batch), st5(rend_p, batch),
            kv5(kwin_s, dec_batch), kv5(vwin_s, dec_batch), st5(rend_s, dec_batch))
```

```python
import functools

import jax
import jax.numpy as jnp
import numpy as np
from jax import lax
from jax.experimental import pallas as pl
from jax.experimental.pallas import tpu as pltpu

HEAD_DIM = 64
ATT_HEADS = 8
ATT_KV_HEADS = 2
ATT_W = ATT_HEADS * HEAD_DIM
KV_W = ATT_KV_HEADS * HEAD_DIM
WINDOW = 128
PAST_LEN = 16384
ROPE_THETA = 10000.0
RET_HEADS = 4
RET_DK = 64
RET_DV = 128
RET_QK_W = RET_HEADS * RET_DK
RET_W = RET_HEADS * RET_DV
RET_CHUNK = 128
N_EXPERTS = 32
TOP_K = 4
SWIGLU_LIMIT = 7.0
SWIGLU_ALPHA = 1.702
EPS = 1e-6

_Q0, _K0, _V0 = 0, ATT_W, ATT_W + KV_W
_RQ0 = ATT_W + 2 * KV_W
_RK0 = _RQ0 + RET_QK_W
_RV0 = _RK0 + RET_QK_W
_RG0 = _RV0 + RET_W
IN_W = _RG0 + RET_W

LANES = 128
SUBLANES = 8
VMEM_LIMIT_BYTES = 56 * 1024 * 1024
NEG_BIG = -1e30

PROMPT_BLOCK = 512
SAMPLE_SEQS = 16
ROUTE_BLOCK = 512
EXPERT_TILE = 512
CHUNK_BITS = ROUTE_BLOCK.bit_length()
STAGE_ROWS = -(-(TOP_K * ROUTE_BLOCK + N_EXPERTS * (SUBLANES - 1)) // LANES) * LANES
STAGE_BITS = STAGE_ROWS.bit_length()

_NT = (((1,), (1,)), ((), ()))
_TN = (((0,), (0,)), ((), ()))


def _bf(x):
    return x.astype(jnp.bfloat16)


def _dot(a, b):
    return jnp.dot(_bf(a), _bf(b), preferred_element_type=jnp.float32)


def _dot_nt(a, b):
    return lax.dot_general(_bf(a), _bf(b), _NT, preferred_element_type=jnp.float32)


def _dot_tn(a, b):
    return lax.dot_general(_bf(a), _bf(b), _TN, preferred_element_type=jnp.float32)


def _lane_lo(shape):
    lane = lax.broadcasted_iota(jnp.int32, shape, len(shape) - 1)
    return (lane % LANES) < HEAD_DIM


def _swap_halves(x):
    lane = lax.broadcasted_iota(jnp.int32, x.shape, 1)
    first = (lane % HEAD_DIM) < (HEAD_DIM // 2)
    return jnp.where(first, pltpu.roll(x, LANES - HEAD_DIM // 2, axis=1),
                     pltpu.roll(x, HEAD_DIM // 2, axis=1))


def _rotate(x, cos, sin_signed):
    outs = []
    for j in range(x.shape[1] // LANES):
        xs = x[:, j * LANES:(j + 1) * LANES]
        outs.append(xs * cos + _swap_halves(xs) * sin_signed)
    return outs[0] if len(outs) == 1 else jnp.concatenate(outs, axis=1)


def _rms_rows(x):
    return x * lax.rsqrt(jnp.mean(x * x, axis=-1, keepdims=True) + EPS)


def _pre_norm(x, gain, shift, scale):
    return _rms_rows(x) * gain * (1.0 + scale) + shift


def _head_norm(x, gain_tiled, blockdiag):
    ms = jnp.dot(_bf(x * x), blockdiag, preferred_element_type=jnp.float32)
    return x * lax.rsqrt(ms + EPS) * gain_tiled


def _project(h, w_in_ref, qg, kg, bd_q, bd_k, cos_a, sin_a, cos_r, sin_r):
    z = jnp.dot(_bf(h), w_in_ref[...], preferred_element_type=jnp.float32)
    q = _rotate(_head_norm(z[:, _Q0:_K0], qg, bd_q), cos_a, sin_a) * (HEAD_DIM ** -0.5)
    k = _rotate(_head_norm(z[:, _K0:_V0], kg, bd_k), cos_a, sin_a)
    v = z[:, _V0:_RQ0]
    rq = _rotate(z[:, _RQ0:_RK0], cos_r, sin_r)
    rk = _rotate(z[:, _RK0:_RV0], cos_r, sin_r) * (RET_DK ** -0.5)
    rv = z[:, _RV0:_RG0]
    rg = z[:, _RG0:IN_W]
    return q, k, v, rq, rk, rv, rg


_NAT_HEADS = (0, 2, 5, 7)
_ROL_HEADS = (1, 3, 4, 6)


def _stack_heads(q, heads):
    lo = _lane_lo((q.shape[0], LANES))
    parts = []
    for j in heads:
        slab = q[:, (j // 2) * LANES:(j // 2 + 1) * LANES]
        parts.append(jnp.where(lo if j % 2 == 0 else ~lo, slab, 0.0))
    return jnp.concatenate(parts, axis=0)


def _sink_column(sinks_ref, heads, rows):
    r = lax.broadcasted_iota(jnp.int32, (len(heads) * rows, 1), 0)
    col = jnp.full((len(heads) * rows, 1), sinks_ref[heads[-1]], jnp.float32)
    for n in range(len(heads) - 2, -1, -1):
        col = jnp.where(r < (n + 1) * rows, sinks_ref[heads[n]], col)
    return col


def _softmax_pv(s, mask, sink, v):
    s = jnp.where(mask, s, NEG_BIG)
    m = jnp.maximum(jnp.max(s, axis=-1, keepdims=True), sink)
    p = jnp.exp(s - m)
    denom = jnp.sum(p, axis=-1, keepdims=True) + jnp.exp(sink - m)
    return _dot(p, v) / denom


def _attention(q, k_all, v_all, mask, sinks_ref):
    rows = q.shape[0]
    k_rol = pltpu.roll(k_all, HEAD_DIM, axis=1)
    v_rol = pltpu.roll(v_all, HEAD_DIM, axis=1)
    mask4 = jnp.concatenate([mask] * 4, axis=0)
    o_nat = _softmax_pv(_dot_nt(_stack_heads(q, _NAT_HEADS), k_all), mask4,
                        _sink_column(sinks_ref, _NAT_HEADS, rows), v_all)
    o_rol = _softmax_pv(_dot_nt(_stack_heads(q, _ROL_HEADS), k_rol), mask4,
                        _sink_column(sinks_ref, _ROL_HEADS, rows), v_rol)
    lo = _lane_lo((rows, LANES))
    blk = lambda o, n: o[n * rows:(n + 1) * rows]
    return jnp.concatenate([
        jnp.where(lo, blk(o_nat, 0), blk(o_rol, 0)),
        jnp.where(lo, blk(o_nat, 1), blk(o_rol, 1)),
        jnp.where(lo, blk(o_rol, 2), blk(o_nat, 2)),
        jnp.where(lo, blk(o_rol, 3), blk(o_nat, 3)),
    ], axis=1)


def _retention(rq, rk, rv, rg, state, decay_ref, qdec, kdec, gpow):
    rows = rq.shape[0]
    lo = _lane_lo((rows, LANES))
    rkd = rk * kdec
    outs, new_state = [], []
    for m in range(RET_HEADS // 2):
        sl = slice(m * LANES, (m + 1) * LANES)
        upd = state[m] * gpow[:, m:m + 1]
        for p in range(2):
            h = 2 * m + p
            half = lo if p == 0 else ~lo
            rqm = jnp.where(half, rq[:, sl], 0.0)
            rv_h = rv[:, h * RET_DV:(h + 1) * RET_DV]
            s = _dot_nt(rqm, rk[:, sl]) * decay_ref[h]
            ret = _dot(s, rv_h) + _dot(rqm, state[m]) * qdec[:, h * RET_DV:(h + 1) * RET_DV]
            upd = upd + _dot_tn(jnp.where(half, rkd[:, sl], 0.0), rv_h)
            rg_h = rg[:, h * RET_DV:(h + 1) * RET_DV]
            outs.append(rg_h * jax.nn.sigmoid(rg_h) * _rms_rows(ret))
        new_state.append(upd)
    return jnp.concatenate(outs, axis=1), new_state


def _route(h2, wr_hi_ref, wr_lo_ref, br):
    h_hi = _bf(h2)
    h_lo = _bf(h2 - h_hi.astype(jnp.float32))
    logits = (jnp.dot(h_hi, wr_hi_ref[...], preferred_element_type=jnp.float32)
              + jnp.dot(h_hi, wr_lo_ref[...], preferred_element_type=jnp.float32)
              + jnp.dot(h_lo, wr_hi_ref[...], preferred_element_type=jnp.float32)) + br
    lane = lax.broadcasted_iota(jnp.int32, logits.shape, 1)
    idx_out = jnp.zeros(logits.shape, jnp.int32)
    val_out = jnp.zeros(logits.shape, jnp.float32)
    onehot = jnp.zeros(logits.shape, jnp.float32)
    vals = []
    work = logits
    for k in range(TOP_K):
        mx = jnp.max(work, axis=-1, keepdims=True)
        ix = jnp.min(jnp.where(work == mx, lane, LANES), axis=-1, keepdims=True)
        work = jnp.where(lane == ix, NEG_BIG, work)
        onehot = jnp.where(lane == ix, 1.0, onehot)
        idx_out = jnp.where(lane == k, ix, idx_out)
        vals.append(mx)
    exps = [jnp.exp(v - vals[0]) for v in vals]
    total = exps[0] + exps[1] + exps[2] + exps[3]
    for k in range(TOP_K):
        val_out = jnp.where(lane == k, exps[k] / total, val_out)
    idx_t = idx_out.astype(jnp.float32).T[0:SUBLANES].astype(jnp.int32)
    counts = jnp.sum(onehot, axis=0, keepdims=True).astype(jnp.int32)
    return idx_out, val_out, idx_t, counts


def _adaln_kernel(c_ref, w_ref, b_ref, o_ref):
    c = c_ref[...]
    o_ref[...] = _dot(c * jax.nn.sigmoid(c), w_ref[...]) + b_ref[...]


def _adaln(c_all, w_ada, b_ada):
    rows, d = c_all.shape
    n = w_ada.shape[1]
    bn = 1536
    return pl.pallas_call(
        _adaln_kernel,
        grid=(n // bn,),
        in_specs=[pl.BlockSpec((rows, d), lambda j: (0, 0)),
                  pl.BlockSpec((d, bn), lambda j: (0, j)),
                  pl.BlockSpec((1, bn), lambda j: (0, j))],
        out_specs=pl.BlockSpec((rows, bn), lambda j: (0, j)),
        out_shape=jax.ShapeDtypeStruct((rows, n), jnp.float32),
        compiler_params=pltpu.CompilerParams(vmem_limit_bytes=VMEM_LIMIT_BYTES),
    )(c_all, w_ada, b_ada.reshape(1, n))


def _prompt_mix_kernel(sinks_ref, x_ref, mod_ref, gmix_ref, gffn_ref, w_in_ref, qg_ref, kg_ref,
                       bdq_ref, bdk_ref, cosa_ref, sina_ref, cosr_ref, sinr_ref,
                       decay_ref, qdec_ref, kdec_ref, gpow_ref, w_out_ref,
                       wrh_ref, wrl_ref, br_ref,
                       x1_ref, h2_ref, topi_ref, topw_ref, topit_ref, cnt_ref, kwin_ref, vwin_ref, rend_ref,
                       kprev, vprev, state, mix):
    step = pl.program_id(0)
    tb = x_ref.shape[0]

    @pl.when(step == 0)
    def _():
        kprev[...] = jnp.zeros_like(kprev)
        vprev[...] = jnp.zeros_like(vprev)
        state[...] = jnp.zeros_like(state)

    x = x_ref[...]
    h = _pre_norm(x, gmix_ref[...], mod_ref[0:1, :], mod_ref[1:2, :])
    q, k, v, rq, rk, rv, rg = _project(
        h, w_in_ref, qg_ref[...], kg_ref[...], bdq_ref[...], bdk_ref[...],
        cosa_ref[...], sina_ref[...], cosr_ref[...], sinr_ref[...])

    qi = lax.broadcasted_iota(jnp.int32, (WINDOW, 2 * WINDOW), 0)
    ci = lax.broadcasted_iota(jnp.int32, (WINDOW, 2 * WINDOW), 1)
    band = (ci > qi) & (ci <= qi + WINDOW)
    qdec, kdec, gpow = qdec_ref[...], kdec_ref[...], gpow_ref[...]

    for sb in range(tb // WINDOW):
        rs = slice(sb * WINDOW, (sb + 1) * WINDOW)
        k_sb, v_sb = k[rs], v[rs]
        k_all = jnp.concatenate([kprev[...], k_sb], axis=0)
        v_all = jnp.concatenate([vprev[...], v_sb], axis=0)
        has_prev = jnp.logical_or(step > 0, sb > 0) if sb == 0 else True
        mask = band & ((ci >= WINDOW) | has_prev)
        att = _attention(q[rs], k_all, v_all, mask, sinks_ref)
        kprev[...] = k_sb
        vprev[...] = v_sb
        yret, new_state = _retention(rq[rs], rk[rs], rv[rs], rg[rs], [state[0], state[1]],
                                     decay_ref, qdec, kdec, gpow)
        state[0] = new_state[0]
        state[1] = new_state[1]
        mix[rs, 0:ATT_W] = _bf(att)
        mix[rs, ATT_W:ATT_W + RET_W] = _bf(yret)

    x1 = x + mod_ref[2:3, :] * jnp.dot(mix[...], w_out_ref[...],
                                       preferred_element_type=jnp.float32)
    x1_ref[...] = x1
    h2 = _pre_norm(x1, gffn_ref[...], mod_ref[3:4, :], mod_ref[4:5, :])
    h2_ref[...] = h2
    topi, topw, topi_t, counts = _route(h2, wrh_ref, wrl_ref, br_ref[...])
    topi_ref[...] = topi
    topw_ref[...] = topw
    topit_ref[...] = topi_t
    cnt_ref[0] = counts

    @pl.when(step == pl.num_programs(0) - 1)
    def _():
        kwin_ref[...] = kprev[...]
        vwin_ref[...] = vprev[...]
        rend_ref[...] = state[...]


def _const_spec(shape):
    nd = len(shape)
    return pl.BlockSpec(shape, lambda *_: (0,) * nd)


def _prompt_mix(x, mod, sinks, gmix, gffn, w_in, qg, kg, bdq, bdk, tabs, ret_tabs, w_out,
                wrh, wrl, br):
    t, d = x.shape
    tb = PROMPT_BLOCK
    cosa, sina, cosr, sinr = tabs
    decay, qdec, kdec, gpow = ret_tabs
    row_spec = lambda w: pl.BlockSpec((tb, w), lambda i, *_: (i, 0))
    consts = [mod, gmix, gffn, w_in, qg, kg, bdq, bdk]
    consts2 = [decay, qdec, kdec, gpow, w_out, wrh, wrl, br]
    grid_spec = pltpu.PrefetchScalarGridSpec(
        num_scalar_prefetch=1,
        grid=(t // tb,),
        in_specs=([row_spec(d)] + [_const_spec(a.shape) for a in consts]
                  + [row_spec(LANES)] * 4 + [_const_spec(a.shape) for a in consts2]),
        out_specs=[row_spec(d), row_spec(d), row_spec(LANES), row_spec(LANES),
                   pl.BlockSpec((SUBLANES, tb), lambda i, *_: (0, i)),
                   pl.BlockSpec((1, 1, LANES), lambda i, *_: (i, 0, 0)),
                   _const_spec((WINDOW, KV_W)), _const_spec((WINDOW, KV_W)),
                   _const_spec((2, LANES, RET_DV))],
        scratch_shapes=[pltpu.VMEM((WINDOW, KV_W), jnp.float32),
                        pltpu.VMEM((WINDOW, KV_W), jnp.float32),
                        pltpu.VMEM((2, LANES, RET_DV), jnp.float32),
                        pltpu.VMEM((tb, ATT_W + RET_W), jnp.bfloat16)],
    )
    return pl.pallas_call(
        _prompt_mix_kernel,
        grid_spec=grid_spec,
        out_shape=[jax.ShapeDtypeStruct((t, d), jnp.float32),
                   jax.ShapeDtypeStruct((t, d), jnp.float32),
                   jax.ShapeDtypeStruct((t, LANES), jnp.int32),
                   jax.ShapeDtypeStruct((t, LANES), jnp.float32),
                   jax.ShapeDtypeStruct((SUBLANES, t), jnp.int32),
                   jax.ShapeDtypeStruct((t // tb, 1, LANES), jnp.int32),
                   jax.ShapeDtypeStruct((WINDOW, KV_W), jnp.float32),
                   jax.ShapeDtypeStruct((WINDOW, KV_W), jnp.float32),
                   jax.ShapeDtypeStruct((2, LANES, RET_DV), jnp.float32)],
        compiler_params=pltpu.CompilerParams(
            dimension_semantics=("arbitrary",), vmem_limit_bytes=VMEM_LIMIT_BYTES),
    )(sinks, x, *consts, cosa, sina, cosr, sinr, *consts2)


def _sample_mix_kernel(sinks_ref, x_ref, mod_ref, gmix_ref, gffn_ref, w_in_ref, qg_ref, kg_ref,
                       bdq_ref, bdk_ref, cosa_ref, sina_ref, cosr_ref, sinr_ref,
                       decay_ref, qdec_ref, kdec_ref, gpow_ref, w_out_ref,
                       wrh_ref, wrl_ref, br_ref, ck_ref, cv_ref, st_ref,
                       x1_ref, h2_ref, topi_ref, topw_ref, topit_ref, cnt_ref, kwin_ref, vwin_ref, rend_ref,
                       mix):
    nseq, cache_w = ck_ref.shape[0], ck_ref.shape[1]
    ds = x_ref.shape[0] // nseq
    d = x_ref.shape[1]
    x = x_ref[...]
    mod = lambda i: mod_ref[:, i * d:(i + 1) * d]
    h = _pre_norm(x, gmix_ref[...], mod(0), mod(1))
    q, k, v, rq, rk, rv, rg = _project(
        h, w_in_ref, qg_ref[...], kg_ref[...], bdq_ref[...], bdk_ref[...],
        cosa_ref[...], sina_ref[...], cosr_ref[...], sinr_ref[...])

    qi = lax.broadcasted_iota(jnp.int32, (ds, cache_w + ds), 0)
    ci = lax.broadcasted_iota(jnp.int32, (ds, cache_w + ds), 1)
    delta = qi + cache_w - ci
    mask = (delta >= 0) & (delta < WINDOW)
    qdec, kdec, gpow = qdec_ref[...], kdec_ref[...], gpow_ref[...]

    for s in range(nseq):
        rs = slice(s * ds, (s + 1) * ds)
        k_all = jnp.concatenate([ck_ref[s], k[rs]], axis=0)
        v_all = jnp.concatenate([cv_ref[s], v[rs]], axis=0)
        att = _attention(q[rs], k_all, v_all, mask, sinks_ref)
        kwin_ref[s] = k_all[ds:]
        vwin_ref[s] = v_all[ds:]
        yret, new_state = _retention(rq[rs], rk[rs], rv[rs], rg[rs],
                                     [st_ref[s, 0], st_ref[s, 1]],
                                     decay_ref, qdec, kdec, gpow)
        rend_ref[s, 0] = new_state[0]
        rend_ref[s, 1] = new_state[1]
        mix[rs, 0:ATT_W] = _bf(att)
        mix[rs, ATT_W:ATT_W + RET_W] = _bf(yret)

    x1 = x + mod(2) * jnp.dot(mix[...], w_out_ref[...], preferred_element_type=jnp.float32)
    x1_ref[...] = x1
    h2 = _pre_norm(x1, gffn_ref[...], mod(3), mod(4))
    h2_ref[...] = h2
    topi, topw, topi_t, counts = _route(h2, wrh_ref, wrl_ref, br_ref[...])
    topi_ref[...] = topi
    topw_ref[...] = topw
    topit_ref[...] = topi_t
    cnt_ref[0] = counts


def _sample_mix(x, mod_tok, sinks, gmix, gffn, w_in, qg, kg, bdq, bdk, tabs, ret_tabs, w_out,
                wrh, wrl, br, cache_k, cache_v, state):
    t, d = x.shape
    nb, cache_w = cache_k.shape[0], cache_k.shape[1]
    ds = t // nb
    g = SAMPLE_SEQS
    tb = g * ds
    cosa, sina, cosr, sinr = tabs
    decay, qdec, kdec, gpow = ret_tabs
    row_spec = lambda w: pl.BlockSpec((tb, w), lambda i, *_: (i, 0))
    seq3 = lambda a: pl.BlockSpec((g,) + a.shape[1:], lambda i, *_: (i,) + (0,) * (a.ndim - 1))
    consts = [gmix, gffn, w_in, qg, kg, bdq, bdk, cosa, sina, cosr, sinr,
              decay, qdec, kdec, gpow, w_out, wrh, wrl, br]
    grid_spec = pltpu.PrefetchScalarGridSpec(
        num_scalar_prefetch=1,
        grid=(nb // g,),
        in_specs=([row_spec(d), row_spec(mod_tok.shape[1])]
                  + [_const_spec(a.shape) for a in consts]
                  + [seq3(cache_k), seq3(cache_v), seq3(state)]),
        out_specs=[row_spec(d), row_spec(d), row_spec(LANES), row_spec(LANES),
                   pl.BlockSpec((SUBLANES, tb), lambda i, *_: (0, i)),
                   pl.BlockSpec((1, 1, LANES), lambda i, *_: (i, 0, 0)),
                   seq3(cache_k), seq3(cache_v), seq3(state)],
        scratch_shapes=[pltpu.VMEM((tb, ATT_W + RET_W), jnp.bfloat16)],
    )
    return pl.pallas_call(
        _sample_mix_kernel,
        grid_spec=grid_spec,
        out_shape=[jax.ShapeDtypeStruct((t, d), jnp.float32),
                   jax.ShapeDtypeStruct((t, d), jnp.float32),
                   jax.ShapeDtypeStruct((t, LANES), jnp.int32),
                   jax.ShapeDtypeStruct((t, LANES), jnp.float32),
                   jax.ShapeDtypeStruct((SUBLANES, t), jnp.int32),
                   jax.ShapeDtypeStruct((t // tb, 1, LANES), jnp.int32),
                   jax.ShapeDtypeStruct(cache_k.shape, jnp.float32),
                   jax.ShapeDtypeStruct(cache_v.shape, jnp.float32),
                   jax.ShapeDtypeStruct(state.shape, jnp.float32)],
        compiler_params=pltpu.CompilerParams(
            dimension_semantics=("arbitrary",), vmem_limit_bytes=VMEM_LIMIT_BYTES),
    )(sinks, x, mod_tok, *consts, cache_k, cache_v, state)


def _chunk_copies(cnt8, src_ref, src_start, dst_ref, dst_start, sem):
    for bit in range(CHUNK_BITS - 1, SUBLANES.bit_length() - 2, -1):
        size = 1 << bit
        done = (cnt8 >> (bit + 1)) << (bit + 1)

        @pl.when(((cnt8 >> bit) & 1) == 1)
        def _():
            src = pl.multiple_of(src_start + done, SUBLANES)
            dst = pl.multiple_of(dst_start + done, SUBLANES)
            pltpu.make_async_copy(src_ref.at[pl.ds(src, size)],
                                  dst_ref.at[pl.ds(dst, size)], sem).start()


def _wait_rows(ref, rows8, sem):
    for bit in range(STAGE_BITS - 1, SUBLANES.bit_length() - 2, -1):
        size = 1 << bit

        @pl.when(((rows8 >> bit) & 1) == 1)
        def _():
            pltpu.make_async_copy(ref.at[pl.ds(0, size)], ref.at[pl.ds(0, size)], sem).wait()


def _dispatch_kernel(cnt_ref, gstart_ref, cb_ref, blkrows_ref, ends_ref, tot_ref,
                     topit_p_ref, topit_s_ref, h2_p_ref, h2_s_ref, cbcol_ref, xs_ref,
                     stage, zeros, sems, zsem, *, nblk_p):
    step = pl.program_id(0)
    nblk = pl.num_programs(0)
    tb = h2_p_ref.shape[0]
    slot = step % 2

    @pl.when(step == 0)
    def _():
        zeros[...] = jnp.zeros_like(zeros)
        tile = zeros.shape[0]

        def fill(e):
            start = pl.multiple_of(ends_ref[e] - tile, tile)
            return pltpu.make_async_copy(zeros, xs_ref.at[pl.ds(start, tile)], zsem)

        for e in range(N_EXPERTS):
            @pl.when(tot_ref[e] > 0)
            def _():
                fill(e).start()
        for e in range(N_EXPERTS):
            @pl.when(tot_ref[e] > 0)
            def _():
                fill(e).wait()

    def tail_fill(t):
        tile = zeros.shape[0]
        return pltpu.make_async_copy(zeros, xs_ref.at[pl.ds(pl.multiple_of(t * tile, tile), tile)],
                                     zsem)

    first_unused = ends_ref[N_EXPERTS - 1] // zeros.shape[0]
    n_tiles = xs_ref.shape[0] // zeros.shape[0]

    @pl.when(step == 0)
    def _():
        lax.fori_loop(first_unused, n_tiles, lambda t, c: (tail_fill(t).start(), c)[1], 0)

    @pl.when(step == nblk - 1)
    def _():
        lax.fori_loop(first_unused, n_tiles, lambda t, c: (tail_fill(t).wait(), c)[1], 0)

    is_p = step < nblk_p
    topit = jnp.where(is_p, topit_p_ref[...], topit_s_ref[...])
    h2 = jnp.where(is_p, h2_p_ref[...], h2_s_ref[...])
    cbcol = cbcol_ref[0]

    sub = lax.broadcasted_iota(jnp.int32, (LANES, tb), 0)
    picks = [sub == topit[k:k + 1, :] for k in range(TOP_K)]
    onehot = jnp.zeros((LANES, tb), jnp.float32)
    for pk in picks:
        onehot = jnp.where(pk, 1.0, onehot)
    r = lax.broadcasted_iota(jnp.int32, (tb, tb), 0)
    c = lax.broadcasted_iota(jnp.int32, (tb, tb), 1)
    earlier = jnp.where(r < c, 1.0, 0.0)
    before = _dot(onehot, earlier)
    base = before + cbcol.astype(jnp.float32)
    row = lax.broadcasted_iota(jnp.int32, (stage.shape[1], tb), 0)
    sel = jnp.zeros((stage.shape[1], tb), jnp.float32)
    for pk in picks:
        col = jnp.sum(jnp.where(pk, base, 0.0), axis=0, keepdims=True).astype(jnp.int32)
        sel = jnp.where(row == col, 1.0, sel)

    stage[slot] = _dot(sel, h2)

    @pl.when(step > 0)
    def _():
        _wait_rows(xs_ref, blkrows_ref[jnp.maximum(step - 1, 0)], sems.at[1 - slot])

    def issue(e, carry):
        i = step * N_EXPERTS + e
        _chunk_copies(cnt_ref[i], stage.at[slot], cb_ref[i], xs_ref, gstart_ref[i], sems.at[slot])
        return carry

    lax.fori_loop(0, N_EXPERTS, issue, 0)

    @pl.when(step == nblk - 1)
    def _():
        _wait_rows(xs_ref, blkrows_ref[step], sems.at[slot])


def _dispatch(tables, topit_p, topit_s, h2_p, h2_s, cbcol, n_rows):
    n_p, d = h2_p.shape
    n_s = h2_s.shape[0]
    tb = ROUTE_BLOCK
    nblk_p, nblk_s = n_p // tb, n_s // tb
    pidx = lambda i, *_: jnp.minimum(i, nblk_p - 1)
    sidx = lambda i, *_: jnp.maximum(i - nblk_p, 0)
    grid_spec = pltpu.PrefetchScalarGridSpec(
        num_scalar_prefetch=6,
        grid=(nblk_p + nblk_s,),
        in_specs=[pl.BlockSpec((SUBLANES, tb), lambda i, *_: (0, pidx(i))),
                  pl.BlockSpec((SUBLANES, tb), lambda i, *_: (0, sidx(i))),
                  pl.BlockSpec((tb, d), lambda i, *_: (pidx(i), 0)),
                  pl.BlockSpec((tb, d), lambda i, *_: (sidx(i), 0)),
                  pl.BlockSpec((1, LANES, 1), lambda i, *_: (i, 0, 0))],
        out_specs=pl.BlockSpec(memory_space=pl.ANY),
        scratch_shapes=[pltpu.VMEM((2, STAGE_ROWS, d), jnp.float32),
                        pltpu.VMEM((EXPERT_TILE, d), jnp.float32),
                        pltpu.SemaphoreType.DMA((2,)),
                        pltpu.SemaphoreType.DMA],
    )
    return pl.pallas_call(
        functools.partial(_dispatch_kernel, nblk_p=nblk_p),
        grid_spec=grid_spec,
        out_shape=jax.ShapeDtypeStruct((n_rows, d), jnp.float32),
        compiler_params=pltpu.CompilerParams(
            dimension_semantics=("arbitrary",), vmem_limit_bytes=VMEM_LIMIT_BYTES),
    )(*tables, topit_p, topit_s, h2_p, h2_s, cbcol)


def _experts_kernel(tile_expert_ref, n_used_ref, x_ref, wgu_ref, bgu_ref, wd_ref, bd_ref, y_ref,
                    wgu_bf, wd_bf):
    step = pl.program_id(0)

    @pl.when(step < n_used_ref[0])
    def _():
        e = tile_expert_ref[step]
        prev = tile_expert_ref[jnp.maximum(step - 1, 0)]

        @pl.when(jnp.logical_or(step == 0, e != prev))
        def _():
            wgu_bf[...] = _bf(wgu_ref[0])
            wd_bf[...] = _bf(wd_ref[0])

        d_ff = wd_bf.shape[0]
        gu = jnp.dot(_bf(x_ref[...]), wgu_bf[...], preferred_element_type=jnp.float32) + bgu_ref[0]
        glu = jnp.minimum(gu[:, :d_ff], SWIGLU_LIMIT)
        lin = jnp.clip(gu[:, d_ff:], -SWIGLU_LIMIT, SWIGLU_LIMIT)
        act = glu * jax.nn.sigmoid(SWIGLU_ALPHA * glu) * (lin + 1.0)
        y_ref[...] = jnp.dot(_bf(act), wd_bf[...], preferred_element_type=jnp.float32) + bd_ref[0]

    @pl.when(step >= n_used_ref[0])
    def _():
        y_ref[...] = jnp.zeros_like(y_ref)


def _experts(tile_expert, n_used, x_sorted, w_gate_up, b_gate_up, w_down, b_down):
    p = x_sorted.shape[0]
    n_e, d, two_ff = w_gate_up.shape
    d_ff = two_ff // 2
    tm = EXPERT_TILE
    tile = lambda i, te, nu: jnp.minimum(i, nu[0] - 1)
    grid_spec = pltpu.PrefetchScalarGridSpec(
        num_scalar_prefetch=2,
        grid=(p // tm,),
        in_specs=[pl.BlockSpec((tm, d), lambda i, te, nu: (tile(i, te, nu), 0)),
                  pl.BlockSpec((1, d, two_ff), lambda i, te, nu: (te[i], 0, 0)),
                  pl.BlockSpec((1, 1, two_ff), lambda i, te, nu: (te[i], 0, 0)),
                  pl.BlockSpec((1, d_ff, d), lambda i, te, nu: (te[i], 0, 0)),
                  pl.BlockSpec((1, 1, d), lambda i, te, nu: (te[i], 0, 0))],
        out_specs=pl.BlockSpec((tm, d), lambda i, te, nu: (i, 0)),
        scratch_shapes=[pltpu.VMEM((d, two_ff), jnp.bfloat16),
                        pltpu.VMEM((d_ff, d), jnp.bfloat16)],
    )
    return pl.pallas_call(
        _experts_kernel,
        grid_spec=grid_spec,
        out_shape=jax.ShapeDtypeStruct(x_sorted.shape, jnp.float32),
        compiler_params=pltpu.CompilerParams(
            dimension_semantics=("arbitrary",), vmem_limit_bytes=VMEM_LIMIT_BYTES),
    )(tile_expert, n_used, x_sorted, w_gate_up, b_gate_up.reshape(n_e, 1, two_ff), w_down,
      b_down.reshape(n_e, 1, d))


def _combine_kernel(cnt_ref, gstart_ref, cb_ref, blkrows_ref,
                    topi_p_ref, topi_s_ref, topw_p_ref, topw_s_ref, x1_p_ref, x1_s_ref,
                    gate_p_ref, gate_s_ref, cbrow_ref, y_ref, out_p_ref, out_s_ref,
                    ybuf, sems, *, nblk_p):
    step = pl.program_id(0)
    nblk = pl.num_programs(0)
    tb = x1_p_ref.shape[0]
    slot = step % 2

    def fetch(blk, into):
        def issue(e, carry):
            i = blk * N_EXPERTS + e
            _chunk_copies(cnt_ref[i], y_ref, gstart_ref[i], ybuf.at[into], cb_ref[i], sems.at[into])
            return carry
        lax.fori_loop(0, N_EXPERTS, issue, 0)

    @pl.when(step == 0)
    def _():
        ybuf[...] = jnp.zeros_like(ybuf)
        fetch(0, 0)

    @pl.when(step + 1 < nblk)
    def _():
        fetch(step + 1, 1 - slot)

    is_p = step < nblk_p
    topi = jnp.where(is_p, topi_p_ref[...], topi_s_ref[...])
    topw = jnp.where(is_p, topw_p_ref[...], topw_s_ref[...])
    cbrow = cbrow_ref[0]

    lane = lax.broadcasted_iota(jnp.int32, (tb, LANES), 1)
    picks = []
    onehot = jnp.zeros((tb, LANES), jnp.float32)
    for k in range(TOP_K):
        ix = jnp.sum(jnp.where(lane == k, topi, 0), axis=-1, keepdims=True)
        picks.append(lane == ix)
        onehot = jnp.where(picks[k], 1.0, onehot)
    r = lax.broadcasted_iota(jnp.int32, (tb, tb), 0)
    c = lax.broadcasted_iota(jnp.int32, (tb, tb), 1)
    earlier = jnp.where(c < r, 1.0, 0.0)
    base = _dot(earlier, onehot) + cbrow.astype(jnp.float32)
    colid = lax.broadcasted_iota(jnp.int32, (tb, ybuf.shape[1]), 1)
    weights = jnp.zeros((tb, ybuf.shape[1]), jnp.float32)
    for k in range(TOP_K):
        col = jnp.sum(jnp.where(picks[k], base, 0.0), axis=-1, keepdims=True).astype(jnp.int32)
        wk = jnp.sum(jnp.where(lane == k, topw, 0.0), axis=-1, keepdims=True)
        weights = jnp.where(colid == col, wk, weights)

    _wait_rows(y_ref, blkrows_ref[step], sems.at[slot])
    moe = _dot(weights, ybuf[slot])

    @pl.when(is_p)
    def _():
        out_p_ref[...] = x1_p_ref[...] + gate_p_ref[...] * moe

    @pl.when(jnp.logical_not(is_p))
    def _():
        out_s_ref[...] = x1_s_ref[...] + gate_s_ref[...] * moe


def _combine(tables, topi_p, topi_s, topw_p, topw_s, x1_p, x1_s, gate_p, gate_s, cbrow, y_sorted):
    n_p, d = x1_p.shape
    n_s = x1_s.shape[0]
    tb = ROUTE_BLOCK
    nblk_p, nblk_s = n_p // tb, n_s // tb
    pidx = lambda i, *_: jnp.minimum(i, nblk_p - 1)
    sidx = lambda i, *_: jnp.maximum(i - nblk_p, 0)
    prow = lambda w: pl.BlockSpec((tb, w), lambda i, *_: (pidx(i), 0))
    srow = lambda w: pl.BlockSpec((tb, w), lambda i, *_: (sidx(i), 0))
    grid_spec = pltpu.PrefetchScalarGridSpec(
        num_scalar_prefetch=4,
        grid=(nblk_p + nblk_s,),
        in_specs=[prow(LANES), srow(LANES), prow(LANES), srow(LANES), prow(d), srow(d),
                  pl.BlockSpec((1, d), lambda i, *_: (0, 0)), srow(d),
                  pl.BlockSpec((1, 1, LANES), lambda i, *_: (i, 0, 0)),
                  pl.BlockSpec(memory_space=pl.ANY)],
        out_specs=[prow(d), srow(d)],
        scratch_shapes=[pltpu.VMEM((2, STAGE_ROWS, d), jnp.float32),
                        pltpu.SemaphoreType.DMA((2,))],
    )
    return pl.pallas_call(
        functools.partial(_combine_kernel, nblk_p=nblk_p),
        grid_spec=grid_spec,
        out_shape=[jax.ShapeDtypeStruct((n_p, d), jnp.float32),
                   jax.ShapeDtypeStruct((n_s, d), jnp.float32)],
        compiler_params=pltpu.CompilerParams(
            dimension_semantics=("arbitrary",), vmem_limit_bytes=VMEM_LIMIT_BYTES),
    )(*tables, topi_p, topi_s, topw_p, topw_s, x1_p, x1_s, gate_p, gate_s, cbrow, y_sorted)


def _rotary_tables(pos, inv_freq):
    ang = pos.astype(jnp.float32)[:, None] * inv_freq[None, :]
    cos, sin = jnp.cos(ang), jnp.sin(ang)
    cos_t = jnp.tile(cos, (1, LANES // cos.shape[1]))
    sin_t = jnp.tile(jnp.concatenate([-sin, sin], axis=1), (1, LANES // (2 * sin.shape[1])))
    return cos_t, sin_t


def _retention_tables(chunk):
    log_gamma = jnp.log1p(-jnp.exp2(-5.0 - jnp.arange(RET_HEADS, dtype=jnp.float32)))
    idx = jnp.arange(chunk, dtype=jnp.float32)
    diff = idx[:, None] - idx[None, :]
    decay = jnp.where(diff[None] >= 0,
                      jnp.exp(jnp.maximum(diff, 0.0)[None] * log_gamma[:, None, None]), 0.0)
    q_decay = jnp.exp((idx + 1.0)[:, None] * log_gamma[None, :])
    k_decay = jnp.exp((chunk - 1.0 - idx)[:, None] * log_gamma[None, :])
    qdec = jnp.repeat(q_decay, RET_DV, axis=1)
    kdec = jnp.repeat(k_decay, RET_DK, axis=1)
    g_chunk = jnp.exp(chunk * log_gamma)
    gpow = jnp.repeat(g_chunk.reshape(RET_HEADS // 2, 2), RET_DK, axis=1).T
    return decay, qdec, kdec, gpow


def _block_diag_mean(width):
    head = jnp.arange(width) // HEAD_DIM
    return jnp.where(head[:, None] == head[None, :], 1.0 / HEAD_DIM, 0.0).astype(jnp.bfloat16)


def kernel(x_prompt, x_sample, cache_k, cache_v, state_ret, c_prompt, c_sample, w_ada, b_ada,
           g_mix, w_in, q_gain, k_gain, sinks, w_out, g_ffn, w_router, b_router, w_gate_up,
           b_gate_up, w_down, b_down):
    depth = w_ada.shape[0]
    batch, seq, d = x_prompt.shape
    dec_batch, dec_seq, _ = x_sample.shape
    cache_w = cache_k.shape[2]
    assert batch == 1 and depth == 1
    assert seq % PROMPT_BLOCK == 0 and PROMPT_BLOCK % WINDOW == 0 and WINDOW == RET_CHUNK
    assert dec_batch % SAMPLE_SEQS == 0 and cache_w == WINDOW
    n_p, n_s = batch * seq, dec_batch * dec_seq
    n_tok = n_p + n_s
    assert n_p % ROUTE_BLOCK == 0 and n_s % ROUTE_BLOCK == 0
    assert ROUTE_BLOCK % (SAMPLE_SEQS * dec_seq) == 0 and d == SUBLANES * LANES

    l = 0
    f32 = jnp.float32
    rope_freq = 1.0 / (ROPE_THETA ** (jnp.arange(0, HEAD_DIM, 2, dtype=f32) / HEAD_DIM))
    ret_freq = 1.0 / (ROPE_THETA ** jnp.linspace(0.0, 1.0, RET_DK // 2, dtype=f32))
    pos_p = jnp.arange(seq, dtype=jnp.int32)
    pos_s = PAST_LEN + jnp.arange(dec_seq, dtype=jnp.int32)
    tabs_p = _rotary_tables(pos_p, rope_freq) + _rotary_tables(pos_p, ret_freq)
    tabs_s = tuple(jnp.tile(a, (SAMPLE_SEQS, 1))
                   for a in _rotary_tables(pos_s, rope_freq) + _rotary_tables(pos_s, ret_freq))
    ret_p = _retention_tables(RET_CHUNK)
    ret_s = _retention_tables(dec_seq)

    n_c = batch + dec_batch
    c_rows = -(-n_c // SUBLANES) * SUBLANES
    c_all = jnp.concatenate([c_prompt, c_sample, jnp.zeros((c_rows - n_c, d), f32)], axis=0)
    mod = _adaln(c_all, w_ada[l], b_ada[l])
    mod_p = mod[0].reshape(6, d)
    mod_s = jnp.repeat(mod[batch:n_c], dec_seq, axis=0)

    w_in_bf = _bf(w_in[l])
    w_out_bf = _bf(w_out[l])
    qg = jnp.tile(q_gain[l], ATT_HEADS).reshape(1, ATT_W)
    kg = jnp.tile(k_gain[l], ATT_KV_HEADS).reshape(1, KV_W)
    bdq, bdk = _block_diag_mean(ATT_W), _block_diag_mean(KV_W)
    gmix = g_mix[l].reshape(1, d)
    gffn = g_ffn[l].reshape(1, d)
    wr = jnp.pad(w_router[l], ((0, 0), (0, LANES - N_EXPERTS)))
    wr_hi = _bf(wr)
    wr_lo = _bf(wr - wr_hi.astype(f32))
    br = jnp.concatenate([b_router[l], jnp.full((LANES - N_EXPERTS,), NEG_BIG, f32)]).reshape(1, LANES)
    shared = (sinks[l], gmix, gffn, w_in_bf, qg, kg, bdq, bdk)

    (x1_p, h2_p, topi_p, topw_p, topit_p, cnt_p, kwin_p, vwin_p, rend_p) = _prompt_mix(
        x_prompt.reshape(n_p, d), mod_p, *shared, tabs_p, ret_p, w_out_bf, wr_hi, wr_lo, br)

    (x1_s, h2_s, topi_s, topw_s, topit_s, cnt_s, kwin_s, vwin_s, rend_s) = _sample_mix(
        x_sample.reshape(n_s, d), mod_s, *shared, tabs_s, ret_s, w_out_bf, wr_hi, wr_lo, br,
        cache_k[l].reshape(dec_batch, cache_w, KV_W), cache_v[l].reshape(dec_batch, cache_w, KV_W),
        state_ret[l].reshape(dec_batch, 2, LANES, RET_DV))

    tb, tm = ROUTE_BLOCK, EXPERT_TILE
    i32 = jnp.int32
    cnt_s = cnt_s.reshape(n_s // tb, -1, LANES).sum(axis=1)
    cnt = jnp.concatenate([cnt_p.reshape(-1, LANES), cnt_s], axis=0)
    cnt8 = -(-cnt // SUBLANES) * SUBLANES
    rs = jnp.cumsum(cnt8, axis=0) - cnt8
    tot = jnp.sum(cnt8, axis=0)[:N_EXPERTS]
    padded = -(-tot // tm) * tm
    ends = jnp.cumsum(padded)
    gstart = (ends - padded)[None, :] + rs[:, :N_EXPERTS]
    cb = jnp.cumsum(cnt8, axis=1) - cnt8
    blk_rows = jnp.sum(cnt8, axis=1).astype(i32)
    max_rows = n_tok * TOP_K + cnt.shape[0] * N_EXPERTS * (SUBLANES - 1)
    n_rows = -(-max_rows // tm) * tm + N_EXPERTS * tm
    tile_start = jnp.arange(n_rows // tm, dtype=i32) * tm
    tile_expert = jnp.minimum(jnp.sum(ends[None, :] <= tile_start[:, None], axis=1),
                              N_EXPERTS - 1).astype(i32)
    n_used = (ends[-1:] // tm).astype(i32)
    flat = lambda a: a[:, :N_EXPERTS].reshape(-1).astype(i32)
    tables = (flat(cnt8), gstart.reshape(-1).astype(i32), flat(cb), blk_rows)

    x_sorted = _dispatch(tables + (ends.astype(i32), tot.astype(i32)), topit_p, topit_s, h2_p, h2_s,
                         cb.astype(i32)[:, :, None], n_rows)
    y_sorted = _experts(tile_expert, n_used, x_sorted, w_gate_up[l], b_gate_up[l], w_down[l],
                        b_down[l])
    y_p, y_s = _combine(tables, topi_p, topi_s, topw_p, topw_s, x1_p, x1_s, mod_p[5:6],
                        mod_s[:, 5 * d:6 * d], cb.astype(i32)[:, None, :], y_sorted)

    kv5 = lambda a, n: a.reshape(1, n, cache_w, ATT_KV_HEADS, HEAD_DIM)
    st5 = lambda a, n: a.reshape(1, n, RET_HEADS, RET_DK, RET_DV)
    return (y_p.reshape(batch, seq, d), y_s.reshape(dec_batch, dec_seq, d),
            kv5(kwin_p, batch), kv5(vwin_p, batch), st5(rend_p, batch),
            kv5(kwin_s, dec_batch), kv5(vwin_s, dec_batch), st5(rend_s, dec_batch))
```

```python
import functools

import jax
import jax.numpy as jnp
import numpy as np
from jax import lax
from jax.experimental import pallas as pl
from jax.experimental.pallas import tpu as pltpu

HEAD_DIM = 64
ATT_HEADS = 8
ATT_KV_HEADS = 2
ATT_W = ATT_HEADS * HEAD_DIM
KV_W = ATT_KV_HEADS * HEAD_DIM
WINDOW = 128
PAST_LEN = 16384
ROPE_THETA = 10000.0
RET_HEADS = 4
RET_DK = 64
RET_DV = 128
RET_QK_W = RET_HEADS * RET_DK
RET_W = RET_HEADS * RET_DV
RET_CHUNK = 128
N_EXPERTS = 32
TOP_K = 4
SWIGLU_LIMIT = 7.0
SWIGLU_ALPHA = 1.702
EPS = 1e-6

_Q0, _K0, _V0 = 0, ATT_W, ATT_W + KV_W
_RQ0 = ATT_W + 2 * KV_W
_RK0 = _RQ0 + RET_QK_W
_RV0 = _RK0 + RET_QK_W
_RG0 = _RV0 + RET_W
IN_W = _RG0 + RET_W

LANES = 128
SUBLANES = 8
VMEM_LIMIT_BYTES = 56 * 1024 * 1024
NEG_BIG = -1e30

PROMPT_BLOCK = 512
SAMPLE_SEQS = 16
ROUTE_BLOCK = 512
EXPERT_TILE = 512
CHUNK_BITS = ROUTE_BLOCK.bit_length()
STAGE_ROWS = -(-(TOP_K * ROUTE_BLOCK + N_EXPERTS * (SUBLANES - 1)) // LANES) * LANES
STAGE_BITS = STAGE_ROWS.bit_length()

_NT = (((1,), (1,)), ((), ()))
_TN = (((0,), (0,)), ((), ()))


def _bf(x):
    return x.astype(jnp.bfloat16)


def _dot(a, b):
    return jnp.dot(_bf(a), _bf(b), preferred_element_type=jnp.float32)


def _dot_nt(a, b):
    return lax.dot_general(_bf(a), _bf(b), _NT, preferred_element_type=jnp.float32)


def _dot_tn(a, b):
    return lax.dot_general(_bf(a), _bf(b), _TN, preferred_element_type=jnp.float32)


def _lane_lo(shape):
    lane = lax.broadcasted_iota(jnp.int32, shape, len(shape) - 1)
    return (lane % LANES) < HEAD_DIM


def _swap_halves(x):
    lane = lax.broadcasted_iota(jnp.int32, x.shape, 1)
    first = (lane % HEAD_DIM) < (HEAD_DIM // 2)
    return jnp.where(first, pltpu.roll(x, LANES - HEAD_DIM // 2, axis=1),
                     pltpu.roll(x, HEAD_DIM // 2, axis=1))


def _rotate(x, cos, sin_signed):
    outs = []
    for j in range(x.shape[1] // LANES):
        xs = x[:, j * LANES:(j + 1) * LANES]
        outs.append(xs * cos + _swap_halves(xs) * sin_signed)
    return outs[0] if len(outs) == 1 else jnp.concatenate(outs, axis=1)


def _rms_rows(x):
    return x * lax.rsqrt(jnp.mean(x * x, axis=-1, keepdims=True) + EPS)


def _pre_norm(x, gain, shift, scale):
    return _rms_rows(x) * gain * (1.0 + scale) + shift


def _head_norm(x, gain_tiled, blockdiag):
    ms = jnp.dot(_bf(x * x), blockdiag, preferred_element_type=jnp.float32)
    return x * lax.rsqrt(ms + EPS) * gain_tiled


def _project(h, w_in_ref, qg, kg, bd_q, bd_k, cos_a, sin_a, cos_r, sin_r):
    z = jnp.dot(_bf(h), w_in_ref[...], preferred_element_type=jnp.float32)
    q = _rotate(_head_norm(z[:, _Q0:_K0], qg, bd_q), cos_a, sin_a) * (HEAD_DIM ** -0.5)
    k = _rotate(_head_norm(z[:, _K0:_V0], kg, bd_k), cos_a, sin_a)
    v = z[:, _V0:_RQ0]
    rq = _rotate(z[:, _RQ0:_RK0], cos_r, sin_r)
    rk = _rotate(z[:, _RK0:_RV0], cos_r, sin_r) * (RET_DK ** -0.5)
    rv = z[:, _RV0:_RG0]
    rg = z[:, _RG0:IN_W]
    return q, k, v, rq, rk, rv, rg


_NAT_HEADS = (0, 2, 5, 7)
_ROL_HEADS = (1, 3, 4, 6)


def _stack_heads(q, heads):
    lo = _lane_lo((q.shape[0], LANES))
    parts = []
    for j in heads:
        slab = q[:, (j // 2) * LANES:(j // 2 + 1) * LANES]
        parts.append(jnp.where(lo if j % 2 == 0 else ~lo, slab, 0.0))
    return jnp.concatenate(parts, axis=0)


def _sink_column(sinks_ref, heads, rows):
    r = lax.broadcasted_iota(jnp.int32, (len(heads) * rows, 1), 0)
    col = jnp.full((len(heads) * rows, 1), sinks_ref[heads[-1]], jnp.float32)
    for n in range(len(heads) - 2, -1, -1):
        col = jnp.where(r < (n + 1) * rows, sinks_ref[heads[n]], col)
    return col


def _softmax_pv(s, mask, sink, v):
    s = jnp.where(mask, s, NEG_BIG)
    m = jnp.maximum(jnp.max(s, axis=-1, keepdims=True), sink)
    p = jnp.exp(s - m)
    denom = jnp.sum(p, axis=-1, keepdims=True) + jnp.exp(sink - m)
    return _dot(p, v) / denom


def _attention(q, k_all, v_all, mask, sinks_ref):
    rows = q.shape[0]
    k_rol = pltpu.roll(k_all, HEAD_DIM, axis=1)
    v_rol = pltpu.roll(v_all, HEAD_DIM, axis=1)
    mask4 = jnp.concatenate([mask] * 4, axis=0)
    o_nat = _softmax_pv(_dot_nt(_stack_heads(q, _NAT_HEADS), k_all), mask4,
                        _sink_column(sinks_ref, _NAT_HEADS, rows), v_all)
    o_rol = _softmax_pv(_dot_nt(_stack_heads(q, _ROL_HEADS), k_rol), mask4,
                        _sink_column(sinks_ref, _ROL_HEADS, rows), v_rol)
    lo = _lane_lo((rows, LANES))
    blk = lambda o, n: o[n * rows:(n + 1) * rows]
    return jnp.concatenate([
        jnp.where(lo, blk(o_nat, 0), blk(o_rol, 0)),
        jnp.where(lo, blk(o_nat, 1), blk(o_rol, 1)),
        jnp.where(lo, blk(o_rol, 2), blk(o_nat, 2)),
        jnp.where(lo, blk(o_rol, 3), blk(o_nat, 3)),
    ], axis=1)


def _retention(rq, rk, rv, rg, state, decay_ref, qdec, kdec, gpow):
    rows = rq.shape[0]
    lo = _lane_lo((rows, LANES))
    rkd = rk * kdec
    outs, new_state = [], []
    for m in range(RET_HEADS // 2):
        sl = slice(m * LANES, (m + 1) * LANES)
        upd = state[m] * gpow[:, m:m + 1]
        for p in range(2):
            h = 2 * m + p
            half = lo if p == 0 else ~lo
            rqm = jnp.where(half, rq[:, sl], 0.0)
            rv_h = rv[:, h * RET_DV:(h + 1) * RET_DV]
            s = _dot_nt(rqm, rk[:, sl]) * decay_ref[h]
            ret = _dot(s, rv_h) + _dot(rqm, state[m]) * qdec[:, h * RET_DV:(h + 1) * RET_DV]
            upd = upd + _dot_tn(jnp.where(half, rkd[:, sl], 0.0), rv_h)
            rg_h = rg[:, h * RET_DV:(h + 1) * RET_DV]
            outs.append(rg_h * jax.nn.sigmoid(rg_h) * _rms_rows(ret))
        new_state.append(upd)
    return jnp.concatenate(outs, axis=1), new_state


def _route(h2, wr_hi_ref, wr_cat_ref, br):
    h_hi = _bf(h2)
    h_lo = _bf(h2 - h_hi.astype(jnp.float32))
    both = jnp.dot(h_hi, wr_cat_ref[...], preferred_element_type=jnp.float32)
    logits = (both[:, :LANES] + both[:, LANES:]
              + jnp.dot(h_lo, wr_hi_ref[...], preferred_element_type=jnp.float32)) + br
    lane = lax.broadcasted_iota(jnp.int32, logits.shape, 1)
    idx_out = jnp.zeros(logits.shape, jnp.int32)
    val_out = jnp.zeros(logits.shape, jnp.float32)
    onehot = jnp.zeros(logits.shape, jnp.float32)
    vals = []
    work = logits
    for k in range(TOP_K):
        mx = jnp.max(work, axis=-1, keepdims=True)
        ix = jnp.min(jnp.where(work == mx, lane, LANES), axis=-1, keepdims=True)
        work = jnp.where(lane == ix, NEG_BIG, work)
        onehot = jnp.where(lane == ix, 1.0, onehot)
        idx_out = jnp.where(lane == k, ix, idx_out)
        vals.append(mx)
    exps = [jnp.exp(v - vals[0]) for v in vals]
    total = exps[0] + exps[1] + exps[2] + exps[3]
    for k in range(TOP_K):
        val_out = jnp.where(lane == k, exps[k] / total, val_out)
    idx_t = idx_out.astype(jnp.float32).T[0:SUBLANES].astype(jnp.int32)
    counts = jnp.sum(onehot, axis=0, keepdims=True).astype(jnp.int32)
    return idx_out, val_out, idx_t, counts


def _adaln_kernel(c_ref, w_ref, b_ref, o_ref):
    c = c_ref[...]
    o_ref[...] = _dot(c * jax.nn.sigmoid(c), w_ref[...]) + b_ref[...]


def _adaln(c_all, w_ada, b_ada):
    rows, d = c_all.shape
    n = w_ada.shape[1]
    bn = 1536
    return pl.pallas_call(
        _adaln_kernel,
        grid=(n // bn,),
        in_specs=[pl.BlockSpec((rows, d), lambda j: (0, 0)),
                  pl.BlockSpec((d, bn), lambda j: (0, j)),
                  pl.BlockSpec((1, bn), lambda j: (0, j))],
        out_specs=pl.BlockSpec((rows, bn), lambda j: (0, j)),
        out_shape=jax.ShapeDtypeStruct((rows, n), jnp.float32),
        compiler_params=pltpu.CompilerParams(vmem_limit_bytes=VMEM_LIMIT_BYTES),
    )(c_all, w_ada, b_ada.reshape(1, n))


def _prompt_mix_kernel(sinks_ref, x_ref, mod_ref, gmix_ref, gffn_ref, w_in_ref, qg_ref, kg_ref,
                       bdq_ref, bdk_ref, cosa_ref, sina_ref, cosr_ref, sinr_ref,
                       decay_ref, qdec_ref, kdec_ref, gpow_ref, w_out_ref,
                       wrh_ref, wrl_ref, br_ref,
                       x1_ref, h2_ref, topi_ref, topw_ref, topit_ref, cnt_ref, kwin_ref, vwin_ref, rend_ref,
                       kprev, vprev, state, mix):
    step = pl.program_id(0)
    tb = x_ref.shape[0]

    @pl.when(step == 0)
    def _():
        kprev[...] = jnp.zeros_like(kprev)
        vprev[...] = jnp.zeros_like(vprev)
        state[...] = jnp.zeros_like(state)

    x = x_ref[...]
    h = _pre_norm(x, gmix_ref[...], mod_ref[0:1, :], mod_ref[1:2, :])
    q, k, v, rq, rk, rv, rg = _project(
        h, w_in_ref, qg_ref[...], kg_ref[...], bdq_ref[...], bdk_ref[...],
        cosa_ref[...], sina_ref[...], cosr_ref[...], sinr_ref[...])

    qi = lax.broadcasted_iota(jnp.int32, (WINDOW, 2 * WINDOW), 0)
    ci = lax.broadcasted_iota(jnp.int32, (WINDOW, 2 * WINDOW), 1)
    band = (ci > qi) & (ci <= qi + WINDOW)
    qdec, kdec, gpow = qdec_ref[...], kdec_ref[...], gpow_ref[...]

    for sb in range(tb // WINDOW):
        rs = slice(sb * WINDOW, (sb + 1) * WINDOW)
        k_sb, v_sb = k[rs], v[rs]
        k_all = jnp.concatenate([kprev[...], k_sb], axis=0)
        v_all = jnp.concatenate([vprev[...], v_sb], axis=0)
        has_prev = jnp.logical_or(step > 0, sb > 0) if sb == 0 else True
        mask = band & ((ci >= WINDOW) | has_prev)
        att = _attention(q[rs], k_all, v_all, mask, sinks_ref)
        kprev[...] = k_sb
        vprev[...] = v_sb
        yret, new_state = _retention(rq[rs], rk[rs], rv[rs], rg[rs], [state[0], state[1]],
                                     decay_ref, qdec, kdec, gpow)
        state[0] = new_state[0]
        state[1] = new_state[1]
        mix[rs, 0:ATT_W] = _bf(att)
        mix[rs, ATT_W:ATT_W + RET_W] = _bf(yret)

    x1 = x + mod_ref[2:3, :] * jnp.dot(mix[...], w_out_ref[...],
                                       preferred_element_type=jnp.float32)
    x1_ref[...] = x1
    h2 = _pre_norm(x1, gffn_ref[...], mod_ref[3:4, :], mod_ref[4:5, :])
    h2_ref[...] = h2
    topi, topw, topi_t, counts = _route(h2, wrh_ref, wrl_ref, br_ref[...])
    topi_ref[...] = topi
    topw_ref[...] = topw
    topit_ref[...] = topi_t
    cnt_ref[0] = counts

    @pl.when(step == pl.num_programs(0) - 1)
    def _():
        kwin_ref[...] = kprev[...]
        vwin_ref[...] = vprev[...]
        rend_ref[...] = state[...]


def _const_spec(shape):
    nd = len(shape)
    return pl.BlockSpec(shape, lambda *_: (0,) * nd)


def _prompt_mix(x, mod, sinks, gmix, gffn, w_in, qg, kg, bdq, bdk, tabs, ret_tabs, w_out,
                wrh, wrl, br):
    t, d = x.shape
    tb = PROMPT_BLOCK
    cosa, sina, cosr, sinr = tabs
    decay, qdec, kdec, gpow = ret_tabs
    row_spec = lambda w: pl.BlockSpec((tb, w), lambda i, *_: (i, 0))
    consts = [mod, gmix, gffn, w_in, qg, kg, bdq, bdk]
    consts2 = [decay, qdec, kdec, gpow, w_out, wrh, wrl, br]
    grid_spec = pltpu.PrefetchScalarGridSpec(
        num_scalar_prefetch=1,
        grid=(t // tb,),
        in_specs=([row_spec(d)] + [_const_spec(a.shape) for a in consts]
                  + [row_spec(LANES)] * 4 + [_const_spec(a.shape) for a in consts2]),
        out_specs=[row_spec(d), row_spec(d), row_spec(LANES), row_spec(LANES),
                   pl.BlockSpec((SUBLANES, tb), lambda i, *_: (0, i)),
                   pl.BlockSpec((1, 1, LANES), lambda i, *_: (i, 0, 0)),
                   _const_spec((WINDOW, KV_W)), _const_spec((WINDOW, KV_W)),
                   _const_spec((2, LANES, RET_DV))],
        scratch_shapes=[pltpu.VMEM((WINDOW, KV_W), jnp.float32),
                        pltpu.VMEM((WINDOW, KV_W), jnp.float32),
                        pltpu.VMEM((2, LANES, RET_DV), jnp.float32),
                        pltpu.VMEM((tb, ATT_W + RET_W), jnp.bfloat16)],
    )
    return pl.pallas_call(
        _prompt_mix_kernel,
        grid_spec=grid_spec,
        out_shape=[jax.ShapeDtypeStruct((t, d), jnp.float32),
                   jax.ShapeDtypeStruct((t, d), jnp.float32),
                   jax.ShapeDtypeStruct((t, LANES), jnp.int32),
                   jax.ShapeDtypeStruct((t, LANES), jnp.float32),
                   jax.ShapeDtypeStruct((SUBLANES, t), jnp.int32),
                   jax.ShapeDtypeStruct((t // tb, 1, LANES), jnp.int32),
                   jax.ShapeDtypeStruct((WINDOW, KV_W), jnp.float32),
                   jax.ShapeDtypeStruct((WINDOW, KV_W), jnp.float32),
                   jax.ShapeDtypeStruct((2, LANES, RET_DV), jnp.float32)],
        compiler_params=pltpu.CompilerParams(
            dimension_semantics=("arbitrary",), vmem_limit_bytes=VMEM_LIMIT_BYTES),
    )(sinks, x, *consts, cosa, sina, cosr, sinr, *consts2)


def _per_seq(stacked, rows, ds, s):
    return jnp.concatenate([stacked[g * rows + s * ds:g * rows + (s + 1) * ds]
                            for g in range(stacked.shape[0] // rows)], axis=0)


def _from_per_seq(parts, ds):
    groups = parts[0].shape[0] // ds
    return jnp.concatenate([p[g * ds:(g + 1) * ds] for g in range(groups) for p in parts], axis=0)


def _sample_attention(q, k_new, v_new, ck_ref, cv_ref, sinks_ref, ds):
    rows = q.shape[0]
    nseq, cache_w = ck_ref.shape[0], ck_ref.shape[1]
    tok_c = lax.broadcasted_iota(jnp.int32, (4 * rows, cache_w), 0) % rows
    col_c = lax.broadcasted_iota(jnp.int32, (4 * rows, cache_w), 1)
    delta = tok_c % ds + cache_w - col_c
    mask_c = (delta >= 0) & (delta < WINDOW)
    tok_n = lax.broadcasted_iota(jnp.int32, (4 * rows, rows), 0) % rows
    col_n = lax.broadcasted_iota(jnp.int32, (4 * rows, rows), 1)
    mask_n = (tok_n // ds == col_n // ds) & (col_n % ds <= tok_n % ds)

    outs = []
    for heads, rolled in ((_NAT_HEADS, False), (_ROL_HEADS, True)):
        arrange = (lambda a: pltpu.roll(a, HEAD_DIM, axis=1)) if rolled else (lambda a: a)
        qst = _stack_heads(q, heads)
        s_new = jnp.where(mask_n, _dot_nt(qst, arrange(k_new)), NEG_BIG)
        s_cache = _from_per_seq(
            [_dot_nt(_per_seq(qst, rows, ds, s), arrange(ck_ref[s])) for s in range(nseq)], ds)
        s_cache = jnp.where(mask_c, s_cache, NEG_BIG)
        sink = _sink_column(sinks_ref, heads, rows)
        m = jnp.maximum(jnp.maximum(jnp.max(s_cache, axis=-1, keepdims=True),
                                    jnp.max(s_new, axis=-1, keepdims=True)), sink)
        p_cache = jnp.exp(s_cache - m)
        p_new = jnp.exp(s_new - m)
        denom = (jnp.sum(p_cache, axis=-1, keepdims=True) + jnp.sum(p_new, axis=-1, keepdims=True)
                 + jnp.exp(sink - m))
        o = _from_per_seq(
            [_dot(_per_seq(p_cache, rows, ds, s), arrange(cv_ref[s])) for s in range(nseq)], ds)
        outs.append((o + _dot(p_new, arrange(v_new))) / denom)
    o_nat, o_rol = outs
    lo = _lane_lo((rows, LANES))
    blk = lambda o, n: o[n * rows:(n + 1) * rows]
    return jnp.concatenate([
        jnp.where(lo, blk(o_nat, 0), blk(o_rol, 0)),
        jnp.where(lo, blk(o_nat, 1), blk(o_rol, 1)),
        jnp.where(lo, blk(o_rol, 2), blk(o_nat, 2)),
        jnp.where(lo, blk(o_rol, 3), blk(o_nat, 3)),
    ], axis=1)


def _sample_retention(rq, rk, rv, rg, st_ref, rend_ref, decay_ref, qdec, kdec, gpow, ds):
    rows = rq.shape[0]
    nseq = st_ref.shape[0]
    lo = _lane_lo((rows, LANES))
    rkd = rk * kdec
    outs = []
    for m in range(RET_HEADS // 2):
        sl = slice(m * LANES, (m + 1) * LANES)
        heads = (2 * m, 2 * m + 1)
        rqm = [jnp.where(lo if p == 0 else ~lo, rq[:, sl], 0.0) for p in range(2)]
        rkdm = [jnp.where(lo if p == 0 else ~lo, rkd[:, sl], 0.0) for p in range(2)]
        rv_h = [rv[:, h * RET_DV:(h + 1) * RET_DV] for h in heads]
        q2 = jnp.concatenate(rqm, axis=0)
        k2 = jnp.concatenate(rkdm, axis=0)
        v2 = jnp.concatenate(rv_h, axis=0)
        inter = _from_per_seq(
            [_dot(_per_seq(q2, rows, ds, s), st_ref[s, m]) for s in range(nseq)], ds)
        for s in range(nseq):
            rend_ref[s, m] = (st_ref[s, m] * gpow[:, m:m + 1]
                              + _dot_tn(_per_seq(k2, rows, ds, s), _per_seq(v2, rows, ds, s)))
        for p, h in enumerate(heads):
            intra = _dot(_dot_nt(rqm[p], rk[:, sl]) * decay_ref[h], rv_h[p])
            ret = intra + inter[p * rows:(p + 1) * rows] * qdec[:, h * RET_DV:(h + 1) * RET_DV]
            rg_h = rg[:, h * RET_DV:(h + 1) * RET_DV]
            outs.append(rg_h * jax.nn.sigmoid(rg_h) * _rms_rows(ret))
    return jnp.concatenate(outs, axis=1)


def _sample_mix_kernel(sinks_ref, x_ref, mod_ref, gmix_ref, gffn_ref, w_in_ref, qg_ref, kg_ref,
                       bdq_ref, bdk_ref, cosa_ref, sina_ref, cosr_ref, sinr_ref,
                       decay_ref, qdec_ref, kdec_ref, gpow_ref, w_out_ref,
                       wrh_ref, wrl_ref, br_ref, ck_ref, cv_ref, st_ref,
                       x1_ref, h2_ref, topi_ref, topw_ref, topit_ref, cnt_ref, kwin_ref, vwin_ref, rend_ref,
                       mix):
    nseq, cache_w = ck_ref.shape[0], ck_ref.shape[1]
    ds = x_ref.shape[0] // nseq
    d = x_ref.shape[1]
    x = x_ref[...]
    mod = lambda i: mod_ref[:, i * d:(i + 1) * d]
    h = _pre_norm(x, gmix_ref[...], mod(0), mod(1))
    q, k, v, rq, rk, rv, rg = _project(
        h, w_in_ref, qg_ref[...], kg_ref[...], bdq_ref[...], bdk_ref[...],
        cosa_ref[...], sina_ref[...], cosr_ref[...], sinr_ref[...])

    for s in range(nseq):
        rs = slice(s * ds, (s + 1) * ds)
        kwin_ref[s] = jnp.concatenate([ck_ref[s, ds:, :], k[rs]], axis=0)
        vwin_ref[s] = jnp.concatenate([cv_ref[s, ds:, :], v[rs]], axis=0)
    mix[:, 0:ATT_W] = _bf(_sample_attention(q, k, v, ck_ref, cv_ref, sinks_ref, ds))
    mix[:, ATT_W:ATT_W + RET_W] = _bf(_sample_retention(
        rq, rk, rv, rg, st_ref, rend_ref, decay_ref, qdec_ref[...], kdec_ref[...], gpow_ref[...], ds))

    x1 = x + mod(2) * jnp.dot(mix[...], w_out_ref[...], preferred_element_type=jnp.float32)
    x1_ref[...] = x1
    h2 = _pre_norm(x1, gffn_ref[...], mod(3), mod(4))
    h2_ref[...] = h2
    topi, topw, topi_t, counts = _route(h2, wrh_ref, wrl_ref, br_ref[...])
    topi_ref[...] = topi
    topw_ref[...] = topw
    topit_ref[...] = topi_t
    cnt_ref[0] = counts


def _sample_mix(x, mod_tok, sinks, gmix, gffn, w_in, qg, kg, bdq, bdk, tabs, ret_tabs, w_out,
                wrh, wrl, br, cache_k, cache_v, state):
    t, d = x.shape
    nb, cache_w = cache_k.shape[0], cache_k.shape[1]
    ds = t // nb
    g = SAMPLE_SEQS
    tb = g * ds
    cosa, sina, cosr, sinr = tabs
    decay, qdec, kdec, gpow = ret_tabs
    row_spec = lambda w: pl.BlockSpec((tb, w), lambda i, *_: (i, 0))
    seq3 = lambda a: pl.BlockSpec((g,) + a.shape[1:], lambda i, *_: (i,) + (0,) * (a.ndim - 1))
    consts = [gmix, gffn, w_in, qg, kg, bdq, bdk, cosa, sina, cosr, sinr,
              decay, qdec, kdec, gpow, w_out, wrh, wrl, br]
    grid_spec = pltpu.PrefetchScalarGridSpec(
        num_scalar_prefetch=1,
        grid=(nb // g,),
        in_specs=([row_spec(d), row_spec(mod_tok.shape[1])]
                  + [_const_spec(a.shape) for a in consts]
                  + [seq3(cache_k), seq3(cache_v), seq3(state)]),
        out_specs=[row_spec(d), row_spec(d), row_spec(LANES), row_spec(LANES),
                   pl.BlockSpec((SUBLANES, tb), lambda i, *_: (0, i)),
                   pl.BlockSpec((1, 1, LANES), lambda i, *_: (i, 0, 0)),
                   seq3(cache_k), seq3(cache_v), seq3(state)],
        scratch_shapes=[pltpu.VMEM((tb, ATT_W + RET_W), jnp.bfloat16)],
    )
    return pl.pallas_call(
        _sample_mix_kernel,
        grid_spec=grid_spec,
        out_shape=[jax.ShapeDtypeStruct((t, d), jnp.float32),
                   jax.ShapeDtypeStruct((t, d), jnp.float32),
                   jax.ShapeDtypeStruct((t, LANES), jnp.int32),
                   jax.ShapeDtypeStruct((t, LANES), jnp.float32),
                   jax.ShapeDtypeStruct((SUBLANES, t), jnp.int32),
                   jax.ShapeDtypeStruct((t // tb, 1, LANES), jnp.int32),
                   jax.ShapeDtypeStruct(cache_k.shape, jnp.float32),
                   jax.ShapeDtypeStruct(cache_v.shape, jnp.float32),
                   jax.ShapeDtypeStruct(state.shape, jnp.float32)],
        compiler_params=pltpu.CompilerParams(
            dimension_semantics=("arbitrary",), vmem_limit_bytes=VMEM_LIMIT_BYTES),
    )(sinks, x, mod_tok, *consts, cache_k, cache_v, state)


def _chunk_copies(cnt8, src_ref, src_start, dst_ref, dst_start, sem):
    for bit in range(CHUNK_BITS - 1, SUBLANES.bit_length() - 2, -1):
        size = 1 << bit
        done = (cnt8 >> (bit + 1)) << (bit + 1)

        @pl.when(((cnt8 >> bit) & 1) == 1)
        def _():
            src = pl.multiple_of(src_start + done, SUBLANES)
            dst = pl.multiple_of(dst_start + done, SUBLANES)
            pltpu.make_async_copy(src_ref.at[pl.ds(src, size)],
                                  dst_ref.at[pl.ds(dst, size)], sem).start()


def _wait_rows(ref, rows8, sem):
    for bit in range(STAGE_BITS - 1, SUBLANES.bit_length() - 2, -1):
        size = 1 << bit

        @pl.when(((rows8 >> bit) & 1) == 1)
        def _():
            pltpu.make_async_copy(ref.at[pl.ds(0, size)], ref.at[pl.ds(0, size)], sem).wait()


def _dispatch_kernel(cnt_ref, gstart_ref, cb_ref, blkrows_ref, ends_ref, tot_ref,
                     topit_p_ref, topit_s_ref, h2_p_ref, h2_s_ref, cbcol_ref, xs_ref,
                     stage, zeros, sems, zsem, *, nblk_p):
    step = pl.program_id(0)
    nblk = pl.num_programs(0)
    tb = h2_p_ref.shape[0]
    slot = step % 2

    @pl.when(step == 0)
    def _():
        zeros[...] = jnp.zeros_like(zeros)
        tile = zeros.shape[0]

        def fill(e):
            start = pl.multiple_of(ends_ref[e] - tile, tile)
            return pltpu.make_async_copy(zeros, xs_ref.at[pl.ds(start, tile)], zsem)

        for e in range(N_EXPERTS):
            @pl.when(tot_ref[e] > 0)
            def _():
                fill(e).start()
        for e in range(N_EXPERTS):
            @pl.when(tot_ref[e] > 0)
            def _():
                fill(e).wait()

    def tail_fill(t):
        tile = zeros.shape[0]
        return pltpu.make_async_copy(zeros, xs_ref.at[pl.ds(pl.multiple_of(t * tile, tile), tile)],
                                     zsem)

    first_unused = ends_ref[N_EXPERTS - 1] // zeros.shape[0]
    n_tiles = xs_ref.shape[0] // zeros.shape[0]

    @pl.when(step == 0)
    def _():
        lax.fori_loop(first_unused, n_tiles, lambda t, c: (tail_fill(t).start(), c)[1], 0)

    @pl.when(step == nblk - 1)
    def _():
        lax.fori_loop(first_unused, n_tiles, lambda t, c: (tail_fill(t).wait(), c)[1], 0)

    is_p = step < nblk_p
    topit = jnp.where(is_p, topit_p_ref[...], topit_s_ref[...])
    h2 = jnp.where(is_p, h2_p_ref[...], h2_s_ref[...])
    cbcol = cbcol_ref[0]

    sub = lax.broadcasted_iota(jnp.int32, (LANES, tb), 0)
    picks = [sub == topit[k:k + 1, :] for k in range(TOP_K)]
    onehot = jnp.zeros((LANES, tb), jnp.float32)
    for pk in picks:
        onehot = jnp.where(pk, 1.0, onehot)
    r = lax.broadcasted_iota(jnp.int32, (tb, tb), 0)
    c = lax.broadcasted_iota(jnp.int32, (tb, tb), 1)
    earlier = jnp.where(r < c, 1.0, 0.0)
    before = _dot(onehot, earlier)
    base = before + cbcol.astype(jnp.float32)
    row = lax.broadcasted_iota(jnp.int32, (stage.shape[1], tb), 0)
    sel = jnp.zeros((stage.shape[1], tb), jnp.float32)
    for pk in picks:
        col = jnp.sum(jnp.where(pk, base, 0.0), axis=0, keepdims=True).astype(jnp.int32)
        sel = jnp.where(row == col, 1.0, sel)

    stage[slot] = _dot(sel, h2)

    @pl.when(step > 0)
    def _():
        _wait_rows(xs_ref, blkrows_ref[jnp.maximum(step - 1, 0)], sems.at[1 - slot])

    def issue(e, carry):
        i = step * N_EXPERTS + e
        _chunk_copies(cnt_ref[i], stage.at[slot], cb_ref[i], xs_ref, gstart_ref[i], sems.at[slot])
        return carry

    lax.fori_loop(0, N_EXPERTS, issue, 0)

    @pl.when(step == nblk - 1)
    def _():
        _wait_rows(xs_ref, blkrows_ref[step], sems.at[slot])


def _dispatch(tables, topit_p, topit_s, h2_p, h2_s, cbcol, n_rows):
    n_p, d = h2_p.shape
    n_s = h2_s.shape[0]
    tb = ROUTE_BLOCK
    nblk_p, nblk_s = n_p // tb, n_s // tb
    pidx = lambda i, *_: jnp.minimum(i, nblk_p - 1)
    sidx = lambda i, *_: jnp.maximum(i - nblk_p, 0)
    grid_spec = pltpu.PrefetchScalarGridSpec(
        num_scalar_prefetch=6,
        grid=(nblk_p + nblk_s,),
        in_specs=[pl.BlockSpec((SUBLANES, tb), lambda i, *_: (0, pidx(i))),
                  pl.BlockSpec((SUBLANES, tb), lambda i, *_: (0, sidx(i))),
                  pl.BlockSpec((tb, d), lambda i, *_: (pidx(i), 0)),
                  pl.BlockSpec((tb, d), lambda i, *_: (sidx(i), 0)),
                  pl.BlockSpec((1, LANES, 1), lambda i, *_: (i, 0, 0))],
        out_specs=pl.BlockSpec(memory_space=pl.ANY),
        scratch_shapes=[pltpu.VMEM((2, STAGE_ROWS, d), jnp.float32),
                        pltpu.VMEM((EXPERT_TILE, d), jnp.float32),
                        pltpu.SemaphoreType.DMA((2,)),
                        pltpu.SemaphoreType.DMA],
    )
    return pl.pallas_call(
        functools.partial(_dispatch_kernel, nblk_p=nblk_p),
        grid_spec=grid_spec,
        out_shape=jax.ShapeDtypeStruct((n_rows, d), jnp.float32),
        compiler_params=pltpu.CompilerParams(
            dimension_semantics=("arbitrary",), vmem_limit_bytes=VMEM_LIMIT_BYTES),
    )(*tables, topit_p, topit_s, h2_p, h2_s, cbcol)


def _experts_kernel(tile_expert_ref, n_used_ref, x_ref, wgu_ref, bgu_ref, wd_ref, bd_ref, y_ref,
                    wgu_bf, wd_bf):
    step = pl.program_id(0)

    @pl.when(step < n_used_ref[0])
    def _():
        e = tile_expert_ref[step]
        prev = tile_expert_ref[jnp.maximum(step - 1, 0)]

        @pl.when(jnp.logical_or(step == 0, e != prev))
        def _():
            wgu_bf[...] = _bf(wgu_ref[0])
            wd_bf[...] = _bf(wd_ref[0])

        d_ff = wd_bf.shape[0]
        gu = jnp.dot(_bf(x_ref[...]), wgu_bf[...], preferred_element_type=jnp.float32) + bgu_ref[0]
        glu = jnp.minimum(gu[:, :d_ff], SWIGLU_LIMIT)
        lin = jnp.clip(gu[:, d_ff:], -SWIGLU_LIMIT, SWIGLU_LIMIT)
        act = glu * jax.nn.sigmoid(SWIGLU_ALPHA * glu) * (lin + 1.0)
        y_ref[...] = jnp.dot(_bf(act), wd_bf[...], preferred_element_type=jnp.float32) + bd_ref[0]

    @pl.when(step >= n_used_ref[0])
    def _():
        y_ref[...] = jnp.zeros_like(y_ref)


def _experts(tile_expert, n_used, x_sorted, w_gate_up, b_gate_up, w_down, b_down):
    p = x_sorted.shape[0]
    n_e, d, two_ff = w_gate_up.shape
    d_ff = two_ff // 2
    tm = EXPERT_TILE
    tile = lambda i, te, nu: jnp.minimum(i, nu[0] - 1)
    grid_spec = pltpu.PrefetchScalarGridSpec(
        num_scalar_prefetch=2,
        grid=(p // tm,),
        in_specs=[pl.BlockSpec((tm, d), lambda i, te, nu: (tile(i, te, nu), 0)),
                  pl.BlockSpec((1, d, two_ff), lambda i, te, nu: (te[i], 0, 0)),
                  pl.BlockSpec((1, 1, two_ff), lambda i, te, nu: (te[i], 0, 0)),
                  pl.BlockSpec((1, d_ff, d), lambda i, te, nu: (te[i], 0, 0)),
                  pl.BlockSpec((1, 1, d), lambda i, te, nu: (te[i], 0, 0))],
        out_specs=pl.BlockSpec((tm, d), lambda i, te, nu: (i, 0)),
        scratch_shapes=[pltpu.VMEM((d, two_ff), jnp.bfloat16),
                        pltpu.VMEM((d_ff, d), jnp.bfloat16)],
    )
    return pl.pallas_call(
        _experts_kernel,
        grid_spec=grid_spec,
        out_shape=jax.ShapeDtypeStruct(x_sorted.shape, jnp.float32),
        compiler_params=pltpu.CompilerParams(
            dimension_semantics=("arbitrary",), vmem_limit_bytes=VMEM_LIMIT_BYTES),
    )(tile_expert, n_used, x_sorted, w_gate_up, b_gate_up.reshape(n_e, 1, two_ff), w_down,
      b_down.reshape(n_e, 1, d))


def _combine_kernel(cnt_ref, gstart_ref, cb_ref, blkrows_ref,
                    topi_p_ref, topi_s_ref, topw_p_ref, topw_s_ref, x1_p_ref, x1_s_ref,
                    gate_p_ref, gate_s_ref, cbrow_ref, y_ref, out_p_ref, out_s_ref,
                    ybuf, sems, *, nblk_p):
    step = pl.program_id(0)
    nblk = pl.num_programs(0)
    tb = x1_p_ref.shape[0]
    slot = step % 2

    def fetch(blk, into):
        def issue(e, carry):
            i = blk * N_EXPERTS + e
            _chunk_copies(cnt_ref[i], y_ref, gstart_ref[i], ybuf.at[into], cb_ref[i], sems.at[into])
            return carry
        lax.fori_loop(0, N_EXPERTS, issue, 0)

    @pl.when(step == 0)
    def _():
        ybuf[...] = jnp.zeros_like(ybuf)
        fetch(0, 0)

    @pl.when(step + 1 < nblk)
    def _():
        fetch(step + 1, 1 - slot)

    is_p = step < nblk_p
    topi = jnp.where(is_p, topi_p_ref[...], topi_s_ref[...])
    topw = jnp.where(is_p, topw_p_ref[...], topw_s_ref[...])
    cbrow = cbrow_ref[0]

    lane = lax.broadcasted_iota(jnp.int32, (tb, LANES), 1)
    picks = []
    onehot = jnp.zeros((tb, LANES), jnp.float32)
    for k in range(TOP_K):
        ix = jnp.sum(jnp.where(lane == k, topi, 0), axis=-1, keepdims=True)
        picks.append(lane == ix)
        onehot = jnp.where(picks[k], 1.0, onehot)
    r = lax.broadcasted_iota(jnp.int32, (tb, tb), 0)
    c = lax.broadcasted_iota(jnp.int32, (tb, tb), 1)
    earlier = jnp.where(c < r, 1.0, 0.0)
    base = _dot(earlier, onehot) + cbrow.astype(jnp.float32)
    colid = lax.broadcasted_iota(jnp.int32, (tb, ybuf.shape[1]), 1)
    weights = jnp.zeros((tb, ybuf.shape[1]), jnp.float32)
    for k in range(TOP_K):
        col = jnp.sum(jnp.where(picks[k], base, 0.0), axis=-1, keepdims=True).astype(jnp.int32)
        wk = jnp.sum(jnp.where(lane == k, topw, 0.0), axis=-1, keepdims=True)
        weights = jnp.where(colid == col, wk, weights)

    _wait_rows(y_ref, blkrows_ref[step], sems.at[slot])
    moe = _dot(weights, ybuf[slot])

    @pl.when(is_p)
    def _():
        out_p_ref[...] = x1_p_ref[...] + gate_p_ref[...] * moe

    @pl.when(jnp.logical_not(is_p))
    def _():
        out_s_ref[...] = x1_s_ref[...] + gate_s_ref[...] * moe


def _combine(tables, topi_p, topi_s, topw_p, topw_s, x1_p, x1_s, gate_p, gate_s, cbrow, y_sorted):
    n_p, d = x1_p.shape
    n_s = x1_s.shape[0]
    tb = ROUTE_BLOCK
    nblk_p, nblk_s = n_p // tb, n_s // tb
    pidx = lambda i, *_: jnp.minimum(i, nblk_p - 1)
    sidx = lambda i, *_: jnp.maximum(i - nblk_p, 0)
    prow = lambda w: pl.BlockSpec((tb, w), lambda i, *_: (pidx(i), 0))
    srow = lambda w: pl.BlockSpec((tb, w), lambda i, *_: (sidx(i), 0))
    grid_spec = pltpu.PrefetchScalarGridSpec(
        num_scalar_prefetch=4,
        grid=(nblk_p + nblk_s,),
        in_specs=[prow(LANES), srow(LANES), prow(LANES), srow(LANES), prow(d), srow(d),
                  pl.BlockSpec((1, d), lambda i, *_: (0, 0)), srow(d),
                  pl.BlockSpec((1, 1, LANES), lambda i, *_: (i, 0, 0)),
                  pl.BlockSpec(memory_space=pl.ANY)],
        out_specs=[prow(d), srow(d)],
        scratch_shapes=[pltpu.VMEM((2, STAGE_ROWS, d), jnp.float32),
                        pltpu.SemaphoreType.DMA((2,))],
    )
    return pl.pallas_call(
        functools.partial(_combine_kernel, nblk_p=nblk_p),
        grid_spec=grid_spec,
        out_shape=[jax.ShapeDtypeStruct((n_p, d), jnp.float32),
                   jax.ShapeDtypeStruct((n_s, d), jnp.float32)],
        compiler_params=pltpu.CompilerParams(
            dimension_semantics=("arbitrary",), vmem_limit_bytes=VMEM_LIMIT_BYTES),
    )(*tables, topi_p, topi_s, topw_p, topw_s, x1_p, x1_s, gate_p, gate_s, cbrow, y_sorted)


def _rotary_tables(pos, inv_freq):
    ang = pos.astype(jnp.float32)[:, None] * inv_freq[None, :]
    cos, sin = jnp.cos(ang), jnp.sin(ang)
    cos_t = jnp.tile(cos, (1, LANES // cos.shape[1]))
    sin_t = jnp.tile(jnp.concatenate([-sin, sin], axis=1), (1, LANES // (2 * sin.shape[1])))
    return cos_t, sin_t


def _retention_tables(chunk):
    log_gamma = jnp.log1p(-jnp.exp2(-5.0 - jnp.arange(RET_HEADS, dtype=jnp.float32)))
    idx = jnp.arange(chunk, dtype=jnp.float32)
    diff = idx[:, None] - idx[None, :]
    decay = jnp.where(diff[None] >= 0,
                      jnp.exp(jnp.maximum(diff, 0.0)[None] * log_gamma[:, None, None]), 0.0)
    q_decay = jnp.exp((idx + 1.0)[:, None] * log_gamma[None, :])
    k_decay = jnp.exp((chunk - 1.0 - idx)[:, None] * log_gamma[None, :])
    qdec = jnp.repeat(q_decay, RET_DV, axis=1)
    kdec = jnp.repeat(k_decay, RET_DK, axis=1)
    g_chunk = jnp.exp(chunk * log_gamma)
    gpow = jnp.repeat(g_chunk.reshape(RET_HEADS // 2, 2), RET_DK, axis=1).T
    return decay, qdec, kdec, gpow


def _block_diag_mean(width):
    head = jnp.arange(width) // HEAD_DIM
    return jnp.where(head[:, None] == head[None, :], 1.0 / HEAD_DIM, 0.0).astype(jnp.bfloat16)


def kernel(x_prompt, x_sample, cache_k, cache_v, state_ret, c_prompt, c_sample, w_ada, b_ada,
           g_mix, w_in, q_gain, k_gain, sinks, w_out, g_ffn, w_router, b_router, w_gate_up,
           b_gate_up, w_down, b_down):
    depth = w_ada.shape[0]
    batch, seq, d = x_prompt.shape
    dec_batch, dec_seq, _ = x_sample.shape
    cache_w = cache_k.shape[2]
    assert batch == 1 and depth == 1
    assert seq % PROMPT_BLOCK == 0 and PROMPT_BLOCK % WINDOW == 0 and WINDOW == RET_CHUNK
    assert dec_batch % SAMPLE_SEQS == 0 and cache_w == WINDOW
    n_p, n_s = batch * seq, dec_batch * dec_seq
    n_tok = n_p + n_s
    assert n_p % ROUTE_BLOCK == 0 and n_s % ROUTE_BLOCK == 0
    assert ROUTE_BLOCK % (SAMPLE_SEQS * dec_seq) == 0 and d == SUBLANES * LANES

    l = 0
    f32 = jnp.float32
    rope_freq = 1.0 / (ROPE_THETA ** (jnp.arange(0, HEAD_DIM, 2, dtype=f32) / HEAD_DIM))
    ret_freq = 1.0 / (ROPE_THETA ** jnp.linspace(0.0, 1.0, RET_DK // 2, dtype=f32))
    pos_p = jnp.arange(seq, dtype=jnp.int32)
    pos_s = PAST_LEN + jnp.arange(dec_seq, dtype=jnp.int32)
    tabs_p = _rotary_tables(pos_p, rope_freq) + _rotary_tables(pos_p, ret_freq)
    tabs_s = tuple(jnp.tile(a, (SAMPLE_SEQS, 1))
                   for a in _rotary_tables(pos_s, rope_freq) + _rotary_tables(pos_s, ret_freq))
    ret_p = _retention_tables(RET_CHUNK)
    decay_s, qdec_s, kdec_s, gpow_s = _retention_tables(dec_seq)
    eye = jnp.eye(SAMPLE_SEQS, dtype=f32)
    rows_s = SAMPLE_SEQS * dec_seq
    ret_s = ((eye[None, :, None, :, None] * decay_s[:, None, :, None, :]).reshape(RET_HEADS, rows_s, rows_s),
             jnp.tile(qdec_s, (SAMPLE_SEQS, 1)), jnp.tile(kdec_s, (SAMPLE_SEQS, 1)), gpow_s)

    n_c = batch + dec_batch
    c_rows = -(-n_c // SUBLANES) * SUBLANES
    c_all = jnp.concatenate([c_prompt, c_sample, jnp.zeros((c_rows - n_c, d), f32)], axis=0)
    mod = _adaln(c_all, w_ada[l], b_ada[l])
    mod_p = mod[0].reshape(6, d)
    mod_s = jnp.repeat(mod[batch:n_c], dec_seq, axis=0)

    w_in_bf = _bf(w_in[l])
    w_out_bf = _bf(w_out[l])
    qg = jnp.tile(q_gain[l], ATT_HEADS).reshape(1, ATT_W)
    kg = jnp.tile(k_gain[l], ATT_KV_HEADS).reshape(1, KV_W)
    bdq, bdk = _block_diag_mean(ATT_W), _block_diag_mean(KV_W)
    gmix = g_mix[l].reshape(1, d)
    gffn = g_ffn[l].reshape(1, d)
    wr = jnp.pad(w_router[l], ((0, 0), (0, LANES - N_EXPERTS)))
    wr_hi = _bf(wr)
    wr_cat = jnp.concatenate([wr_hi, _bf(wr - wr_hi.astype(f32))], axis=1)
    br = jnp.concatenate([b_router[l], jnp.full((LANES - N_EXPERTS,), NEG_BIG, f32)]).reshape(1, LANES)
    shared = (sinks[l], gmix, gffn, w_in_bf, qg, kg, bdq, bdk)

    (x1_p, h2_p, topi_p, topw_p, topit_p, cnt_p, kwin_p, vwin_p, rend_p) = _prompt_mix(
        x_prompt.reshape(n_p, d), mod_p, *shared, tabs_p, ret_p, w_out_bf, wr_hi, wr_cat, br)

    (x1_s, h2_s, topi_s, topw_s, topit_s, cnt_s, kwin_s, vwin_s, rend_s) = _sample_mix(
        x_sample.reshape(n_s, d), mod_s, *shared, tabs_s, ret_s, w_out_bf, wr_hi, wr_cat, br,
        cache_k[l].reshape(dec_batch, cache_w, KV_W), cache_v[l].reshape(dec_batch, cache_w, KV_W),
        state_ret[l].reshape(dec_batch, 2, LANES, RET_DV))

    tb, tm = ROUTE_BLOCK, EXPERT_TILE
    i32 = jnp.int32
    cnt_s = cnt_s.reshape(n_s // tb, -1, LANES).sum(axis=1)
    cnt = jnp.concatenate([cnt_p.reshape(-1, LANES), cnt_s], axis=0)
    cnt8 = -(-cnt // SUBLANES) * SUBLANES
    rs = jnp.cumsum(cnt8, axis=0) - cnt8
    tot = jnp.sum(cnt8, axis=0)[:N_EXPERTS]
    padded = -(-tot // tm) * tm
    ends = jnp.cumsum(padded)
    gstart = (ends - padded)[None, :] + rs[:, :N_EXPERTS]
    cb = jnp.cumsum(cnt8, axis=1) - cnt8
    blk_rows = jnp.sum(cnt8, axis=1).astype(i32)
    max_rows = n_tok * TOP_K + cnt.shape[0] * N_EXPERTS * (SUBLANES - 1)
    n_rows = -(-max_rows // tm) * tm + N_EXPERTS * tm
    tile_start = jnp.arange(n_rows // tm, dtype=i32) * tm
    tile_expert = jnp.minimum(jnp.sum(ends[None, :] <= tile_start[:, None], axis=1),
                              N_EXPERTS - 1).astype(i32)
    n_used = (ends[-1:] // tm).astype(i32)
    flat = lambda a: a[:, :N_EXPERTS].reshape(-1).astype(i32)
    tables = (flat(cnt8), gstart.reshape(-1).astype(i32), flat(cb), blk_rows)

    x_sorted = _dispatch(tables + (ends.astype(i32), tot.astype(i32)), topit_p, topit_s, h2_p, h2_s,
                         cb.astype(i32)[:, :, None], n_rows)
    y_sorted = _experts(tile_expert, n_used, x_sorted, w_gate_up[l], b_gate_up[l], w_down[l],
                        b_down[l])
    y_p, y_s = _combine(tables, topi_p, topi_s, topw_p, topw_s, x1_p, x1_s, mod_p[5:6],
                        mod_s[:, 5 * d:6 * d], cb.astype(i32)[:, None, :], y_sorted)

    kv5 = lambda a, n: a.reshape(1, n, cache_w, ATT_KV_HEADS, HEAD_DIM)
    st5 = lambda a, n: a.reshape(1, n, RET_HEADS, RET_DK, RET_DV)
    return (y_p.reshape(batch, seq, d), y_s.reshape(dec_batch, dec_seq, d),
            kv5(kwin_p, batch), kv5(vwin_p, batch), st5(rend_p, batch),
            kv5(kwin_s, dec_batch), kv5(vwin_s, dec_batch), st5(rend_s, dec_batch))
```

```python
import functools

import jax
import jax.numpy as jnp
import numpy as np
from jax import lax
from jax.experimental import pallas as pl
from jax.experimental.pallas import tpu as pltpu

HEAD_DIM = 64
ATT_HEADS = 8
ATT_KV_HEADS = 2
ATT_W = ATT_HEADS * HEAD_DIM
KV_W = ATT_KV_HEADS * HEAD_DIM
WINDOW = 128
PAST_LEN = 16384
ROPE_THETA = 10000.0
RET_HEADS = 4
RET_DK = 64
RET_DV = 128
RET_QK_W = RET_HEADS * RET_DK
RET_W = RET_HEADS * RET_DV
RET_CHUNK = 128
N_EXPERTS = 32
TOP_K = 4
SWIGLU_LIMIT = 7.0
SWIGLU_ALPHA = 1.702
EPS = 1e-6

_Q0, _K0, _V0 = 0, ATT_W, ATT_W + KV_W
_RQ0 = ATT_W + 2 * KV_W
_RK0 = _RQ0 + RET_QK_W
_RV0 = _RK0 + RET_QK_W
_RG0 = _RV0 + RET_W
IN_W = _RG0 + RET_W

LANES = 128
SUBLANES = 8
VMEM_LIMIT_BYTES = 56 * 1024 * 1024
NEG_BIG = -1e30

PROMPT_BLOCK = 512
SAMPLE_SEQS = 16
ROUTE_BLOCK = 512
EXPERT_TILE = 512
EXPERT_GROUPS = 4
CHUNK_BITS = ROUTE_BLOCK.bit_length()
STAGE_ROWS = -(-(TOP_K * ROUTE_BLOCK + N_EXPERTS * (SUBLANES - 1)) // LANES) * LANES
STAGE_BITS = STAGE_ROWS.bit_length()
COMBINE_SLAB = 256

_NT = (((1,), (1,)), ((), ()))
_TN = (((0,), (0,)), ((), ()))


def _bf(x):
    return x.astype(jnp.bfloat16)


def _dot(a, b):
    return jnp.dot(_bf(a), _bf(b), preferred_element_type=jnp.float32)


def _dot_nt(a, b):
    return lax.dot_general(_bf(a), _bf(b), _NT, preferred_element_type=jnp.float32)


def _dot_tn(a, b):
    return lax.dot_general(_bf(a), _bf(b), _TN, preferred_element_type=jnp.float32)


def _lane_lo(shape):
    lane = lax.broadcasted_iota(jnp.int32, shape, len(shape) - 1)
    return (lane % LANES) < HEAD_DIM


def _swap_halves(x):
    lane = lax.broadcasted_iota(jnp.int32, x.shape, 1)
    first = (lane % HEAD_DIM) < (HEAD_DIM // 2)
    return jnp.where(first, pltpu.roll(x, LANES - HEAD_DIM // 2, axis=1),
                     pltpu.roll(x, HEAD_DIM // 2, axis=1))


def _rotate(x, cos, sin_signed):
    outs = []
    for j in range(x.shape[1] // LANES):
        xs = x[:, j * LANES:(j + 1) * LANES]
        outs.append(xs * cos + _swap_halves(xs) * sin_signed)
    return outs[0] if len(outs) == 1 else jnp.concatenate(outs, axis=1)


def _repeat_rows(m, n):
    r, w = m.shape
    return jnp.broadcast_to(m[:, None, :], (r, n, w)).reshape(r * n, w)


def _block_rotary(blk_ref, tab_ref, which):
    ca, sa = blk_ref[0, 2 * which:2 * which + 1, :], blk_ref[0, 2 * which + 1:2 * which + 2, :]
    cb, sb, cbs, sbs = (tab_ref[4 * which + n] for n in range(4))
    return ca * cb - sa * sb, sa * cbs + ca * sbs


def _rms_rows(x):
    return x * lax.rsqrt(jnp.mean(x * x, axis=-1, keepdims=True) + EPS)


def _pre_norm(x, gain, shift, scale):
    return _rms_rows(x) * gain * (1.0 + scale) + shift


def _head_norm(x, gain_tiled, blockdiag):
    ms = jnp.dot(_bf(x * x), blockdiag, preferred_element_type=jnp.float32)
    return x * lax.rsqrt(ms + EPS) * gain_tiled


def _project(h, w_in_ref, qg, kg, bd_q, bd_k, cos_a, sin_a, cos_r, sin_r):
    z = jnp.dot(_bf(h), w_in_ref[...], preferred_element_type=jnp.float32)
    q = _rotate(_head_norm(z[:, _Q0:_K0], qg, bd_q), cos_a, sin_a) * (HEAD_DIM ** -0.5)
    k = _rotate(_head_norm(z[:, _K0:_V0], kg, bd_k), cos_a, sin_a)
    v = z[:, _V0:_RQ0]
    rq = _rotate(z[:, _RQ0:_RK0], cos_r, sin_r)
    rk = _rotate(z[:, _RK0:_RV0], cos_r, sin_r) * (RET_DK ** -0.5)
    rv = z[:, _RV0:_RG0]
    rg = z[:, _RG0:IN_W]
    return q, k, v, rq, rk, rv, rg


_NAT_HEADS = (0, 2, 5, 7)
_ROL_HEADS = (1, 3, 4, 6)


def _stack_heads(q, heads):
    lo = _lane_lo((q.shape[0], LANES))
    parts = []
    for j in heads:
        slab = q[:, (j // 2) * LANES:(j // 2 + 1) * LANES]
        parts.append(jnp.where(lo if j % 2 == 0 else ~lo, slab, 0.0))
    return jnp.concatenate(parts, axis=0)


def _sink_column(sinks_ref, heads, rows):
    r = lax.broadcasted_iota(jnp.int32, (len(heads) * rows, 1), 0)
    col = jnp.full((len(heads) * rows, 1), sinks_ref[heads[-1]], jnp.float32)
    for n in range(len(heads) - 2, -1, -1):
        col = jnp.where(r < (n + 1) * rows, sinks_ref[heads[n]], col)
    return col


def _softmax_pv(s, mask, sink, v):
    s = jnp.where(mask, s, NEG_BIG)
    m = jnp.maximum(jnp.max(s, axis=-1, keepdims=True), sink)
    p = jnp.exp(s - m)
    denom = jnp.sum(p, axis=-1, keepdims=True) + jnp.exp(sink - m)
    return _dot(p, v) / denom


def _attention(q, k_all, v_all, mask, sinks_ref):
    rows = q.shape[0]
    k_rol = pltpu.roll(k_all, HEAD_DIM, axis=1)
    v_rol = pltpu.roll(v_all, HEAD_DIM, axis=1)
    mask4 = jnp.concatenate([mask] * 4, axis=0)
    o_nat = _softmax_pv(_dot_nt(_stack_heads(q, _NAT_HEADS), k_all), mask4,
                        _sink_column(sinks_ref, _NAT_HEADS, rows), v_all)
    o_rol = _softmax_pv(_dot_nt(_stack_heads(q, _ROL_HEADS), k_rol), mask4,
                        _sink_column(sinks_ref, _ROL_HEADS, rows), v_rol)
    lo = _lane_lo((rows, LANES))
    blk = lambda o, n: o[n * rows:(n + 1) * rows]
    return jnp.concatenate([
        jnp.where(lo, blk(o_nat, 0), blk(o_rol, 0)),
        jnp.where(lo, blk(o_nat, 1), blk(o_rol, 1)),
        jnp.where(lo, blk(o_rol, 2), blk(o_nat, 2)),
        jnp.where(lo, blk(o_rol, 3), blk(o_nat, 3)),
    ], axis=1)


def _retention(rq, rk, rv, rg, state, decay_ref, qdec, kdec, gpow):
    rows = rq.shape[0]
    lo = _lane_lo((rows, LANES))
    rkd = rk * kdec
    outs, new_state = [], []
    for m in range(RET_HEADS // 2):
        sl = slice(m * LANES, (m + 1) * LANES)
        upd = state[m] * gpow[:, m:m + 1]
        for p in range(2):
            h = 2 * m + p
            half = lo if p == 0 else ~lo
            rqm = jnp.where(half, rq[:, sl], 0.0)
            rv_h = rv[:, h * RET_DV:(h + 1) * RET_DV]
            s = _dot_nt(rqm, rk[:, sl]) * decay_ref[h]
            ret = _dot(s, rv_h) + _dot(rqm, state[m]) * qdec[:, h * RET_DV:(h + 1) * RET_DV]
            upd = upd + _dot_tn(jnp.where(half, rkd[:, sl], 0.0), rv_h)
            rg_h = rg[:, h * RET_DV:(h + 1) * RET_DV]
            outs.append(rg_h * jax.nn.sigmoid(rg_h) * _rms_rows(ret))
        new_state.append(upd)
    return jnp.concatenate(outs, axis=1), new_state


def _route(h2, wr_hi_ref, wr_cat_ref, br):
    h_hi = _bf(h2)
    h_lo = _bf(h2 - h_hi.astype(jnp.float32))
    both = jnp.dot(h_hi, wr_cat_ref[...], preferred_element_type=jnp.float32)
    logits = (both[:, :LANES] + both[:, LANES:]
              + jnp.dot(h_lo, wr_hi_ref[...], preferred_element_type=jnp.float32)) + br
    lane = lax.broadcasted_iota(jnp.int32, logits.shape, 1)
    idx_out = jnp.zeros(logits.shape, jnp.int32)
    val_out = jnp.zeros(logits.shape, jnp.float32)
    onehot = jnp.zeros(logits.shape, jnp.float32)
    vals = []
    work = logits
    for k in range(TOP_K):
        mx = jnp.max(work, axis=-1, keepdims=True)
        ix = jnp.min(jnp.where(work == mx, lane, LANES), axis=-1, keepdims=True)
        work = jnp.where(lane == ix, NEG_BIG, work)
        onehot = jnp.where(lane == ix, 1.0, onehot)
        idx_out = jnp.where(lane == k, ix, idx_out)
        vals.append(mx)
    exps = [jnp.exp(v - vals[0]) for v in vals]
    total = exps[0] + exps[1] + exps[2] + exps[3]
    for k in range(TOP_K):
        val_out = jnp.where(lane == k, exps[k] / total, val_out)
    idx_t = idx_out.astype(jnp.float32).T[0:SUBLANES].astype(jnp.int32)
    counts = jnp.sum(onehot, axis=0, keepdims=True).astype(jnp.int32)
    return idx_out, val_out, idx_t, counts


def _adaln_kernel(c_ref, w_ref, b_ref, o_ref):
    c = c_ref[...]
    o_ref[...] = _dot(c * jax.nn.sigmoid(c), w_ref[...]) + b_ref[...]


def _adaln(c_all, w_ada, b_ada):
    rows, d = c_all.shape
    n = w_ada.shape[1]
    bn = 1536
    return pl.pallas_call(
        _adaln_kernel,
        grid=(n // bn,),
        in_specs=[pl.BlockSpec((rows, d), lambda j: (0, 0)),
                  pl.BlockSpec((d, bn), lambda j: (0, j)),
                  pl.BlockSpec((1, bn), lambda j: (0, j))],
        out_specs=pl.BlockSpec((rows, bn), lambda j: (0, j)),
        out_shape=jax.ShapeDtypeStruct((rows, n), jnp.float32),
        compiler_params=pltpu.CompilerParams(vmem_limit_bytes=VMEM_LIMIT_BYTES),
    )(c_all, w_ada, b_ada.reshape(1, n))


def _prompt_mix_kernel(sinks_ref, x_ref, mod_ref, gmix_ref, gffn_ref, w_in_ref, qg_ref, kg_ref,
                       bdq_ref, bdk_ref, rotblk_ref, rottab_ref,
                       decay_ref, qdec_ref, kdec_ref, gpow_ref, w_out_ref,
                       wrh_ref, wrl_ref, br_ref,
                       x1_ref, h2_ref, topi_ref, topw_ref, topit_ref, cnt_ref, kwin_ref, vwin_ref, rend_ref,
                       kprev, vprev, state, mix):
    step = pl.program_id(0)
    tb = x_ref.shape[0]

    @pl.when(step == 0)
    def _():
        kprev[...] = jnp.zeros_like(kprev)
        vprev[...] = jnp.zeros_like(vprev)
        state[...] = jnp.zeros_like(state)

    x = x_ref[...]
    h = _pre_norm(x, gmix_ref[...], mod_ref[0:1, :], mod_ref[1:2, :])
    q, k, v, rq, rk, rv, rg = _project(
        h, w_in_ref, qg_ref[...], kg_ref[...], bdq_ref[...], bdk_ref[...],
        *_block_rotary(rotblk_ref, rottab_ref, 0), *_block_rotary(rotblk_ref, rottab_ref, 1))

    qi = lax.broadcasted_iota(jnp.int32, (WINDOW, 2 * WINDOW), 0)
    ci = lax.broadcasted_iota(jnp.int32, (WINDOW, 2 * WINDOW), 1)
    band = (ci > qi) & (ci <= qi + WINDOW)
    qdec, kdec, gpow = qdec_ref[...], kdec_ref[...], gpow_ref[...]

    for sb in range(tb // WINDOW):
        rs = slice(sb * WINDOW, (sb + 1) * WINDOW)
        k_sb, v_sb = k[rs], v[rs]
        k_all = jnp.concatenate([kprev[...], k_sb], axis=0)
        v_all = jnp.concatenate([vprev[...], v_sb], axis=0)
        has_prev = jnp.logical_or(step > 0, sb > 0) if sb == 0 else True
        mask = band & ((ci >= WINDOW) | has_prev)
        att = _attention(q[rs], k_all, v_all, mask, sinks_ref)
        kprev[...] = k_sb
        vprev[...] = v_sb
        yret, new_state = _retention(rq[rs], rk[rs], rv[rs], rg[rs], [state[0], state[1]],
                                     decay_ref, qdec, kdec, gpow)
        state[0] = new_state[0]
        state[1] = new_state[1]
        mix[rs, 0:ATT_W] = _bf(att)
        mix[rs, ATT_W:ATT_W + RET_W] = _bf(yret)

    x1 = x + mod_ref[2:3, :] * jnp.dot(mix[...], w_out_ref[...],
                                       preferred_element_type=jnp.float32)
    x1_ref[...] = x1
    h2 = _pre_norm(x1, gffn_ref[...], mod_ref[3:4, :], mod_ref[4:5, :])
    h2_ref[...] = h2
    topi, topw, topi_t, counts = _route(h2, wrh_ref, wrl_ref, br_ref[...])
    topi_ref[...] = topi
    topw_ref[...] = topw
    topit_ref[...] = topi_t
    cnt_ref[0] = counts

    @pl.when(step == pl.num_programs(0) - 1)
    def _():
        kwin_ref[...] = kprev[...]
        vwin_ref[...] = vprev[...]
        rend_ref[...] = state[...]


def _const_spec(shape):
    nd = len(shape)
    return pl.BlockSpec(shape, lambda *_: (0,) * nd)


def _prompt_mix(x, mod, sinks, gmix, gffn, w_in, qg, kg, bdq, bdk, tabs, ret_tabs, w_out,
                wrh, wrl, br):
    t, d = x.shape
    tb = PROMPT_BLOCK
    rot_blk, rot_tab = tabs
    decay, qdec, kdec, gpow = ret_tabs
    row_spec = lambda w: pl.BlockSpec((tb, w), lambda i, *_: (i, 0))
    consts = [mod, gmix, gffn, w_in, qg, kg, bdq, bdk]
    consts2 = [rot_tab, decay, qdec, kdec, gpow, w_out, wrh, wrl, br]
    grid_spec = pltpu.PrefetchScalarGridSpec(
        num_scalar_prefetch=1,
        grid=(t // tb,),
        in_specs=([row_spec(d)] + [_const_spec(a.shape) for a in consts]
                  + [pl.BlockSpec((1,) + rot_blk.shape[1:], lambda i, *_: (i, 0, 0))]
                  + [_const_spec(a.shape) for a in consts2]),
        out_specs=[row_spec(d), row_spec(d), row_spec(LANES), row_spec(LANES),
                   pl.BlockSpec((SUBLANES, tb), lambda i, *_: (0, i)),
                   pl.BlockSpec((1, 1, LANES), lambda i, *_: (i, 0, 0)),
                   _const_spec((WINDOW, KV_W)), _const_spec((WINDOW, KV_W)),
                   _const_spec((2, LANES, RET_DV))],
        scratch_shapes=[pltpu.VMEM((WINDOW, KV_W), jnp.float32),
                        pltpu.VMEM((WINDOW, KV_W), jnp.float32),
                        pltpu.VMEM((2, LANES, RET_DV), jnp.float32),
                        pltpu.VMEM((tb, ATT_W + RET_W), jnp.bfloat16)],
    )
    return pl.pallas_call(
        _prompt_mix_kernel,
        grid_spec=grid_spec,
        out_shape=[jax.ShapeDtypeStruct((t, d), jnp.float32),
                   jax.ShapeDtypeStruct((t, d), jnp.float32),
                   jax.ShapeDtypeStruct((t, LANES), jnp.int32),
                   jax.ShapeDtypeStruct((t, LANES), jnp.float32),
                   jax.ShapeDtypeStruct((SUBLANES, t), jnp.int32),
                   jax.ShapeDtypeStruct((t // tb, 1, LANES), jnp.int32),
                   jax.ShapeDtypeStruct((WINDOW, KV_W), jnp.float32),
                   jax.ShapeDtypeStruct((WINDOW, KV_W), jnp.float32),
                   jax.ShapeDtypeStruct((2, LANES, RET_DV), jnp.float32)],
        compiler_params=pltpu.CompilerParams(
            dimension_semantics=("arbitrary",), vmem_limit_bytes=VMEM_LIMIT_BYTES),
    )(sinks, x, *consts, rot_blk, *consts2)


def _per_seq(stacked, rows, ds, s):
    return jnp.concatenate([stacked[g * rows + s * ds:g * rows + (s + 1) * ds]
                            for g in range(stacked.shape[0] // rows)], axis=0)


def _from_per_seq(parts, ds):
    groups = parts[0].shape[0] // ds
    return jnp.concatenate([p[g * ds:(g + 1) * ds] for g in range(groups) for p in parts], axis=0)


def _sample_attention(q, k_new, v_new, ck_ref, cv_ref, sinks_ref, ds):
    rows = q.shape[0]
    nseq, cache_w = ck_ref.shape[0], ck_ref.shape[1]
    tok_c = lax.broadcasted_iota(jnp.int32, (4 * rows, cache_w), 0) % rows
    col_c = lax.broadcasted_iota(jnp.int32, (4 * rows, cache_w), 1)
    delta = tok_c % ds + cache_w - col_c
    mask_c = (delta >= 0) & (delta < WINDOW)
    tok_n = lax.broadcasted_iota(jnp.int32, (4 * rows, rows), 0) % rows
    col_n = lax.broadcasted_iota(jnp.int32, (4 * rows, rows), 1)
    mask_n = (tok_n // ds == col_n // ds) & (col_n % ds <= tok_n % ds)

    outs = []
    for heads, rolled in ((_NAT_HEADS, False), (_ROL_HEADS, True)):
        arrange = (lambda a: pltpu.roll(a, HEAD_DIM, axis=1)) if rolled else (lambda a: a)
        qst = _stack_heads(q, heads)
        s_new = jnp.where(mask_n, _dot_nt(qst, arrange(k_new)), NEG_BIG)
        s_cache = _from_per_seq(
            [_dot_nt(_per_seq(qst, rows, ds, s), arrange(ck_ref[s])) for s in range(nseq)], ds)
        s_cache = jnp.where(mask_c, s_cache, NEG_BIG)
        sink = _sink_column(sinks_ref, heads, rows)
        m = jnp.maximum(jnp.maximum(jnp.max(s_cache, axis=-1, keepdims=True),
                                    jnp.max(s_new, axis=-1, keepdims=True)), sink)
        p_cache = jnp.exp(s_cache - m)
        p_new = jnp.exp(s_new - m)
        denom = (jnp.sum(p_cache, axis=-1, keepdims=True) + jnp.sum(p_new, axis=-1, keepdims=True)
                 + jnp.exp(sink - m))
        o = _from_per_seq(
            [_dot(_per_seq(p_cache, rows, ds, s), arrange(cv_ref[s])) for s in range(nseq)], ds)
        outs.append((o + _dot(p_new, arrange(v_new))) / denom)
    o_nat, o_rol = outs
    lo = _lane_lo((rows, LANES))
    blk = lambda o, n: o[n * rows:(n + 1) * rows]
    return jnp.concatenate([
        jnp.where(lo, blk(o_nat, 0), blk(o_rol, 0)),
        jnp.where(lo, blk(o_nat, 1), blk(o_rol, 1)),
        jnp.where(lo, blk(o_rol, 2), blk(o_nat, 2)),
        jnp.where(lo, blk(o_rol, 3), blk(o_nat, 3)),
    ], axis=1)


def _sample_retention(rq, rk, rv, rg, st_ref, rend_ref, decay_ref, qdec, kdec, gpow, ds):
    rows = rq.shape[0]
    nseq = st_ref.shape[0]
    lo = _lane_lo((rows, LANES))
    rkd = rk * kdec
    outs = []
    for m in range(RET_HEADS // 2):
        sl = slice(m * LANES, (m + 1) * LANES)
        heads = (2 * m, 2 * m + 1)
        rqm = [jnp.where(lo if p == 0 else ~lo, rq[:, sl], 0.0) for p in range(2)]
        rkdm = [jnp.where(lo if p == 0 else ~lo, rkd[:, sl], 0.0) for p in range(2)]
        rv_h = [rv[:, h * RET_DV:(h + 1) * RET_DV] for h in heads]
        q2 = jnp.concatenate(rqm, axis=0)
        k2 = jnp.concatenate(rkdm, axis=0)
        v2 = jnp.concatenate(rv_h, axis=0)
        inter = _from_per_seq(
            [_dot(_per_seq(q2, rows, ds, s), st_ref[s, m]) for s in range(nseq)], ds)
        for s in range(nseq):
            rend_ref[s, m] = (st_ref[s, m] * gpow[:, m:m + 1]
                              + _dot_tn(_per_seq(k2, rows, ds, s), _per_seq(v2, rows, ds, s)))
        for p, h in enumerate(heads):
            intra = _dot(_dot_nt(rqm[p], rk[:, sl]) * decay_ref[h], rv_h[p])
            ret = intra + inter[p * rows:(p + 1) * rows] * qdec[:, h * RET_DV:(h + 1) * RET_DV]
            rg_h = rg[:, h * RET_DV:(h + 1) * RET_DV]
            outs.append(rg_h * jax.nn.sigmoid(rg_h) * _rms_rows(ret))
    return jnp.concatenate(outs, axis=1)


def _sample_mix_kernel(sinks_ref, x_ref, mod_ref, gmix_ref, gffn_ref, w_in_ref, qg_ref, kg_ref,
                       bdq_ref, bdk_ref, cosa_ref, sina_ref, cosr_ref, sinr_ref,
                       decay_ref, qdec_ref, kdec_ref, gpow_ref, w_out_ref,
                       wrh_ref, wrl_ref, br_ref, ck_ref, cv_ref, st_ref,
                       x1_ref, h2_ref, topi_ref, topw_ref, topit_ref, cnt_ref, kwin_ref, vwin_ref, rend_ref,
                       mix):
    nseq, cache_w = ck_ref.shape[0], ck_ref.shape[1]
    ds = x_ref.shape[0] // nseq
    d = x_ref.shape[1]
    x = x_ref[...]
    mod = lambda i: _repeat_rows(mod_ref[:, i * d:(i + 1) * d], ds)
    h = _pre_norm(x, gmix_ref[...], mod(0), mod(1))
    q, k, v, rq, rk, rv, rg = _project(
        h, w_in_ref, qg_ref[...], kg_ref[...], bdq_ref[...], bdk_ref[...],
        cosa_ref[...], sina_ref[...], cosr_ref[...], sinr_ref[...])

    for s in range(nseq):
        rs = slice(s * ds, (s + 1) * ds)
        kwin_ref[s] = jnp.concatenate([ck_ref[s, ds:, :], k[rs]], axis=0)
        vwin_ref[s] = jnp.concatenate([cv_ref[s, ds:, :], v[rs]], axis=0)
    mix[:, 0:ATT_W] = _bf(_sample_attention(q, k, v, ck_ref, cv_ref, sinks_ref, ds))
    mix[:, ATT_W:ATT_W + RET_W] = _bf(_sample_retention(
        rq, rk, rv, rg, st_ref, rend_ref, decay_ref, qdec_ref[...], kdec_ref[...], gpow_ref[...], ds))

    x1 = x + mod(2) * jnp.dot(mix[...], w_out_ref[...], preferred_element_type=jnp.float32)
    x1_ref[...] = x1
    h2 = _pre_norm(x1, gffn_ref[...], mod(3), mod(4))
    h2_ref[...] = h2
    topi, topw, topi_t, counts = _route(h2, wrh_ref, wrl_ref, br_ref[...])
    topi_ref[...] = topi
    topw_ref[...] = topw
    topit_ref[...] = topi_t
    cnt_ref[0] = counts


def _sample_mix(x, mod_seq, sinks, gmix, gffn, w_in, qg, kg, bdq, bdk, tabs, ret_tabs, w_out,
                wrh, wrl, br, cache_k, cache_v, state):
    t, d = x.shape
    nb, cache_w = cache_k.shape[0], cache_k.shape[1]
    ds = t // nb
    g = SAMPLE_SEQS
    tb = g * ds
    cosa, sina, cosr, sinr = tabs
    decay, qdec, kdec, gpow = ret_tabs
    row_spec = lambda w: pl.BlockSpec((tb, w), lambda i, *_: (i, 0))
    seq3 = lambda a: pl.BlockSpec((g,) + a.shape[1:], lambda i, *_: (i,) + (0,) * (a.ndim - 1))
    consts = [gmix, gffn, w_in, qg, kg, bdq, bdk, cosa, sina, cosr, sinr,
              decay, qdec, kdec, gpow, w_out, wrh, wrl, br]
    grid_spec = pltpu.PrefetchScalarGridSpec(
        num_scalar_prefetch=1,
        grid=(nb // g,),
        in_specs=([row_spec(d), pl.BlockSpec((g, mod_seq.shape[1]), lambda i, *_: (i, 0))]
                  + [_const_spec(a.shape) for a in consts]
                  + [seq3(cache_k), seq3(cache_v), seq3(state)]),
        out_specs=[row_spec(d), row_spec(d), row_spec(LANES), row_spec(LANES),
                   pl.BlockSpec((SUBLANES, tb), lambda i, *_: (0, i)),
                   pl.BlockSpec((1, 1, LANES), lambda i, *_: (i, 0, 0)),
                   seq3(cache_k), seq3(cache_v), seq3(state)],
        scratch_shapes=[pltpu.VMEM((tb, ATT_W + RET_W), jnp.bfloat16)],
    )
    return pl.pallas_call(
        _sample_mix_kernel,
        grid_spec=grid_spec,
        out_shape=[jax.ShapeDtypeStruct((t, d), jnp.float32),
                   jax.ShapeDtypeStruct((t, d), jnp.float32),
                   jax.ShapeDtypeStruct((t, LANES), jnp.int32),
                   jax.ShapeDtypeStruct((t, LANES), jnp.float32),
                   jax.ShapeDtypeStruct((SUBLANES, t), jnp.int32),
                   jax.ShapeDtypeStruct((t // tb, 1, LANES), jnp.int32),
                   jax.ShapeDtypeStruct(cache_k.shape, jnp.float32),
                   jax.ShapeDtypeStruct(cache_v.shape, jnp.float32),
                   jax.ShapeDtypeStruct(state.shape, jnp.float32)],
        compiler_params=pltpu.CompilerParams(
            dimension_semantics=("arbitrary",), vmem_limit_bytes=VMEM_LIMIT_BYTES),
    )(sinks, x, mod_seq, *consts, cache_k, cache_v, state)


def _chunk_copies(cnt8, src_ref, src_start, dst_ref, dst_start, sem):
    for bit in range(CHUNK_BITS - 1, SUBLANES.bit_length() - 2, -1):
        size = 1 << bit
        done = (cnt8 >> (bit + 1)) << (bit + 1)

        @pl.when(((cnt8 >> bit) & 1) == 1)
        def _():
            src = pl.multiple_of(src_start + done, SUBLANES)
            dst = pl.multiple_of(dst_start + done, SUBLANES)
            pltpu.make_async_copy(src_ref.at[pl.ds(src, size)],
                                  dst_ref.at[pl.ds(dst, size)], sem).start()


def _wait_rows(ref, rows8, sem, bits=STAGE_BITS):
    for bit in range(bits - 1, SUBLANES.bit_length() - 2, -1):
        size = 1 << bit

        @pl.when(((rows8 >> bit) & 1) == 1)
        def _():
            pltpu.make_async_copy(ref.at[pl.ds(0, size)], ref.at[pl.ds(0, size)], sem).wait()


def _dispatch_kernel(cnt_ref, gstart_ref, cb_ref, blkrows_ref, ends_ref, padcnt_ref,
                     topit_p_ref, topit_s_ref, h2_p_ref, h2_s_ref, cbcol_ref, xs_ref,
                     stage, zeros, sems, zsem, *, nblk_p):
    step = pl.program_id(0)
    nblk = pl.num_programs(0)
    tb = h2_p_ref.shape[0]
    slot = step % 2

    tile = zeros.shape[0]
    first_unused = ends_ref[N_EXPERTS - 1] // tile
    n_tiles = xs_ref.shape[0] // tile

    def tail_fill(t):
        return pltpu.make_async_copy(zeros, xs_ref.at[pl.ds(pl.multiple_of(t * tile, tile), tile)],
                                     zsem)

    @pl.when(step == 0)
    def _():
        zeros[...] = jnp.zeros_like(zeros)

        def pad_fill(e, carry):
            _chunk_copies(padcnt_ref[e], zeros, 0, xs_ref, ends_ref[e] - padcnt_ref[e], zsem)
            return carry

        lax.fori_loop(0, N_EXPERTS, pad_fill, 0)
        lax.fori_loop(first_unused, n_tiles, lambda t, c: (tail_fill(t).start(), c)[1], 0)

    @pl.when(step == nblk - 1)
    def _():
        _wait_rows(xs_ref, padcnt_ref[N_EXPERTS], zsem, bits=(N_EXPERTS * tile).bit_length())
        lax.fori_loop(first_unused, n_tiles, lambda t, c: (tail_fill(t).wait(), c)[1], 0)

    is_p = step < nblk_p
    topit = jnp.where(is_p, topit_p_ref[...], topit_s_ref[...])
    h2 = jnp.where(is_p, h2_p_ref[...], h2_s_ref[...])
    cbcol = cbcol_ref[0]

    sub = lax.broadcasted_iota(jnp.int32, (LANES, tb), 0)
    picks = [sub == topit[k:k + 1, :] for k in range(TOP_K)]
    onehot = jnp.zeros((LANES, tb), jnp.float32)
    for pk in picks:
        onehot = jnp.where(pk, 1.0, onehot)
    r = lax.broadcasted_iota(jnp.int32, (tb, tb), 0)
    c = lax.broadcasted_iota(jnp.int32, (tb, tb), 1)
    earlier = jnp.where(r < c, 1.0, 0.0)
    before = _dot(onehot, earlier)
    base = before + cbcol.astype(jnp.float32)
    row = lax.broadcasted_iota(jnp.int32, (stage.shape[1], tb), 0)
    sel = jnp.zeros((stage.shape[1], tb), jnp.float32)
    for pk in picks:
        col = jnp.sum(jnp.where(pk, base, 0.0), axis=0, keepdims=True).astype(jnp.int32)
        sel = jnp.where(row == col, 1.0, sel)

    stage[slot] = _dot(sel, h2)

    @pl.when(step > 0)
    def _():
        _wait_rows(xs_ref, blkrows_ref[jnp.maximum(step - 1, 0)], sems.at[1 - slot])

    def issue(e, carry):
        i = step * N_EXPERTS + e
        _chunk_copies(cnt_ref[i], stage.at[slot], cb_ref[i], xs_ref, gstart_ref[i], sems.at[slot])
        return carry

    lax.fori_loop(0, N_EXPERTS, issue, 0)

    @pl.when(step == nblk - 1)
    def _():
        _wait_rows(xs_ref, blkrows_ref[step], sems.at[slot])


def _dispatch(tables, topit_p, topit_s, h2_p, h2_s, cbcol, n_rows):
    n_p, d = h2_p.shape
    n_s = h2_s.shape[0]
    tb = ROUTE_BLOCK
    nblk_p, nblk_s = n_p // tb, n_s // tb
    pidx = lambda i, *_: jnp.minimum(i, nblk_p - 1)
    sidx = lambda i, *_: jnp.maximum(i - nblk_p, 0)
    grid_spec = pltpu.PrefetchScalarGridSpec(
        num_scalar_prefetch=6,
        grid=(nblk_p + nblk_s,),
        in_specs=[pl.BlockSpec((SUBLANES, tb), lambda i, *_: (0, pidx(i))),
                  pl.BlockSpec((SUBLANES, tb), lambda i, *_: (0, sidx(i))),
                  pl.BlockSpec((tb, d), lambda i, *_: (pidx(i), 0)),
                  pl.BlockSpec((tb, d), lambda i, *_: (sidx(i), 0)),
                  pl.BlockSpec((1, LANES, 1), lambda i, *_: (i, 0, 0))],
        out_specs=pl.BlockSpec(memory_space=pl.ANY),
        scratch_shapes=[pltpu.VMEM((2, STAGE_ROWS, d), jnp.float32),
                        pltpu.VMEM((EXPERT_TILE, d), jnp.float32),
                        pltpu.SemaphoreType.DMA((2,)),
                        pltpu.SemaphoreType.DMA],
    )
    return pl.pallas_call(
        functools.partial(_dispatch_kernel, nblk_p=nblk_p),
        grid_spec=grid_spec,
        out_shape=jax.ShapeDtypeStruct((n_rows, d), jnp.float32),
        compiler_params=pltpu.CompilerParams(
            dimension_semantics=("arbitrary",), vmem_limit_bytes=VMEM_LIMIT_BYTES),
    )(*tables, topit_p, topit_s, h2_p, h2_s, cbcol)


def _expert_mlp(x, wgu_bf, bgu, wd_bf, bd):
    d_ff = wd_bf.shape[0]
    gu = jnp.dot(_bf(x), wgu_bf[...], preferred_element_type=jnp.float32) + bgu
    glu = jnp.minimum(gu[:, :d_ff], SWIGLU_LIMIT)
    lin = jnp.clip(gu[:, d_ff:], -SWIGLU_LIMIT, SWIGLU_LIMIT)
    act = glu * jax.nn.sigmoid(SWIGLU_ALPHA * glu) * (lin + 1.0)
    return jnp.dot(_bf(act), wd_bf[...], preferred_element_type=jnp.float32) + bd


def _experts_kernel(tile_expert_ref, n_used_ref, groups_ref, next_ref, slot_ref,
                    x_ref, wgu_hbm, bgu_ref, wd_hbm, bd_ref, y_ref,
                    wgu_f32, wd_f32, wgu_bf, wd_bf, sems):
    step = pl.program_id(0)
    groups = groups_ref[step]

    def weight_copies(expert, slot):
        return (pltpu.make_async_copy(wgu_hbm.at[expert], wgu_f32.at[slot], sems.at[slot, 0]),
                pltpu.make_async_copy(wd_hbm.at[expert], wd_f32.at[slot], sems.at[slot, 1]))

    @pl.when(groups > 0)
    def _():
        e = tile_expert_ref[step]
        prev = tile_expert_ref[jnp.maximum(step - 1, 0)]
        slot = slot_ref[e]

        @pl.when(step == 0)
        def _():
            for cp in weight_copies(e, slot):
                cp.start()

        @pl.when(jnp.logical_or(step == 0, e != prev))
        def _():
            for cp in weight_copies(e, slot):
                cp.wait()
            nxt = next_ref[e]

            @pl.when(nxt >= 0)
            def _():
                for cp in weight_copies(nxt, 1 - slot):
                    cp.start()

            wgu_bf[...] = _bf(wgu_f32[slot])
            wd_bf[...] = _bf(wd_f32[slot])

    tm = x_ref.shape[0]
    for n in range(1, EXPERT_GROUPS + 1):
        @pl.when(groups == n)
        def _():
            rows = n * (tm // EXPERT_GROUPS)
            y_ref[0:rows, :] = _expert_mlp(x_ref[0:rows, :], wgu_bf, bgu_ref[0], wd_bf, bd_ref[0])
            if rows < tm:
                y_ref[rows:, :] = jnp.zeros((tm - rows, y_ref.shape[1]), y_ref.dtype)

    @pl.when(groups == 0)
    def _():
        y_ref[...] = jnp.zeros_like(y_ref)


def _experts(tile_expert, n_used, tile_groups, next_used, slot_of, x_sorted, w_gate_up, b_gate_up,
             w_down, b_down):
    p = x_sorted.shape[0]
    n_e, d, two_ff = w_gate_up.shape
    d_ff = two_ff // 2
    tm = EXPERT_TILE
    tile = lambda i, te, nu: jnp.minimum(i, nu[0] - 1)
    grid_spec = pltpu.PrefetchScalarGridSpec(
        num_scalar_prefetch=5,
        grid=(p // tm,),
        in_specs=[pl.BlockSpec((tm, d), lambda i, te, nu, *_: (tile(i, te, nu), 0)),
                  pl.BlockSpec(memory_space=pl.ANY),
                  pl.BlockSpec((1, 1, two_ff), lambda i, te, *_: (te[i], 0, 0)),
                  pl.BlockSpec(memory_space=pl.ANY),
                  pl.BlockSpec((1, 1, d), lambda i, te, *_: (te[i], 0, 0))],
        out_specs=pl.BlockSpec((tm, d), lambda i, *_: (i, 0)),
        scratch_shapes=[pltpu.VMEM((2, d, two_ff), jnp.float32),
                        pltpu.VMEM((2, d_ff, d), jnp.float32),
                        pltpu.VMEM((d, two_ff), jnp.bfloat16),
                        pltpu.VMEM((d_ff, d), jnp.bfloat16),
                        pltpu.SemaphoreType.DMA((2, 2))],
    )
    return pl.pallas_call(
        _experts_kernel,
        grid_spec=grid_spec,
        out_shape=jax.ShapeDtypeStruct(x_sorted.shape, jnp.float32),
        compiler_params=pltpu.CompilerParams(
            dimension_semantics=("arbitrary",), vmem_limit_bytes=VMEM_LIMIT_BYTES),
    )(tile_expert, n_used, tile_groups, next_used, slot_of, x_sorted, w_gate_up,
      b_gate_up.reshape(n_e, 1, two_ff), w_down, b_down.reshape(n_e, 1, d))


def _combine_kernel(cnt_ref, gstart_ref, cb_ref, blkrows_ref,
                    topi_p_ref, topi_s_ref, topw_p_ref, topw_s_ref, x1_p_ref, x1_s_ref,
                    gate_p_ref, gate_s_ref, cbrow_ref, y_ref, out_p_ref, out_s_ref,
                    ybuf, sems, *, nblk_p):
    step = pl.program_id(0)
    nblk = pl.num_programs(0)
    tb = x1_p_ref.shape[0]
    slot = step % 2

    def fetch(blk, into):
        def issue(e, carry):
            i = blk * N_EXPERTS + e
            _chunk_copies(cnt_ref[i], y_ref, gstart_ref[i], ybuf.at[into], cb_ref[i], sems.at[into])
            return carry
        lax.fori_loop(0, N_EXPERTS, issue, 0)

    @pl.when(step == 0)
    def _():
        ybuf[...] = jnp.zeros_like(ybuf)
        fetch(0, 0)

    @pl.when(step + 1 < nblk)
    def _():
        fetch(step + 1, 1 - slot)

    _wait_rows(y_ref, blkrows_ref[step], sems.at[slot])

    is_p = step < nblk_p
    topi = jnp.where(is_p, topi_p_ref[...], topi_s_ref[...])
    topw = jnp.where(is_p, topw_p_ref[...], topw_s_ref[...])
    cbrow = cbrow_ref[0]

    lane = lax.broadcasted_iota(jnp.int32, (tb, LANES), 1)
    picks = []
    onehot = jnp.zeros((tb, LANES), jnp.float32)
    for k in range(TOP_K):
        ix = jnp.sum(jnp.where(lane == k, topi, 0), axis=-1, keepdims=True)
        picks.append(lane == ix)
        onehot = jnp.where(picks[k], 1.0, onehot)
    r = lax.broadcasted_iota(jnp.int32, (tb, tb), 0)
    c = lax.broadcasted_iota(jnp.int32, (tb, tb), 1)
    earlier = jnp.where(c < r, 1.0, 0.0)
    base = _dot(earlier, onehot) + cbrow.astype(jnp.float32)
    cols = [jnp.sum(jnp.where(picks[k], base, 0.0), axis=-1, keepdims=True).astype(jnp.int32)
            for k in range(TOP_K)]
    wks = [jnp.sum(jnp.where(lane == k, topw, 0.0), axis=-1, keepdims=True) for k in range(TOP_K)]
    moe = None
    for c0 in range(0, ybuf.shape[1], COMBINE_SLAB):
        colid = lax.broadcasted_iota(jnp.int32, (tb, COMBINE_SLAB), 1) + c0
        weights = jnp.zeros((tb, COMBINE_SLAB), jnp.float32)
        for k in range(TOP_K):
            weights = jnp.where(colid == cols[k], wks[k], weights)
        part = _dot(weights, ybuf[slot, c0:c0 + COMBINE_SLAB, :])
        moe = part if moe is None else moe + part

    @pl.when(is_p)
    def _():
        out_p_ref[...] = x1_p_ref[...] + gate_p_ref[...] * moe

    @pl.when(jnp.logical_not(is_p))
    def _():
        gate_s = _repeat_rows(gate_s_ref[...], tb // gate_s_ref.shape[0])
        out_s_ref[...] = x1_s_ref[...] + gate_s * moe


def _combine(tables, topi_p, topi_s, topw_p, topw_s, x1_p, x1_s, gate_p, mod_seq, dec_seq, cbrow,
             y_sorted):
    n_p, d = x1_p.shape
    n_s = x1_s.shape[0]
    tb = ROUTE_BLOCK
    nblk_p, nblk_s = n_p // tb, n_s // tb
    pidx = lambda i, *_: jnp.minimum(i, nblk_p - 1)
    sidx = lambda i, *_: jnp.maximum(i - nblk_p, 0)
    prow = lambda w: pl.BlockSpec((tb, w), lambda i, *_: (pidx(i), 0))
    srow = lambda w: pl.BlockSpec((tb, w), lambda i, *_: (sidx(i), 0))
    grid_spec = pltpu.PrefetchScalarGridSpec(
        num_scalar_prefetch=4,
        grid=(nblk_p + nblk_s,),
        in_specs=[prow(LANES), srow(LANES), prow(LANES), srow(LANES), prow(d), srow(d),
                  pl.BlockSpec((1, d), lambda i, *_: (0, 0)),
                  pl.BlockSpec((tb // dec_seq, d), lambda i, *_: (sidx(i), 5)),
                  pl.BlockSpec((1, 1, LANES), lambda i, *_: (i, 0, 0)),
                  pl.BlockSpec(memory_space=pl.ANY)],
        out_specs=[prow(d), srow(d)],
        scratch_shapes=[pltpu.VMEM((2, STAGE_ROWS, d), jnp.float32),
                        pltpu.SemaphoreType.DMA((2,))],
    )
    return pl.pallas_call(
        functools.partial(_combine_kernel, nblk_p=nblk_p),
        grid_spec=grid_spec,
        out_shape=[jax.ShapeDtypeStruct((n_p, d), jnp.float32),
                   jax.ShapeDtypeStruct((n_s, d), jnp.float32)],
        compiler_params=pltpu.CompilerParams(
            dimension_semantics=("arbitrary",), vmem_limit_bytes=VMEM_LIMIT_BYTES),
    )(*tables, topi_p, topi_s, topw_p, topw_s, x1_p, x1_s, gate_p, mod_seq, cbrow, y_sorted)


def _rotary_tables(pos, inv_freq):
    ang = pos.astype(jnp.float32)[:, None] * inv_freq[None, :]
    cos, sin = jnp.cos(ang), jnp.sin(ang)
    cos_t = jnp.tile(cos, (1, LANES // cos.shape[1]))
    sin_t = jnp.tile(jnp.concatenate([-sin, sin], axis=1), (1, LANES // (2 * sin.shape[1])))
    return cos_t, sin_t


def _rotary_split_tables(block, nblk, inv_freq):
    half = inv_freq.shape[0]
    freq = jnp.tile(inv_freq, LANES // half)
    sign = jnp.tile(jnp.concatenate([-jnp.ones((half,), jnp.float32),
                                     jnp.ones((half,), jnp.float32)]), LANES // (2 * half))
    base = (jnp.arange(nblk, dtype=jnp.int32) * block).astype(jnp.float32)[:, None] * freq[None, :]
    off = jnp.arange(block, dtype=jnp.int32).astype(jnp.float32)[:, None] * freq[None, :]
    blk = jnp.stack([jnp.cos(base), jnp.sin(base)], axis=1)
    tab = jnp.stack([jnp.cos(off), jnp.sin(off), jnp.cos(off) * sign, jnp.sin(off) * sign])
    return blk, tab


def _retention_tables(chunk):
    log_gamma = jnp.log1p(-jnp.exp2(-5.0 - jnp.arange(RET_HEADS, dtype=jnp.float32)))
    idx = jnp.arange(chunk, dtype=jnp.float32)
    diff = idx[:, None] - idx[None, :]
    decay = jnp.where(diff[None] >= 0,
                      jnp.exp(jnp.maximum(diff, 0.0)[None] * log_gamma[:, None, None]), 0.0)
    q_decay = jnp.exp((idx + 1.0)[:, None] * log_gamma[None, :])
    k_decay = jnp.exp((chunk - 1.0 - idx)[:, None] * log_gamma[None, :])
    qdec = jnp.repeat(q_decay, RET_DV, axis=1)
    kdec = jnp.repeat(k_decay, RET_DK, axis=1)
    g_chunk = jnp.exp(chunk * log_gamma)
    gpow = jnp.repeat(g_chunk.reshape(RET_HEADS // 2, 2), RET_DK, axis=1).T
    return decay, qdec, kdec, gpow


def _block_diag_mean(width):
    head = jnp.arange(width) // HEAD_DIM
    return jnp.where(head[:, None] == head[None, :], 1.0 / HEAD_DIM, 0.0).astype(jnp.bfloat16)


def kernel(x_prompt, x_sample, cache_k, cache_v, state_ret, c_prompt, c_sample, w_ada, b_ada,
           g_mix, w_in, q_gain, k_gain, sinks, w_out, g_ffn, w_router, b_router, w_gate_up,
           b_gate_up, w_down, b_down):
    depth = w_ada.shape[0]
    batch, seq, d = x_prompt.shape
    dec_batch, dec_seq, _ = x_sample.shape
    cache_w = cache_k.shape[2]
    assert batch == 1 and depth == 1
    assert seq % PROMPT_BLOCK == 0 and PROMPT_BLOCK % WINDOW == 0 and WINDOW == RET_CHUNK
    assert dec_batch % SAMPLE_SEQS == 0 and cache_w == WINDOW
    n_p, n_s = batch * seq, dec_batch * dec_seq
    n_tok = n_p + n_s
    assert n_p % ROUTE_BLOCK == 0 and n_s % ROUTE_BLOCK == 0
    assert ROUTE_BLOCK % (SAMPLE_SEQS * dec_seq) == 0
    assert STAGE_ROWS % COMBINE_SLAB == 0 and EXPERT_TILE % (EXPERT_GROUPS * SUBLANES) == 0

    l = 0
    f32 = jnp.float32
    rope_freq = 1.0 / (ROPE_THETA ** (jnp.arange(0, HEAD_DIM, 2, dtype=f32) / HEAD_DIM))
    ret_freq = 1.0 / (ROPE_THETA ** jnp.linspace(0.0, 1.0, RET_DK // 2, dtype=f32))
    pos_s = PAST_LEN + jnp.arange(dec_seq, dtype=jnp.int32)
    blk_a, tab_a = _rotary_split_tables(PROMPT_BLOCK, seq // PROMPT_BLOCK, rope_freq)
    blk_r, tab_r = _rotary_split_tables(PROMPT_BLOCK, seq // PROMPT_BLOCK, ret_freq)
    tabs_p = (jnp.concatenate([blk_a, blk_r], axis=1), jnp.concatenate([tab_a, tab_r], axis=0))
    tabs_s = tuple(jnp.tile(a, (SAMPLE_SEQS, 1))
                   for a in _rotary_tables(pos_s, rope_freq) + _rotary_tables(pos_s, ret_freq))
    ret_p = _retention_tables(RET_CHUNK)
    decay_s, qdec_s, kdec_s, gpow_s = _retention_tables(dec_seq)
    eye = jnp.eye(SAMPLE_SEQS, dtype=f32)
    rows_s = SAMPLE_SEQS * dec_seq
    ret_s = ((eye[None, :, None, :, None] * decay_s[:, None, :, None, :]).reshape(RET_HEADS, rows_s, rows_s),
             jnp.tile(qdec_s, (SAMPLE_SEQS, 1)), jnp.tile(kdec_s, (SAMPLE_SEQS, 1)), gpow_s)

    n_c = batch + dec_batch
    c_rows = -(-n_c // SUBLANES) * SUBLANES
    c_all = jnp.concatenate([c_sample, c_prompt, jnp.zeros((c_rows - n_c, d), f32)], axis=0)
    mod = _adaln(c_all, w_ada[l], b_ada[l])
    mod_p = mod[dec_batch].reshape(6, d)

    w_in_bf = _bf(w_in[l])
    w_out_bf = _bf(w_out[l])
    qg = jnp.tile(q_gain[l], ATT_HEADS).reshape(1, ATT_W)
    kg = jnp.tile(k_gain[l], ATT_KV_HEADS).reshape(1, KV_W)
    bdq, bdk = _block_diag_mean(ATT_W), _block_diag_mean(KV_W)
    gmix = g_mix[l].reshape(1, d)
    gffn = g_ffn[l].reshape(1, d)
    wr = jnp.pad(w_router[l], ((0, 0), (0, LANES - N_EXPERTS)))
    wr_hi = _bf(wr)
    wr_cat = jnp.concatenate([wr_hi, _bf(wr - wr_hi.astype(f32))], axis=1)
    br = jnp.concatenate([b_router[l], jnp.full((LANES - N_EXPERTS,), NEG_BIG, f32)]).reshape(1, LANES)
    shared = (sinks[l], gmix, gffn, w_in_bf, qg, kg, bdq, bdk)

    (x1_p, h2_p, topi_p, topw_p, topit_p, cnt_p, kwin_p, vwin_p, rend_p) = _prompt_mix(
        x_prompt.reshape(n_p, d), mod_p, *shared, tabs_p, ret_p, w_out_bf, wr_hi, wr_cat, br)

    (x1_s, h2_s, topi_s, topw_s, topit_s, cnt_s, kwin_s, vwin_s, rend_s) = _sample_mix(
        x_sample.reshape(n_s, d), mod, *shared, tabs_s, ret_s, w_out_bf, wr_hi, wr_cat, br,
        cache_k[l].reshape(dec_batch, cache_w, KV_W), cache_v[l].reshape(dec_batch, cache_w, KV_W),
        state_ret[l].reshape(dec_batch, 2, LANES, RET_DV))

    tb, tm = ROUTE_BLOCK, EXPERT_TILE
    i32 = jnp.int32
    cnt_s = cnt_s.reshape(n_s // tb, -1, LANES).sum(axis=1)
    cnt = jnp.concatenate([cnt_p.reshape(-1, LANES), cnt_s], axis=0)
    cnt8 = -(-cnt // SUBLANES) * SUBLANES
    rs = jnp.cumsum(cnt8, axis=0) - cnt8
    tot = jnp.sum(cnt8, axis=0)[:N_EXPERTS]
    padded = -(-tot // tm) * tm
    ends = jnp.cumsum(padded)
    gstart = (ends - padded)[None, :] + rs[:, :N_EXPERTS]
    cb = jnp.cumsum(cnt8, axis=1) - cnt8
    blk_rows = jnp.sum(cnt8, axis=1).astype(i32)
    max_rows = n_tok * TOP_K + cnt.shape[0] * N_EXPERTS * (SUBLANES - 1)
    n_rows = -(-max_rows // tm) * tm + N_EXPERTS * tm
    tile_start = jnp.arange(n_rows // tm, dtype=i32) * tm
    tile_expert = jnp.minimum(jnp.sum(ends[None, :] <= tile_start[:, None], axis=1),
                              N_EXPERTS - 1).astype(i32)
    n_used = (ends[-1:] // tm).astype(i32)
    eid = jnp.arange(N_EXPERTS, dtype=i32)
    of_tile = lambda v: jnp.sum(jnp.where(tile_expert[:, None] == eid[None, :], v[None, :], 0), axis=1)
    tile_rows = jnp.clip(of_tile(tot) - (tile_start - of_tile(ends - padded)), 0, tm)
    tile_rows = jnp.where(tile_start < ends[-1], tile_rows, 0)
    tile_groups = (-(-tile_rows // (tm // EXPERT_GROUPS))).astype(i32)
    flat = lambda a: a[:, :N_EXPERTS].reshape(-1).astype(i32)
    tables = (flat(cnt8), gstart.reshape(-1).astype(i32), flat(cb), blk_rows)

    pad_cnt = padded - tot
    pad_cnt = jnp.concatenate([pad_cnt, jnp.sum(pad_cnt, keepdims=True)]).astype(i32)
    x_sorted = _dispatch(tables + (ends.astype(i32), pad_cnt), topit_p, topit_s, h2_p, h2_s,
                         cb.astype(i32)[:, :, None], n_rows)
    used = tot > 0
    later = jnp.where(used[None, :] & (eid[None, :] > eid[:, None]), eid[None, :], N_EXPERTS)
    next_used = jnp.min(later, axis=1)
    next_used = jnp.where(next_used == N_EXPERTS, -1, next_used).astype(i32)
    slot_of = ((jnp.cumsum(used.astype(i32)) - 1) % 2).astype(i32)
    y_sorted = _experts(tile_expert, n_used, tile_groups, next_used, slot_of, x_sorted, w_gate_up[l], b_gate_up[l], w_down[l],
                        b_down[l])
    y_p, y_s = _combine(tables, topi_p, topi_s, topw_p, topw_s, x1_p, x1_s, mod_p[5:6], mod,
                        dec_seq, cb.astype(i32)[:, None, :], y_sorted)

    kv5 = lambda a, n: a.reshape(1, n, cache_w, ATT_KV_HEADS, HEAD_DIM)
    st5 = lambda a, n: a.reshape(1, n, RET_HEADS, RET_DK, RET_DV)
    return (y_p.reshape(batch, seq, d), y_s.reshape(dec_batch, dec_seq, d),
            kv5(kwin_p, batch), kv5(vwin_p, batch), st5(rend_p, batch),
            kv5(kwin_s, dec_batch), kv5(vwin_s, dec_batch), st5(rend_s, dec_batch))
```

```python
import functools

import jax
import jax.numpy as jnp
import numpy as np
from jax import lax
from jax.experimental import pallas as pl
from jax.experimental.pallas import tpu as pltpu

HEAD_DIM = 64
ATT_HEADS = 8
ATT_KV_HEADS = 2
ATT_W = ATT_HEADS * HEAD_DIM
KV_W = ATT_KV_HEADS * HEAD_DIM
WINDOW = 128
PAST_LEN = 16384
ROPE_THETA = 10000.0
RET_HEADS = 4
RET_DK = 64
RET_DV = 128
RET_QK_W = RET_HEADS * RET_DK
RET_W = RET_HEADS * RET_DV
RET_CHUNK = 128
N_EXPERTS = 32
TOP_K = 4
SWIGLU_LIMIT = 7.0
SWIGLU_ALPHA = 1.702
EPS = 1e-6

_Q0, _K0, _V0 = 0, ATT_W, ATT_W + KV_W
_RQ0 = ATT_W + 2 * KV_W
_RK0 = _RQ0 + RET_QK_W
_RV0 = _RK0 + RET_QK_W
_RG0 = _RV0 + RET_W
IN_W = _RG0 + RET_W

LANES = 128
SUBLANES = 8
VMEM_LIMIT_BYTES = 56 * 1024 * 1024
NEG_BIG = -1e30

PROMPT_BLOCK = 512
SAMPLE_SEQS = 16
ROUTE_BLOCK = 512
EXPERT_TILE = 512
EXPERT_GROUPS = 4
CHUNK_BITS = ROUTE_BLOCK.bit_length()
STAGE_ROWS = -(-(TOP_K * ROUTE_BLOCK + N_EXPERTS * (SUBLANES - 1)) // LANES) * LANES
STAGE_BITS = STAGE_ROWS.bit_length()
STAGE_SLOTS = 3
COMBINE_SLAB = 256

_NT = (((1,), (1,)), ((), ()))
_TN = (((0,), (0,)), ((), ()))


def _bf(x):
    return x.astype(jnp.bfloat16)


def _dot(a, b):
    return jnp.dot(_bf(a), _bf(b), preferred_element_type=jnp.float32)


def _dot_nt(a, b):
    return lax.dot_general(_bf(a), _bf(b), _NT, preferred_element_type=jnp.float32)


def _dot_tn(a, b):
    return lax.dot_general(_bf(a), _bf(b), _TN, preferred_element_type=jnp.float32)


def _lane_lo(shape):
    lane = lax.broadcasted_iota(jnp.int32, shape, len(shape) - 1)
    return (lane % LANES) < HEAD_DIM


def _swap_halves(x):
    lane = lax.broadcasted_iota(jnp.int32, x.shape, 1)
    first = (lane % HEAD_DIM) < (HEAD_DIM // 2)
    return jnp.where(first, pltpu.roll(x, LANES - HEAD_DIM // 2, axis=1),
                     pltpu.roll(x, HEAD_DIM // 2, axis=1))


def _rotate(x, cos, sin_signed):
    outs = []
    for j in range(x.shape[1] // LANES):
        xs = x[:, j * LANES:(j + 1) * LANES]
        outs.append(xs * cos + _swap_halves(xs) * sin_signed)
    return outs[0] if len(outs) == 1 else jnp.concatenate(outs, axis=1)


def _repeat_rows(m, n):
    r, w = m.shape
    return jnp.broadcast_to(m[:, None, :], (r, n, w)).reshape(r * n, w)


def _block_rotary(blk_ref, tab_ref, which):
    ca, sa = blk_ref[0, 2 * which:2 * which + 1, :], blk_ref[0, 2 * which + 1:2 * which + 2, :]
    cb, sb, cbs, sbs = (tab_ref[4 * which + n] for n in range(4))
    return ca * cb - sa * sb, sa * cbs + ca * sbs


def _rms_rows(x):
    return x * lax.rsqrt(jnp.mean(x * x, axis=-1, keepdims=True) + EPS)


def _pre_norm(x, gain, shift, scale):
    return _rms_rows(x) * gain * (1.0 + scale) + shift


def _head_norm(x, gain_tiled, blockdiag):
    ms = jnp.dot(_bf(x * x), blockdiag, preferred_element_type=jnp.float32)
    return x * lax.rsqrt(ms + EPS) * gain_tiled


def _project(h, w_in_ref, qg, kg, bd_q, bd_k, cos_a, sin_a, cos_r, sin_r):
    z = jnp.dot(_bf(h), w_in_ref[...], preferred_element_type=jnp.float32)
    q = _rotate(_head_norm(z[:, _Q0:_K0], qg, bd_q), cos_a, sin_a) * (HEAD_DIM ** -0.5)
    k = _rotate(_head_norm(z[:, _K0:_V0], kg, bd_k), cos_a, sin_a)
    v = z[:, _V0:_RQ0]
    rq = _rotate(z[:, _RQ0:_RK0], cos_r, sin_r)
    rk = _rotate(z[:, _RK0:_RV0], cos_r, sin_r) * (RET_DK ** -0.5)
    rv = z[:, _RV0:_RG0]
    rg = z[:, _RG0:IN_W]
    return q, k, v, rq, rk, rv, rg


_NAT_HEADS = (0, 2, 5, 7)
_ROL_HEADS = (1, 3, 4, 6)


def _stack_heads(q, heads):
    lo = _lane_lo((q.shape[0], LANES))
    parts = []
    for j in heads:
        slab = q[:, (j // 2) * LANES:(j // 2 + 1) * LANES]
        parts.append(jnp.where(lo if j % 2 == 0 else ~lo, slab, 0.0))
    return jnp.concatenate(parts, axis=0)


def _sink_column(sinks_ref, heads, rows):
    r = lax.broadcasted_iota(jnp.int32, (len(heads) * rows, 1), 0)
    col = jnp.full((len(heads) * rows, 1), sinks_ref[heads[-1]], jnp.float32)
    for n in range(len(heads) - 2, -1, -1):
        col = jnp.where(r < (n + 1) * rows, sinks_ref[heads[n]], col)
    return col


def _softmax_pv(s, mask, sink, v):
    s = jnp.where(mask, s, NEG_BIG)
    m = jnp.maximum(jnp.max(s, axis=-1, keepdims=True), sink)
    p = jnp.exp(s - m)
    denom = jnp.sum(p, axis=-1, keepdims=True) + jnp.exp(sink - m)
    return _dot(p, v) / denom


def _attention(q, k_all, v_all, mask, sinks_ref):
    rows = q.shape[0]
    k_rol = pltpu.roll(k_all, HEAD_DIM, axis=1)
    v_rol = pltpu.roll(v_all, HEAD_DIM, axis=1)
    mask4 = jnp.concatenate([mask] * 4, axis=0)
    o_nat = _softmax_pv(_dot_nt(_stack_heads(q, _NAT_HEADS), k_all), mask4,
                        _sink_column(sinks_ref, _NAT_HEADS, rows), v_all)
    o_rol = _softmax_pv(_dot_nt(_stack_heads(q, _ROL_HEADS), k_rol), mask4,
                        _sink_column(sinks_ref, _ROL_HEADS, rows), v_rol)
    lo = _lane_lo((rows, LANES))
    blk = lambda o, n: o[n * rows:(n + 1) * rows]
    return jnp.concatenate([
        jnp.where(lo, blk(o_nat, 0), blk(o_rol, 0)),
        jnp.where(lo, blk(o_nat, 1), blk(o_rol, 1)),
        jnp.where(lo, blk(o_rol, 2), blk(o_nat, 2)),
        jnp.where(lo, blk(o_rol, 3), blk(o_nat, 3)),
    ], axis=1)


def _retention(rq, rk, rv, rg, state, decay_ref, qdec, kdec, gpow):
    rows = rq.shape[0]
    lo = _lane_lo((rows, LANES))
    rkd = rk * kdec
    outs, new_state = [], []
    for m in range(RET_HEADS // 2):
        sl = slice(m * LANES, (m + 1) * LANES)
        upd = state[m] * gpow[:, m:m + 1]
        for p in range(2):
            h = 2 * m + p
            half = lo if p == 0 else ~lo
            rqm = jnp.where(half, rq[:, sl], 0.0)
            rv_h = rv[:, h * RET_DV:(h + 1) * RET_DV]
            s = _dot_nt(rqm, rk[:, sl]) * decay_ref[h]
            ret = _dot(s, rv_h) + _dot(rqm, state[m]) * qdec[:, h * RET_DV:(h + 1) * RET_DV]
            upd = upd + _dot_tn(jnp.where(half, rkd[:, sl], 0.0), rv_h)
            rg_h = rg[:, h * RET_DV:(h + 1) * RET_DV]
            outs.append(rg_h * jax.nn.sigmoid(rg_h) * _rms_rows(ret))
        new_state.append(upd)
    return jnp.concatenate(outs, axis=1), new_state


def _route(h2, wr_hi_ref, wr_cat_ref, br):
    h_hi = _bf(h2)
    h_lo = _bf(h2 - h_hi.astype(jnp.float32))
    both = jnp.dot(h_hi, wr_cat_ref[...], preferred_element_type=jnp.float32)
    logits = (both[:, :LANES] + both[:, LANES:]
              + jnp.dot(h_lo, wr_hi_ref[...], preferred_element_type=jnp.float32)) + br
    lane = lax.broadcasted_iota(jnp.int32, logits.shape, 1)
    idx_out = jnp.zeros(logits.shape, jnp.int32)
    val_out = jnp.zeros(logits.shape, jnp.float32)
    onehot = jnp.zeros(logits.shape, jnp.float32)
    vals = []
    work = logits
    for k in range(TOP_K):
        mx = jnp.max(work, axis=-1, keepdims=True)
        ix = jnp.min(jnp.where(work == mx, lane, LANES), axis=-1, keepdims=True)
        work = jnp.where(lane == ix, NEG_BIG, work)
        onehot = jnp.where(lane == ix, 1.0, onehot)
        idx_out = jnp.where(lane == k, ix, idx_out)
        vals.append(mx)
    exps = [jnp.exp(v - vals[0]) for v in vals]
    total = exps[0] + exps[1] + exps[2] + exps[3]
    for k in range(TOP_K):
        val_out = jnp.where(lane == k, exps[k] / total, val_out)
    idx_t = idx_out.astype(jnp.float32).T[0:SUBLANES].astype(jnp.int32)
    counts = jnp.sum(onehot, axis=0, keepdims=True).astype(jnp.int32)
    return idx_out, val_out, idx_t, counts


def _adaln_kernel(c_ref, w_ref, b_ref, o_ref):
    c = c_ref[...]
    o_ref[...] = _dot(c * jax.nn.sigmoid(c), w_ref[...]) + b_ref[...]


def _adaln(c_all, w_ada, b_ada):
    rows, d = c_all.shape
    n = w_ada.shape[1]
    bn = 1536
    return pl.pallas_call(
        _adaln_kernel,
        grid=(n // bn,),
        in_specs=[pl.BlockSpec((rows, d), lambda j: (0, 0)),
                  pl.BlockSpec((d, bn), lambda j: (0, j)),
                  pl.BlockSpec((1, bn), lambda j: (0, j))],
        out_specs=pl.BlockSpec((rows, bn), lambda j: (0, j)),
        out_shape=jax.ShapeDtypeStruct((rows, n), jnp.float32),
        compiler_params=pltpu.CompilerParams(vmem_limit_bytes=VMEM_LIMIT_BYTES),
    )(c_all, w_ada, b_ada.reshape(1, n))


def _prompt_mix_kernel(sinks_ref, x_ref, mod_ref, gmix_ref, gffn_ref, w_in_ref, qg_ref, kg_ref,
                       bdq_ref, bdk_ref, rotblk_ref, rottab_ref,
                       decay_ref, qdec_ref, kdec_ref, gpow_ref, w_out_ref,
                       wrh_ref, wrl_ref, br_ref,
                       x1_ref, h2_ref, topi_ref, topw_ref, topit_ref, cnt_ref, kwin_ref, vwin_ref, rend_ref,
                       kprev, vprev, state, mix):
    step = pl.program_id(0)
    tb = x_ref.shape[0]

    @pl.when(step == 0)
    def _():
        kprev[...] = jnp.zeros_like(kprev)
        vprev[...] = jnp.zeros_like(vprev)
        state[...] = jnp.zeros_like(state)

    x = x_ref[...]
    h = _pre_norm(x, gmix_ref[...], mod_ref[0:1, :], mod_ref[1:2, :])
    q, k, v, rq, rk, rv, rg = _project(
        h, w_in_ref, qg_ref[...], kg_ref[...], bdq_ref[...], bdk_ref[...],
        *_block_rotary(rotblk_ref, rottab_ref, 0), *_block_rotary(rotblk_ref, rottab_ref, 1))

    qi = lax.broadcasted_iota(jnp.int32, (WINDOW, 2 * WINDOW), 0)
    ci = lax.broadcasted_iota(jnp.int32, (WINDOW, 2 * WINDOW), 1)
    band = (ci > qi) & (ci <= qi + WINDOW)
    qdec, kdec, gpow = qdec_ref[...], kdec_ref[...], gpow_ref[...]

    n_sb = tb // WINDOW
    rows = [slice(sb * WINDOW, (sb + 1) * WINDOW) for sb in range(n_sb)]
    k_prev, v_prev = kprev[...], vprev[...]
    for sb, rs in enumerate(rows):
        k_sb, v_sb = k[rs], v[rs]
        mask = band & ((ci >= WINDOW) | (step > 0)) if sb == 0 else band
        att = _attention(q[rs], jnp.concatenate([k_prev, k_sb], axis=0),
                         jnp.concatenate([v_prev, v_sb], axis=0), mask, sinks_ref)
        mix[rs, 0:ATT_W] = _bf(att)
        k_prev, v_prev = k_sb, v_sb
    kprev[...] = k_prev
    vprev[...] = v_prev

    cur_state = [state[0], state[1]]
    for rs in rows:
        yret, cur_state = _retention(rq[rs], rk[rs], rv[rs], rg[rs], cur_state,
                                     decay_ref, qdec, kdec, gpow)
        mix[rs, ATT_W:ATT_W + RET_W] = _bf(yret)
    state[0] = cur_state[0]
    state[1] = cur_state[1]

    x1 = x + mod_ref[2:3, :] * jnp.dot(mix[...], w_out_ref[...],
                                       preferred_element_type=jnp.float32)
    x1_ref[...] = x1
    h2 = _pre_norm(x1, gffn_ref[...], mod_ref[3:4, :], mod_ref[4:5, :])
    h2_ref[...] = h2
    topi, topw, topi_t, counts = _route(h2, wrh_ref, wrl_ref, br_ref[...])
    topi_ref[...] = topi
    topw_ref[...] = topw
    topit_ref[...] = topi_t
    cnt_ref[0] = counts

    @pl.when(step == pl.num_programs(0) - 1)
    def _():
        kwin_ref[...] = kprev[...]
        vwin_ref[...] = vprev[...]
        rend_ref[...] = state[...]


def _const_spec(shape):
    nd = len(shape)
    return pl.BlockSpec(shape, lambda *_: (0,) * nd)


def _prompt_mix(x, mod, sinks, gmix, gffn, w_in, qg, kg, bdq, bdk, tabs, ret_tabs, w_out,
                wrh, wrl, br):
    t, d = x.shape
    tb = PROMPT_BLOCK
    rot_blk, rot_tab = tabs
    decay, qdec, kdec, gpow = ret_tabs
    row_spec = lambda w: pl.BlockSpec((tb, w), lambda i, *_: (i, 0))
    consts = [mod, gmix, gffn, w_in, qg, kg, bdq, bdk]
    consts2 = [rot_tab, decay, qdec, kdec, gpow, w_out, wrh, wrl, br]
    grid_spec = pltpu.PrefetchScalarGridSpec(
        num_scalar_prefetch=1,
        grid=(t // tb,),
        in_specs=([row_spec(d)] + [_const_spec(a.shape) for a in consts]
                  + [pl.BlockSpec((1,) + rot_blk.shape[1:], lambda i, *_: (i, 0, 0))]
                  + [_const_spec(a.shape) for a in consts2]),
        out_specs=[row_spec(d), row_spec(d), row_spec(LANES), row_spec(LANES),
                   pl.BlockSpec((SUBLANES, tb), lambda i, *_: (0, i)),
                   pl.BlockSpec((1, 1, LANES), lambda i, *_: (i, 0, 0)),
                   _const_spec((WINDOW, KV_W)), _const_spec((WINDOW, KV_W)),
                   _const_spec((2, LANES, RET_DV))],
        scratch_shapes=[pltpu.VMEM((WINDOW, KV_W), jnp.float32),
                        pltpu.VMEM((WINDOW, KV_W), jnp.float32),
                        pltpu.VMEM((2, LANES, RET_DV), jnp.float32),
                        pltpu.VMEM((tb, ATT_W + RET_W), jnp.bfloat16)],
    )
    return pl.pallas_call(
        _prompt_mix_kernel,
        grid_spec=grid_spec,
        out_shape=[jax.ShapeDtypeStruct((t, d), jnp.float32),
                   jax.ShapeDtypeStruct((t, d), jnp.float32),
                   jax.ShapeDtypeStruct((t, LANES), jnp.int32),
                   jax.ShapeDtypeStruct((t, LANES), jnp.float32),
                   jax.ShapeDtypeStruct((SUBLANES, t), jnp.int32),
                   jax.ShapeDtypeStruct((t // tb, 1, LANES), jnp.int32),
                   jax.ShapeDtypeStruct((WINDOW, KV_W), jnp.float32),
                   jax.ShapeDtypeStruct((WINDOW, KV_W), jnp.float32),
                   jax.ShapeDtypeStruct((2, LANES, RET_DV), jnp.float32)],
        compiler_params=pltpu.CompilerParams(
            dimension_semantics=("arbitrary",), vmem_limit_bytes=VMEM_LIMIT_BYTES),
    )(sinks, x, *consts, rot_blk, *consts2)


def _per_seq(stacked, rows, ds, s):
    return jnp.concatenate([stacked[g * rows + s * ds:g * rows + (s + 1) * ds]
                            for g in range(stacked.shape[0] // rows)], axis=0)


def _from_per_seq(parts, ds):
    groups = parts[0].shape[0] // ds
    return jnp.concatenate([p[g * ds:(g + 1) * ds] for g in range(groups) for p in parts], axis=0)


def _sample_attention(q, k_new, v_new, ck_ref, cv_ref, sinks_ref, ds):
    rows = q.shape[0]
    nseq, cache_w = ck_ref.shape[0], ck_ref.shape[1]
    tok_c = lax.broadcasted_iota(jnp.int32, (4 * rows, cache_w), 0) % rows
    col_c = lax.broadcasted_iota(jnp.int32, (4 * rows, cache_w), 1)
    delta = tok_c % ds + cache_w - col_c
    mask_c = (delta >= 0) & (delta < WINDOW)
    tok_n = lax.broadcasted_iota(jnp.int32, (4 * rows, rows), 0) % rows
    col_n = lax.broadcasted_iota(jnp.int32, (4 * rows, rows), 1)
    mask_n = (tok_n // ds == col_n // ds) & (col_n % ds <= tok_n % ds)

    outs = []
    for heads, rolled in ((_NAT_HEADS, False), (_ROL_HEADS, True)):
        arrange = (lambda a: pltpu.roll(a, HEAD_DIM, axis=1)) if rolled else (lambda a: a)
        qst = _stack_heads(q, heads)
        s_new = jnp.where(mask_n, _dot_nt(qst, arrange(k_new)), NEG_BIG)
        s_cache = _from_per_seq(
            [_dot_nt(_per_seq(qst, rows, ds, s), arrange(ck_ref[s])) for s in range(nseq)], ds)
        s_cache = jnp.where(mask_c, s_cache, NEG_BIG)
        sink = _sink_column(sinks_ref, heads, rows)
        m = jnp.maximum(jnp.maximum(jnp.max(s_cache, axis=-1, keepdims=True),
                                    jnp.max(s_new, axis=-1, keepdims=True)), sink)
        p_cache = jnp.exp(s_cache - m)
        p_new = jnp.exp(s_new - m)
        denom = (jnp.sum(p_cache, axis=-1, keepdims=True) + jnp.sum(p_new, axis=-1, keepdims=True)
                 + jnp.exp(sink - m))
        o = _from_per_seq(
            [_dot(_per_seq(p_cache, rows, ds, s), arrange(cv_ref[s])) for s in range(nseq)], ds)
        outs.append((o + _dot(p_new, arrange(v_new))) / denom)
    o_nat, o_rol = outs
    lo = _lane_lo((rows, LANES))
    blk = lambda o, n: o[n * rows:(n + 1) * rows]
    return jnp.concatenate([
        jnp.where(lo, blk(o_nat, 0), blk(o_rol, 0)),
        jnp.where(lo, blk(o_nat, 1), blk(o_rol, 1)),
        jnp.where(lo, blk(o_rol, 2), blk(o_nat, 2)),
        jnp.where(lo, blk(o_rol, 3), blk(o_nat, 3)),
    ], axis=1)


def _sample_retention(rq, rk, rv, rg, st_ref, rend_ref, decay_ref, qdec, kdec, gpow, ds):
    rows = rq.shape[0]
    nseq = st_ref.shape[0]
    lo = _lane_lo((rows, LANES))
    rkd = rk * kdec
    outs = []
    for m in range(RET_HEADS // 2):
        sl = slice(m * LANES, (m + 1) * LANES)
        heads = (2 * m, 2 * m + 1)
        rqm = [jnp.where(lo if p == 0 else ~lo, rq[:, sl], 0.0) for p in range(2)]
        rkdm = [jnp.where(lo if p == 0 else ~lo, rkd[:, sl], 0.0) for p in range(2)]
        rv_h = [rv[:, h * RET_DV:(h + 1) * RET_DV] for h in heads]
        q2 = jnp.concatenate(rqm, axis=0)
        k2 = jnp.concatenate(rkdm, axis=0)
        v2 = jnp.concatenate(rv_h, axis=0)
        inter = _from_per_seq(
            [_dot(_per_seq(q2, rows, ds, s), st_ref[s, m]) for s in range(nseq)], ds)
        for s in range(nseq):
            rend_ref[s, m] = (st_ref[s, m] * gpow[:, m:m + 1]
                              + _dot_tn(_per_seq(k2, rows, ds, s), _per_seq(v2, rows, ds, s)))
        for p, h in enumerate(heads):
            intra = _dot(_dot_nt(rqm[p], rk[:, sl]) * decay_ref[h], rv_h[p])
            ret = intra + inter[p * rows:(p + 1) * rows] * qdec[:, h * RET_DV:(h + 1) * RET_DV]
            rg_h = rg[:, h * RET_DV:(h + 1) * RET_DV]
            outs.append(rg_h * jax.nn.sigmoid(rg_h) * _rms_rows(ret))
    return jnp.concatenate(outs, axis=1)


def _sample_mix_kernel(sinks_ref, x_ref, mod_ref, gmix_ref, gffn_ref, w_in_ref, qg_ref, kg_ref,
                       bdq_ref, bdk_ref, cosa_ref, sina_ref, cosr_ref, sinr_ref,
                       decay_ref, qdec_ref, kdec_ref, gpow_ref, w_out_ref,
                       wrh_ref, wrl_ref, br_ref, ck_ref, cv_ref, st_ref,
                       x1_ref, h2_ref, topi_ref, topw_ref, topit_ref, cnt_ref, kwin_ref, vwin_ref, rend_ref,
                       mix):
    nseq, cache_w = ck_ref.shape[0], ck_ref.shape[1]
    ds = x_ref.shape[0] // nseq
    d = x_ref.shape[1]
    x = x_ref[...]
    mod = lambda i: _repeat_rows(mod_ref[:, i * d:(i + 1) * d], ds)
    h = _pre_norm(x, gmix_ref[...], mod(0), mod(1))
    q, k, v, rq, rk, rv, rg = _project(
        h, w_in_ref, qg_ref[...], kg_ref[...], bdq_ref[...], bdk_ref[...],
        cosa_ref[...], sina_ref[...], cosr_ref[...], sinr_ref[...])

    for s in range(nseq):
        rs = slice(s * ds, (s + 1) * ds)
        kwin_ref[s] = jnp.concatenate([ck_ref[s, ds:, :], k[rs]], axis=0)
        vwin_ref[s] = jnp.concatenate([cv_ref[s, ds:, :], v[rs]], axis=0)
    mix[:, 0:ATT_W] = _bf(_sample_attention(q, k, v, ck_ref, cv_ref, sinks_ref, ds))
    mix[:, ATT_W:ATT_W + RET_W] = _bf(_sample_retention(
        rq, rk, rv, rg, st_ref, rend_ref, decay_ref, qdec_ref[...], kdec_ref[...], gpow_ref[...], ds))

    x1 = x + mod(2) * jnp.dot(mix[...], w_out_ref[...], preferred_element_type=jnp.float32)
    x1_ref[...] = x1
    h2 = _pre_norm(x1, gffn_ref[...], mod(3), mod(4))
    h2_ref[...] = h2
    topi, topw, topi_t, counts = _route(h2, wrh_ref, wrl_ref, br_ref[...])
    topi_ref[...] = topi
    topw_ref[...] = topw
    topit_ref[...] = topi_t
    cnt_ref[0] = counts


def _sample_mix(x, mod_seq, sinks, gmix, gffn, w_in, qg, kg, bdq, bdk, tabs, ret_tabs, w_out,
                wrh, wrl, br, cache_k, cache_v, state):
    t, d = x.shape
    nb, cache_w = cache_k.shape[0], cache_k.shape[1]
    ds = t // nb
    g = SAMPLE_SEQS
    tb = g * ds
    cosa, sina, cosr, sinr = tabs
    decay, qdec, kdec, gpow = ret_tabs
    row_spec = lambda w: pl.BlockSpec((tb, w), lambda i, *_: (i, 0))
    seq3 = lambda a: pl.BlockSpec((g,) + a.shape[1:], lambda i, *_: (i,) + (0,) * (a.ndim - 1))
    consts = [gmix, gffn, w_in, qg, kg, bdq, bdk, cosa, sina, cosr, sinr,
              decay, qdec, kdec, gpow, w_out, wrh, wrl, br]
    grid_spec = pltpu.PrefetchScalarGridSpec(
        num_scalar_prefetch=1,
        grid=(nb // g,),
        in_specs=([row_spec(d), pl.BlockSpec((g, mod_seq.shape[1]), lambda i, *_: (i, 0))]
                  + [_const_spec(a.shape) for a in consts]
                  + [seq3(cache_k), seq3(cache_v), seq3(state)]),
        out_specs=[row_spec(d), row_spec(d), row_spec(LANES), row_spec(LANES),
                   pl.BlockSpec((SUBLANES, tb), lambda i, *_: (0, i)),
                   pl.BlockSpec((1, 1, LANES), lambda i, *_: (i, 0, 0)),
                   seq3(cache_k), seq3(cache_v), seq3(state)],
        scratch_shapes=[pltpu.VMEM((tb, ATT_W + RET_W), jnp.bfloat16)],
    )
    return pl.pallas_call(
        _sample_mix_kernel,
        grid_spec=grid_spec,
        out_shape=[jax.ShapeDtypeStruct((t, d), jnp.float32),
                   jax.ShapeDtypeStruct((t, d), jnp.float32),
                   jax.ShapeDtypeStruct((t, LANES), jnp.int32),
                   jax.ShapeDtypeStruct((t, LANES), jnp.float32),
                   jax.ShapeDtypeStruct((SUBLANES, t), jnp.int32),
                   jax.ShapeDtypeStruct((t // tb, 1, LANES), jnp.int32),
                   jax.ShapeDtypeStruct(cache_k.shape, jnp.float32),
                   jax.ShapeDtypeStruct(cache_v.shape, jnp.float32),
                   jax.ShapeDtypeStruct(state.shape, jnp.float32)],
        compiler_params=pltpu.CompilerParams(
            dimension_semantics=("arbitrary",), vmem_limit_bytes=VMEM_LIMIT_BYTES),
    )(sinks, x, mod_seq, *consts, cache_k, cache_v, state)


def _chunk_copies(cnt8, src_ref, src_start, dst_ref, dst_start, sem):
    for bit in range(CHUNK_BITS - 1, SUBLANES.bit_length() - 2, -1):
        size = 1 << bit
        done = (cnt8 >> (bit + 1)) << (bit + 1)

        @pl.when(((cnt8 >> bit) & 1) == 1)
        def _():
            src = pl.multiple_of(src_start + done, SUBLANES)
            dst = pl.multiple_of(dst_start + done, SUBLANES)
            pltpu.make_async_copy(src_ref.at[pl.ds(src, size)],
                                  dst_ref.at[pl.ds(dst, size)], sem).start()


def _wait_rows(ref, rows8, sem, bits=STAGE_BITS):
    for bit in range(bits - 1, SUBLANES.bit_length() - 2, -1):
        size = 1 << bit

        @pl.when(((rows8 >> bit) & 1) == 1)
        def _():
            pltpu.make_async_copy(ref.at[pl.ds(0, size)], ref.at[pl.ds(0, size)], sem).wait()


def _dispatch_kernel(cnt_ref, gstart_ref, cb_ref, blkrows_ref, ends_ref, padcnt_ref,
                     topit_p_ref, topit_s_ref, h2_p_ref, h2_s_ref, cbcol_ref, xs_ref,
                     stage, zeros, sems, zsem, *, nblk_p):
    step = pl.program_id(0)
    nblk = pl.num_programs(0)
    tb = h2_p_ref.shape[0]
    slot = step % STAGE_SLOTS

    tile = zeros.shape[0]
    first_unused = ends_ref[N_EXPERTS - 1] // tile
    n_tiles = xs_ref.shape[0] // tile

    def tail_fill(t):
        return pltpu.make_async_copy(zeros, xs_ref.at[pl.ds(pl.multiple_of(t * tile, tile), tile)],
                                     zsem)

    @pl.when(step == 0)
    def _():
        zeros[...] = jnp.zeros_like(zeros)

        def pad_fill(e, carry):
            _chunk_copies(padcnt_ref[e], zeros, 0, xs_ref, ends_ref[e] - padcnt_ref[e], zsem)
            return carry

        lax.fori_loop(0, N_EXPERTS, pad_fill, 0)
        lax.fori_loop(first_unused, n_tiles, lambda t, c: (tail_fill(t).start(), c)[1], 0)

    @pl.when(step == nblk - 1)
    def _():
        _wait_rows(xs_ref, padcnt_ref[N_EXPERTS], zsem, bits=(N_EXPERTS * tile).bit_length())
        lax.fori_loop(first_unused, n_tiles, lambda t, c: (tail_fill(t).wait(), c)[1], 0)

    is_p = step < nblk_p
    topit = jnp.where(is_p, topit_p_ref[...], topit_s_ref[...])
    h2 = jnp.where(is_p, h2_p_ref[...], h2_s_ref[...])
    cbcol = cbcol_ref[0]

    sub = lax.broadcasted_iota(jnp.int32, (LANES, tb), 0)
    picks = [sub == topit[k:k + 1, :] for k in range(TOP_K)]
    onehot = jnp.zeros((LANES, tb), jnp.float32)
    for pk in picks:
        onehot = jnp.where(pk, 1.0, onehot)
    r = lax.broadcasted_iota(jnp.int32, (tb, tb), 0)
    c = lax.broadcasted_iota(jnp.int32, (tb, tb), 1)
    earlier = jnp.where(r < c, 1.0, 0.0)
    before = _dot(onehot, earlier)
    base = before + cbcol.astype(jnp.float32)
    row = lax.broadcasted_iota(jnp.int32, (stage.shape[1], tb), 0)
    sel = jnp.zeros((stage.shape[1], tb), jnp.float32)
    for pk in picks:
        col = jnp.sum(jnp.where(pk, base, 0.0), axis=0, keepdims=True).astype(jnp.int32)
        sel = jnp.where(row == col, 1.0, sel)

    stage[slot] = _dot(sel, h2)

    def issue(e, carry):
        i = step * N_EXPERTS + e
        _chunk_copies(cnt_ref[i], stage.at[slot], cb_ref[i], xs_ref, gstart_ref[i], sems.at[slot])
        return carry

    lax.fori_loop(0, N_EXPERTS, issue, 0)

    oldest = step - (STAGE_SLOTS - 1)

    @pl.when(oldest >= 0)
    def _():
        _wait_rows(xs_ref, blkrows_ref[jnp.maximum(oldest, 0)], sems.at[(step + 1) % STAGE_SLOTS])

    @pl.when(step == nblk - 1)
    def _():
        for back in range(STAGE_SLOTS - 2, -1, -1):
            @pl.when(step - back >= 0)
            def _():
                blk = jnp.maximum(step - back, 0)
                _wait_rows(xs_ref, blkrows_ref[blk], sems.at[blk % STAGE_SLOTS])


def _dispatch(tables, topit_p, topit_s, h2_p, h2_s, cbcol, n_rows):
    n_p, d = h2_p.shape
    n_s = h2_s.shape[0]
    tb = ROUTE_BLOCK
    nblk_p, nblk_s = n_p // tb, n_s // tb
    pidx = lambda i, *_: jnp.minimum(i, nblk_p - 1)
    sidx = lambda i, *_: jnp.maximum(i - nblk_p, 0)
    grid_spec = pltpu.PrefetchScalarGridSpec(
        num_scalar_prefetch=6,
        grid=(nblk_p + nblk_s,),
        in_specs=[pl.BlockSpec((SUBLANES, tb), lambda i, *_: (0, pidx(i))),
                  pl.BlockSpec((SUBLANES, tb), lambda i, *_: (0, sidx(i))),
                  pl.BlockSpec((tb, d), lambda i, *_: (pidx(i), 0)),
                  pl.BlockSpec((tb, d), lambda i, *_: (sidx(i), 0)),
                  pl.BlockSpec((1, LANES, 1), lambda i, *_: (i, 0, 0))],
        out_specs=pl.BlockSpec(memory_space=pl.ANY),
        scratch_shapes=[pltpu.VMEM((STAGE_SLOTS, STAGE_ROWS, d), jnp.float32),
                        pltpu.VMEM((EXPERT_TILE, d), jnp.float32),
                        pltpu.SemaphoreType.DMA((STAGE_SLOTS,)),
                        pltpu.SemaphoreType.DMA],
    )
    return pl.pallas_call(
        functools.partial(_dispatch_kernel, nblk_p=nblk_p),
        grid_spec=grid_spec,
        out_shape=jax.ShapeDtypeStruct((n_rows, d), jnp.float32),
        compiler_params=pltpu.CompilerParams(
            dimension_semantics=("arbitrary",), vmem_limit_bytes=VMEM_LIMIT_BYTES),
    )(*tables, topit_p, topit_s, h2_p, h2_s, cbcol)


def _expert_mlp(x, wgu_bf, bgu, wd_bf, bd):
    d_ff = wd_bf.shape[0]
    gu = jnp.dot(_bf(x), wgu_bf[...], preferred_element_type=jnp.float32) + bgu
    glu = jnp.minimum(gu[:, :d_ff], SWIGLU_LIMIT)
    lin = jnp.clip(gu[:, d_ff:], -SWIGLU_LIMIT, SWIGLU_LIMIT)
    act = glu * jax.nn.sigmoid(SWIGLU_ALPHA * glu) * (lin + 1.0)
    return jnp.dot(_bf(act), wd_bf[...], preferred_element_type=jnp.float32) + bd


def _experts_kernel(tile_expert_ref, n_used_ref, groups_ref, next_ref, slot_ref,
                    x_ref, wgu_hbm, bgu_ref, wd_hbm, bd_ref, y_ref,
                    wgu_f32, wd_f32, wgu_bf, wd_bf, sems):
    step = pl.program_id(0)
    groups = groups_ref[step]

    def weight_copies(expert, slot):
        return (pltpu.make_async_copy(wgu_hbm.at[expert], wgu_f32.at[slot], sems.at[slot, 0]),
                pltpu.make_async_copy(wd_hbm.at[expert], wd_f32.at[slot], sems.at[slot, 1]))

    @pl.when(groups > 0)
    def _():
        e = tile_expert_ref[step]
        prev = tile_expert_ref[jnp.maximum(step - 1, 0)]
        slot = slot_ref[e]

        @pl.when(step == 0)
        def _():
            for cp in weight_copies(e, slot):
                cp.start()

        @pl.when(jnp.logical_or(step == 0, e != prev))
        def _():
            for cp in weight_copies(e, slot):
                cp.wait()
            nxt = next_ref[e]

            @pl.when(nxt >= 0)
            def _():
                for cp in weight_copies(nxt, 1 - slot):
                    cp.start()

            wgu_bf[...] = _bf(wgu_f32[slot])
            wd_bf[...] = _bf(wd_f32[slot])

    tm = x_ref.shape[0]
    for n in range(1, EXPERT_GROUPS + 1):
        @pl.when(groups == n)
        def _():
            rows = n * (tm // EXPERT_GROUPS)
            y_ref[0:rows, :] = _expert_mlp(x_ref[0:rows, :], wgu_bf, bgu_ref[0], wd_bf, bd_ref[0])
            if rows < tm:
                y_ref[rows:, :] = jnp.zeros((tm - rows, y_ref.shape[1]), y_ref.dtype)

    @pl.when(groups == 0)
    def _():
        y_ref[...] = jnp.zeros_like(y_ref)


def _experts(tile_expert, n_used, tile_groups, next_used, slot_of, x_sorted, w_gate_up, b_gate_up,
             w_down, b_down):
    p = x_sorted.shape[0]
    n_e, d, two_ff = w_gate_up.shape
    d_ff = two_ff // 2
    tm = EXPERT_TILE
    tile = lambda i, te, nu: jnp.minimum(i, nu[0] - 1)
    grid_spec = pltpu.PrefetchScalarGridSpec(
        num_scalar_prefetch=5,
        grid=(p // tm,),
        in_specs=[pl.BlockSpec((tm, d), lambda i, te, nu, *_: (tile(i, te, nu), 0)),
                  pl.BlockSpec(memory_space=pl.ANY),
                  pl.BlockSpec((1, 1, two_ff), lambda i, te, *_: (te[i], 0, 0)),
                  pl.BlockSpec(memory_space=pl.ANY),
                  pl.BlockSpec((1, 1, d), lambda i, te, *_: (te[i], 0, 0))],
        out_specs=pl.BlockSpec((tm, d), lambda i, *_: (i, 0)),
        scratch_shapes=[pltpu.VMEM((2, d, two_ff), jnp.float32),
                        pltpu.VMEM((2, d_ff, d), jnp.float32),
                        pltpu.VMEM((d, two_ff), jnp.bfloat16),
                        pltpu.VMEM((d_ff, d), jnp.bfloat16),
                        pltpu.SemaphoreType.DMA((2, 2))],
    )
    return pl.pallas_call(
        _experts_kernel,
        grid_spec=grid_spec,
        out_shape=jax.ShapeDtypeStruct(x_sorted.shape, jnp.float32),
        compiler_params=pltpu.CompilerParams(
            dimension_semantics=("arbitrary",), vmem_limit_bytes=VMEM_LIMIT_BYTES),
    )(tile_expert, n_used, tile_groups, next_used, slot_of, x_sorted, w_gate_up,
      b_gate_up.reshape(n_e, 1, two_ff), w_down, b_down.reshape(n_e, 1, d))


def _combine_kernel(cnt_ref, gstart_ref, cb_ref, blkrows_ref,
                    topi_p_ref, topi_s_ref, topw_p_ref, topw_s_ref, x1_p_ref, x1_s_ref,
                    gate_p_ref, gate_s_ref, cbrow_ref, y_ref, out_p_ref, out_s_ref,
                    ybuf, sems, *, nblk_p):
    step = pl.program_id(0)
    nblk = pl.num_programs(0)
    tb = x1_p_ref.shape[0]
    slot = step % STAGE_SLOTS

    def fetch(blk, into):
        def issue(e, carry):
            i = blk * N_EXPERTS + e
            _chunk_copies(cnt_ref[i], y_ref, gstart_ref[i], ybuf.at[into], cb_ref[i], sems.at[into])
            return carry
        lax.fori_loop(0, N_EXPERTS, issue, 0)

    @pl.when(step == 0)
    def _():
        ybuf[...] = jnp.zeros_like(ybuf)
        for ahead in range(STAGE_SLOTS - 1):
            @pl.when(ahead < nblk)
            def _():
                fetch(ahead, ahead)

    @pl.when(step + STAGE_SLOTS - 1 < nblk)
    def _():
        fetch(step + STAGE_SLOTS - 1, (step + STAGE_SLOTS - 1) % STAGE_SLOTS)

    _wait_rows(y_ref, blkrows_ref[step], sems.at[slot])

    is_p = step < nblk_p
    topi = jnp.where(is_p, topi_p_ref[...], topi_s_ref[...])
    topw = jnp.where(is_p, topw_p_ref[...], topw_s_ref[...])
    cbrow = cbrow_ref[0]

    lane = lax.broadcasted_iota(jnp.int32, (tb, LANES), 1)
    picks = []
    onehot = jnp.zeros((tb, LANES), jnp.float32)
    for k in range(TOP_K):
        ix = jnp.sum(jnp.where(lane == k, topi, 0), axis=-1, keepdims=True)
        picks.append(lane == ix)
        onehot = jnp.where(picks[k], 1.0, onehot)
    r = lax.broadcasted_iota(jnp.int32, (tb, tb), 0)
    c = lax.broadcasted_iota(jnp.int32, (tb, tb), 1)
    earlier = jnp.where(c < r, 1.0, 0.0)
    base = _dot(earlier, onehot) + cbrow.astype(jnp.float32)
    cols = [jnp.sum(jnp.where(picks[k], base, 0.0), axis=-1, keepdims=True).astype(jnp.int32)
            for k in range(TOP_K)]
    wks = [jnp.sum(jnp.where(lane == k, topw, 0.0), axis=-1, keepdims=True) for k in range(TOP_K)]
    moe = None
    for c0 in range(0, ybuf.shape[1], COMBINE_SLAB):
        colid = lax.broadcasted_iota(jnp.int32, (tb, COMBINE_SLAB), 1) + c0
        weights = jnp.zeros((tb, COMBINE_SLAB), jnp.float32)
        for k in range(TOP_K):
            weights = jnp.where(colid == cols[k], wks[k], weights)
        part = _dot(weights, ybuf[slot, c0:c0 + COMBINE_SLAB, :])
        moe = part if moe is None else moe + part

    @pl.when(is_p)
    def _():
        out_p_ref[...] = x1_p_ref[...] + gate_p_ref[...] * moe

    @pl.when(jnp.logical_not(is_p))
    def _():
        gate_s = _repeat_rows(gate_s_ref[...], tb // gate_s_ref.shape[0])
        out_s_ref[...] = x1_s_ref[...] + gate_s * moe


def _combine(tables, topi_p, topi_s, topw_p, topw_s, x1_p, x1_s, gate_p, mod_seq, dec_seq, cbrow,
             y_sorted):
    n_p, d = x1_p.shape
    n_s = x1_s.shape[0]
    tb = ROUTE_BLOCK
    nblk_p, nblk_s = n_p // tb, n_s // tb
    pidx = lambda i, *_: jnp.minimum(i, nblk_p - 1)
    sidx = lambda i, *_: jnp.maximum(i - nblk_p, 0)
    prow = lambda w: pl.BlockSpec((tb, w), lambda i, *_: (pidx(i), 0))
    srow = lambda w: pl.BlockSpec((tb, w), lambda i, *_: (sidx(i), 0))
    grid_spec = pltpu.PrefetchScalarGridSpec(
        num_scalar_prefetch=4,
        grid=(nblk_p + nblk_s,),
        in_specs=[prow(LANES), srow(LANES), prow(LANES), srow(LANES), prow(d), srow(d),
                  pl.BlockSpec((1, d), lambda i, *_: (0, 0)),
                  pl.BlockSpec((tb // dec_seq, d), lambda i, *_: (sidx(i), 5)),
                  pl.BlockSpec((1, 1, LANES), lambda i, *_: (i, 0, 0)),
                  pl.BlockSpec(memory_space=pl.ANY)],
        out_specs=[prow(d), srow(d)],
        scratch_shapes=[pltpu.VMEM((STAGE_SLOTS, STAGE_ROWS, d), jnp.float32),
                        pltpu.SemaphoreType.DMA((STAGE_SLOTS,))],
    )
    return pl.pallas_call(
        functools.partial(_combine_kernel, nblk_p=nblk_p),
        grid_spec=grid_spec,
        out_shape=[jax.ShapeDtypeStruct((n_p, d), jnp.float32),
                   jax.ShapeDtypeStruct((n_s, d), jnp.float32)],
        compiler_params=pltpu.CompilerParams(
            dimension_semantics=("arbitrary",), vmem_limit_bytes=VMEM_LIMIT_BYTES),
    )(*tables, topi_p, topi_s, topw_p, topw_s, x1_p, x1_s, gate_p, mod_seq, cbrow, y_sorted)


def _rotary_tables(pos, inv_freq):
    ang = pos.astype(jnp.float32)[:, None] * inv_freq[None, :]
    cos, sin = jnp.cos(ang), jnp.sin(ang)
    cos_t = jnp.tile(cos, (1, LANES // cos.shape[1]))
    sin_t = jnp.tile(jnp.concatenate([-sin, sin], axis=1), (1, LANES // (2 * sin.shape[1])))
    return cos_t, sin_t


def _rotary_split_tables(block, nblk, inv_freq):
    half = inv_freq.shape[0]
    freq = jnp.tile(inv_freq, LANES // half)
    sign = jnp.tile(jnp.concatenate([-jnp.ones((half,), jnp.float32),
                                     jnp.ones((half,), jnp.float32)]), LANES // (2 * half))
    base = (jnp.arange(nblk, dtype=jnp.int32) * block).astype(jnp.float32)[:, None] * freq[None, :]
    off = jnp.arange(block, dtype=jnp.int32).astype(jnp.float32)[:, None] * freq[None, :]
    blk = jnp.stack([jnp.cos(base), jnp.sin(base)], axis=1)
    tab = jnp.stack([jnp.cos(off), jnp.sin(off), jnp.cos(off) * sign, jnp.sin(off) * sign])
    return blk, tab


def _retention_tables(chunk):
    log_gamma = jnp.log1p(-jnp.exp2(-5.0 - jnp.arange(RET_HEADS, dtype=jnp.float32)))
    idx = jnp.arange(chunk, dtype=jnp.float32)
    diff = idx[:, None] - idx[None, :]
    decay = jnp.where(diff[None] >= 0,
                      jnp.exp(jnp.maximum(diff, 0.0)[None] * log_gamma[:, None, None]), 0.0)
    q_decay = jnp.exp((idx + 1.0)[:, None] * log_gamma[None, :])
    k_decay = jnp.exp((chunk - 1.0 - idx)[:, None] * log_gamma[None, :])
    qdec = jnp.repeat(q_decay, RET_DV, axis=1)
    kdec = jnp.repeat(k_decay, RET_DK, axis=1)
    g_chunk = jnp.exp(chunk * log_gamma)
    gpow = jnp.repeat(g_chunk.reshape(RET_HEADS // 2, 2), RET_DK, axis=1).T
    return decay, qdec, kdec, gpow


def _block_diag_mean(width):
    head = jnp.arange(width) // HEAD_DIM
    return jnp.where(head[:, None] == head[None, :], 1.0 / HEAD_DIM, 0.0).astype(jnp.bfloat16)


def kernel(x_prompt, x_sample, cache_k, cache_v, state_ret, c_prompt, c_sample, w_ada, b_ada,
           g_mix, w_in, q_gain, k_gain, sinks, w_out, g_ffn, w_router, b_router, w_gate_up,
           b_gate_up, w_down, b_down):
    depth = w_ada.shape[0]
    batch, seq, d = x_prompt.shape
    dec_batch, dec_seq, _ = x_sample.shape
    cache_w = cache_k.shape[2]
    assert batch == 1 and depth == 1
    assert seq % PROMPT_BLOCK == 0 and PROMPT_BLOCK % WINDOW == 0 and WINDOW == RET_CHUNK
    assert dec_batch % SAMPLE_SEQS == 0 and cache_w == WINDOW
    n_p, n_s = batch * seq, dec_batch * dec_seq
    n_tok = n_p + n_s
    assert n_p % ROUTE_BLOCK == 0 and n_s % ROUTE_BLOCK == 0
    assert ROUTE_BLOCK % (SAMPLE_SEQS * dec_seq) == 0
    assert STAGE_ROWS % COMBINE_SLAB == 0 and EXPERT_TILE % (EXPERT_GROUPS * SUBLANES) == 0

    l = 0
    f32 = jnp.float32
    rope_freq = 1.0 / (ROPE_THETA ** (jnp.arange(0, HEAD_DIM, 2, dtype=f32) / HEAD_DIM))
    ret_freq = 1.0 / (ROPE_THETA ** jnp.linspace(0.0, 1.0, RET_DK // 2, dtype=f32))
    pos_s = PAST_LEN + jnp.arange(dec_seq, dtype=jnp.int32)
    blk_a, tab_a = _rotary_split_tables(PROMPT_BLOCK, seq // PROMPT_BLOCK, rope_freq)
    blk_r, tab_r = _rotary_split_tables(PROMPT_BLOCK, seq // PROMPT_BLOCK, ret_freq)
    tabs_p = (jnp.concatenate([blk_a, blk_r], axis=1), jnp.concatenate([tab_a, tab_r], axis=0))
    tabs_s = tuple(jnp.tile(a, (SAMPLE_SEQS, 1))
                   for a in _rotary_tables(pos_s, rope_freq) + _rotary_tables(pos_s, ret_freq))
    ret_p = _retention_tables(RET_CHUNK)
    decay_s, qdec_s, kdec_s, gpow_s = _retention_tables(dec_seq)
    eye = jnp.eye(SAMPLE_SEQS, dtype=f32)
    rows_s = SAMPLE_SEQS * dec_seq
    ret_s = ((eye[None, :, None, :, None] * decay_s[:, None, :, None, :]).reshape(RET_HEADS, rows_s, rows_s),
             jnp.tile(qdec_s, (SAMPLE_SEQS, 1)), jnp.tile(kdec_s, (SAMPLE_SEQS, 1)), gpow_s)

    n_c = batch + dec_batch
    c_rows = -(-n_c // SUBLANES) * SUBLANES
    c_all = jnp.concatenate([c_sample, c_prompt, jnp.zeros((c_rows - n_c, d), f32)], axis=0)
    mod = _adaln(c_all, w_ada[l], b_ada[l])
    mod_p = mod[dec_batch].reshape(6, d)

    w_in_bf = _bf(w_in[l])
    w_out_bf = _bf(w_out[l])
    qg = jnp.tile(q_gain[l], ATT_HEADS).reshape(1, ATT_W)
    kg = jnp.tile(k_gain[l], ATT_KV_HEADS).reshape(1, KV_W)
    bdq, bdk = _block_diag_mean(ATT_W), _block_diag_mean(KV_W)
    gmix = g_mix[l].reshape(1, d)
    gffn = g_ffn[l].reshape(1, d)
    wr = jnp.pad(w_router[l], ((0, 0), (0, LANES - N_EXPERTS)))
    wr_hi = _bf(wr)
    wr_cat = jnp.concatenate([wr_hi, _bf(wr - wr_hi.astype(f32))], axis=1)
    br = jnp.concatenate([b_router[l], jnp.full((LANES - N_EXPERTS,), NEG_BIG, f32)]).reshape(1, LANES)
    shared = (sinks[l], gmix, gffn, w_in_bf, qg, kg, bdq, bdk)

    (x1_p, h2_p, topi_p, topw_p, topit_p, cnt_p, kwin_p, vwin_p, rend_p) = _prompt_mix(
        x_prompt.reshape(n_p, d), mod_p, *shared, tabs_p, ret_p, w_out_bf, wr_hi, wr_cat, br)

    (x1_s, h2_s, topi_s, topw_s, topit_s, cnt_s, kwin_s, vwin_s, rend_s) = _sample_mix(
        x_sample.reshape(n_s, d), mod, *shared, tabs_s, ret_s, w_out_bf, wr_hi, wr_cat, br,
        cache_k[l].reshape(dec_batch, cache_w, KV_W), cache_v[l].reshape(dec_batch, cache_w, KV_W),
        state_ret[l].reshape(dec_batch, 2, LANES, RET_DV))

    tb, tm = ROUTE_BLOCK, EXPERT_TILE
    i32 = jnp.int32
    cnt_s = cnt_s.reshape(n_s // tb, -1, LANES).sum(axis=1)
    cnt = jnp.concatenate([cnt_p.reshape(-1, LANES), cnt_s], axis=0)
    cnt8 = -(-cnt // SUBLANES) * SUBLANES
    rs = jnp.cumsum(cnt8, axis=0) - cnt8
    tot = jnp.sum(cnt8, axis=0)[:N_EXPERTS]
    padded = -(-tot // tm) * tm
    ends = jnp.cumsum(padded)
    gstart = (ends - padded)[None, :] + rs[:, :N_EXPERTS]
    cb = jnp.cumsum(cnt8, axis=1) - cnt8
    blk_rows = jnp.sum(cnt8, axis=1).astype(i32)
    max_rows = n_tok * TOP_K + cnt.shape[0] * N_EXPERTS * (SUBLANES - 1)
    n_rows = -(-max_rows // tm) * tm + N_EXPERTS * tm
    tile_start = jnp.arange(n_rows // tm, dtype=i32) * tm
    tile_expert = jnp.minimum(jnp.sum(ends[None, :] <= tile_start[:, None], axis=1),
                              N_EXPERTS - 1).astype(i32)
    n_used = (ends[-1:] // tm).astype(i32)
    eid = jnp.arange(N_EXPERTS, dtype=i32)
    of_tile = lambda v: jnp.sum(jnp.where(tile_expert[:, None] == eid[None, :], v[None, :], 0), axis=1)
    tile_rows = jnp.clip(of_tile(tot) - (tile_start - of_tile(ends - padded)), 0, tm)
    tile_rows = jnp.where(tile_start < ends[-1], tile_rows, 0)
    tile_groups = (-(-tile_rows // (tm // EXPERT_GROUPS))).astype(i32)
    flat = lambda a: a[:, :N_EXPERTS].reshape(-1).astype(i32)
    tables = (flat(cnt8), gstart.reshape(-1).astype(i32), flat(cb), blk_rows)

    pad_cnt = padded - tot
    pad_cnt = jnp.concatenate([pad_cnt, jnp.sum(pad_cnt, keepdims=True)]).astype(i32)
    x_sorted = _dispatch(tables + (ends.astype(i32), pad_cnt), topit_p, topit_s, h2_p, h2_s,
                         cb.astype(i32)[:, :, None], n_rows)
    used = tot > 0
    later = jnp.where(used[None, :] & (eid[None, :] > eid[:, None]), eid[None, :], N_EXPERTS)
    next_used = jnp.min(later, axis=1)
    next_used = jnp.where(next_used == N_EXPERTS, -1, next_used).astype(i32)
    slot_of = ((jnp.cumsum(used.astype(i32)) - 1) % 2).astype(i32)
    y_sorted = _experts(tile_expert, n_used, tile_groups, next_used, slot_of, x_sorted, w_gate_up[l], b_gate_up[l], w_down[l],
                        b_down[l])
    y_p, y_s = _combine(tables, topi_p, topi_s, topw_p, topw_s, x1_p, x1_s, mod_p[5:6], mod,
                        dec_seq, cb.astype(i32)[:, None, :], y_sorted)

    kv5 = lambda a, n: a.reshape(1, n, cache_w, ATT_KV_HEADS, HEAD_DIM)
    st5 = lambda a, n: a.reshape(1, n, RET_HEADS, RET_DK, RET_DV)
    return (y_p.reshape(batch, seq, d), y_s.reshape(dec_batch, dec_seq, d),
            kv5(kwin_p, batch), kv5(vwin_p, batch), st5(rend_p, batch),
            kv5(kwin_s, dec_batch), kv5(vwin_s, dec_batch), st5(rend_s, dec_batch))
```

```python
import functools

import jax
import jax.numpy as jnp
import numpy as np
from jax import lax
from jax.experimental import pallas as pl
from jax.experimental.pallas import tpu as pltpu

HEAD_DIM = 64
ATT_HEADS = 8
ATT_KV_HEADS = 2
ATT_W = ATT_HEADS * HEAD_DIM
KV_W = ATT_KV_HEADS * HEAD_DIM
WINDOW = 128
PAST_LEN = 16384
ROPE_THETA = 10000.0
RET_HEADS = 4
RET_DK = 64
RET_DV = 128
RET_QK_W = RET_HEADS * RET_DK
RET_W = RET_HEADS * RET_DV
RET_CHUNK = 128
N_EXPERTS = 32
TOP_K = 4
SWIGLU_LIMIT = 7.0
SWIGLU_ALPHA = 1.702
EPS = 1e-6

_Q0, _K0, _V0 = 0, ATT_W, ATT_W + KV_W
_RQ0 = ATT_W + 2 * KV_W
_RK0 = _RQ0 + RET_QK_W
_RV0 = _RK0 + RET_QK_W
_RG0 = _RV0 + RET_W
IN_W = _RG0 + RET_W

LANES = 128
SUBLANES = 8
VMEM_LIMIT_BYTES = 56 * 1024 * 1024
NEG_BIG = -1e30

PROMPT_BLOCK = 512
SAMPLE_SEQS = 16
COUNT_BLOCK = 128
ROUTE_BLOCK = 512
EXPERT_TILE = 512
EXPERT_GROUPS = 4
CHUNK_BITS = ROUTE_BLOCK.bit_length()
STAGE_ROWS = -(-(TOP_K * ROUTE_BLOCK + N_EXPERTS * (SUBLANES - 1)) // LANES) * LANES
STAGE_BITS = STAGE_ROWS.bit_length()
STAGE_SLOTS = 3
COMBINE_SLAB = 256

_NT = (((1,), (1,)), ((), ()))
_TN = (((0,), (0,)), ((), ()))


def _bf(x):
    return x.astype(jnp.bfloat16)


def _dot(a, b):
    return jnp.dot(_bf(a), _bf(b), preferred_element_type=jnp.float32)


def _dot_nt(a, b):
    return lax.dot_general(_bf(a), _bf(b), _NT, preferred_element_type=jnp.float32)


def _dot_tn(a, b):
    return lax.dot_general(_bf(a), _bf(b), _TN, preferred_element_type=jnp.float32)


def _lane_lo(shape):
    lane = lax.broadcasted_iota(jnp.int32, shape, len(shape) - 1)
    return (lane % LANES) < HEAD_DIM


def _swap_halves(x):
    lane = lax.broadcasted_iota(jnp.int32, x.shape, 1)
    first = (lane % HEAD_DIM) < (HEAD_DIM // 2)
    return jnp.where(first, pltpu.roll(x, LANES - HEAD_DIM // 2, axis=1),
                     pltpu.roll(x, HEAD_DIM // 2, axis=1))


def _rotate(x, cos, sin_signed):
    outs = []
    for j in range(x.shape[1] // LANES):
        xs = x[:, j * LANES:(j + 1) * LANES]
        outs.append(xs * cos + _swap_halves(xs) * sin_signed)
    return outs[0] if len(outs) == 1 else jnp.concatenate(outs, axis=1)


def _repeat_rows(m, n):
    r, w = m.shape
    return jnp.broadcast_to(m[:, None, :], (r, n, w)).reshape(r * n, w)


def _block_rotary(blk_ref, tab_ref, which):
    ca, sa = blk_ref[0, 2 * which:2 * which + 1, :], blk_ref[0, 2 * which + 1:2 * which + 2, :]
    cb, sb, cbs, sbs = (tab_ref[4 * which + n] for n in range(4))
    return ca * cb - sa * sb, sa * cbs + ca * sbs


def _rms_rows(x):
    return x * lax.rsqrt(jnp.mean(x * x, axis=-1, keepdims=True) + EPS)


def _pre_norm(x, gain, shift, scale):
    return _rms_rows(x) * gain * (1.0 + scale) + shift


def _head_norm(x, gain_tiled, blockdiag):
    ms = jnp.dot(_bf(x * x), blockdiag, preferred_element_type=jnp.float32)
    return x * lax.rsqrt(ms + EPS) * gain_tiled


def _project(h, w_in_ref, qg, kg, bd_q, bd_k, cos_a, sin_a, cos_r, sin_r):
    z = jnp.dot(_bf(h), w_in_ref[...], preferred_element_type=jnp.float32)
    q = _rotate(_head_norm(z[:, _Q0:_K0], qg, bd_q), cos_a, sin_a) * (HEAD_DIM ** -0.5)
    k = _rotate(_head_norm(z[:, _K0:_V0], kg, bd_k), cos_a, sin_a)
    v = z[:, _V0:_RQ0]
    rq = _rotate(z[:, _RQ0:_RK0], cos_r, sin_r)
    rk = _rotate(z[:, _RK0:_RV0], cos_r, sin_r) * (RET_DK ** -0.5)
    rv = z[:, _RV0:_RG0]
    rg = z[:, _RG0:IN_W]
    return q, k, v, rq, rk, rv, rg


_NAT_HEADS = (0, 2, 5, 7)
_ROL_HEADS = (1, 3, 4, 6)


def _stack_heads(q, heads):
    lo = _lane_lo((q.shape[0], LANES))
    parts = []
    for j in heads:
        slab = q[:, (j // 2) * LANES:(j // 2 + 1) * LANES]
        parts.append(jnp.where(lo if j % 2 == 0 else ~lo, slab, 0.0))
    return jnp.concatenate(parts, axis=0)


def _sink_column(sinks_ref, heads, rows):
    r = lax.broadcasted_iota(jnp.int32, (len(heads) * rows, 1), 0)
    col = jnp.full((len(heads) * rows, 1), sinks_ref[heads[-1]], jnp.float32)
    for n in range(len(heads) - 2, -1, -1):
        col = jnp.where(r < (n + 1) * rows, sinks_ref[heads[n]], col)
    return col


def _softmax_pv(s, mask, sink, v):
    s = jnp.where(mask, s, NEG_BIG)
    m = jnp.maximum(jnp.max(s, axis=-1, keepdims=True), sink)
    p = jnp.exp(s - m)
    denom = jnp.sum(p, axis=-1, keepdims=True) + jnp.exp(sink - m)
    return _dot(p, v) / denom


def _attention(q, k_all, v_all, mask, sinks_ref):
    rows = q.shape[0]
    k_rol = pltpu.roll(k_all, HEAD_DIM, axis=1)
    v_rol = pltpu.roll(v_all, HEAD_DIM, axis=1)
    mask4 = jnp.concatenate([mask] * 4, axis=0)
    o_nat = _softmax_pv(_dot_nt(_stack_heads(q, _NAT_HEADS), k_all), mask4,
                        _sink_column(sinks_ref, _NAT_HEADS, rows), v_all)
    o_rol = _softmax_pv(_dot_nt(_stack_heads(q, _ROL_HEADS), k_rol), mask4,
                        _sink_column(sinks_ref, _ROL_HEADS, rows), v_rol)
    lo = _lane_lo((rows, LANES))
    blk = lambda o, n: o[n * rows:(n + 1) * rows]
    return jnp.concatenate([
        jnp.where(lo, blk(o_nat, 0), blk(o_rol, 0)),
        jnp.where(lo, blk(o_nat, 1), blk(o_rol, 1)),
        jnp.where(lo, blk(o_rol, 2), blk(o_nat, 2)),
        jnp.where(lo, blk(o_rol, 3), blk(o_nat, 3)),
    ], axis=1)


def _retention(rq, rk, rv, rg, state, decay_ref, qdec, kdec, gpow):
    rows = rq.shape[0]
    lo = _lane_lo((rows, LANES))
    rkd = rk * kdec
    outs, new_state = [], []
    for m in range(RET_HEADS // 2):
        sl = slice(m * LANES, (m + 1) * LANES)
        upd = state[m] * gpow[:, m:m + 1]
        for p in range(2):
            h = 2 * m + p
            half = lo if p == 0 else ~lo
            rqm = jnp.where(half, rq[:, sl], 0.0)
            rv_h = rv[:, h * RET_DV:(h + 1) * RET_DV]
            s = _dot_nt(rqm, rk[:, sl]) * decay_ref[h]
            ret = _dot(s, rv_h) + _dot(rqm, state[m]) * qdec[:, h * RET_DV:(h + 1) * RET_DV]
            upd = upd + _dot_tn(jnp.where(half, rkd[:, sl], 0.0), rv_h)
            rg_h = rg[:, h * RET_DV:(h + 1) * RET_DV]
            outs.append(rg_h * jax.nn.sigmoid(rg_h) * _rms_rows(ret))
        new_state.append(upd)
    return jnp.concatenate(outs, axis=1), new_state


def _route(h2, wr_hi_ref, wr_cat_ref, br):
    h_hi = _bf(h2)
    h_lo = _bf(h2 - h_hi.astype(jnp.float32))
    both = jnp.dot(h_hi, wr_cat_ref[...], preferred_element_type=jnp.float32)
    logits = (both[:, :LANES] + both[:, LANES:]
              + jnp.dot(h_lo, wr_hi_ref[...], preferred_element_type=jnp.float32)) + br
    lane = lax.broadcasted_iota(jnp.int32, logits.shape, 1)
    idx_out = jnp.zeros(logits.shape, jnp.int32)
    val_out = jnp.zeros(logits.shape, jnp.float32)
    onehot = jnp.zeros(logits.shape, jnp.float32)
    vals = []
    work = logits
    for k in range(TOP_K):
        mx = jnp.max(work, axis=-1, keepdims=True)
        ix = jnp.min(jnp.where(work == mx, lane, LANES), axis=-1, keepdims=True)
        work = jnp.where(lane == ix, NEG_BIG, work)
        onehot = jnp.where(lane == ix, 1.0, onehot)
        idx_out = jnp.where(lane == k, ix, idx_out)
        vals.append(mx)
    exps = [jnp.exp(v - vals[0]) for v in vals]
    total = exps[0] + exps[1] + exps[2] + exps[3]
    for k in range(TOP_K):
        val_out = jnp.where(lane == k, exps[k] / total, val_out)
    idx_t = idx_out.astype(jnp.float32).T[0:SUBLANES].astype(jnp.int32)
    counts = jnp.sum(onehot.reshape(-1, COUNT_BLOCK, LANES), axis=1).astype(jnp.int32)
    return idx_out, val_out, idx_t, counts


def _adaln_kernel(c_ref, w_ref, b_ref, o_ref):
    c = c_ref[...]
    o_ref[...] = _dot(c * jax.nn.sigmoid(c), w_ref[...]) + b_ref[...]


def _adaln(c_all, w_ada, b_ada):
    rows, d = c_all.shape
    n = w_ada.shape[1]
    bn = 1536
    return pl.pallas_call(
        _adaln_kernel,
        grid=(n // bn,),
        in_specs=[pl.BlockSpec((rows, d), lambda j: (0, 0)),
                  pl.BlockSpec((d, bn), lambda j: (0, j)),
                  pl.BlockSpec((1, bn), lambda j: (0, j))],
        out_specs=pl.BlockSpec((rows, bn), lambda j: (0, j)),
        out_shape=jax.ShapeDtypeStruct((rows, n), jnp.float32),
        compiler_params=pltpu.CompilerParams(vmem_limit_bytes=VMEM_LIMIT_BYTES),
    )(c_all, w_ada, b_ada.reshape(1, n))


def _prompt_mix_kernel(sinks_ref, x_ref, mod_ref, gmix_ref, gffn_ref, w_in_ref, qg_ref, kg_ref,
                       bdq_ref, bdk_ref, rotblk_ref, rottab_ref,
                       decay_ref, qdec_ref, kdec_ref, gpow_ref, w_out_ref,
                       wrh_ref, wrl_ref, br_ref,
                       x1_ref, h2_ref, topi_ref, topw_ref, topit_ref, cnt_ref, kwin_ref, vwin_ref, rend_ref,
                       kprev, vprev, state, mix):
    step = pl.program_id(0)
    tb = x_ref.shape[0]

    @pl.when(step == 0)
    def _():
        kprev[...] = jnp.zeros_like(kprev)
        vprev[...] = jnp.zeros_like(vprev)
        state[...] = jnp.zeros_like(state)

    x = x_ref[...]
    h = _pre_norm(x, gmix_ref[...], mod_ref[0:1, :], mod_ref[1:2, :])
    q, k, v, rq, rk, rv, rg = _project(
        h, w_in_ref, qg_ref[...], kg_ref[...], bdq_ref[...], bdk_ref[...],
        *_block_rotary(rotblk_ref, rottab_ref, 0), *_block_rotary(rotblk_ref, rottab_ref, 1))

    qi = lax.broadcasted_iota(jnp.int32, (WINDOW, 2 * WINDOW), 0)
    ci = lax.broadcasted_iota(jnp.int32, (WINDOW, 2 * WINDOW), 1)
    band = (ci > qi) & (ci <= qi + WINDOW)
    qdec, kdec, gpow = qdec_ref[...], kdec_ref[...], gpow_ref[...]

    n_sb = tb // WINDOW
    rows = [slice(sb * WINDOW, (sb + 1) * WINDOW) for sb in range(n_sb)]
    k_prev, v_prev = kprev[...], vprev[...]
    for sb, rs in enumerate(rows):
        k_sb, v_sb = k[rs], v[rs]
        mask = band & ((ci >= WINDOW) | (step > 0)) if sb == 0 else band
        att = _attention(q[rs], jnp.concatenate([k_prev, k_sb], axis=0),
                         jnp.concatenate([v_prev, v_sb], axis=0), mask, sinks_ref)
        mix[rs, 0:ATT_W] = _bf(att)
        k_prev, v_prev = k_sb, v_sb
    kprev[...] = k_prev
    vprev[...] = v_prev

    cur_state = [state[0], state[1]]
    for rs in rows:
        yret, cur_state = _retention(rq[rs], rk[rs], rv[rs], rg[rs], cur_state,
                                     decay_ref, qdec, kdec, gpow)
        mix[rs, ATT_W:ATT_W + RET_W] = _bf(yret)
    state[0] = cur_state[0]
    state[1] = cur_state[1]

    x1 = x + mod_ref[2:3, :] * jnp.dot(mix[...], w_out_ref[...],
                                       preferred_element_type=jnp.float32)
    x1_ref[...] = x1
    h2 = _pre_norm(x1, gffn_ref[...], mod_ref[3:4, :], mod_ref[4:5, :])
    h2_ref[...] = h2
    topi, topw, topi_t, counts = _route(h2, wrh_ref, wrl_ref, br_ref[...])
    topi_ref[...] = topi
    topw_ref[...] = topw
    topit_ref[...] = topi_t
    cnt_ref[:, 0, :] = counts

    @pl.when(step == pl.num_programs(0) - 1)
    def _():
        kwin_ref[...] = kprev[...]
        vwin_ref[...] = vprev[...]
        rend_ref[...] = state[...]


def _const_spec(shape):
    nd = len(shape)
    return pl.BlockSpec(shape, lambda *_: (0,) * nd)


def _prompt_mix(x, mod, sinks, gmix, gffn, w_in, qg, kg, bdq, bdk, tabs, ret_tabs, w_out,
                wrh, wrl, br):
    t, d = x.shape
    tb = PROMPT_BLOCK
    rot_blk, rot_tab = tabs
    decay, qdec, kdec, gpow = ret_tabs
    row_spec = lambda w: pl.BlockSpec((tb, w), lambda i, *_: (i, 0))
    consts = [mod, gmix, gffn, w_in, qg, kg, bdq, bdk]
    consts2 = [rot_tab, decay, qdec, kdec, gpow, w_out, wrh, wrl, br]
    grid_spec = pltpu.PrefetchScalarGridSpec(
        num_scalar_prefetch=1,
        grid=(t // tb,),
        in_specs=([row_spec(d)] + [_const_spec(a.shape) for a in consts]
                  + [pl.BlockSpec((1,) + rot_blk.shape[1:], lambda i, *_: (i, 0, 0))]
                  + [_const_spec(a.shape) for a in consts2]),
        out_specs=[row_spec(d), row_spec(d), row_spec(LANES), row_spec(LANES),
                   pl.BlockSpec((SUBLANES, tb), lambda i, *_: (0, i)),
                   pl.BlockSpec((tb // COUNT_BLOCK, 1, LANES), lambda i, *_: (i, 0, 0)),
                   _const_spec((WINDOW, KV_W)), _const_spec((WINDOW, KV_W)),
                   _const_spec((2, LANES, RET_DV))],
        scratch_shapes=[pltpu.VMEM((WINDOW, KV_W), jnp.float32),
                        pltpu.VMEM((WINDOW, KV_W), jnp.float32),
                        pltpu.VMEM((2, LANES, RET_DV), jnp.float32),
                        pltpu.VMEM((tb, ATT_W + RET_W), jnp.bfloat16)],
    )
    return pl.pallas_call(
        _prompt_mix_kernel,
        grid_spec=grid_spec,
        out_shape=[jax.ShapeDtypeStruct((t, d), jnp.float32),
                   jax.ShapeDtypeStruct((t, d), jnp.float32),
                   jax.ShapeDtypeStruct((t, LANES), jnp.int32),
                   jax.ShapeDtypeStruct((t, LANES), jnp.float32),
                   jax.ShapeDtypeStruct((SUBLANES, t), jnp.int32),
                   jax.ShapeDtypeStruct((t // COUNT_BLOCK, 1, LANES), jnp.int32),
                   jax.ShapeDtypeStruct((WINDOW, KV_W), jnp.float32),
                   jax.ShapeDtypeStruct((WINDOW, KV_W), jnp.float32),
                   jax.ShapeDtypeStruct((2, LANES, RET_DV), jnp.float32)],
        compiler_params=pltpu.CompilerParams(
            dimension_semantics=("arbitrary",), vmem_limit_bytes=VMEM_LIMIT_BYTES),
    )(sinks, x, *consts, rot_blk, *consts2)


def _per_seq(stacked, rows, ds, s):
    return jnp.concatenate([stacked[g * rows + s * ds:g * rows + (s + 1) * ds]
                            for g in range(stacked.shape[0] // rows)], axis=0)


def _from_per_seq(parts, ds):
    groups = parts[0].shape[0] // ds
    return jnp.concatenate([p[g * ds:(g + 1) * ds] for g in range(groups) for p in parts], axis=0)


def _sample_attention(q, k_new, v_new, ck_ref, cv_ref, sinks_ref, ds):
    rows = q.shape[0]
    nseq, cache_w = ck_ref.shape[0], ck_ref.shape[1]
    tok_c = lax.broadcasted_iota(jnp.int32, (4 * rows, cache_w), 0) % rows
    col_c = lax.broadcasted_iota(jnp.int32, (4 * rows, cache_w), 1)
    delta = tok_c % ds + cache_w - col_c
    mask_c = (delta >= 0) & (delta < WINDOW)
    tok_n = lax.broadcasted_iota(jnp.int32, (4 * rows, rows), 0) % rows
    col_n = lax.broadcasted_iota(jnp.int32, (4 * rows, rows), 1)
    mask_n = (tok_n // ds == col_n // ds) & (col_n % ds <= tok_n % ds)

    outs = []
    for heads, rolled in ((_NAT_HEADS, False), (_ROL_HEADS, True)):
        arrange = (lambda a: pltpu.roll(a, HEAD_DIM, axis=1)) if rolled else (lambda a: a)
        qst = _stack_heads(q, heads)
        s_new = jnp.where(mask_n, _dot_nt(qst, arrange(k_new)), NEG_BIG)
        s_cache = _from_per_seq(
            [_dot_nt(_per_seq(qst, rows, ds, s), arrange(ck_ref[s])) for s in range(nseq)], ds)
        s_cache = jnp.where(mask_c, s_cache, NEG_BIG)
        sink = _sink_column(sinks_ref, heads, rows)
        m = jnp.maximum(jnp.maximum(jnp.max(s_cache, axis=-1, keepdims=True),
                                    jnp.max(s_new, axis=-1, keepdims=True)), sink)
        p_cache = jnp.exp(s_cache - m)
        p_new = jnp.exp(s_new - m)
        denom = (jnp.sum(p_cache, axis=-1, keepdims=True) + jnp.sum(p_new, axis=-1, keepdims=True)
                 + jnp.exp(sink - m))
        o = _from_per_seq(
            [_dot(_per_seq(p_cache, rows, ds, s), arrange(cv_ref[s])) for s in range(nseq)], ds)
        outs.append((o + _dot(p_new, arrange(v_new))) / denom)
    o_nat, o_rol = outs
    lo = _lane_lo((rows, LANES))
    blk = lambda o, n: o[n * rows:(n + 1) * rows]
    return jnp.concatenate([
        jnp.where(lo, blk(o_nat, 0), blk(o_rol, 0)),
        jnp.where(lo, blk(o_nat, 1), blk(o_rol, 1)),
        jnp.where(lo, blk(o_rol, 2), blk(o_nat, 2)),
        jnp.where(lo, blk(o_rol, 3), blk(o_nat, 3)),
    ], axis=1)


def _sample_retention(rq, rk, rv, rg, st_ref, rend_ref, decay_ref, qdec, kdec, gpow, ds):
    rows = rq.shape[0]
    nseq = st_ref.shape[0]
    lo = _lane_lo((rows, LANES))
    rkd = rk * kdec
    outs = []
    for m in range(RET_HEADS // 2):
        sl = slice(m * LANES, (m + 1) * LANES)
        heads = (2 * m, 2 * m + 1)
        rqm = [jnp.where(lo if p == 0 else ~lo, rq[:, sl], 0.0) for p in range(2)]
        rkdm = [jnp.where(lo if p == 0 else ~lo, rkd[:, sl], 0.0) for p in range(2)]
        rv_h = [rv[:, h * RET_DV:(h + 1) * RET_DV] for h in heads]
        q2 = jnp.concatenate(rqm, axis=0)
        k2 = jnp.concatenate(rkdm, axis=0)
        v2 = jnp.concatenate(rv_h, axis=0)
        inter = _from_per_seq(
            [_dot(_per_seq(q2, rows, ds, s), st_ref[s, m]) for s in range(nseq)], ds)
        for s in range(nseq):
            rend_ref[s, m] = (st_ref[s, m] * gpow[:, m:m + 1]
                              + _dot_tn(_per_seq(k2, rows, ds, s), _per_seq(v2, rows, ds, s)))
        for p, h in enumerate(heads):
            intra = _dot(_dot_nt(rqm[p], rk[:, sl]) * decay_ref[h], rv_h[p])
            ret = intra + inter[p * rows:(p + 1) * rows] * qdec[:, h * RET_DV:(h + 1) * RET_DV]
            rg_h = rg[:, h * RET_DV:(h + 1) * RET_DV]
            outs.append(rg_h * jax.nn.sigmoid(rg_h) * _rms_rows(ret))
    return jnp.concatenate(outs, axis=1)


def _sample_mix_kernel(sinks_ref, x_ref, mod_ref, gmix_ref, gffn_ref, w_in_ref, qg_ref, kg_ref,
                       bdq_ref, bdk_ref, cosa_ref, sina_ref, cosr_ref, sinr_ref,
                       decay_ref, qdec_ref, kdec_ref, gpow_ref, w_out_ref,
                       wrh_ref, wrl_ref, br_ref, ck_ref, cv_ref, st_ref,
                       x1_ref, h2_ref, topi_ref, topw_ref, topit_ref, cnt_ref, kwin_ref, vwin_ref, rend_ref,
                       mix):
    nseq, cache_w = ck_ref.shape[0], ck_ref.shape[1]
    ds = x_ref.shape[0] // nseq
    d = x_ref.shape[1]
    x = x_ref[...]
    mod = lambda i: _repeat_rows(mod_ref[:, i * d:(i + 1) * d], ds)
    h = _pre_norm(x, gmix_ref[...], mod(0), mod(1))
    q, k, v, rq, rk, rv, rg = _project(
        h, w_in_ref, qg_ref[...], kg_ref[...], bdq_ref[...], bdk_ref[...],
        cosa_ref[...], sina_ref[...], cosr_ref[...], sinr_ref[...])

    for s in range(nseq):
        rs = slice(s * ds, (s + 1) * ds)
        kwin_ref[s] = jnp.concatenate([ck_ref[s, ds:, :], k[rs]], axis=0)
        vwin_ref[s] = jnp.concatenate([cv_ref[s, ds:, :], v[rs]], axis=0)
    mix[:, 0:ATT_W] = _bf(_sample_attention(q, k, v, ck_ref, cv_ref, sinks_ref, ds))
    mix[:, ATT_W:ATT_W + RET_W] = _bf(_sample_retention(
        rq, rk, rv, rg, st_ref, rend_ref, decay_ref, qdec_ref[...], kdec_ref[...], gpow_ref[...], ds))

    x1 = x + mod(2) * jnp.dot(mix[...], w_out_ref[...], preferred_element_type=jnp.float32)
    x1_ref[...] = x1
    h2 = _pre_norm(x1, gffn_ref[...], mod(3), mod(4))
    h2_ref[...] = h2
    topi, topw, topi_t, counts = _route(h2, wrh_ref, wrl_ref, br_ref[...])
    topi_ref[...] = topi
    topw_ref[...] = topw
    topit_ref[...] = topi_t
    cnt_ref[:, 0, :] = counts


def _sample_mix(x, mod_seq, sinks, gmix, gffn, w_in, qg, kg, bdq, bdk, tabs, ret_tabs, w_out,
                wrh, wrl, br, cache_k, cache_v, state):
    t, d = x.shape
    nb, cache_w = cache_k.shape[0], cache_k.shape[1]
    ds = t // nb
    g = SAMPLE_SEQS
    tb = g * ds
    cosa, sina, cosr, sinr = tabs
    decay, qdec, kdec, gpow = ret_tabs
    row_spec = lambda w: pl.BlockSpec((tb, w), lambda i, *_: (i, 0))
    seq3 = lambda a: pl.BlockSpec((g,) + a.shape[1:], lambda i, *_: (i,) + (0,) * (a.ndim - 1))
    consts = [gmix, gffn, w_in, qg, kg, bdq, bdk, cosa, sina, cosr, sinr,
              decay, qdec, kdec, gpow, w_out, wrh, wrl, br]
    grid_spec = pltpu.PrefetchScalarGridSpec(
        num_scalar_prefetch=1,
        grid=(nb // g,),
        in_specs=([row_spec(d), pl.BlockSpec((g, mod_seq.shape[1]), lambda i, *_: (i, 0))]
                  + [_const_spec(a.shape) for a in consts]
                  + [seq3(cache_k), seq3(cache_v), seq3(state)]),
        out_specs=[row_spec(d), row_spec(d), row_spec(LANES), row_spec(LANES),
                   pl.BlockSpec((SUBLANES, tb), lambda i, *_: (0, i)),
                   pl.BlockSpec((tb // COUNT_BLOCK, 1, LANES), lambda i, *_: (i, 0, 0)),
                   seq3(cache_k), seq3(cache_v), seq3(state)],
        scratch_shapes=[pltpu.VMEM((tb, ATT_W + RET_W), jnp.bfloat16)],
    )
    return pl.pallas_call(
        _sample_mix_kernel,
        grid_spec=grid_spec,
        out_shape=[jax.ShapeDtypeStruct((t, d), jnp.float32),
                   jax.ShapeDtypeStruct((t, d), jnp.float32),
                   jax.ShapeDtypeStruct((t, LANES), jnp.int32),
                   jax.ShapeDtypeStruct((t, LANES), jnp.float32),
                   jax.ShapeDtypeStruct((SUBLANES, t), jnp.int32),
                   jax.ShapeDtypeStruct((t // COUNT_BLOCK, 1, LANES), jnp.int32),
                   jax.ShapeDtypeStruct(cache_k.shape, jnp.float32),
                   jax.ShapeDtypeStruct(cache_v.shape, jnp.float32),
                   jax.ShapeDtypeStruct(state.shape, jnp.float32)],
        compiler_params=pltpu.CompilerParams(
            dimension_semantics=("arbitrary",), vmem_limit_bytes=VMEM_LIMIT_BYTES),
    )(sinks, x, mod_seq, *consts, cache_k, cache_v, state)


def _chunk_copies(cnt8, src_ref, src_start, dst_ref, dst_start, sem):
    for bit in range(CHUNK_BITS - 1, SUBLANES.bit_length() - 2, -1):
        size = 1 << bit
        done = (cnt8 >> (bit + 1)) << (bit + 1)

        @pl.when(((cnt8 >> bit) & 1) == 1)
        def _():
            src = pl.multiple_of(src_start + done, SUBLANES)
            dst = pl.multiple_of(dst_start + done, SUBLANES)
            pltpu.make_async_copy(src_ref.at[pl.ds(src, size)],
                                  dst_ref.at[pl.ds(dst, size)], sem).start()


def _wait_rows(ref, rows8, sem, bits=STAGE_BITS):
    for bit in range(bits - 1, SUBLANES.bit_length() - 2, -1):
        size = 1 << bit

        @pl.when(((rows8 >> bit) & 1) == 1)
        def _():
            pltpu.make_async_copy(ref.at[pl.ds(0, size)], ref.at[pl.ds(0, size)], sem).wait()


def _dispatch_kernel(cnt_ref, gstart_ref, cb_ref, blkrows_ref, ends_ref, padcnt_ref,
                     topit_p_ref, topit_s_ref, h2_p_ref, h2_s_ref, cbcol_ref, xs_ref,
                     stage, zeros, sems, zsem, *, nblk_p):
    step = pl.program_id(0)
    nblk = pl.num_programs(0)
    tb = h2_p_ref.shape[0]
    slot = step % STAGE_SLOTS

    tile = zeros.shape[0]
    first_unused = ends_ref[N_EXPERTS - 1] // tile
    n_tiles = xs_ref.shape[0] // tile

    def tail_fill(t):
        return pltpu.make_async_copy(zeros, xs_ref.at[pl.ds(pl.multiple_of(t * tile, tile), tile)],
                                     zsem)

    @pl.when(step == 0)
    def _():
        zeros[...] = jnp.zeros_like(zeros)

        def pad_fill(e, carry):
            _chunk_copies(padcnt_ref[e], zeros, 0, xs_ref, ends_ref[e] - padcnt_ref[e], zsem)
            return carry

        lax.fori_loop(0, N_EXPERTS, pad_fill, 0)
        lax.fori_loop(first_unused, n_tiles, lambda t, c: (tail_fill(t).start(), c)[1], 0)

    @pl.when(step == nblk - 1)
    def _():
        _wait_rows(xs_ref, padcnt_ref[N_EXPERTS], zsem, bits=(N_EXPERTS * tile).bit_length())
        lax.fori_loop(first_unused, n_tiles, lambda t, c: (tail_fill(t).wait(), c)[1], 0)

    is_p = step < nblk_p
    topit = jnp.where(is_p, topit_p_ref[...], topit_s_ref[...])
    h2 = jnp.where(is_p, h2_p_ref[...], h2_s_ref[...])
    cbcol = cbcol_ref[0]

    sub = lax.broadcasted_iota(jnp.int32, (LANES, tb), 0)
    picks = [sub == topit[k:k + 1, :] for k in range(TOP_K)]
    onehot = jnp.zeros((LANES, tb), jnp.float32)
    for pk in picks:
        onehot = jnp.where(pk, 1.0, onehot)
    r = lax.broadcasted_iota(jnp.int32, (tb, tb), 0)
    c = lax.broadcasted_iota(jnp.int32, (tb, tb), 1)
    earlier = jnp.where(r < c, 1.0, 0.0)
    before = _dot(onehot, earlier)
    base = before + cbcol.astype(jnp.float32)
    row = lax.broadcasted_iota(jnp.int32, (stage.shape[1], tb), 0)
    sel = jnp.zeros((stage.shape[1], tb), jnp.float32)
    for pk in picks:
        col = jnp.sum(jnp.where(pk, base, 0.0), axis=0, keepdims=True).astype(jnp.int32)
        sel = jnp.where(row == col, 1.0, sel)

    stage[slot] = _dot(sel, h2)

    def issue(e, carry):
        i = step * N_EXPERTS + e
        _chunk_copies(cnt_ref[i], stage.at[slot], cb_ref[i], xs_ref, gstart_ref[i], sems.at[slot])
        return carry

    lax.fori_loop(0, N_EXPERTS, issue, 0)

    oldest = step - (STAGE_SLOTS - 1)

    @pl.when(oldest >= 0)
    def _():
        _wait_rows(xs_ref, blkrows_ref[jnp.maximum(oldest, 0)], sems.at[(step + 1) % STAGE_SLOTS])

    @pl.when(step == nblk - 1)
    def _():
        for back in range(STAGE_SLOTS - 2, -1, -1):
            @pl.when(step - back >= 0)
            def _():
                blk = jnp.maximum(step - back, 0)
                _wait_rows(xs_ref, blkrows_ref[blk], sems.at[blk % STAGE_SLOTS])


def _dispatch(tables, topit_p, topit_s, h2_p, h2_s, cbcol, n_rows):
    n_p, d = h2_p.shape
    n_s = h2_s.shape[0]
    tb = ROUTE_BLOCK
    nblk_p, nblk_s = n_p // tb, n_s // tb
    pidx = lambda i, *_: jnp.minimum(i, nblk_p - 1)
    sidx = lambda i, *_: jnp.maximum(i - nblk_p, 0)
    grid_spec = pltpu.PrefetchScalarGridSpec(
        num_scalar_prefetch=6,
        grid=(nblk_p + nblk_s,),
        in_specs=[pl.BlockSpec((SUBLANES, tb), lambda i, *_: (0, pidx(i))),
                  pl.BlockSpec((SUBLANES, tb), lambda i, *_: (0, sidx(i))),
                  pl.BlockSpec((tb, d), lambda i, *_: (pidx(i), 0)),
                  pl.BlockSpec((tb, d), lambda i, *_: (sidx(i), 0)),
                  pl.BlockSpec((1, LANES, 1), lambda i, *_: (i, 0, 0))],
        out_specs=pl.BlockSpec(memory_space=pl.ANY),
        scratch_shapes=[pltpu.VMEM((STAGE_SLOTS, STAGE_ROWS, d), jnp.float32),
                        pltpu.VMEM((EXPERT_TILE, d), jnp.float32),
                        pltpu.SemaphoreType.DMA((STAGE_SLOTS,)),
                        pltpu.SemaphoreType.DMA],
    )
    return pl.pallas_call(
        functools.partial(_dispatch_kernel, nblk_p=nblk_p),
        grid_spec=grid_spec,
        out_shape=jax.ShapeDtypeStruct((n_rows, d), jnp.float32),
        compiler_params=pltpu.CompilerParams(
            dimension_semantics=("arbitrary",), vmem_limit_bytes=VMEM_LIMIT_BYTES),
    )(*tables, topit_p, topit_s, h2_p, h2_s, cbcol)


def _expert_mlp(x, wgu_bf, bgu, wd_bf, bd):
    d_ff = wd_bf.shape[0]
    gu = jnp.dot(_bf(x), wgu_bf[...], preferred_element_type=jnp.float32) + bgu
    glu = jnp.minimum(gu[:, :d_ff], SWIGLU_LIMIT)
    lin = jnp.clip(gu[:, d_ff:], -SWIGLU_LIMIT, SWIGLU_LIMIT)
    act = glu * jax.nn.sigmoid(SWIGLU_ALPHA * glu) * (lin + 1.0)
    return jnp.dot(_bf(act), wd_bf[...], preferred_element_type=jnp.float32) + bd


def _experts_kernel(tile_expert_ref, n_used_ref, groups_ref, next_ref, slot_ref,
                    x_ref, wgu_hbm, bgu_ref, wd_hbm, bd_ref, y_ref,
                    wgu_f32, wd_f32, wgu_bf, wd_bf, sems):
    step = pl.program_id(0)
    groups = groups_ref[step]

    def weight_copies(expert, slot):
        return (pltpu.make_async_copy(wgu_hbm.at[expert], wgu_f32.at[slot], sems.at[slot, 0]),
                pltpu.make_async_copy(wd_hbm.at[expert], wd_f32.at[slot], sems.at[slot, 1]))

    @pl.when(groups > 0)
    def _():
        e = tile_expert_ref[step]
        prev = tile_expert_ref[jnp.maximum(step - 1, 0)]
        slot = slot_ref[e]

        @pl.when(step == 0)
        def _():
            for cp in weight_copies(e, slot):
                cp.start()

        @pl.when(jnp.logical_or(step == 0, e != prev))
        def _():
            for cp in weight_copies(e, slot):
                cp.wait()
            nxt = next_ref[e]

            @pl.when(nxt >= 0)
            def _():
                for cp in weight_copies(nxt, 1 - slot):
                    cp.start()

            wgu_bf[...] = _bf(wgu_f32[slot])
            wd_bf[...] = _bf(wd_f32[slot])

    tm = x_ref.shape[0]
    for n in range(1, EXPERT_GROUPS + 1):
        @pl.when(groups == n)
        def _():
            rows = n * (tm // EXPERT_GROUPS)
            y_ref[0:rows, :] = _expert_mlp(x_ref[0:rows, :], wgu_bf, bgu_ref[0], wd_bf, bd_ref[0])
            if rows < tm:
                y_ref[rows:, :] = jnp.zeros((tm - rows, y_ref.shape[1]), y_ref.dtype)

    @pl.when(groups == 0)
    def _():
        y_ref[...] = jnp.zeros_like(y_ref)


def _experts(tile_expert, n_used, tile_groups, next_used, slot_of, x_sorted, w_gate_up, b_gate_up,
             w_down, b_down):
    p = x_sorted.shape[0]
    n_e, d, two_ff = w_gate_up.shape
    d_ff = two_ff // 2
    tm = EXPERT_TILE
    tile = lambda i, te, nu: jnp.minimum(i, nu[0] - 1)
    grid_spec = pltpu.PrefetchScalarGridSpec(
        num_scalar_prefetch=5,
        grid=(p // tm,),
        in_specs=[pl.BlockSpec((tm, d), lambda i, te, nu, *_: (tile(i, te, nu), 0)),
                  pl.BlockSpec(memory_space=pl.ANY),
                  pl.BlockSpec((1, 1, two_ff), lambda i, te, *_: (te[i], 0, 0)),
                  pl.BlockSpec(memory_space=pl.ANY),
                  pl.BlockSpec((1, 1, d), lambda i, te, *_: (te[i], 0, 0))],
        out_specs=pl.BlockSpec((tm, d), lambda i, *_: (i, 0)),
        scratch_shapes=[pltpu.VMEM((2, d, two_ff), jnp.float32),
                        pltpu.VMEM((2, d_ff, d), jnp.float32),
                        pltpu.VMEM((d, two_ff), jnp.bfloat16),
                        pltpu.VMEM((d_ff, d), jnp.bfloat16),
                        pltpu.SemaphoreType.DMA((2, 2))],
    )
    return pl.pallas_call(
        _experts_kernel,
        grid_spec=grid_spec,
        out_shape=jax.ShapeDtypeStruct(x_sorted.shape, jnp.float32),
        compiler_params=pltpu.CompilerParams(
            dimension_semantics=("arbitrary",), vmem_limit_bytes=VMEM_LIMIT_BYTES),
    )(tile_expert, n_used, tile_groups, next_used, slot_of, x_sorted, w_gate_up,
      b_gate_up.reshape(n_e, 1, two_ff), w_down, b_down.reshape(n_e, 1, d))


def _combine_kernel(cnt_ref, gstart_ref, cb_ref, blkrows_ref,
                    topi_p_ref, topi_s_ref, topw_p_ref, topw_s_ref, x1_p_ref, x1_s_ref,
                    gate_p_ref, gate_s_ref, cbrow_ref, y_ref, out_p_ref, out_s_ref,
                    ybuf, sems, *, nblk_p):
    step = pl.program_id(0)
    nblk = pl.num_programs(0)
    tb = x1_p_ref.shape[0]
    slot = step % STAGE_SLOTS

    def fetch_chunk(blk, into, e):
        i = blk * N_EXPERTS + e
        _chunk_copies(cnt_ref[i], y_ref, gstart_ref[i], ybuf.at[into], cb_ref[i], sems.at[into])

    def fetch(blk, into):
        lax.fori_loop(0, N_EXPERTS, lambda e, c: (fetch_chunk(blk, into, e), c)[1], 0)

    @pl.when(step == 0)
    def _():
        ybuf[...] = jnp.zeros_like(ybuf)
        for ahead in range(STAGE_SLOTS - 1):
            @pl.when(ahead < nblk)
            def _():
                fetch(ahead, ahead)

    for e in range(N_EXPERTS):
        fetch_chunk(step + STAGE_SLOTS - 1, (step + STAGE_SLOTS - 1) % STAGE_SLOTS, e)

    _wait_rows(y_ref, blkrows_ref[step], sems.at[slot])

    is_p = step < nblk_p
    topi = jnp.where(is_p, topi_p_ref[...], topi_s_ref[...])
    topw = jnp.where(is_p, topw_p_ref[...], topw_s_ref[...])
    cbrow = cbrow_ref[0]

    lane = lax.broadcasted_iota(jnp.int32, (tb, LANES), 1)
    picks = []
    onehot = jnp.zeros((tb, LANES), jnp.float32)
    for k in range(TOP_K):
        ix = jnp.sum(jnp.where(lane == k, topi, 0), axis=-1, keepdims=True)
        picks.append(lane == ix)
        onehot = jnp.where(picks[k], 1.0, onehot)
    r = lax.broadcasted_iota(jnp.int32, (tb, tb), 0)
    c = lax.broadcasted_iota(jnp.int32, (tb, tb), 1)
    earlier = jnp.where(c < r, 1.0, 0.0)
    base = _dot(earlier, onehot) + cbrow.astype(jnp.float32)
    cols = [jnp.sum(jnp.where(picks[k], base, 0.0), axis=-1, keepdims=True).astype(jnp.int32)
            for k in range(TOP_K)]
    wks = [jnp.sum(jnp.where(lane == k, topw, 0.0), axis=-1, keepdims=True) for k in range(TOP_K)]
    moe = None
    for c0 in range(0, ybuf.shape[1], COMBINE_SLAB):
        colid = lax.broadcasted_iota(jnp.int32, (tb, COMBINE_SLAB), 1) + c0
        weights = jnp.zeros((tb, COMBINE_SLAB), jnp.float32)
        for k in range(TOP_K):
            weights = jnp.where(colid == cols[k], wks[k], weights)
        part = _dot(weights, ybuf[slot, c0:c0 + COMBINE_SLAB, :])
        moe = part if moe is None else moe + part

    @pl.when(is_p)
    def _():
        out_p_ref[...] = x1_p_ref[...] + gate_p_ref[...] * moe

    @pl.when(jnp.logical_not(is_p))
    def _():
        gate_s = _repeat_rows(gate_s_ref[...], tb // gate_s_ref.shape[0])
        out_s_ref[...] = x1_s_ref[...] + gate_s * moe


def _combine(tables, topi_p, topi_s, topw_p, topw_s, x1_p, x1_s, gate_p, mod_seq, dec_seq, cbrow,
             y_sorted):
    n_p, d = x1_p.shape
    n_s = x1_s.shape[0]
    tb = ROUTE_BLOCK
    nblk_p, nblk_s = n_p // tb, n_s // tb
    pidx = lambda i, *_: jnp.minimum(i, nblk_p - 1)
    sidx = lambda i, *_: jnp.maximum(i - nblk_p, 0)
    prow = lambda w: pl.BlockSpec((tb, w), lambda i, *_: (pidx(i), 0))
    srow = lambda w: pl.BlockSpec((tb, w), lambda i, *_: (sidx(i), 0))
    grid_spec = pltpu.PrefetchScalarGridSpec(
        num_scalar_prefetch=4,
        grid=(nblk_p + nblk_s,),
        in_specs=[prow(LANES), srow(LANES), prow(LANES), srow(LANES), prow(d), srow(d),
                  pl.BlockSpec((1, d), lambda i, *_: (0, 0)),
                  pl.BlockSpec((tb // dec_seq, d), lambda i, *_: (sidx(i), 5)),
                  pl.BlockSpec((1, 1, LANES), lambda i, *_: (i, 0, 0)),
                  pl.BlockSpec(memory_space=pl.ANY)],
        out_specs=[prow(d), srow(d)],
        scratch_shapes=[pltpu.VMEM((STAGE_SLOTS, STAGE_ROWS, d), jnp.float32),
                        pltpu.SemaphoreType.DMA((STAGE_SLOTS,))],
    )
    return pl.pallas_call(
        functools.partial(_combine_kernel, nblk_p=nblk_p),
        grid_spec=grid_spec,
        out_shape=[jax.ShapeDtypeStruct((n_p, d), jnp.float32),
                   jax.ShapeDtypeStruct((n_s, d), jnp.float32)],
        compiler_params=pltpu.CompilerParams(
            dimension_semantics=("arbitrary",), vmem_limit_bytes=VMEM_LIMIT_BYTES),
    )(*tables, topi_p, topi_s, topw_p, topw_s, x1_p, x1_s, gate_p, mod_seq, cbrow, y_sorted)


def _rotary_tables(pos, inv_freq):
    ang = pos.astype(jnp.float32)[:, None] * inv_freq[None, :]
    cos, sin = jnp.cos(ang), jnp.sin(ang)
    cos_t = jnp.tile(cos, (1, LANES // cos.shape[1]))
    sin_t = jnp.tile(jnp.concatenate([-sin, sin], axis=1), (1, LANES // (2 * sin.shape[1])))
    return cos_t, sin_t


def _rotary_split_tables(block, nblk, inv_freq):
    half = inv_freq.shape[0]
    freq = jnp.tile(inv_freq, LANES // half)
    sign = jnp.tile(jnp.concatenate([-jnp.ones((half,), jnp.float32),
                                     jnp.ones((half,), jnp.float32)]), LANES // (2 * half))
    base = (jnp.arange(nblk, dtype=jnp.int32) * block).astype(jnp.float32)[:, None] * freq[None, :]
    off = jnp.arange(block, dtype=jnp.int32).astype(jnp.float32)[:, None] * freq[None, :]
    blk = jnp.stack([jnp.cos(base), jnp.sin(base)], axis=1)
    tab = jnp.stack([jnp.cos(off), jnp.sin(off), jnp.cos(off) * sign, jnp.sin(off) * sign])
    return blk, tab


def _retention_tables(chunk):
    log_gamma = jnp.log1p(-jnp.exp2(-5.0 - jnp.arange(RET_HEADS, dtype=jnp.float32)))
    idx = jnp.arange(chunk, dtype=jnp.float32)
    diff = idx[:, None] - idx[None, :]
    decay = jnp.where(diff[None] >= 0,
                      jnp.exp(jnp.maximum(diff, 0.0)[None] * log_gamma[:, None, None]), 0.0)
    q_decay = jnp.exp((idx + 1.0)[:, None] * log_gamma[None, :])
    k_decay = jnp.exp((chunk - 1.0 - idx)[:, None] * log_gamma[None, :])
    qdec = jnp.repeat(q_decay, RET_DV, axis=1)
    kdec = jnp.repeat(k_decay, RET_DK, axis=1)
    g_chunk = jnp.exp(chunk * log_gamma)
    gpow = jnp.repeat(g_chunk.reshape(RET_HEADS // 2, 2), RET_DK, axis=1).T
    return decay, qdec, kdec, gpow


def _block_diag_mean(width):
    head = jnp.arange(width) // HEAD_DIM
    return jnp.where(head[:, None] == head[None, :], 1.0 / HEAD_DIM, 0.0).astype(jnp.bfloat16)


def kernel(x_prompt, x_sample, cache_k, cache_v, state_ret, c_prompt, c_sample, w_ada, b_ada,
           g_mix, w_in, q_gain, k_gain, sinks, w_out, g_ffn, w_router, b_router, w_gate_up,
           b_gate_up, w_down, b_down):
    depth = w_ada.shape[0]
    batch, seq, d = x_prompt.shape
    dec_batch, dec_seq, _ = x_sample.shape
    cache_w = cache_k.shape[2]
    assert batch == 1 and depth == 1
    assert seq % PROMPT_BLOCK == 0 and PROMPT_BLOCK % WINDOW == 0 and WINDOW == RET_CHUNK
    assert dec_batch % SAMPLE_SEQS == 0 and cache_w == WINDOW
    n_p, n_s = batch * seq, dec_batch * dec_seq
    n_tok = n_p + n_s
    assert n_p % ROUTE_BLOCK == 0 and n_s % ROUTE_BLOCK == 0
    assert ROUTE_BLOCK % COUNT_BLOCK == 0 and (SAMPLE_SEQS * dec_seq) % COUNT_BLOCK == 0
    assert PROMPT_BLOCK % COUNT_BLOCK == 0
    assert STAGE_ROWS % COMBINE_SLAB == 0 and EXPERT_TILE % (EXPERT_GROUPS * SUBLANES) == 0

    l = 0
    f32 = jnp.float32
    rope_freq = 1.0 / (ROPE_THETA ** (jnp.arange(0, HEAD_DIM, 2, dtype=f32) / HEAD_DIM))
    ret_freq = 1.0 / (ROPE_THETA ** jnp.linspace(0.0, 1.0, RET_DK // 2, dtype=f32))
    pos_s = PAST_LEN + jnp.arange(dec_seq, dtype=jnp.int32)
    blk_a, tab_a = _rotary_split_tables(PROMPT_BLOCK, seq // PROMPT_BLOCK, rope_freq)
    blk_r, tab_r = _rotary_split_tables(PROMPT_BLOCK, seq // PROMPT_BLOCK, ret_freq)
    tabs_p = (jnp.concatenate([blk_a, blk_r], axis=1), jnp.concatenate([tab_a, tab_r], axis=0))
    tabs_s = tuple(jnp.tile(a, (SAMPLE_SEQS, 1))
                   for a in _rotary_tables(pos_s, rope_freq) + _rotary_tables(pos_s, ret_freq))
    ret_p = _retention_tables(RET_CHUNK)
    decay_s, qdec_s, kdec_s, gpow_s = _retention_tables(dec_seq)
    eye = jnp.eye(SAMPLE_SEQS, dtype=f32)
    rows_s = SAMPLE_SEQS * dec_seq
    ret_s = ((eye[None, :, None, :, None] * decay_s[:, None, :, None, :]).reshape(RET_HEADS, rows_s, rows_s),
             jnp.tile(qdec_s, (SAMPLE_SEQS, 1)), jnp.tile(kdec_s, (SAMPLE_SEQS, 1)), gpow_s)

    n_c = batch + dec_batch
    c_rows = -(-n_c // SUBLANES) * SUBLANES
    c_all = jnp.concatenate([c_sample, c_prompt, jnp.zeros((c_rows - n_c, d), f32)], axis=0)
    mod = _adaln(c_all, w_ada[l], b_ada[l])
    mod_p = mod[dec_batch].reshape(6, d)

    w_in_bf = _bf(w_in[l])
    w_out_bf = _bf(w_out[l])
    qg = jnp.tile(q_gain[l], ATT_HEADS).reshape(1, ATT_W)
    kg = jnp.tile(k_gain[l], ATT_KV_HEADS).reshape(1, KV_W)
    bdq, bdk = _block_diag_mean(ATT_W), _block_diag_mean(KV_W)
    gmix = g_mix[l].reshape(1, d)
    gffn = g_ffn[l].reshape(1, d)
    wr = jnp.pad(w_router[l], ((0, 0), (0, LANES - N_EXPERTS)))
    wr_hi = _bf(wr)
    wr_cat = jnp.concatenate([wr_hi, _bf(wr - wr_hi.astype(f32))], axis=1)
    br = jnp.concatenate([b_router[l], jnp.full((LANES - N_EXPERTS,), NEG_BIG, f32)]).reshape(1, LANES)
    shared = (sinks[l], gmix, gffn, w_in_bf, qg, kg, bdq, bdk)

    (x1_p, h2_p, topi_p, topw_p, topit_p, cnt_p, kwin_p, vwin_p, rend_p) = _prompt_mix(
        x_prompt.reshape(n_p, d), mod_p, *shared, tabs_p, ret_p, w_out_bf, wr_hi, wr_cat, br)

    (x1_s, h2_s, topi_s, topw_s, topit_s, cnt_s, kwin_s, vwin_s, rend_s) = _sample_mix(
        x_sample.reshape(n_s, d), mod, *shared, tabs_s, ret_s, w_out_bf, wr_hi, wr_cat, br,
        cache_k[l].reshape(dec_batch, cache_w, KV_W), cache_v[l].reshape(dec_batch, cache_w, KV_W),
        state_ret[l].reshape(dec_batch, 2, LANES, RET_DV))

    tb, tm = ROUTE_BLOCK, EXPERT_TILE
    i32 = jnp.int32
    cnt = jnp.concatenate([cnt_p, cnt_s], axis=0)
    cnt = cnt.reshape(n_tok // tb, tb // COUNT_BLOCK, LANES).sum(axis=1)
    cnt8 = -(-cnt // SUBLANES) * SUBLANES
    rs = jnp.cumsum(cnt8, axis=0) - cnt8
    tot = jnp.sum(cnt8, axis=0)[:N_EXPERTS]
    padded = -(-tot // tm) * tm
    ends = jnp.cumsum(padded)
    gstart = (ends - padded)[None, :] + rs[:, :N_EXPERTS]
    cb = jnp.cumsum(cnt8, axis=1) - cnt8
    blk_rows = jnp.sum(cnt8, axis=1).astype(i32)
    max_rows = n_tok * TOP_K + cnt.shape[0] * N_EXPERTS * (SUBLANES - 1)
    n_rows = -(-max_rows // tm) * tm + N_EXPERTS * tm
    tile_start = jnp.arange(n_rows // tm, dtype=i32) * tm
    tile_expert = jnp.minimum(jnp.sum(ends[None, :] <= tile_start[:, None], axis=1),
                              N_EXPERTS - 1).astype(i32)
    n_used = (ends[-1:] // tm).astype(i32)
    eid = jnp.arange(N_EXPERTS, dtype=i32)
    of_tile = lambda v: jnp.sum(jnp.where(tile_expert[:, None] == eid[None, :], v[None, :], 0), axis=1)
    tile_rows = jnp.clip(of_tile(tot) - (tile_start - of_tile(ends - padded)), 0, tm)
    tile_rows = jnp.where(tile_start < ends[-1], tile_rows, 0)
    tile_groups = (-(-tile_rows // (tm // EXPERT_GROUPS))).astype(i32)
    flat = lambda a: a[:, :N_EXPERTS].reshape(-1).astype(i32)
    tables = (flat(cnt8), gstart.reshape(-1).astype(i32), flat(cb), blk_rows)
    ahead = jnp.zeros(((STAGE_SLOTS - 1) * N_EXPERTS,), i32)
    tables_c = tuple(jnp.concatenate([t, ahead]) for t in tables[:3]) + tables[3:]

    pad_cnt = padded - tot
    pad_cnt = jnp.concatenate([pad_cnt, jnp.sum(pad_cnt, keepdims=True)]).astype(i32)
    x_sorted = _dispatch(tables + (ends.astype(i32), pad_cnt), topit_p, topit_s, h2_p, h2_s,
                         cb.astype(i32)[:, :, None], n_rows)
    used = tot > 0
    later = jnp.where(used[None, :] & (eid[None, :] > eid[:, None]), eid[None, :], N_EXPERTS)
    next_used = jnp.min(later, axis=1)
    next_used = jnp.where(next_used == N_EXPERTS, -1, next_used).astype(i32)
    slot_of = ((jnp.cumsum(used.astype(i32)) - 1) % 2).astype(i32)
    y_sorted = _experts(tile_expert, n_used, tile_groups, next_used, slot_of, x_sorted, w_gate_up[l], b_gate_up[l], w_down[l],
                        b_down[l])
    y_p, y_s = _combine(tables_c, topi_p, topi_s, topw_p, topw_s, x1_p, x1_s, mod_p[5:6], mod,
                        dec_seq, cb.astype(i32)[:, None, :], y_sorted)

    kv5 = lambda a, n: a.reshape(1, n, cache_w, ATT_KV_HEADS, HEAD_DIM)
    st5 = lambda a, n: a.reshape(1, n, RET_HEADS, RET_DK, RET_DV)
    return (y_p.reshape(batch, seq, d), y_s.reshape(dec_batch, dec_seq, d),
            kv5(kwin_p, batch), kv5(vwin_p, batch), st5(rend_p, batch),
            kv5(kwin_s, dec_batch), kv5(vwin_s, dec_batch), st5(rend_s, dec_batch))
```

```python
import functools

import jax
import jax.numpy as jnp
import numpy as np
from jax import lax
from jax.experimental import pallas as pl
from jax.experimental.pallas import tpu as pltpu

HEAD_DIM = 64
ATT_HEADS = 8
ATT_KV_HEADS = 2
ATT_W = ATT_HEADS * HEAD_DIM
KV_W = ATT_KV_HEADS * HEAD_DIM
WINDOW = 128
PAST_LEN = 16384
ROPE_THETA = 10000.0
RET_HEADS = 4
RET_DK = 64
RET_DV = 128
RET_QK_W = RET_HEADS * RET_DK
RET_W = RET_HEADS * RET_DV
RET_CHUNK = 128
N_EXPERTS = 32
TOP_K = 4
SWIGLU_LIMIT = 7.0
SWIGLU_ALPHA = 1.702
EPS = 1e-6

_Q0, _K0, _V0 = 0, ATT_W, ATT_W + KV_W
_RQ0 = ATT_W + 2 * KV_W
_RK0 = _RQ0 + RET_QK_W
_RV0 = _RK0 + RET_QK_W
_RG0 = _RV0 + RET_W
IN_W = _RG0 + RET_W

LANES = 128
SUBLANES = 8
VMEM_LIMIT_BYTES = 56 * 1024 * 1024
NEG_BIG = -1e30

PROMPT_BLOCK = 512
SAMPLE_SEQS = 16
COUNT_BLOCK = 128
ROUTE_BLOCK = 512
EXPERT_TILE = 512
EXPERT_GROUPS = 4
CHUNK_BITS = ROUTE_BLOCK.bit_length()
STAGE_ROWS = -(-(TOP_K * ROUTE_BLOCK + N_EXPERTS * (SUBLANES - 1)) // LANES) * LANES
STAGE_BITS = STAGE_ROWS.bit_length()
STAGE_SLOTS = 3
COMBINE_SLAB = 256

_NT = (((1,), (1,)), ((), ()))
_TN = (((0,), (0,)), ((), ()))


def _bf(x):
    return x.astype(jnp.bfloat16)


def _dot(a, b):
    return jnp.dot(_bf(a), _bf(b), preferred_element_type=jnp.float32)


def _dot_nt(a, b):
    return lax.dot_general(_bf(a), _bf(b), _NT, preferred_element_type=jnp.float32)


def _dot_tn(a, b):
    return lax.dot_general(_bf(a), _bf(b), _TN, preferred_element_type=jnp.float32)


def _lane_lo(shape):
    lane = lax.broadcasted_iota(jnp.int32, shape, len(shape) - 1)
    return (lane % LANES) < HEAD_DIM


def _swap_halves(x):
    lane = lax.broadcasted_iota(jnp.int32, x.shape, 1)
    first = (lane % HEAD_DIM) < (HEAD_DIM // 2)
    return jnp.where(first, pltpu.roll(x, LANES - HEAD_DIM // 2, axis=1),
                     pltpu.roll(x, HEAD_DIM // 2, axis=1))


def _rotate(x, cos, sin_signed):
    outs = []
    for j in range(x.shape[1] // LANES):
        xs = x[:, j * LANES:(j + 1) * LANES]
        outs.append(xs * cos + _swap_halves(xs) * sin_signed)
    return outs[0] if len(outs) == 1 else jnp.concatenate(outs, axis=1)


def _repeat_rows(m, n):
    r, w = m.shape
    return jnp.broadcast_to(m[:, None, :], (r, n, w)).reshape(r * n, w)


def _block_rotary(blk_ref, tab_ref, which):
    ca, sa = blk_ref[0, 2 * which:2 * which + 1, :], blk_ref[0, 2 * which + 1:2 * which + 2, :]
    cb, sb, cbs, sbs = (tab_ref[4 * which + n] for n in range(4))
    return ca * cb - sa * sb, sa * cbs + ca * sbs


def _rms_rows(x):
    return x * lax.rsqrt(jnp.mean(x * x, axis=-1, keepdims=True) + EPS)


def _pre_norm(x, gain, shift, scale):
    return _rms_rows(x) * gain * (1.0 + scale) + shift


def _head_norm(x, gain_tiled, blockdiag):
    ms = jnp.dot(_bf(x * x), blockdiag, preferred_element_type=jnp.float32)
    return x * lax.rsqrt(ms + EPS) * gain_tiled


def _project(h, w_in_ref, qg, kg, bd_q, bd_k, cos_a, sin_a, cos_r, sin_r):
    z = jnp.dot(_bf(h), w_in_ref[...], preferred_element_type=jnp.float32)
    q = _rotate(_head_norm(z[:, _Q0:_K0], qg, bd_q), cos_a, sin_a) * (HEAD_DIM ** -0.5)
    k = _rotate(_head_norm(z[:, _K0:_V0], kg, bd_k), cos_a, sin_a)
    v = z[:, _V0:_RQ0]
    rq = _rotate(z[:, _RQ0:_RK0], cos_r, sin_r)
    rk = _rotate(z[:, _RK0:_RV0], cos_r, sin_r) * (RET_DK ** -0.5)
    rv = z[:, _RV0:_RG0]
    rg = z[:, _RG0:IN_W]
    return q, k, v, rq, rk, rv, rg


_NAT_HEADS = (0, 2, 5, 7)
_ROL_HEADS = (1, 3, 4, 6)


def _stack_heads(q, heads):
    lo = _lane_lo((q.shape[0], LANES))
    parts = []
    for j in heads:
        slab = q[:, (j // 2) * LANES:(j // 2 + 1) * LANES]
        parts.append(jnp.where(lo if j % 2 == 0 else ~lo, slab, 0.0))
    return jnp.concatenate(parts, axis=0)


def _sink_column(sinks_ref, heads, rows):
    r = lax.broadcasted_iota(jnp.int32, (len(heads) * rows, 1), 0)
    col = jnp.full((len(heads) * rows, 1), sinks_ref[heads[-1]], jnp.float32)
    for n in range(len(heads) - 2, -1, -1):
        col = jnp.where(r < (n + 1) * rows, sinks_ref[heads[n]], col)
    return col


def _softmax_pv(s, mask, sink, v):
    s = jnp.where(mask, s, NEG_BIG)
    m = jnp.maximum(jnp.max(s, axis=-1, keepdims=True), sink)
    p = jnp.exp(s - m)
    denom = jnp.sum(p, axis=-1, keepdims=True) + jnp.exp(sink - m)
    return _dot(p, v) / denom


def _attention(q, k_all, v_all, mask, sinks_ref):
    rows = q.shape[0]
    k_rol = pltpu.roll(k_all, HEAD_DIM, axis=1)
    v_rol = pltpu.roll(v_all, HEAD_DIM, axis=1)
    mask4 = jnp.concatenate([mask] * 4, axis=0)
    o_nat = _softmax_pv(_dot_nt(_stack_heads(q, _NAT_HEADS), k_all), mask4,
                        _sink_column(sinks_ref, _NAT_HEADS, rows), v_all)
    o_rol = _softmax_pv(_dot_nt(_stack_heads(q, _ROL_HEADS), k_rol), mask4,
                        _sink_column(sinks_ref, _ROL_HEADS, rows), v_rol)
    lo = _lane_lo((rows, LANES))
    blk = lambda o, n: o[n * rows:(n + 1) * rows]
    return jnp.concatenate([
        jnp.where(lo, blk(o_nat, 0), blk(o_rol, 0)),
        jnp.where(lo, blk(o_nat, 1), blk(o_rol, 1)),
        jnp.where(lo, blk(o_rol, 2), blk(o_nat, 2)),
        jnp.where(lo, blk(o_rol, 3), blk(o_nat, 3)),
    ], axis=1)


def _retention(rq, rk, rv, rg, state, decay_ref, qdec, kdec, gpow):
    rows = rq.shape[0]
    lo = _lane_lo((rows, LANES))
    rkd = rk * kdec
    outs, new_state = [], []
    for m in range(RET_HEADS // 2):
        sl = slice(m * LANES, (m + 1) * LANES)
        upd = state[m] * gpow[:, m:m + 1]
        for p in range(2):
            h = 2 * m + p
            half = lo if p == 0 else ~lo
            rqm = jnp.where(half, rq[:, sl], 0.0)
            rv_h = rv[:, h * RET_DV:(h + 1) * RET_DV]
            s = _dot_nt(rqm, rk[:, sl]) * decay_ref[h]
            ret = _dot(s, rv_h) + _dot(rqm, state[m]) * qdec[:, h * RET_DV:(h + 1) * RET_DV]
            upd = upd + _dot_tn(jnp.where(half, rkd[:, sl], 0.0), rv_h)
            rg_h = rg[:, h * RET_DV:(h + 1) * RET_DV]
            outs.append(rg_h * jax.nn.sigmoid(rg_h) * _rms_rows(ret))
        new_state.append(upd)
    return jnp.concatenate(outs, axis=1), new_state


def _route(h2, wr_hi_ref, wr_cat_ref, br):
    h_hi = _bf(h2)
    h_lo = _bf(h2 - h_hi.astype(jnp.float32))
    both = jnp.dot(h_hi, wr_cat_ref[...], preferred_element_type=jnp.float32)
    logits = (both[:, :LANES] + both[:, LANES:]
              + jnp.dot(h_lo, wr_hi_ref[...], preferred_element_type=jnp.float32)) + br
    lane = lax.broadcasted_iota(jnp.int32, logits.shape, 1)
    idx_out = jnp.zeros(logits.shape, jnp.int32)
    val_out = jnp.zeros(logits.shape, jnp.float32)
    onehot = jnp.zeros(logits.shape, jnp.float32)
    vals = []
    work = logits
    for k in range(TOP_K):
        mx = jnp.max(work, axis=-1, keepdims=True)
        ix = jnp.min(jnp.where(work == mx, lane, LANES), axis=-1, keepdims=True)
        work = jnp.where(lane == ix, NEG_BIG, work)
        onehot = jnp.where(lane == ix, 1.0, onehot)
        idx_out = jnp.where(lane == k, ix, idx_out)
        vals.append(mx)
    exps = [jnp.exp(v - vals[0]) for v in vals]
    total = exps[0] + exps[1] + exps[2] + exps[3]
    for k in range(TOP_K):
        val_out = jnp.where(lane == k, exps[k] / total, val_out)
    idx_t = idx_out.astype(jnp.float32).T[0:SUBLANES].astype(jnp.int32)
    counts = jnp.sum(onehot.reshape(-1, COUNT_BLOCK, LANES), axis=1).astype(jnp.int32)
    return idx_out, val_out, idx_t, counts


def _adaln_kernel(c_ref, w_ref, b_ref, o_ref):
    c = c_ref[...]
    o_ref[...] = _dot(c * jax.nn.sigmoid(c), w_ref[...]) + b_ref[...]


def _adaln(c_all, w_ada, b_ada):
    rows, d = c_all.shape
    n = w_ada.shape[1]
    bn = 1536
    return pl.pallas_call(
        _adaln_kernel,
        grid=(n // bn,),
        in_specs=[pl.BlockSpec((rows, d), lambda j: (0, 0)),
                  pl.BlockSpec((d, bn), lambda j: (0, j)),
                  pl.BlockSpec((1, bn), lambda j: (0, j))],
        out_specs=pl.BlockSpec((rows, bn), lambda j: (0, j)),
        out_shape=jax.ShapeDtypeStruct((rows, n), jnp.float32),
        compiler_params=pltpu.CompilerParams(vmem_limit_bytes=VMEM_LIMIT_BYTES),
    )(c_all, w_ada, b_ada.reshape(1, n))


def _prompt_mix_kernel(sinks_ref, x_ref, mod_ref, gmix_ref, gffn_ref, w_in_ref, qg_ref, kg_ref,
                       bdq_ref, bdk_ref, rotblk_ref, rottab_ref,
                       decay_ref, qdec_ref, kdec_ref, gpow_ref, w_out_ref,
                       wrh_ref, wrl_ref, br_ref,
                       x1_ref, h2_ref, topi_ref, topw_ref, topit_ref, cnt_ref, kwin_ref, vwin_ref, rend_ref,
                       kprev, vprev, state, mix):
    step = pl.program_id(0)
    tb = x_ref.shape[0]

    @pl.when(step == 0)
    def _():
        kprev[...] = jnp.zeros_like(kprev)
        vprev[...] = jnp.zeros_like(vprev)
        state[...] = jnp.zeros_like(state)

    x = x_ref[...]
    h = _pre_norm(x, gmix_ref[...], mod_ref[0:1, :], mod_ref[1:2, :])
    q, k, v, rq, rk, rv, rg = _project(
        h, w_in_ref, qg_ref[...], kg_ref[...], bdq_ref[...], bdk_ref[...],
        *_block_rotary(rotblk_ref, rottab_ref, 0), *_block_rotary(rotblk_ref, rottab_ref, 1))

    qi = lax.broadcasted_iota(jnp.int32, (WINDOW, 2 * WINDOW), 0)
    ci = lax.broadcasted_iota(jnp.int32, (WINDOW, 2 * WINDOW), 1)
    band = (ci > qi) & (ci <= qi + WINDOW)
    qdec, kdec, gpow = qdec_ref[...], kdec_ref[...], gpow_ref[...]

    n_sb = tb // WINDOW
    rows = [slice(sb * WINDOW, (sb + 1) * WINDOW) for sb in range(n_sb)]
    k_prev, v_prev = kprev[...], vprev[...]
    for sb, rs in enumerate(rows):
        k_sb, v_sb = k[rs], v[rs]
        mask = band & ((ci >= WINDOW) | (step > 0)) if sb == 0 else band
        att = _attention(q[rs], jnp.concatenate([k_prev, k_sb], axis=0),
                         jnp.concatenate([v_prev, v_sb], axis=0), mask, sinks_ref)
        mix[rs, 0:ATT_W] = _bf(att)
        k_prev, v_prev = k_sb, v_sb
    kprev[...] = k_prev
    vprev[...] = v_prev

    cur_state = [state[0], state[1]]
    for rs in rows:
        yret, cur_state = _retention(rq[rs], rk[rs], rv[rs], rg[rs], cur_state,
                                     decay_ref, qdec, kdec, gpow)
        mix[rs, ATT_W:ATT_W + RET_W] = _bf(yret)
    state[0] = cur_state[0]
    state[1] = cur_state[1]

    x1 = x + mod_ref[2:3, :] * jnp.dot(mix[...], w_out_ref[...],
                                       preferred_element_type=jnp.float32)
    x1_ref[...] = x1
    h2 = _pre_norm(x1, gffn_ref[...], mod_ref[3:4, :], mod_ref[4:5, :])
    h2_ref[...] = h2
    topi, topw, topi_t, counts = _route(h2, wrh_ref, wrl_ref, br_ref[...])
    topi_ref[...] = topi
    topw_ref[...] = topw
    topit_ref[...] = topi_t
    cnt_ref[:, 0, :] = counts

    @pl.when(step == pl.num_programs(0) - 1)
    def _():
        kwin_ref[...] = kprev[...]
        vwin_ref[...] = vprev[...]
        rend_ref[...] = state[...]


def _const_spec(shape):
    nd = len(shape)
    return pl.BlockSpec(shape, lambda *_: (0,) * nd)


def _prompt_mix(x, mod, sinks, gmix, gffn, w_in, qg, kg, bdq, bdk, tabs, ret_tabs, w_out,
                wrh, wrl, br):
    t, d = x.shape
    tb = PROMPT_BLOCK
    rot_blk, rot_tab = tabs
    decay, qdec, kdec, gpow = ret_tabs
    row_spec = lambda w: pl.BlockSpec((tb, w), lambda i, *_: (i, 0))
    consts = [mod, gmix, gffn, w_in, qg, kg, bdq, bdk]
    consts2 = [rot_tab, decay, qdec, kdec, gpow, w_out, wrh, wrl, br]
    grid_spec = pltpu.PrefetchScalarGridSpec(
        num_scalar_prefetch=1,
        grid=(t // tb,),
        in_specs=([row_spec(d)] + [_const_spec(a.shape) for a in consts]
                  + [pl.BlockSpec((1,) + rot_blk.shape[1:], lambda i, *_: (i, 0, 0))]
                  + [_const_spec(a.shape) for a in consts2]),
        out_specs=[row_spec(d), row_spec(d), row_spec(LANES), row_spec(LANES),
                   pl.BlockSpec((SUBLANES, tb), lambda i, *_: (0, i)),
                   pl.BlockSpec((tb // COUNT_BLOCK, 1, LANES), lambda i, *_: (i, 0, 0)),
                   _const_spec((WINDOW, KV_W)), _const_spec((WINDOW, KV_W)),
                   _const_spec((2, LANES, RET_DV))],
        scratch_shapes=[pltpu.VMEM((WINDOW, KV_W), jnp.float32),
                        pltpu.VMEM((WINDOW, KV_W), jnp.float32),
                        pltpu.VMEM((2, LANES, RET_DV), jnp.float32),
                        pltpu.VMEM((tb, ATT_W + RET_W), jnp.bfloat16)],
    )
    return pl.pallas_call(
        _prompt_mix_kernel,
        grid_spec=grid_spec,
        out_shape=[jax.ShapeDtypeStruct((t, d), jnp.float32),
                   jax.ShapeDtypeStruct((t, d), jnp.float32),
                   jax.ShapeDtypeStruct((t, LANES), jnp.int32),
                   jax.ShapeDtypeStruct((t, LANES), jnp.float32),
                   jax.ShapeDtypeStruct((SUBLANES, t), jnp.int32),
                   jax.ShapeDtypeStruct((t // COUNT_BLOCK, 1, LANES), jnp.int32),
                   jax.ShapeDtypeStruct((WINDOW, KV_W), jnp.float32),
                   jax.ShapeDtypeStruct((WINDOW, KV_W), jnp.float32),
                   jax.ShapeDtypeStruct((2, LANES, RET_DV), jnp.float32)],
        compiler_params=pltpu.CompilerParams(
            dimension_semantics=("arbitrary",), vmem_limit_bytes=VMEM_LIMIT_BYTES),
    )(sinks, x, *consts, rot_blk, *consts2)


def _per_seq(stacked, rows, ds, s):
    return jnp.concatenate([stacked[g * rows + s * ds:g * rows + (s + 1) * ds]
                            for g in range(stacked.shape[0] // rows)], axis=0)


def _from_per_seq(parts, ds):
    groups = parts[0].shape[0] // ds
    return jnp.concatenate([p[g * ds:(g + 1) * ds] for g in range(groups) for p in parts], axis=0)


def _sample_attention(q, k_new, v_new, ck_ref, cv_ref, sinks_ref, ds):
    rows = q.shape[0]
    nseq, cache_w = ck_ref.shape[0], ck_ref.shape[1]
    tok_c = lax.broadcasted_iota(jnp.int32, (4 * rows, cache_w), 0) % rows
    col_c = lax.broadcasted_iota(jnp.int32, (4 * rows, cache_w), 1)
    delta = tok_c % ds + cache_w - col_c
    mask_c = (delta >= 0) & (delta < WINDOW)
    tok_n = lax.broadcasted_iota(jnp.int32, (4 * rows, rows), 0) % rows
    col_n = lax.broadcasted_iota(jnp.int32, (4 * rows, rows), 1)
    mask_n = (tok_n // ds == col_n // ds) & (col_n % ds <= tok_n % ds)

    outs = []
    for heads, rolled in ((_NAT_HEADS, False), (_ROL_HEADS, True)):
        arrange = (lambda a: pltpu.roll(a, HEAD_DIM, axis=1)) if rolled else (lambda a: a)
        qst = _stack_heads(q, heads)
        s_new = jnp.where(mask_n, _dot_nt(qst, arrange(k_new)), NEG_BIG)
        s_cache = _from_per_seq(
            [_dot_nt(_per_seq(qst, rows, ds, s), arrange(ck_ref[s])) for s in range(nseq)], ds)
        s_cache = jnp.where(mask_c, s_cache, NEG_BIG)
        sink = _sink_column(sinks_ref, heads, rows)
        m = jnp.maximum(jnp.maximum(jnp.max(s_cache, axis=-1, keepdims=True),
                                    jnp.max(s_new, axis=-1, keepdims=True)), sink)
        p_cache = jnp.exp(s_cache - m)
        p_new = jnp.exp(s_new - m)
        denom = (jnp.sum(p_cache, axis=-1, keepdims=True) + jnp.sum(p_new, axis=-1, keepdims=True)
                 + jnp.exp(sink - m))
        o = _from_per_seq(
            [_dot(_per_seq(p_cache, rows, ds, s), arrange(cv_ref[s])) for s in range(nseq)], ds)
        outs.append((o + _dot(p_new, arrange(v_new))) / denom)
    o_nat, o_rol = outs
    lo = _lane_lo((rows, LANES))
    blk = lambda o, n: o[n * rows:(n + 1) * rows]
    return jnp.concatenate([
        jnp.where(lo, blk(o_nat, 0), blk(o_rol, 0)),
        jnp.where(lo, blk(o_nat, 1), blk(o_rol, 1)),
        jnp.where(lo, blk(o_rol, 2), blk(o_nat, 2)),
        jnp.where(lo, blk(o_rol, 3), blk(o_nat, 3)),
    ], axis=1)


def _sample_retention(rq, rk, rv, rg, st_ref, rend_ref, decay_ref, qdec, kdec, gpow, ds):
    rows = rq.shape[0]
    nseq = st_ref.shape[0]
    lo = _lane_lo((rows, LANES))
    rkd = rk * kdec
    outs = []
    for m in range(RET_HEADS // 2):
        sl = slice(m * LANES, (m + 1) * LANES)
        heads = (2 * m, 2 * m + 1)
        rqm = [jnp.where(lo if p == 0 else ~lo, rq[:, sl], 0.0) for p in range(2)]
        rkdm = [jnp.where(lo if p == 0 else ~lo, rkd[:, sl], 0.0) for p in range(2)]
        rv_h = [rv[:, h * RET_DV:(h + 1) * RET_DV] for h in heads]
        q2 = jnp.concatenate(rqm, axis=0)
        k2 = jnp.concatenate(rkdm, axis=0)
        v2 = jnp.concatenate(rv_h, axis=0)
        inter = _from_per_seq(
            [_dot(_per_seq(q2, rows, ds, s), st_ref[s, m]) for s in range(nseq)], ds)
        for s in range(nseq):
            rend_ref[s, m] = (st_ref[s, m] * gpow[:, m:m + 1]
                              + _dot_tn(_per_seq(k2, rows, ds, s), _per_seq(v2, rows, ds, s)))
        for p, h in enumerate(heads):
            intra = _dot(_dot_nt(rqm[p], rk[:, sl]) * decay_ref[h], rv_h[p])
            ret = intra + inter[p * rows:(p + 1) * rows] * qdec[:, h * RET_DV:(h + 1) * RET_DV]
            rg_h = rg[:, h * RET_DV:(h + 1) * RET_DV]
            outs.append(rg_h * jax.nn.sigmoid(rg_h) * _rms_rows(ret))
    return jnp.concatenate(outs, axis=1)


def _sample_mix_kernel(sinks_ref, x_ref, mod_ref, gmix_ref, gffn_ref, w_in_ref, qg_ref, kg_ref,
                       bdq_ref, bdk_ref, cosa_ref, sina_ref, cosr_ref, sinr_ref,
                       decay_ref, qdec_ref, kdec_ref, gpow_ref, w_out_ref,
                       wrh_ref, wrl_ref, br_ref, ck_ref, cv_ref, st_ref,
                       x1_ref, h2_ref, topi_ref, topw_ref, topit_ref, cnt_ref, kwin_ref, vwin_ref, rend_ref,
                       mix):
    nseq, cache_w = ck_ref.shape[0], ck_ref.shape[1]
    ds = x_ref.shape[0] // nseq
    d = x_ref.shape[1]
    x = x_ref[...]
    mod = lambda i: _repeat_rows(mod_ref[:, i * d:(i + 1) * d], ds)
    h = _pre_norm(x, gmix_ref[...], mod(0), mod(1))
    q, k, v, rq, rk, rv, rg = _project(
        h, w_in_ref, qg_ref[...], kg_ref[...], bdq_ref[...], bdk_ref[...],
        cosa_ref[...], sina_ref[...], cosr_ref[...], sinr_ref[...])

    for s in range(nseq):
        rs = slice(s * ds, (s + 1) * ds)
        kwin_ref[s] = jnp.concatenate([ck_ref[s, ds:, :], k[rs]], axis=0)
        vwin_ref[s] = jnp.concatenate([cv_ref[s, ds:, :], v[rs]], axis=0)
    mix[:, 0:ATT_W] = _bf(_sample_attention(q, k, v, ck_ref, cv_ref, sinks_ref, ds))
    mix[:, ATT_W:ATT_W + RET_W] = _bf(_sample_retention(
        rq, rk, rv, rg, st_ref, rend_ref, decay_ref, qdec_ref[...], kdec_ref[...], gpow_ref[...], ds))

    x1 = x + mod(2) * jnp.dot(mix[...], w_out_ref[...], preferred_element_type=jnp.float32)
    x1_ref[...] = x1
    h2 = _pre_norm(x1, gffn_ref[...], mod(3), mod(4))
    h2_ref[...] = h2
    topi, topw, topi_t, counts = _route(h2, wrh_ref, wrl_ref, br_ref[...])
    topi_ref[...] = topi
    topw_ref[...] = topw
    topit_ref[...] = topi_t
    cnt_ref[:, 0, :] = counts


def _sample_mix(x, mod_seq, sinks, gmix, gffn, w_in, qg, kg, bdq, bdk, tabs, ret_tabs, w_out,
                wrh, wrl, br, cache_k, cache_v, state):
    t, d = x.shape
    nb, cache_w = cache_k.shape[0], cache_k.shape[1]
    ds = t // nb
    g = SAMPLE_SEQS
    tb = g * ds
    cosa, sina, cosr, sinr = tabs
    decay, qdec, kdec, gpow = ret_tabs
    row_spec = lambda w: pl.BlockSpec((tb, w), lambda i, *_: (i, 0))
    seq3 = lambda a: pl.BlockSpec((g,) + a.shape[1:], lambda i, *_: (i,) + (0,) * (a.ndim - 1))
    consts = [gmix, gffn, w_in, qg, kg, bdq, bdk, cosa, sina, cosr, sinr,
              decay, qdec, kdec, gpow, w_out, wrh, wrl, br]
    grid_spec = pltpu.PrefetchScalarGridSpec(
        num_scalar_prefetch=1,
        grid=(nb // g,),
        in_specs=([row_spec(d), pl.BlockSpec((g, mod_seq.shape[1]), lambda i, *_: (i, 0))]
                  + [_const_spec(a.shape) for a in consts]
                  + [seq3(cache_k), seq3(cache_v), seq3(state)]),
        out_specs=[row_spec(d), row_spec(d), row_spec(LANES), row_spec(LANES),
                   pl.BlockSpec((SUBLANES, tb), lambda i, *_: (0, i)),
                   pl.BlockSpec((tb // COUNT_BLOCK, 1, LANES), lambda i, *_: (i, 0, 0)),
                   seq3(cache_k), seq3(cache_v), seq3(state)],
        scratch_shapes=[pltpu.VMEM((tb, ATT_W + RET_W), jnp.bfloat16)],
    )
    return pl.pallas_call(
        _sample_mix_kernel,
        grid_spec=grid_spec,
        out_shape=[jax.ShapeDtypeStruct((t, d), jnp.float32),
                   jax.ShapeDtypeStruct((t, d), jnp.float32),
                   jax.ShapeDtypeStruct((t, LANES), jnp.int32),
                   jax.ShapeDtypeStruct((t, LANES), jnp.float32),
                   jax.ShapeDtypeStruct((SUBLANES, t), jnp.int32),
                   jax.ShapeDtypeStruct((t // COUNT_BLOCK, 1, LANES), jnp.int32),
                   jax.ShapeDtypeStruct(cache_k.shape, jnp.float32),
                   jax.ShapeDtypeStruct(cache_v.shape, jnp.float32),
                   jax.ShapeDtypeStruct(state.shape, jnp.float32)],
        compiler_params=pltpu.CompilerParams(
            dimension_semantics=("arbitrary",), vmem_limit_bytes=VMEM_LIMIT_BYTES),
    )(sinks, x, mod_seq, *consts, cache_k, cache_v, state)


def _chunk_copies(cnt8, src_ref, src_start, dst_ref, dst_start, sem):
    for bit in range(CHUNK_BITS - 1, SUBLANES.bit_length() - 2, -1):
        size = 1 << bit
        done = (cnt8 >> (bit + 1)) << (bit + 1)

        @pl.when(((cnt8 >> bit) & 1) == 1)
        def _():
            src = pl.multiple_of(src_start + done, SUBLANES)
            dst = pl.multiple_of(dst_start + done, SUBLANES)
            pltpu.make_async_copy(src_ref.at[pl.ds(src, size)],
                                  dst_ref.at[pl.ds(dst, size)], sem).start()


def _wait_rows(ref, rows8, sem, bits=STAGE_BITS):
    for bit in range(bits - 1, SUBLANES.bit_length() - 2, -1):
        size = 1 << bit

        @pl.when(((rows8 >> bit) & 1) == 1)
        def _():
            pltpu.make_async_copy(ref.at[pl.ds(0, size)], ref.at[pl.ds(0, size)], sem).wait()


def _dispatch_kernel(cnt_ref, gstart_ref, cb_ref, blkrows_ref, ends_ref, padcnt_ref,
                     topit_p_ref, topit_s_ref, h2_p_ref, h2_s_ref, cbcol_ref, xs_ref,
                     stage, zeros, sems, zsem, *, nblk_p):
    step = pl.program_id(0)
    nblk = pl.num_programs(0)
    tb = h2_p_ref.shape[0]
    slot = step % STAGE_SLOTS

    tile = zeros.shape[0]
    first_unused = ends_ref[N_EXPERTS - 1] // tile
    n_tiles = xs_ref.shape[0] // tile

    def tail_fill(t):
        return pltpu.make_async_copy(zeros, xs_ref.at[pl.ds(pl.multiple_of(t * tile, tile), tile)],
                                     zsem)

    @pl.when(step == 0)
    def _():
        zeros[...] = jnp.zeros_like(zeros)

        def pad_fill(e, carry):
            _chunk_copies(padcnt_ref[e], zeros, 0, xs_ref, ends_ref[e] - padcnt_ref[e], zsem)
            return carry

        lax.fori_loop(0, N_EXPERTS, pad_fill, 0)
        lax.fori_loop(first_unused, n_tiles, lambda t, c: (tail_fill(t).start(), c)[1], 0)

    @pl.when(step == nblk - 1)
    def _():
        _wait_rows(xs_ref, padcnt_ref[N_EXPERTS], zsem, bits=(N_EXPERTS * tile).bit_length())
        lax.fori_loop(first_unused, n_tiles, lambda t, c: (tail_fill(t).wait(), c)[1], 0)

    is_p = step < nblk_p
    topit = jnp.where(is_p, topit_p_ref[...], topit_s_ref[...])
    h2 = jnp.where(is_p, h2_p_ref[...], h2_s_ref[...])
    cbcol = cbcol_ref[0]

    sub = lax.broadcasted_iota(jnp.int32, (LANES, tb), 0)
    picks = [sub == topit[k:k + 1, :] for k in range(TOP_K)]
    onehot = jnp.zeros((LANES, tb), jnp.float32)
    for pk in picks:
        onehot = jnp.where(pk, 1.0, onehot)
    r = lax.broadcasted_iota(jnp.int32, (tb, tb), 0)
    c = lax.broadcasted_iota(jnp.int32, (tb, tb), 1)
    earlier = jnp.where(r < c, 1.0, 0.0)
    before = _dot(onehot, earlier)
    base = before + cbcol.astype(jnp.float32)
    row = lax.broadcasted_iota(jnp.int32, (stage.shape[1], tb), 0)
    sel = jnp.zeros((stage.shape[1], tb), jnp.float32)
    for pk in picks:
        col = jnp.sum(jnp.where(pk, base, 0.0), axis=0, keepdims=True).astype(jnp.int32)
        sel = jnp.where(row == col, 1.0, sel)

    stage[slot] = _dot(sel, h2)

    def issue(e, carry):
        i = step * N_EXPERTS + e
        _chunk_copies(cnt_ref[i], stage.at[slot], cb_ref[i], xs_ref, gstart_ref[i], sems.at[slot])
        return carry

    lax.fori_loop(0, N_EXPERTS, issue, 0)

    oldest = step - (STAGE_SLOTS - 1)

    @pl.when(oldest >= 0)
    def _():
        _wait_rows(xs_ref, blkrows_ref[jnp.maximum(oldest, 0)], sems.at[(step + 1) % STAGE_SLOTS])

    @pl.when(step == nblk - 1)
    def _():
        for back in range(STAGE_SLOTS - 2, -1, -1):
            @pl.when(step - back >= 0)
            def _():
                blk = jnp.maximum(step - back, 0)
                _wait_rows(xs_ref, blkrows_ref[blk], sems.at[blk % STAGE_SLOTS])


def _dispatch(tables, topit_p, topit_s, h2_p, h2_s, cbcol, n_rows):
    n_p, d = h2_p.shape
    n_s = h2_s.shape[0]
    tb = ROUTE_BLOCK
    nblk_p, nblk_s = n_p // tb, n_s // tb
    pidx = lambda i, *_: jnp.minimum(i, nblk_p - 1)
    sidx = lambda i, *_: jnp.maximum(i - nblk_p, 0)
    grid_spec = pltpu.PrefetchScalarGridSpec(
        num_scalar_prefetch=6,
        grid=(nblk_p + nblk_s,),
        in_specs=[pl.BlockSpec((SUBLANES, tb), lambda i, *_: (0, pidx(i))),
                  pl.BlockSpec((SUBLANES, tb), lambda i, *_: (0, sidx(i))),
                  pl.BlockSpec((tb, d), lambda i, *_: (pidx(i), 0)),
                  pl.BlockSpec((tb, d), lambda i, *_: (sidx(i), 0)),
                  pl.BlockSpec((1, LANES, 1), lambda i, *_: (i, 0, 0))],
        out_specs=pl.BlockSpec(memory_space=pl.ANY),
        scratch_shapes=[pltpu.VMEM((STAGE_SLOTS, STAGE_ROWS, d), jnp.float32),
                        pltpu.VMEM((EXPERT_TILE, d), jnp.float32),
                        pltpu.SemaphoreType.DMA((STAGE_SLOTS,)),
                        pltpu.SemaphoreType.DMA],
    )
    return pl.pallas_call(
        functools.partial(_dispatch_kernel, nblk_p=nblk_p),
        grid_spec=grid_spec,
        out_shape=jax.ShapeDtypeStruct((n_rows, d), jnp.float32),
        compiler_params=pltpu.CompilerParams(
            dimension_semantics=("arbitrary",), vmem_limit_bytes=VMEM_LIMIT_BYTES),
    )(*tables, topit_p, topit_s, h2_p, h2_s, cbcol)


def _expert_mlp(x, wgu_bf, bgu, wd_bf, bd):
    d_ff = wd_bf.shape[0]
    gu = jnp.dot(_bf(x), wgu_bf[...], preferred_element_type=jnp.float32) + bgu
    glu = jnp.minimum(gu[:, :d_ff], SWIGLU_LIMIT)
    lin = jnp.clip(gu[:, d_ff:], -SWIGLU_LIMIT, SWIGLU_LIMIT)
    act = glu * jax.nn.sigmoid(SWIGLU_ALPHA * glu) * (lin + 1.0)
    return jnp.dot(_bf(act), wd_bf[...], preferred_element_type=jnp.float32) + bd


def _experts_kernel(tile_expert_ref, n_used_ref, groups_ref, next_ref, slot_ref,
                    x_ref, wgu_hbm, bgu_ref, wd_hbm, bd_ref, y_ref,
                    wgu_f32, wd_f32, wgu_bf, wd_bf, sems):
    step = pl.program_id(0)
    groups = groups_ref[step]

    def weight_copies(expert, slot):
        return (pltpu.make_async_copy(wgu_hbm.at[expert], wgu_f32.at[slot], sems.at[slot, 0]),
                pltpu.make_async_copy(wd_hbm.at[expert], wd_f32.at[slot], sems.at[slot, 1]))

    @pl.when(groups > 0)
    def _():
        e = tile_expert_ref[step]
        prev = tile_expert_ref[jnp.maximum(step - 1, 0)]
        slot = slot_ref[e]

        @pl.when(step == 0)
        def _():
            for cp in weight_copies(e, slot):
                cp.start()

        @pl.when(jnp.logical_or(step == 0, e != prev))
        def _():
            for cp in weight_copies(e, slot):
                cp.wait()
            nxt = next_ref[e]

            @pl.when(nxt >= 0)
            def _():
                for cp in weight_copies(nxt, 1 - slot):
                    cp.start()

            wgu_bf[...] = _bf(wgu_f32[slot])
            wd_bf[...] = _bf(wd_f32[slot])

    tm = x_ref.shape[0]
    for n in range(1, EXPERT_GROUPS + 1):
        @pl.when(groups == n)
        def _():
            rows = n * (tm // EXPERT_GROUPS)
            y_ref[0:rows, :] = _expert_mlp(x_ref[0:rows, :], wgu_bf, bgu_ref[0], wd_bf, bd_ref[0])
            if rows < tm:
                y_ref[rows:, :] = jnp.zeros((tm - rows, y_ref.shape[1]), y_ref.dtype)

    @pl.when(groups == 0)
    def _():
        y_ref[...] = jnp.zeros_like(y_ref)


def _experts(tile_expert, n_used, tile_groups, next_used, slot_of, x_sorted, w_gate_up, b_gate_up,
             w_down, b_down):
    p = x_sorted.shape[0]
    n_e, d, two_ff = w_gate_up.shape
    d_ff = two_ff // 2
    tm = EXPERT_TILE
    tile = lambda i, te, nu: jnp.minimum(i, nu[0] - 1)
    grid_spec = pltpu.PrefetchScalarGridSpec(
        num_scalar_prefetch=5,
        grid=(p // tm,),
        in_specs=[pl.BlockSpec((tm, d), lambda i, te, nu, *_: (tile(i, te, nu), 0)),
                  pl.BlockSpec(memory_space=pl.ANY),
                  pl.BlockSpec((1, 1, two_ff), lambda i, te, *_: (te[i], 0, 0)),
                  pl.BlockSpec(memory_space=pl.ANY),
                  pl.BlockSpec((1, 1, d), lambda i, te, *_: (te[i], 0, 0))],
        out_specs=pl.BlockSpec((tm, d), lambda i, *_: (i, 0)),
        scratch_shapes=[pltpu.VMEM((2, d, two_ff), jnp.float32),
                        pltpu.VMEM((2, d_ff, d), jnp.float32),
                        pltpu.VMEM((d, two_ff), jnp.bfloat16),
                        pltpu.VMEM((d_ff, d), jnp.bfloat16),
                        pltpu.SemaphoreType.DMA((2, 2))],
    )
    return pl.pallas_call(
        _experts_kernel,
        grid_spec=grid_spec,
        out_shape=jax.ShapeDtypeStruct(x_sorted.shape, jnp.float32),
        compiler_params=pltpu.CompilerParams(
            dimension_semantics=("arbitrary",), vmem_limit_bytes=VMEM_LIMIT_BYTES),
    )(tile_expert, n_used, tile_groups, next_used, slot_of, x_sorted, w_gate_up,
      b_gate_up.reshape(n_e, 1, two_ff), w_down, b_down.reshape(n_e, 1, d))


def _combine_kernel(cnt_ref, gstart_ref, cb_ref, blkrows_ref,
                    topi_p_ref, topi_s_ref, topw_p_ref, topw_s_ref, x1_p_ref, x1_s_ref,
                    gate_p_ref, gate_s_ref, cbrow_ref, y_ref, out_p_ref, out_s_ref,
                    ybuf, sems, *, nblk_p):
    step = pl.program_id(0)
    nblk = pl.num_programs(0)
    tb = x1_p_ref.shape[0]
    slot = step % STAGE_SLOTS

    def fetch_chunk(blk, into, e):
        i = blk * N_EXPERTS + e
        _chunk_copies(cnt_ref[i], y_ref, gstart_ref[i], ybuf.at[into], cb_ref[i], sems.at[into])

    def fetch(blk, into):
        lax.fori_loop(0, N_EXPERTS, lambda e, c: (fetch_chunk(blk, into, e), c)[1], 0)

    @pl.when(step == 0)
    def _():
        ybuf[...] = jnp.zeros_like(ybuf)
        for ahead in range(STAGE_SLOTS - 1):
            @pl.when(ahead < nblk)
            def _():
                fetch(ahead, ahead)

    for e in range(N_EXPERTS):
        fetch_chunk(step + STAGE_SLOTS - 1, (step + STAGE_SLOTS - 1) % STAGE_SLOTS, e)

    _wait_rows(y_ref, blkrows_ref[step], sems.at[slot])

    is_p = step < nblk_p
    topi = jnp.where(is_p, topi_p_ref[...], topi_s_ref[...])
    topw = jnp.where(is_p, topw_p_ref[...], topw_s_ref[...])
    cbrow = cbrow_ref[0]

    lane = lax.broadcasted_iota(jnp.int32, (tb, LANES), 1)
    picks = []
    onehot = jnp.zeros((tb, LANES), jnp.float32)
    for k in range(TOP_K):
        ix = jnp.sum(jnp.where(lane == k, topi, 0), axis=-1, keepdims=True)
        picks.append(lane == ix)
        onehot = jnp.where(picks[k], 1.0, onehot)
    r = lax.broadcasted_iota(jnp.int32, (tb, tb), 0)
    c = lax.broadcasted_iota(jnp.int32, (tb, tb), 1)
    earlier = jnp.where(c < r, 1.0, 0.0)
    base = _dot(earlier, onehot) + cbrow.astype(jnp.float32)
    cols = [jnp.sum(jnp.where(picks[k], base, 0.0), axis=-1, keepdims=True).astype(jnp.int32)
            for k in range(TOP_K)]
    wks = [jnp.sum(jnp.where(lane == k, topw, 0.0), axis=-1, keepdims=True) for k in range(TOP_K)]
    moe = None
    for c0 in range(0, ybuf.shape[1], COMBINE_SLAB):
        colid = lax.broadcasted_iota(jnp.int32, (tb, COMBINE_SLAB), 1) + c0
        weights = jnp.zeros((tb, COMBINE_SLAB), jnp.float32)
        for k in range(TOP_K):
            weights = jnp.where(colid == cols[k], wks[k], weights)
        part = _dot(weights, ybuf[slot, c0:c0 + COMBINE_SLAB, :])
        moe = part if moe is None else moe + part

    @pl.when(is_p)
    def _():
        out_p_ref[...] = x1_p_ref[...] + gate_p_ref[...] * moe

    @pl.when(jnp.logical_not(is_p))
    def _():
        gate_s = _repeat_rows(gate_s_ref[...], tb // gate_s_ref.shape[0])
        out_s_ref[...] = x1_s_ref[...] + gate_s * moe


def _combine(tables, topi_p, topi_s, topw_p, topw_s, x1_p, x1_s, gate_p, mod_seq, dec_seq, cbrow,
             y_sorted):
    n_p, d = x1_p.shape
    n_s = x1_s.shape[0]
    tb = ROUTE_BLOCK
    nblk_p, nblk_s = n_p // tb, n_s // tb
    pidx = lambda i, *_: jnp.minimum(i, nblk_p - 1)
    sidx = lambda i, *_: jnp.maximum(i - nblk_p, 0)
    prow = lambda w: pl.BlockSpec((tb, w), lambda i, *_: (pidx(i), 0))
    srow = lambda w: pl.BlockSpec((tb, w), lambda i, *_: (sidx(i), 0))
    grid_spec = pltpu.PrefetchScalarGridSpec(
        num_scalar_prefetch=4,
        grid=(nblk_p + nblk_s,),
        in_specs=[prow(LANES), srow(LANES), prow(LANES), srow(LANES), prow(d), srow(d),
                  pl.BlockSpec((1, d), lambda i, *_: (0, 0)),
                  pl.BlockSpec((tb // dec_seq, d), lambda i, *_: (sidx(i), 5)),
                  pl.BlockSpec((1, 1, LANES), lambda i, *_: (i, 0, 0)),
                  pl.BlockSpec(memory_space=pl.ANY)],
        out_specs=[prow(d), srow(d)],
        scratch_shapes=[pltpu.VMEM((STAGE_SLOTS, STAGE_ROWS, d), jnp.float32),
                        pltpu.SemaphoreType.DMA((STAGE_SLOTS,))],
    )
    return pl.pallas_call(
        functools.partial(_combine_kernel, nblk_p=nblk_p),
        grid_spec=grid_spec,
        out_shape=[jax.ShapeDtypeStruct((n_p, d), jnp.float32),
                   jax.ShapeDtypeStruct((n_s, d), jnp.float32)],
        compiler_params=pltpu.CompilerParams(
            dimension_semantics=("arbitrary",), vmem_limit_bytes=VMEM_LIMIT_BYTES),
    )(*tables, topi_p, topi_s, topw_p, topw_s, x1_p, x1_s, gate_p, mod_seq, cbrow, y_sorted)


def _rotary_tables(pos, inv_freq):
    ang = np.asarray(pos, np.float64)[:, None] * inv_freq[None, :]
    cos, sin = np.cos(ang), np.sin(ang)
    cos_t = np.tile(cos, (1, LANES // cos.shape[1]))
    sin_t = np.tile(np.concatenate([-sin, sin], axis=1), (1, LANES // (2 * sin.shape[1])))
    return cos_t.astype(np.float32), sin_t.astype(np.float32)


def _rotary_split_tables(block, nblk, inv_freq):
    half = inv_freq.shape[0]
    freq = np.tile(inv_freq, LANES // half)
    sign = np.tile(np.concatenate([-np.ones(half), np.ones(half)]), LANES // (2 * half))
    base = (np.arange(nblk) * block).astype(np.float64)[:, None] * freq[None, :]
    off = np.arange(block).astype(np.float64)[:, None] * freq[None, :]
    blk = np.stack([np.cos(base), np.sin(base)], axis=1)
    tab = np.stack([np.cos(off), np.sin(off), np.cos(off) * sign, np.sin(off) * sign])
    return blk.astype(np.float32), tab.astype(np.float32)


def _retention_tables(chunk, nseq=1):
    log_gamma = np.log1p(-np.exp2(-5.0 - np.arange(RET_HEADS, dtype=np.float64)))
    idx = np.arange(chunk, dtype=np.float64)
    diff = idx[:, None] - idx[None, :]
    decay = np.where(diff[None] >= 0,
                     np.exp(np.maximum(diff, 0.0)[None] * log_gamma[:, None, None]), 0.0)
    decay = np.stack([np.kron(np.eye(nseq), decay[h]) for h in range(RET_HEADS)])
    q_decay = np.exp((idx + 1.0)[:, None] * log_gamma[None, :])
    k_decay = np.exp((chunk - 1.0 - idx)[:, None] * log_gamma[None, :])
    qdec = np.tile(np.repeat(q_decay, RET_DV, axis=1), (nseq, 1))
    kdec = np.tile(np.repeat(k_decay, RET_DK, axis=1), (nseq, 1))
    g_chunk = np.exp(chunk * log_gamma)
    gpow = np.repeat(g_chunk.reshape(RET_HEADS // 2, 2), RET_DK, axis=1).T
    return tuple(a.astype(np.float32) for a in (decay, qdec, kdec, gpow))


def _block_diag_mean(width):
    head = np.arange(width) // HEAD_DIM
    return jnp.asarray(np.where(head[:, None] == head[None, :], 1.0 / HEAD_DIM, 0.0), jnp.bfloat16)


def kernel(x_prompt, x_sample, cache_k, cache_v, state_ret, c_prompt, c_sample, w_ada, b_ada,
           g_mix, w_in, q_gain, k_gain, sinks, w_out, g_ffn, w_router, b_router, w_gate_up,
           b_gate_up, w_down, b_down):
    depth = w_ada.shape[0]
    batch, seq, d = x_prompt.shape
    dec_batch, dec_seq, _ = x_sample.shape
    cache_w = cache_k.shape[2]
    assert batch == 1 and depth == 1
    assert seq % PROMPT_BLOCK == 0 and PROMPT_BLOCK % WINDOW == 0 and WINDOW == RET_CHUNK
    assert dec_batch % SAMPLE_SEQS == 0 and cache_w == WINDOW
    n_p, n_s = batch * seq, dec_batch * dec_seq
    n_tok = n_p + n_s
    assert n_p % ROUTE_BLOCK == 0 and n_s % ROUTE_BLOCK == 0
    assert ROUTE_BLOCK % COUNT_BLOCK == 0 and (SAMPLE_SEQS * dec_seq) % COUNT_BLOCK == 0
    assert PROMPT_BLOCK % COUNT_BLOCK == 0
    assert STAGE_ROWS % COMBINE_SLAB == 0 and EXPERT_TILE % (EXPERT_GROUPS * SUBLANES) == 0

    l = 0
    f32 = jnp.float32
    rope_freq = 1.0 / (ROPE_THETA ** (np.arange(0, HEAD_DIM, 2, dtype=np.float64) / HEAD_DIM))
    ret_freq = 1.0 / (ROPE_THETA ** np.linspace(0.0, 1.0, RET_DK // 2))
    pos_s = PAST_LEN + np.arange(dec_seq)
    blk_a, tab_a = _rotary_split_tables(PROMPT_BLOCK, seq // PROMPT_BLOCK, rope_freq)
    blk_r, tab_r = _rotary_split_tables(PROMPT_BLOCK, seq // PROMPT_BLOCK, ret_freq)
    tabs_p = (np.concatenate([blk_a, blk_r], axis=1), np.concatenate([tab_a, tab_r], axis=0))
    tabs_s = tuple(np.tile(a, (SAMPLE_SEQS, 1))
                   for a in _rotary_tables(pos_s, rope_freq) + _rotary_tables(pos_s, ret_freq))
    ret_p = _retention_tables(RET_CHUNK)
    ret_s = _retention_tables(dec_seq, SAMPLE_SEQS)

    n_c = batch + dec_batch
    c_rows = -(-n_c // SUBLANES) * SUBLANES
    c_all = jnp.concatenate([c_sample, c_prompt, jnp.zeros((c_rows - n_c, d), f32)], axis=0)
    mod = _adaln(c_all, w_ada[l], b_ada[l])
    mod_p = mod[dec_batch].reshape(6, d)

    w_in_bf = _bf(w_in[l])
    w_out_bf = _bf(w_out[l])
    qg = jnp.tile(q_gain[l], ATT_HEADS).reshape(1, ATT_W)
    kg = jnp.tile(k_gain[l], ATT_KV_HEADS).reshape(1, KV_W)
    bdq, bdk = _block_diag_mean(ATT_W), _block_diag_mean(KV_W)
    gmix = g_mix[l].reshape(1, d)
    gffn = g_ffn[l].reshape(1, d)
    wr = jnp.pad(w_router[l], ((0, 0), (0, LANES - N_EXPERTS)))
    wr_hi = _bf(wr)
    wr_cat = jnp.concatenate([wr_hi, _bf(wr - wr_hi.astype(f32))], axis=1)
    br = jnp.concatenate([b_router[l], jnp.full((LANES - N_EXPERTS,), NEG_BIG, f32)]).reshape(1, LANES)
    shared = (sinks[l], gmix, gffn, w_in_bf, qg, kg, bdq, bdk)

    (x1_p, h2_p, topi_p, topw_p, topit_p, cnt_p, kwin_p, vwin_p, rend_p) = _prompt_mix(
        x_prompt.reshape(n_p, d), mod_p, *shared, tabs_p, ret_p, w_out_bf, wr_hi, wr_cat, br)

    (x1_s, h2_s, topi_s, topw_s, topit_s, cnt_s, kwin_s, vwin_s, rend_s) = _sample_mix(
        x_sample.reshape(n_s, d), mod, *shared, tabs_s, ret_s, w_out_bf, wr_hi, wr_cat, br,
        cache_k[l].reshape(dec_batch, cache_w, KV_W), cache_v[l].reshape(dec_batch, cache_w, KV_W),
        state_ret[l].reshape(dec_batch, 2, LANES, RET_DV))

    tb, tm = ROUTE_BLOCK, EXPERT_TILE
    i32 = jnp.int32
    cnt = jnp.concatenate([cnt_p, cnt_s], axis=0)
    cnt = cnt.reshape(n_tok // tb, tb // COUNT_BLOCK, LANES).sum(axis=1)
    cnt8 = -(-cnt // SUBLANES) * SUBLANES
    rs = jnp.cumsum(cnt8, axis=0) - cnt8
    tot = jnp.sum(cnt8, axis=0)[:N_EXPERTS]
    padded = -(-tot // tm) * tm
    ends = jnp.cumsum(padded)
    gstart = (ends - padded)[None, :] + rs[:, :N_EXPERTS]
    cb = jnp.cumsum(cnt8, axis=1) - cnt8
    blk_rows = jnp.sum(cnt8, axis=1).astype(i32)
    max_rows = n_tok * TOP_K + cnt.shape[0] * N_EXPERTS * (SUBLANES - 1)
    n_rows = -(-max_rows // tm) * tm + N_EXPERTS * tm
    tile_start = jnp.arange(n_rows // tm, dtype=i32) * tm
    tile_expert = jnp.minimum(jnp.sum(ends[None, :] <= tile_start[:, None], axis=1),
                              N_EXPERTS - 1).astype(i32)
    n_used = (ends[-1:] // tm).astype(i32)
    eid = jnp.arange(N_EXPERTS, dtype=i32)
    of_tile = lambda v: jnp.sum(jnp.where(tile_expert[:, None] == eid[None, :], v[None, :], 0), axis=1)
    tile_rows = jnp.clip(of_tile(tot) - (tile_start - of_tile(ends - padded)), 0, tm)
    tile_rows = jnp.where(tile_start < ends[-1], tile_rows, 0)
    tile_groups = (-(-tile_rows // (tm // EXPERT_GROUPS))).astype(i32)
    flat = lambda a: a[:, :N_EXPERTS].reshape(-1).astype(i32)
    tables = (flat(cnt8), gstart.reshape(-1).astype(i32), flat(cb), blk_rows)
    ahead = jnp.zeros(((STAGE_SLOTS - 1) * N_EXPERTS,), i32)
    tables_c = tuple(jnp.concatenate([t, ahead]) for t in tables[:3]) + tables[3:]

    pad_cnt = padded - tot
    pad_cnt = jnp.concatenate([pad_cnt, jnp.sum(pad_cnt, keepdims=True)]).astype(i32)
    x_sorted = _dispatch(tables + (ends.astype(i32), pad_cnt), topit_p, topit_s, h2_p, h2_s,
                         cb.astype(i32)[:, :, None], n_rows)
    used = tot > 0
    later = jnp.where(used[None, :] & (eid[None, :] > eid[:, None]), eid[None, :], N_EXPERTS)
    next_used = jnp.min(later, axis=1)
    next_used = jnp.where(next_used == N_EXPERTS, -1, next_used).astype(i32)
    slot_of = ((jnp.cumsum(used.astype(i32)) - 1) % 2).astype(i32)
    y_sorted = _experts(tile_expert, n_used, tile_groups, next_used, slot_of, x_sorted, w_gate_up[l], b_gate_up[l], w_down[l],
                        b_down[l])
    y_p, y_s = _combine(tables_c, topi_p, topi_s, topw_p, topw_s, x1_p, x1_s, mod_p[5:6], mod,
                        dec_seq, cb.astype(i32)[:, None, :], y_sorted)

    kv5 = lambda a, n: a.reshape(1, n, cache_w, ATT_KV_HEADS, HEAD_DIM)
    st5 = lambda a, n: a.reshape(1, n, RET_HEADS, RET_DK, RET_DV)
    return (y_p.reshape(batch, seq, d), y_s.reshape(dec_batch, dec_seq, d),
            kv5(kwin_p, batch), kv5(vwin_p, batch), st5(rend_p, batch),
            kv5(kwin_s, dec_batch), kv5(vwin_s, dec_batch), st5(rend_s, dec_batch))
```

```python
import functools

import jax
import jax.numpy as jnp
import numpy as np
from jax import lax
from jax.experimental import pallas as pl
from jax.experimental.pallas import tpu as pltpu

HEAD_DIM = 64
ATT_HEADS = 8
ATT_KV_HEADS = 2
ATT_W = ATT_HEADS * HEAD_DIM
KV_W = ATT_KV_HEADS * HEAD_DIM
WINDOW = 128
PAST_LEN = 16384
ROPE_THETA = 10000.0
RET_HEADS = 4
RET_DK = 64
RET_DV = 128
RET_QK_W = RET_HEADS * RET_DK
RET_W = RET_HEADS * RET_DV
RET_CHUNK = 128
N_EXPERTS = 32
TOP_K = 4
SWIGLU_LIMIT = 7.0
SWIGLU_ALPHA = 1.702
EPS = 1e-6

_Q0, _K0, _V0 = 0, ATT_W, ATT_W + KV_W
_RQ0 = ATT_W + 2 * KV_W
_RK0 = _RQ0 + RET_QK_W
_RV0 = _RK0 + RET_QK_W
_RG0 = _RV0 + RET_W
IN_W = _RG0 + RET_W

LANES = 128
SUBLANES = 8
VMEM_LIMIT_BYTES = 56 * 1024 * 1024
NEG_BIG = -1e30

PROMPT_BLOCK = 512
SAMPLE_SEQS = 16
COUNT_BLOCK = 128
ROUTE_BLOCK = 512
EXPERT_TILE = 1024
EXPERT_GROUPS = 4
CHUNK_BITS = ROUTE_BLOCK.bit_length()
STAGE_ROWS = -(-(TOP_K * ROUTE_BLOCK + N_EXPERTS * (SUBLANES - 1)) // LANES) * LANES
STAGE_BITS = STAGE_ROWS.bit_length()
STAGE_SLOTS = 3
COMBINE_SLAB = 256

_NT = (((1,), (1,)), ((), ()))
_TN = (((0,), (0,)), ((), ()))


def _bf(x):
    return x.astype(jnp.bfloat16)


def _dot(a, b):
    return jnp.dot(_bf(a), _bf(b), preferred_element_type=jnp.float32)


def _dot_nt(a, b):
    return lax.dot_general(_bf(a), _bf(b), _NT, preferred_element_type=jnp.float32)


def _dot_tn(a, b):
    return lax.dot_general(_bf(a), _bf(b), _TN, preferred_element_type=jnp.float32)


def _lane_lo(shape):
    lane = lax.broadcasted_iota(jnp.int32, shape, len(shape) - 1)
    return (lane % LANES) < HEAD_DIM


def _swap_halves(x):
    lane = lax.broadcasted_iota(jnp.int32, x.shape, 1)
    first = (lane % HEAD_DIM) < (HEAD_DIM // 2)
    return jnp.where(first, pltpu.roll(x, LANES - HEAD_DIM // 2, axis=1),
                     pltpu.roll(x, HEAD_DIM // 2, axis=1))


def _rotate(x, cos, sin_signed):
    outs = []
    for j in range(x.shape[1] // LANES):
        xs = x[:, j * LANES:(j + 1) * LANES]
        outs.append(xs * cos + _swap_halves(xs) * sin_signed)
    return outs[0] if len(outs) == 1 else jnp.concatenate(outs, axis=1)


def _repeat_rows(m, n):
    r, w = m.shape
    return jnp.broadcast_to(m[:, None, :], (r, n, w)).reshape(r * n, w)


def _block_rotary(blk_ref, tab_ref, which):
    ca, sa = blk_ref[0, 2 * which:2 * which + 1, :], blk_ref[0, 2 * which + 1:2 * which + 2, :]
    cb, sb, cbs, sbs = (tab_ref[4 * which + n] for n in range(4))
    return ca * cb - sa * sb, sa * cbs + ca * sbs


def _rms_rows(x):
    return x * lax.rsqrt(jnp.mean(x * x, axis=-1, keepdims=True) + EPS)


def _pre_norm(x, gain, shift, scale):
    return _rms_rows(x) * gain * (1.0 + scale) + shift


def _head_norm(x, gain_tiled, blockdiag):
    ms = jnp.dot(_bf(x * x), blockdiag, preferred_element_type=jnp.float32)
    return x * lax.rsqrt(ms + EPS) * gain_tiled


def _project(h, w_in_ref, qg, kg, bd_q, bd_k, cos_a, sin_a, cos_r, sin_r):
    z = jnp.dot(_bf(h), w_in_ref[...], preferred_element_type=jnp.float32)
    q = _rotate(_head_norm(z[:, _Q0:_K0], qg, bd_q), cos_a, sin_a) * (HEAD_DIM ** -0.5)
    k = _rotate(_head_norm(z[:, _K0:_V0], kg, bd_k), cos_a, sin_a)
    v = z[:, _V0:_RQ0]
    rq = _rotate(z[:, _RQ0:_RK0], cos_r, sin_r)
    rk = _rotate(z[:, _RK0:_RV0], cos_r, sin_r) * (RET_DK ** -0.5)
    rv = z[:, _RV0:_RG0]
    rg = z[:, _RG0:IN_W]
    return q, k, v, rq, rk, rv, rg


_NAT_HEADS = (0, 2, 5, 7)
_ROL_HEADS = (1, 3, 4, 6)


def _stack_heads(q, heads):
    lo = _lane_lo((q.shape[0], LANES))
    parts = []
    for j in heads:
        slab = q[:, (j // 2) * LANES:(j // 2 + 1) * LANES]
        parts.append(jnp.where(lo if j % 2 == 0 else ~lo, slab, 0.0))
    return jnp.concatenate(parts, axis=0)


def _sink_column(sinks_ref, heads, rows):
    r = lax.broadcasted_iota(jnp.int32, (len(heads) * rows, 1), 0)
    col = jnp.full((len(heads) * rows, 1), sinks_ref[heads[-1]], jnp.float32)
    for n in range(len(heads) - 2, -1, -1):
        col = jnp.where(r < (n + 1) * rows, sinks_ref[heads[n]], col)
    return col


def _softmax_pv(s, mask, sink, v):
    s = jnp.where(mask, s, NEG_BIG)
    m = jnp.maximum(jnp.max(s, axis=-1, keepdims=True), sink)
    p = jnp.exp(s - m)
    denom = jnp.sum(p, axis=-1, keepdims=True) + jnp.exp(sink - m)
    return _dot(p, v) / denom


def _attention(q, k_all, v_all, mask, sinks_ref):
    rows = q.shape[0]
    k_rol = pltpu.roll(k_all, HEAD_DIM, axis=1)
    v_rol = pltpu.roll(v_all, HEAD_DIM, axis=1)
    mask4 = jnp.concatenate([mask] * 4, axis=0)
    o_nat = _softmax_pv(_dot_nt(_stack_heads(q, _NAT_HEADS), k_all), mask4,
                        _sink_column(sinks_ref, _NAT_HEADS, rows), v_all)
    o_rol = _softmax_pv(_dot_nt(_stack_heads(q, _ROL_HEADS), k_rol), mask4,
                        _sink_column(sinks_ref, _ROL_HEADS, rows), v_rol)
    lo = _lane_lo((rows, LANES))
    blk = lambda o, n: o[n * rows:(n + 1) * rows]
    return jnp.concatenate([
        jnp.where(lo, blk(o_nat, 0), blk(o_rol, 0)),
        jnp.where(lo, blk(o_nat, 1), blk(o_rol, 1)),
        jnp.where(lo, blk(o_rol, 2), blk(o_nat, 2)),
        jnp.where(lo, blk(o_rol, 3), blk(o_nat, 3)),
    ], axis=1)


def _retention(rq, rk, rv, rg, state, decay_ref, qdec, kdec, gpow):
    rows = rq.shape[0]
    lo = _lane_lo((rows, LANES))
    rkd = rk * kdec
    outs, new_state = [], []
    for m in range(RET_HEADS // 2):
        sl = slice(m * LANES, (m + 1) * LANES)
        upd = state[m] * gpow[:, m:m + 1]
        for p in range(2):
            h = 2 * m + p
            half = lo if p == 0 else ~lo
            rqm = jnp.where(half, rq[:, sl], 0.0)
            rv_h = rv[:, h * RET_DV:(h + 1) * RET_DV]
            s = _dot_nt(rqm, rk[:, sl]) * decay_ref[h]
            ret = _dot(s, rv_h) + _dot(rqm, state[m]) * qdec[:, h * RET_DV:(h + 1) * RET_DV]
            upd = upd + _dot_tn(jnp.where(half, rkd[:, sl], 0.0), rv_h)
            rg_h = rg[:, h * RET_DV:(h + 1) * RET_DV]
            outs.append(rg_h * jax.nn.sigmoid(rg_h) * _rms_rows(ret))
        new_state.append(upd)
    return jnp.concatenate(outs, axis=1), new_state


def _route(h2, wr_hi_ref, wr_cat_ref, br):
    h_hi = _bf(h2)
    h_lo = _bf(h2 - h_hi.astype(jnp.float32))
    both = jnp.dot(h_hi, wr_cat_ref[...], preferred_element_type=jnp.float32)
    logits = (both[:, :LANES] + both[:, LANES:]
              + jnp.dot(h_lo, wr_hi_ref[...], preferred_element_type=jnp.float32)) + br
    lane = lax.broadcasted_iota(jnp.int32, logits.shape, 1)
    idx_out = jnp.zeros(logits.shape, jnp.int32)
    val_out = jnp.zeros(logits.shape, jnp.float32)
    onehot = jnp.zeros(logits.shape, jnp.float32)
    vals = []
    work = logits
    for k in range(TOP_K):
        mx = jnp.max(work, axis=-1, keepdims=True)
        ix = jnp.min(jnp.where(work == mx, lane, LANES), axis=-1, keepdims=True)
        work = jnp.where(lane == ix, NEG_BIG, work)
        onehot = jnp.where(lane == ix, 1.0, onehot)
        idx_out = jnp.where(lane == k, ix, idx_out)
        vals.append(mx)
    exps = [jnp.exp(v - vals[0]) for v in vals]
    total = exps[0] + exps[1] + exps[2] + exps[3]
    for k in range(TOP_K):
        val_out = jnp.where(lane == k, exps[k] / total, val_out)
    idx_t = idx_out.astype(jnp.float32).T[0:SUBLANES].astype(jnp.int32)
    counts = jnp.sum(onehot.reshape(-1, COUNT_BLOCK, LANES), axis=1).astype(jnp.int32)
    return idx_out, val_out, idx_t, counts


def _adaln_kernel(c_ref, w_ref, b_ref, o_ref):
    c = c_ref[...]
    o_ref[...] = _dot(c * jax.nn.sigmoid(c), w_ref[...]) + b_ref[...]


def _adaln(c_all, w_ada, b_ada):
    rows, d = c_all.shape
    n = w_ada.shape[1]
    bn = 1536
    return pl.pallas_call(
        _adaln_kernel,
        grid=(n // bn,),
        in_specs=[pl.BlockSpec((rows, d), lambda j: (0, 0)),
                  pl.BlockSpec((d, bn), lambda j: (0, j)),
                  pl.BlockSpec((1, bn), lambda j: (0, j))],
        out_specs=pl.BlockSpec((rows, bn), lambda j: (0, j)),
        out_shape=jax.ShapeDtypeStruct((rows, n), jnp.float32),
        compiler_params=pltpu.CompilerParams(vmem_limit_bytes=VMEM_LIMIT_BYTES),
    )(c_all, w_ada, b_ada.reshape(1, n))


def _prompt_mix_kernel(sinks_ref, x_ref, mod_ref, gmix_ref, gffn_ref, w_in_ref, qg_ref, kg_ref,
                       bdq_ref, bdk_ref, rotblk_ref, rottab_ref,
                       decay_ref, qdec_ref, kdec_ref, gpow_ref, w_out_ref,
                       wrh_ref, wrl_ref, br_ref,
                       x1_ref, h2_ref, topi_ref, topw_ref, topit_ref, cnt_ref, kwin_ref, vwin_ref, rend_ref,
                       kprev, vprev, state, mix):
    step = pl.program_id(0)
    tb = x_ref.shape[0]

    @pl.when(step == 0)
    def _():
        kprev[...] = jnp.zeros_like(kprev)
        vprev[...] = jnp.zeros_like(vprev)
        state[...] = jnp.zeros_like(state)

    x = x_ref[...]
    h = _pre_norm(x, gmix_ref[...], mod_ref[0:1, :], mod_ref[1:2, :])
    q, k, v, rq, rk, rv, rg = _project(
        h, w_in_ref, qg_ref[...], kg_ref[...], bdq_ref[...], bdk_ref[...],
        *_block_rotary(rotblk_ref, rottab_ref, 0), *_block_rotary(rotblk_ref, rottab_ref, 1))

    qi = lax.broadcasted_iota(jnp.int32, (WINDOW, 2 * WINDOW), 0)
    ci = lax.broadcasted_iota(jnp.int32, (WINDOW, 2 * WINDOW), 1)
    band = (ci > qi) & (ci <= qi + WINDOW)
    qdec, kdec, gpow = qdec_ref[...], kdec_ref[...], gpow_ref[...]

    n_sb = tb // WINDOW
    rows = [slice(sb * WINDOW, (sb + 1) * WINDOW) for sb in range(n_sb)]
    k_prev, v_prev = kprev[...], vprev[...]
    for sb, rs in enumerate(rows):
        k_sb, v_sb = k[rs], v[rs]
        mask = band & ((ci >= WINDOW) | (step > 0)) if sb == 0 else band
        att = _attention(q[rs], jnp.concatenate([k_prev, k_sb], axis=0),
                         jnp.concatenate([v_prev, v_sb], axis=0), mask, sinks_ref)
        mix[rs, 0:ATT_W] = _bf(att)
        k_prev, v_prev = k_sb, v_sb
    kprev[...] = k_prev
    vprev[...] = v_prev

    cur_state = [state[0], state[1]]
    for rs in rows:
        yret, cur_state = _retention(rq[rs], rk[rs], rv[rs], rg[rs], cur_state,
                                     decay_ref, qdec, kdec, gpow)
        mix[rs, ATT_W:ATT_W + RET_W] = _bf(yret)
    state[0] = cur_state[0]
    state[1] = cur_state[1]

    x1 = x + mod_ref[2:3, :] * jnp.dot(mix[...], w_out_ref[...],
                                       preferred_element_type=jnp.float32)
    x1_ref[...] = x1
    h2 = _pre_norm(x1, gffn_ref[...], mod_ref[3:4, :], mod_ref[4:5, :])
    h2_ref[...] = h2
    topi, topw, topi_t, counts = _route(h2, wrh_ref, wrl_ref, br_ref[...])
    topi_ref[...] = topi
    topw_ref[...] = topw
    topit_ref[...] = topi_t
    cnt_ref[:, 0, :] = counts

    @pl.when(step == pl.num_programs(0) - 1)
    def _():
        kwin_ref[...] = kprev[...]
        vwin_ref[...] = vprev[...]
        rend_ref[...] = state[...]


def _const_spec(shape):
    nd = len(shape)
    return pl.BlockSpec(shape, lambda *_: (0,) * nd)


def _prompt_mix(x, mod, sinks, gmix, gffn, w_in, qg, kg, bdq, bdk, tabs, ret_tabs, w_out,
                wrh, wrl, br):
    t, d = x.shape
    tb = PROMPT_BLOCK
    rot_blk, rot_tab = tabs
    decay, qdec, kdec, gpow = ret_tabs
    row_spec = lambda w: pl.BlockSpec((tb, w), lambda i, *_: (i, 0))
    consts = [mod, gmix, gffn, w_in, qg, kg, bdq, bdk]
    consts2 = [rot_tab, decay, qdec, kdec, gpow, w_out, wrh, wrl, br]
    grid_spec = pltpu.PrefetchScalarGridSpec(
        num_scalar_prefetch=1,
        grid=(t // tb,),
        in_specs=([row_spec(d)] + [_const_spec(a.shape) for a in consts]
                  + [pl.BlockSpec((1,) + rot_blk.shape[1:], lambda i, *_: (i, 0, 0))]
                  + [_const_spec(a.shape) for a in consts2]),
        out_specs=[row_spec(d), row_spec(d), row_spec(LANES), row_spec(LANES),
                   pl.BlockSpec((SUBLANES, tb), lambda i, *_: (0, i)),
                   pl.BlockSpec((tb // COUNT_BLOCK, 1, LANES), lambda i, *_: (i, 0, 0)),
                   _const_spec((WINDOW, KV_W)), _const_spec((WINDOW, KV_W)),
                   _const_spec((2, LANES, RET_DV))],
        scratch_shapes=[pltpu.VMEM((WINDOW, KV_W), jnp.float32),
                        pltpu.VMEM((WINDOW, KV_W), jnp.float32),
                        pltpu.VMEM((2, LANES, RET_DV), jnp.float32),
                        pltpu.VMEM((tb, ATT_W + RET_W), jnp.bfloat16)],
    )
    return pl.pallas_call(
        _prompt_mix_kernel,
        grid_spec=grid_spec,
        out_shape=[jax.ShapeDtypeStruct((t, d), jnp.float32),
                   jax.ShapeDtypeStruct((t, d), jnp.float32),
                   jax.ShapeDtypeStruct((t, LANES), jnp.int32),
                   jax.ShapeDtypeStruct((t, LANES), jnp.float32),
                   jax.ShapeDtypeStruct((SUBLANES, t), jnp.int32),
                   jax.ShapeDtypeStruct((t // COUNT_BLOCK, 1, LANES), jnp.int32),
                   jax.ShapeDtypeStruct((WINDOW, KV_W), jnp.float32),
                   jax.ShapeDtypeStruct((WINDOW, KV_W), jnp.float32),
                   jax.ShapeDtypeStruct((2, LANES, RET_DV), jnp.float32)],
        compiler_params=pltpu.CompilerParams(
            dimension_semantics=("arbitrary",), vmem_limit_bytes=VMEM_LIMIT_BYTES),
    )(sinks, x, *consts, rot_blk, *consts2)


def _per_seq(stacked, rows, ds, s):
    return jnp.concatenate([stacked[g * rows + s * ds:g * rows + (s + 1) * ds]
                            for g in range(stacked.shape[0] // rows)], axis=0)


def _from_per_seq(parts, ds):
    groups = parts[0].shape[0] // ds
    return jnp.concatenate([p[g * ds:(g + 1) * ds] for g in range(groups) for p in parts], axis=0)


def _sample_attention(q, k_new, v_new, ck_ref, cv_ref, sinks_ref, ds):
    rows = q.shape[0]
    nseq, cache_w = ck_ref.shape[0], ck_ref.shape[1]
    tok_c = lax.broadcasted_iota(jnp.int32, (4 * rows, cache_w), 0) % rows
    col_c = lax.broadcasted_iota(jnp.int32, (4 * rows, cache_w), 1)
    delta = tok_c % ds + cache_w - col_c
    mask_c = (delta >= 0) & (delta < WINDOW)
    tok_n = lax.broadcasted_iota(jnp.int32, (4 * rows, rows), 0) % rows
    col_n = lax.broadcasted_iota(jnp.int32, (4 * rows, rows), 1)
    mask_n = (tok_n // ds == col_n // ds) & (col_n % ds <= tok_n % ds)

    outs = []
    for heads, rolled in ((_NAT_HEADS, False), (_ROL_HEADS, True)):
        arrange = (lambda a: pltpu.roll(a, HEAD_DIM, axis=1)) if rolled else (lambda a: a)
        qst = _stack_heads(q, heads)
        s_new = jnp.where(mask_n, _dot_nt(qst, arrange(k_new)), NEG_BIG)
        s_cache = _from_per_seq(
            [_dot_nt(_per_seq(qst, rows, ds, s), arrange(ck_ref[s])) for s in range(nseq)], ds)
        s_cache = jnp.where(mask_c, s_cache, NEG_BIG)
        sink = _sink_column(sinks_ref, heads, rows)
        m = jnp.maximum(jnp.maximum(jnp.max(s_cache, axis=-1, keepdims=True),
                                    jnp.max(s_new, axis=-1, keepdims=True)), sink)
        p_cache = jnp.exp(s_cache - m)
        p_new = jnp.exp(s_new - m)
        denom = (jnp.sum(p_cache, axis=-1, keepdims=True) + jnp.sum(p_new, axis=-1, keepdims=True)
                 + jnp.exp(sink - m))
        o = _from_per_seq(
            [_dot(_per_seq(p_cache, rows, ds, s), arrange(cv_ref[s])) for s in range(nseq)], ds)
        outs.append((o + _dot(p_new, arrange(v_new))) / denom)
    o_nat, o_rol = outs
    lo = _lane_lo((rows, LANES))
    blk = lambda o, n: o[n * rows:(n + 1) * rows]
    return jnp.concatenate([
        jnp.where(lo, blk(o_nat, 0), blk(o_rol, 0)),
        jnp.where(lo, blk(o_nat, 1), blk(o_rol, 1)),
        jnp.where(lo, blk(o_rol, 2), blk(o_nat, 2)),
        jnp.where(lo, blk(o_rol, 3), blk(o_nat, 3)),
    ], axis=1)


def _sample_retention(rq, rk, rv, rg, st_ref, rend_ref, decay_ref, qdec, kdec, gpow, ds):
    rows = rq.shape[0]
    nseq = st_ref.shape[0]
    lo = _lane_lo((rows, LANES))
    rkd = rk * kdec
    outs = []
    for m in range(RET_HEADS // 2):
        sl = slice(m * LANES, (m + 1) * LANES)
        heads = (2 * m, 2 * m + 1)
        rqm = [jnp.where(lo if p == 0 else ~lo, rq[:, sl], 0.0) for p in range(2)]
        rkdm = [jnp.where(lo if p == 0 else ~lo, rkd[:, sl], 0.0) for p in range(2)]
        rv_h = [rv[:, h * RET_DV:(h + 1) * RET_DV] for h in heads]
        q2 = jnp.concatenate(rqm, axis=0)
        k2 = jnp.concatenate(rkdm, axis=0)
        v2 = jnp.concatenate(rv_h, axis=0)
        inter = _from_per_seq(
            [_dot(_per_seq(q2, rows, ds, s), st_ref[s, m]) for s in range(nseq)], ds)
        for s in range(nseq):
            rend_ref[s, m] = (st_ref[s, m] * gpow[:, m:m + 1]
                              + _dot_tn(_per_seq(k2, rows, ds, s), _per_seq(v2, rows, ds, s)))
        for p, h in enumerate(heads):
            intra = _dot(_dot_nt(rqm[p], rk[:, sl]) * decay_ref[h], rv_h[p])
            ret = intra + inter[p * rows:(p + 1) * rows] * qdec[:, h * RET_DV:(h + 1) * RET_DV]
            rg_h = rg[:, h * RET_DV:(h + 1) * RET_DV]
            outs.append(rg_h * jax.nn.sigmoid(rg_h) * _rms_rows(ret))
    return jnp.concatenate(outs, axis=1)


def _sample_mix_kernel(sinks_ref, x_ref, mod_ref, gmix_ref, gffn_ref, w_in_ref, qg_ref, kg_ref,
                       bdq_ref, bdk_ref, cosa_ref, sina_ref, cosr_ref, sinr_ref,
                       decay_ref, qdec_ref, kdec_ref, gpow_ref, w_out_ref,
                       wrh_ref, wrl_ref, br_ref, ck_ref, cv_ref, st_ref,
                       x1_ref, h2_ref, topi_ref, topw_ref, topit_ref, cnt_ref, kwin_ref, vwin_ref, rend_ref,
                       mix):
    nseq, cache_w = ck_ref.shape[0], ck_ref.shape[1]
    ds = x_ref.shape[0] // nseq
    d = x_ref.shape[1]
    x = x_ref[...]
    mod = lambda i: _repeat_rows(mod_ref[:, i * d:(i + 1) * d], ds)
    h = _pre_norm(x, gmix_ref[...], mod(0), mod(1))
    q, k, v, rq, rk, rv, rg = _project(
        h, w_in_ref, qg_ref[...], kg_ref[...], bdq_ref[...], bdk_ref[...],
        cosa_ref[...], sina_ref[...], cosr_ref[...], sinr_ref[...])

    for s in range(nseq):
        rs = slice(s * ds, (s + 1) * ds)
        kwin_ref[s] = jnp.concatenate([ck_ref[s, ds:, :], k[rs]], axis=0)
        vwin_ref[s] = jnp.concatenate([cv_ref[s, ds:, :], v[rs]], axis=0)
    mix[:, 0:ATT_W] = _bf(_sample_attention(q, k, v, ck_ref, cv_ref, sinks_ref, ds))
    mix[:, ATT_W:ATT_W + RET_W] = _bf(_sample_retention(
        rq, rk, rv, rg, st_ref, rend_ref, decay_ref, qdec_ref[...], kdec_ref[...], gpow_ref[...], ds))

    x1 = x + mod(2) * jnp.dot(mix[...], w_out_ref[...], preferred_element_type=jnp.float32)
    x1_ref[...] = x1
    h2 = _pre_norm(x1, gffn_ref[...], mod(3), mod(4))
    h2_ref[...] = h2
    topi, topw, topi_t, counts = _route(h2, wrh_ref, wrl_ref, br_ref[...])
    topi_ref[...] = topi
    topw_ref[...] = topw
    topit_ref[...] = topi_t
    cnt_ref[:, 0, :] = counts


def _sample_mix(x, mod_seq, sinks, gmix, gffn, w_in, qg, kg, bdq, bdk, tabs, ret_tabs, w_out,
                wrh, wrl, br, cache_k, cache_v, state):
    t, d = x.shape
    nb, cache_w = cache_k.shape[0], cache_k.shape[1]
    ds = t // nb
    g = SAMPLE_SEQS
    tb = g * ds
    cosa, sina, cosr, sinr = tabs
    decay, qdec, kdec, gpow = ret_tabs
    row_spec = lambda w: pl.BlockSpec((tb, w), lambda i, *_: (i, 0))
    seq3 = lambda a: pl.BlockSpec((g,) + a.shape[1:], lambda i, *_: (i,) + (0,) * (a.ndim - 1))
    consts = [gmix, gffn, w_in, qg, kg, bdq, bdk, cosa, sina, cosr, sinr,
              decay, qdec, kdec, gpow, w_out, wrh, wrl, br]
    grid_spec = pltpu.PrefetchScalarGridSpec(
        num_scalar_prefetch=1,
        grid=(nb // g,),
        in_specs=([row_spec(d), pl.BlockSpec((g, mod_seq.shape[1]), lambda i, *_: (i, 0))]
                  + [_const_spec(a.shape) for a in consts]
                  + [seq3(cache_k), seq3(cache_v), seq3(state)]),
        out_specs=[row_spec(d), row_spec(d), row_spec(LANES), row_spec(LANES),
                   pl.BlockSpec((SUBLANES, tb), lambda i, *_: (0, i)),
                   pl.BlockSpec((tb // COUNT_BLOCK, 1, LANES), lambda i, *_: (i, 0, 0)),
                   seq3(cache_k), seq3(cache_v), seq3(state)],
        scratch_shapes=[pltpu.VMEM((tb, ATT_W + RET_W), jnp.bfloat16)],
    )
    return pl.pallas_call(
        _sample_mix_kernel,
        grid_spec=grid_spec,
        out_shape=[jax.ShapeDtypeStruct((t, d), jnp.float32),
                   jax.ShapeDtypeStruct((t, d), jnp.float32),
                   jax.ShapeDtypeStruct((t, LANES), jnp.int32),
                   jax.ShapeDtypeStruct((t, LANES), jnp.float32),
                   jax.ShapeDtypeStruct((SUBLANES, t), jnp.int32),
                   jax.ShapeDtypeStruct((t // COUNT_BLOCK, 1, LANES), jnp.int32),
                   jax.ShapeDtypeStruct(cache_k.shape, jnp.float32),
                   jax.ShapeDtypeStruct(cache_v.shape, jnp.float32),
                   jax.ShapeDtypeStruct(state.shape, jnp.float32)],
        compiler_params=pltpu.CompilerParams(
            dimension_semantics=("arbitrary",), vmem_limit_bytes=VMEM_LIMIT_BYTES),
    )(sinks, x, mod_seq, *consts, cache_k, cache_v, state)


def _chunk_copies(cnt8, src_ref, src_start, dst_ref, dst_start, sem):
    for bit in range(CHUNK_BITS - 1, SUBLANES.bit_length() - 2, -1):
        size = 1 << bit
        done = (cnt8 >> (bit + 1)) << (bit + 1)

        @pl.when(((cnt8 >> bit) & 1) == 1)
        def _():
            src = pl.multiple_of(src_start + done, SUBLANES)
            dst = pl.multiple_of(dst_start + done, SUBLANES)
            pltpu.make_async_copy(src_ref.at[pl.ds(src, size)],
                                  dst_ref.at[pl.ds(dst, size)], sem).start()


def _wait_rows(ref, rows8, sem, bits=STAGE_BITS):
    for bit in range(bits - 1, SUBLANES.bit_length() - 2, -1):
        size = 1 << bit

        @pl.when(((rows8 >> bit) & 1) == 1)
        def _():
            pltpu.make_async_copy(ref.at[pl.ds(0, size)], ref.at[pl.ds(0, size)], sem).wait()


def _dispatch_kernel(cnt_ref, gstart_ref, cb_ref, blkrows_ref, ends_ref, padcnt_ref,
                     topit_p_ref, topit_s_ref, h2_p_ref, h2_s_ref, cbcol_ref, xs_ref,
                     stage, zeros, sems, zsem, *, nblk_p):
    step = pl.program_id(0)
    nblk = pl.num_programs(0)
    tb = h2_p_ref.shape[0]
    slot = step % STAGE_SLOTS

    tile = zeros.shape[0]
    first_unused = ends_ref[N_EXPERTS - 1] // tile
    n_tiles = xs_ref.shape[0] // tile

    def tail_fill(t):
        return pltpu.make_async_copy(zeros, xs_ref.at[pl.ds(pl.multiple_of(t * tile, tile), tile)],
                                     zsem)

    @pl.when(step == 0)
    def _():
        zeros[...] = jnp.zeros_like(zeros)

        def pad_fill(e, carry):
            _chunk_copies(padcnt_ref[e], zeros, 0, xs_ref, ends_ref[e] - padcnt_ref[e], zsem)
            return carry

        lax.fori_loop(0, N_EXPERTS, pad_fill, 0)
        lax.fori_loop(first_unused, n_tiles, lambda t, c: (tail_fill(t).start(), c)[1], 0)

    @pl.when(step == nblk - 1)
    def _():
        _wait_rows(xs_ref, padcnt_ref[N_EXPERTS], zsem, bits=(N_EXPERTS * tile).bit_length())
        lax.fori_loop(first_unused, n_tiles, lambda t, c: (tail_fill(t).wait(), c)[1], 0)

    is_p = step < nblk_p
    topit = jnp.where(is_p, topit_p_ref[...], topit_s_ref[...])
    h2 = jnp.where(is_p, h2_p_ref[...], h2_s_ref[...])
    cbcol = cbcol_ref[0]

    sub = lax.broadcasted_iota(jnp.int32, (LANES, tb), 0)
    picks = [sub == topit[k:k + 1, :] for k in range(TOP_K)]
    onehot = jnp.zeros((LANES, tb), jnp.float32)
    for pk in picks:
        onehot = jnp.where(pk, 1.0, onehot)
    r = lax.broadcasted_iota(jnp.int32, (tb, tb), 0)
    c = lax.broadcasted_iota(jnp.int32, (tb, tb), 1)
    earlier = jnp.where(r < c, 1.0, 0.0)
    before = _dot(onehot, earlier)
    base = before + cbcol.astype(jnp.float32)
    row = lax.broadcasted_iota(jnp.int32, (stage.shape[1], tb), 0)
    sel = jnp.zeros((stage.shape[1], tb), jnp.float32)
    for pk in picks:
        col = jnp.sum(jnp.where(pk, base, 0.0), axis=0, keepdims=True).astype(jnp.int32)
        sel = jnp.where(row == col, 1.0, sel)

    stage[slot] = _dot(sel, h2)

    def issue(e, carry):
        i = step * N_EXPERTS + e
        _chunk_copies(cnt_ref[i], stage.at[slot], cb_ref[i], xs_ref, gstart_ref[i], sems.at[slot])
        return carry

    lax.fori_loop(0, N_EXPERTS, issue, 0)

    oldest = step - (STAGE_SLOTS - 1)

    @pl.when(oldest >= 0)
    def _():
        _wait_rows(xs_ref, blkrows_ref[jnp.maximum(oldest, 0)], sems.at[(step + 1) % STAGE_SLOTS])

    @pl.when(step == nblk - 1)
    def _():
        for back in range(STAGE_SLOTS - 2, -1, -1):
            @pl.when(step - back >= 0)
            def _():
                blk = jnp.maximum(step - back, 0)
                _wait_rows(xs_ref, blkrows_ref[blk], sems.at[blk % STAGE_SLOTS])


def _dispatch(tables, topit_p, topit_s, h2_p, h2_s, cbcol, n_rows):
    n_p, d = h2_p.shape
    n_s = h2_s.shape[0]
    tb = ROUTE_BLOCK
    nblk_p, nblk_s = n_p // tb, n_s // tb
    pidx = lambda i, *_: jnp.minimum(i, nblk_p - 1)
    sidx = lambda i, *_: jnp.maximum(i - nblk_p, 0)
    grid_spec = pltpu.PrefetchScalarGridSpec(
        num_scalar_prefetch=6,
        grid=(nblk_p + nblk_s,),
        in_specs=[pl.BlockSpec((SUBLANES, tb), lambda i, *_: (0, pidx(i))),
                  pl.BlockSpec((SUBLANES, tb), lambda i, *_: (0, sidx(i))),
                  pl.BlockSpec((tb, d), lambda i, *_: (pidx(i), 0)),
                  pl.BlockSpec((tb, d), lambda i, *_: (sidx(i), 0)),
                  pl.BlockSpec((1, LANES, 1), lambda i, *_: (i, 0, 0))],
        out_specs=pl.BlockSpec(memory_space=pl.ANY),
        scratch_shapes=[pltpu.VMEM((STAGE_SLOTS, STAGE_ROWS, d), jnp.float32),
                        pltpu.VMEM((EXPERT_TILE, d), jnp.float32),
                        pltpu.SemaphoreType.DMA((STAGE_SLOTS,)),
                        pltpu.SemaphoreType.DMA],
    )
    return pl.pallas_call(
        functools.partial(_dispatch_kernel, nblk_p=nblk_p),
        grid_spec=grid_spec,
        out_shape=jax.ShapeDtypeStruct((n_rows, d), jnp.float32),
        compiler_params=pltpu.CompilerParams(
            dimension_semantics=("arbitrary",), vmem_limit_bytes=VMEM_LIMIT_BYTES),
    )(*tables, topit_p, topit_s, h2_p, h2_s, cbcol)


def _expert_mlp(x, wgu_bf, bgu, wd_bf, bd):
    d_ff = wd_bf.shape[0]
    gu = jnp.dot(_bf(x), wgu_bf[...], preferred_element_type=jnp.float32) + bgu
    glu = jnp.minimum(gu[:, :d_ff], SWIGLU_LIMIT)
    lin = jnp.clip(gu[:, d_ff:], -SWIGLU_LIMIT, SWIGLU_LIMIT)
    act = glu * jax.nn.sigmoid(SWIGLU_ALPHA * glu) * (lin + 1.0)
    return jnp.dot(_bf(act), wd_bf[...], preferred_element_type=jnp.float32) + bd


def _experts_kernel(tile_expert_ref, n_used_ref, groups_ref, next_ref,
                    x_ref, wgu_hbm, bgu_ref, wd_hbm, bd_ref, y_ref,
                    wgu_f32, wd_f32, wgu_bf, wd_bf, sems):
    step = pl.program_id(0)
    groups = groups_ref[step]

    def weight_copies(expert):
        return (pltpu.make_async_copy(wgu_hbm.at[expert], wgu_f32, sems.at[0]),
                pltpu.make_async_copy(wd_hbm.at[expert], wd_f32, sems.at[1]))

    @pl.when(groups > 0)
    def _():
        e = tile_expert_ref[step]
        prev = tile_expert_ref[jnp.maximum(step - 1, 0)]

        @pl.when(step == 0)
        def _():
            for cp in weight_copies(e):
                cp.start()

        @pl.when(jnp.logical_or(step == 0, e != prev))
        def _():
            for cp in weight_copies(e):
                cp.wait()
            wgu_bf[...] = _bf(wgu_f32[...])
            wd_bf[...] = _bf(wd_f32[...])
            nxt = next_ref[e]

            @pl.when(nxt >= 0)
            def _():
                for cp in weight_copies(nxt):
                    cp.start()

    tm = x_ref.shape[0]
    for n in range(1, EXPERT_GROUPS + 1):
        @pl.when(groups == n)
        def _():
            rows = n * (tm // EXPERT_GROUPS)
            y_ref[0:rows, :] = _expert_mlp(x_ref[0:rows, :], wgu_bf, bgu_ref[0], wd_bf, bd_ref[0])
            if rows < tm:
                y_ref[rows:, :] = jnp.zeros((tm - rows, y_ref.shape[1]), y_ref.dtype)

    @pl.when(groups == 0)
    def _():
        y_ref[...] = jnp.zeros_like(y_ref)


def _experts(tile_expert, n_used, tile_groups, next_used, x_sorted, w_gate_up, b_gate_up,
             w_down, b_down):
    p = x_sorted.shape[0]
    n_e, d, two_ff = w_gate_up.shape
    d_ff = two_ff // 2
    tm = EXPERT_TILE
    tile = lambda i, te, nu: jnp.minimum(i, nu[0] - 1)
    grid_spec = pltpu.PrefetchScalarGridSpec(
        num_scalar_prefetch=4,
        grid=(p // tm,),
        in_specs=[pl.BlockSpec((tm, d), lambda i, te, nu, *_: (tile(i, te, nu), 0)),
                  pl.BlockSpec(memory_space=pl.ANY),
                  pl.BlockSpec((1, 1, two_ff), lambda i, te, *_: (te[i], 0, 0)),
                  pl.BlockSpec(memory_space=pl.ANY),
                  pl.BlockSpec((1, 1, d), lambda i, te, *_: (te[i], 0, 0))],
        out_specs=pl.BlockSpec((tm, d), lambda i, *_: (i, 0)),
        scratch_shapes=[pltpu.VMEM((d, two_ff), jnp.float32),
                        pltpu.VMEM((d_ff, d), jnp.float32),
                        pltpu.VMEM((d, two_ff), jnp.bfloat16),
                        pltpu.VMEM((d_ff, d), jnp.bfloat16),
                        pltpu.SemaphoreType.DMA((2,))],
    )
    return pl.pallas_call(
        _experts_kernel,
        grid_spec=grid_spec,
        out_shape=jax.ShapeDtypeStruct(x_sorted.shape, jnp.float32),
        compiler_params=pltpu.CompilerParams(
            dimension_semantics=("arbitrary",), vmem_limit_bytes=VMEM_LIMIT_BYTES),
    )(tile_expert, n_used, tile_groups, next_used, x_sorted, w_gate_up,
      b_gate_up.reshape(n_e, 1, two_ff), w_down, b_down.reshape(n_e, 1, d))


def _combine_kernel(cnt_ref, gstart_ref, cb_ref, blkrows_ref,
                    topi_p_ref, topi_s_ref, topw_p_ref, topw_s_ref, x1_p_ref, x1_s_ref,
                    gate_p_ref, gate_s_ref, cbrow_ref, y_ref, out_p_ref, out_s_ref,
                    ybuf, sems, *, nblk_p):
    step = pl.program_id(0)
    nblk = pl.num_programs(0)
    tb = x1_p_ref.shape[0]
    slot = step % STAGE_SLOTS

    def fetch_chunk(blk, into, e):
        i = blk * N_EXPERTS + e
        _chunk_copies(cnt_ref[i], y_ref, gstart_ref[i], ybuf.at[into], cb_ref[i], sems.at[into])

    def fetch(blk, into):
        lax.fori_loop(0, N_EXPERTS, lambda e, c: (fetch_chunk(blk, into, e), c)[1], 0)

    @pl.when(step == 0)
    def _():
        ybuf[...] = jnp.zeros_like(ybuf)
        for ahead in range(STAGE_SLOTS - 1):
            @pl.when(ahead < nblk)
            def _():
                fetch(ahead, ahead)

    for e in range(N_EXPERTS):
        fetch_chunk(step + STAGE_SLOTS - 1, (step + STAGE_SLOTS - 1) % STAGE_SLOTS, e)

    _wait_rows(y_ref, blkrows_ref[step], sems.at[slot])

    is_p = step < nblk_p
    topi = jnp.where(is_p, topi_p_ref[...], topi_s_ref[...])
    topw = jnp.where(is_p, topw_p_ref[...], topw_s_ref[...])
    cbrow = cbrow_ref[0]

    lane = lax.broadcasted_iota(jnp.int32, (tb, LANES), 1)
    picks = []
    onehot = jnp.zeros((tb, LANES), jnp.float32)
    for k in range(TOP_K):
        ix = jnp.sum(jnp.where(lane == k, topi, 0), axis=-1, keepdims=True)
        picks.append(lane == ix)
        onehot = jnp.where(picks[k], 1.0, onehot)
    r = lax.broadcasted_iota(jnp.int32, (tb, tb), 0)
    c = lax.broadcasted_iota(jnp.int32, (tb, tb), 1)
    earlier = jnp.where(c < r, 1.0, 0.0)
    base = _dot(earlier, onehot) + cbrow.astype(jnp.float32)
    cols = [jnp.sum(jnp.where(picks[k], base, 0.0), axis=-1, keepdims=True).astype(jnp.int32)
            for k in range(TOP_K)]
    wks = [jnp.sum(jnp.where(lane == k, topw, 0.0), axis=-1, keepdims=True) for k in range(TOP_K)]
    moe = None
    for c0 in range(0, ybuf.shape[1], COMBINE_SLAB):
        colid = lax.broadcasted_iota(jnp.int32, (tb, COMBINE_SLAB), 1) + c0
        weights = jnp.zeros((tb, COMBINE_SLAB), jnp.float32)
        for k in range(TOP_K):
            weights = jnp.where(colid == cols[k], wks[k], weights)
        part = _dot(weights, ybuf[slot, c0:c0 + COMBINE_SLAB, :])
        moe = part if moe is None else moe + part

    @pl.when(is_p)
    def _():
        out_p_ref[...] = x1_p_ref[...] + gate_p_ref[...] * moe

    @pl.when(jnp.logical_not(is_p))
    def _():
        gate_s = _repeat_rows(gate_s_ref[...], tb // gate_s_ref.shape[0])
        out_s_ref[...] = x1_s_ref[...] + gate_s * moe


def _combine(tables, topi_p, topi_s, topw_p, topw_s, x1_p, x1_s, gate_p, mod_seq, dec_seq, cbrow,
             y_sorted):
    n_p, d = x1_p.shape
    n_s = x1_s.shape[0]
    tb = ROUTE_BLOCK
    nblk_p, nblk_s = n_p // tb, n_s // tb
    pidx = lambda i, *_: jnp.minimum(i, nblk_p - 1)
    sidx = lambda i, *_: jnp.maximum(i - nblk_p, 0)
    prow = lambda w: pl.BlockSpec((tb, w), lambda i, *_: (pidx(i), 0))
    srow = lambda w: pl.BlockSpec((tb, w), lambda i, *_: (sidx(i), 0))
    grid_spec = pltpu.PrefetchScalarGridSpec(
        num_scalar_prefetch=4,
        grid=(nblk_p + nblk_s,),
        in_specs=[prow(LANES), srow(LANES), prow(LANES), srow(LANES), prow(d), srow(d),
                  pl.BlockSpec((1, d), lambda i, *_: (0, 0)),
                  pl.BlockSpec((tb // dec_seq, d), lambda i, *_: (sidx(i), 5)),
                  pl.BlockSpec((1, 1, LANES), lambda i, *_: (i, 0, 0)),
                  pl.BlockSpec(memory_space=pl.ANY)],
        out_specs=[prow(d), srow(d)],
        scratch_shapes=[pltpu.VMEM((STAGE_SLOTS, STAGE_ROWS, d), jnp.float32),
                        pltpu.SemaphoreType.DMA((STAGE_SLOTS,))],
    )
    return pl.pallas_call(
        functools.partial(_combine_kernel, nblk_p=nblk_p),
        grid_spec=grid_spec,
        out_shape=[jax.ShapeDtypeStruct((n_p, d), jnp.float32),
                   jax.ShapeDtypeStruct((n_s, d), jnp.float32)],
        compiler_params=pltpu.CompilerParams(
            dimension_semantics=("arbitrary",), vmem_limit_bytes=VMEM_LIMIT_BYTES),
    )(*tables, topi_p, topi_s, topw_p, topw_s, x1_p, x1_s, gate_p, mod_seq, cbrow, y_sorted)


def _rotary_tables(pos, inv_freq):
    ang = np.asarray(pos, np.float64)[:, None] * inv_freq[None, :]
    cos, sin = np.cos(ang), np.sin(ang)
    cos_t = np.tile(cos, (1, LANES // cos.shape[1]))
    sin_t = np.tile(np.concatenate([-sin, sin], axis=1), (1, LANES // (2 * sin.shape[1])))
    return cos_t.astype(np.float32), sin_t.astype(np.float32)


def _rotary_split_tables(block, nblk, inv_freq):
    half = inv_freq.shape[0]
    freq = np.tile(inv_freq, LANES // half)
    sign = np.tile(np.concatenate([-np.ones(half), np.ones(half)]), LANES // (2 * half))
    base = (np.arange(nblk) * block).astype(np.float64)[:, None] * freq[None, :]
    off = np.arange(block).astype(np.float64)[:, None] * freq[None, :]
    blk = np.stack([np.cos(base), np.sin(base)], axis=1)
    tab = np.stack([np.cos(off), np.sin(off), np.cos(off) * sign, np.sin(off) * sign])
    return blk.astype(np.float32), tab.astype(np.float32)


def _retention_tables(chunk, nseq=1):
    log_gamma = np.log1p(-np.exp2(-5.0 - np.arange(RET_HEADS, dtype=np.float64)))
    idx = np.arange(chunk, dtype=np.float64)
    diff = idx[:, None] - idx[None, :]
    decay = np.where(diff[None] >= 0,
                     np.exp(np.maximum(diff, 0.0)[None] * log_gamma[:, None, None]), 0.0)
    decay = np.stack([np.kron(np.eye(nseq), decay[h]) for h in range(RET_HEADS)])
    q_decay = np.exp((idx + 1.0)[:, None] * log_gamma[None, :])
    k_decay = np.exp((chunk - 1.0 - idx)[:, None] * log_gamma[None, :])
    qdec = np.tile(np.repeat(q_decay, RET_DV, axis=1), (nseq, 1))
    kdec = np.tile(np.repeat(k_decay, RET_DK, axis=1), (nseq, 1))
    g_chunk = np.exp(chunk * log_gamma)
    gpow = np.repeat(g_chunk.reshape(RET_HEADS // 2, 2), RET_DK, axis=1).T
    return tuple(a.astype(np.float32) for a in (decay, qdec, kdec, gpow))


def _block_diag_mean(width):
    head = np.arange(width) // HEAD_DIM
    return jnp.asarray(np.where(head[:, None] == head[None, :], 1.0 / HEAD_DIM, 0.0), jnp.bfloat16)


def kernel(x_prompt, x_sample, cache_k, cache_v, state_ret, c_prompt, c_sample, w_ada, b_ada,
           g_mix, w_in, q_gain, k_gain, sinks, w_out, g_ffn, w_router, b_router, w_gate_up,
           b_gate_up, w_down, b_down):
    depth = w_ada.shape[0]
    batch, seq, d = x_prompt.shape
    dec_batch, dec_seq, _ = x_sample.shape
    cache_w = cache_k.shape[2]
    assert batch == 1 and depth == 1
    assert seq % PROMPT_BLOCK == 0 and PROMPT_BLOCK % WINDOW == 0 and WINDOW == RET_CHUNK
    assert dec_batch % SAMPLE_SEQS == 0 and cache_w == WINDOW
    n_p, n_s = batch * seq, dec_batch * dec_seq
    n_tok = n_p + n_s
    assert n_p % ROUTE_BLOCK == 0 and n_s % ROUTE_BLOCK == 0
    assert ROUTE_BLOCK % COUNT_BLOCK == 0 and (SAMPLE_SEQS * dec_seq) % COUNT_BLOCK == 0
    assert PROMPT_BLOCK % COUNT_BLOCK == 0
    assert STAGE_ROWS % COMBINE_SLAB == 0 and EXPERT_TILE % (EXPERT_GROUPS * SUBLANES) == 0

    l = 0
    f32 = jnp.float32
    rope_freq = 1.0 / (ROPE_THETA ** (np.arange(0, HEAD_DIM, 2, dtype=np.float64) / HEAD_DIM))
    ret_freq = 1.0 / (ROPE_THETA ** np.linspace(0.0, 1.0, RET_DK // 2))
    pos_s = PAST_LEN + np.arange(dec_seq)
    blk_a, tab_a = _rotary_split_tables(PROMPT_BLOCK, seq // PROMPT_BLOCK, rope_freq)
    blk_r, tab_r = _rotary_split_tables(PROMPT_BLOCK, seq // PROMPT_BLOCK, ret_freq)
    tabs_p = (np.concatenate([blk_a, blk_r], axis=1), np.concatenate([tab_a, tab_r], axis=0))
    tabs_s = tuple(np.tile(a, (SAMPLE_SEQS, 1))
                   for a in _rotary_tables(pos_s, rope_freq) + _rotary_tables(pos_s, ret_freq))
    ret_p = _retention_tables(RET_CHUNK)
    ret_s = _retention_tables(dec_seq, SAMPLE_SEQS)

    n_c = batch + dec_batch
    c_rows = -(-n_c // SUBLANES) * SUBLANES
    c_all = jnp.concatenate([c_sample, c_prompt, jnp.zeros((c_rows - n_c, d), f32)], axis=0)
    mod = _adaln(c_all, w_ada[l], b_ada[l])
    mod_p = mod[dec_batch].reshape(6, d)

    w_in_bf = _bf(w_in[l])
    w_out_bf = _bf(w_out[l])
    qg = jnp.tile(q_gain[l], ATT_HEADS).reshape(1, ATT_W)
    kg = jnp.tile(k_gain[l], ATT_KV_HEADS).reshape(1, KV_W)
    bdq, bdk = _block_diag_mean(ATT_W), _block_diag_mean(KV_W)
    gmix = g_mix[l].reshape(1, d)
    gffn = g_ffn[l].reshape(1, d)
    wr = jnp.pad(w_router[l], ((0, 0), (0, LANES - N_EXPERTS)))
    wr_hi = _bf(wr)
    wr_cat = jnp.concatenate([wr_hi, _bf(wr - wr_hi.astype(f32))], axis=1)
    br = jnp.concatenate([b_router[l], jnp.full((LANES - N_EXPERTS,), NEG_BIG, f32)]).reshape(1, LANES)
    shared = (sinks[l], gmix, gffn, w_in_bf, qg, kg, bdq, bdk)

    (x1_p, h2_p, topi_p, topw_p, topit_p, cnt_p, kwin_p, vwin_p, rend_p) = _prompt_mix(
        x_prompt.reshape(n_p, d), mod_p, *shared, tabs_p, ret_p, w_out_bf, wr_hi, wr_cat, br)

    (x1_s, h2_s, topi_s, topw_s, topit_s, cnt_s, kwin_s, vwin_s, rend_s) = _sample_mix(
        x_sample.reshape(n_s, d), mod, *shared, tabs_s, ret_s, w_out_bf, wr_hi, wr_cat, br,
        cache_k[l].reshape(dec_batch, cache_w, KV_W), cache_v[l].reshape(dec_batch, cache_w, KV_W),
        state_ret[l].reshape(dec_batch, 2, LANES, RET_DV))

    tb, tm = ROUTE_BLOCK, EXPERT_TILE
    i32 = jnp.int32
    cnt = jnp.concatenate([cnt_p, cnt_s], axis=0)
    cnt = cnt.reshape(n_tok // tb, tb // COUNT_BLOCK, LANES).sum(axis=1)
    cnt8 = -(-cnt // SUBLANES) * SUBLANES
    rs = jnp.cumsum(cnt8, axis=0) - cnt8
    tot = jnp.sum(cnt8, axis=0)[:N_EXPERTS]
    padded = -(-tot // tm) * tm
    ends = jnp.cumsum(padded)
    gstart = (ends - padded)[None, :] + rs[:, :N_EXPERTS]
    cb = jnp.cumsum(cnt8, axis=1) - cnt8
    blk_rows = jnp.sum(cnt8, axis=1).astype(i32)
    max_rows = n_tok * TOP_K + cnt.shape[0] * N_EXPERTS * (SUBLANES - 1)
    n_rows = -(-max_rows // tm) * tm + N_EXPERTS * tm
    tile_start = jnp.arange(n_rows // tm, dtype=i32) * tm
    tile_expert = jnp.minimum(jnp.sum(ends[None, :] <= tile_start[:, None], axis=1),
                              N_EXPERTS - 1).astype(i32)
    n_used = (ends[-1:] // tm).astype(i32)
    eid = jnp.arange(N_EXPERTS, dtype=i32)
    of_tile = lambda v: jnp.sum(jnp.where(tile_expert[:, None] == eid[None, :], v[None, :], 0), axis=1)
    tile_rows = jnp.clip(of_tile(tot) - (tile_start - of_tile(ends - padded)), 0, tm)
    tile_rows = jnp.where(tile_start < ends[-1], tile_rows, 0)
    tile_groups = (-(-tile_rows // (tm // EXPERT_GROUPS))).astype(i32)
    flat = lambda a: a[:, :N_EXPERTS].reshape(-1).astype(i32)
    tables = (flat(cnt8), gstart.reshape(-1).astype(i32), flat(cb), blk_rows)
    ahead = jnp.zeros(((STAGE_SLOTS - 1) * N_EXPERTS,), i32)
    tables_c = tuple(jnp.concatenate([t, ahead]) for t in tables[:3]) + tables[3:]

    pad_cnt = padded - tot
    pad_cnt = jnp.concatenate([pad_cnt, jnp.sum(pad_cnt, keepdims=True)]).astype(i32)
    x_sorted = _dispatch(tables + (ends.astype(i32), pad_cnt), topit_p, topit_s, h2_p, h2_s,
                         cb.astype(i32)[:, :, None], n_rows)
    used = tot > 0
    later = jnp.where(used[None, :] & (eid[None, :] > eid[:, None]), eid[None, :], N_EXPERTS)
    next_used = jnp.min(later, axis=1)
    next_used = jnp.where(next_used == N_EXPERTS, -1, next_used).astype(i32)
    y_sorted = _experts(tile_expert, n_used, tile_groups, next_used, x_sorted, w_gate_up[l], b_gate_up[l], w_down[l],
                        b_down[l])
    y_p, y_s = _combine(tables_c, topi_p, topi_s, topw_p, topw_s, x1_p, x1_s, mod_p[5:6], mod,
                        dec_seq, cb.astype(i32)[:, None, :], y_sorted)

    kv5 = lambda a, n: a.reshape(1, n, cache_w, ATT_KV_HEADS, HEAD_DIM)
    st5 = lambda a, n: a.reshape(1, n, RET_HEADS, RET_DK, RET_DV)
    return (y_p.reshape(batch, seq, d), y_s.reshape(dec_batch, dec_seq, d),
            kv5(kwin_p, batch), kv5(vwin_p, batch), st5(rend_p, batch),
            kv5(kwin_s, dec_batch), kv5(vwin_s, dec_batch), st5(rend_s, dec_batch))
```

```python
import functools

import jax
import jax.numpy as jnp
import numpy as np
from jax import lax
from jax.experimental import pallas as pl
from jax.experimental.pallas import tpu as pltpu

HEAD_DIM = 64
ATT_HEADS = 8
ATT_KV_HEADS = 2
ATT_W = ATT_HEADS * HEAD_DIM
KV_W = ATT_KV_HEADS * HEAD_DIM
WINDOW = 128
PAST_LEN = 16384
ROPE_THETA = 10000.0
RET_HEADS = 4
RET_DK = 64
RET_DV = 128
RET_QK_W = RET_HEADS * RET_DK
RET_W = RET_HEADS * RET_DV
RET_CHUNK = 128
N_EXPERTS = 32
TOP_K = 4
SWIGLU_LIMIT = 7.0
SWIGLU_ALPHA = 1.702
EPS = 1e-6

_Q0, _K0, _V0 = 0, ATT_W, ATT_W + KV_W
_RQ0 = ATT_W + 2 * KV_W
_RK0 = _RQ0 + RET_QK_W
_RV0 = _RK0 + RET_QK_W
_RG0 = _RV0 + RET_W
IN_W = _RG0 + RET_W

LANES = 128
SUBLANES = 8
VMEM_LIMIT_BYTES = 56 * 1024 * 1024
NEG_BIG = -1e30

PROMPT_BLOCK = 512
SAMPLE_SEQS = 16
COUNT_BLOCK = 128
ROUTE_BLOCK = 512
EXPERT_TILE = 512
EXPERT_GROUPS = 4
CHUNK_BITS = ROUTE_BLOCK.bit_length()
STAGE_ROWS = -(-(TOP_K * ROUTE_BLOCK + N_EXPERTS * (SUBLANES - 1)) // LANES) * LANES
STAGE_BITS = STAGE_ROWS.bit_length()
STAGE_SLOTS = 3
COMBINE_SLAB = 256

_NT = (((1,), (1,)), ((), ()))
_TN = (((0,), (0,)), ((), ()))


def _bf(x):
    return x.astype(jnp.bfloat16)


def _dot(a, b):
    return jnp.dot(_bf(a), _bf(b), preferred_element_type=jnp.float32)


def _dot_nt(a, b):
    return lax.dot_general(_bf(a), _bf(b), _NT, preferred_element_type=jnp.float32)


def _dot_tn(a, b):
    return lax.dot_general(_bf(a), _bf(b), _TN, preferred_element_type=jnp.float32)


def _lane_lo(shape):
    lane = lax.broadcasted_iota(jnp.int32, shape, len(shape) - 1)
    return (lane % LANES) < HEAD_DIM


def _swap_halves(x):
    lane = lax.broadcasted_iota(jnp.int32, x.shape, 1)
    first = (lane % HEAD_DIM) < (HEAD_DIM // 2)
    return jnp.where(first, pltpu.roll(x, LANES - HEAD_DIM // 2, axis=1),
                     pltpu.roll(x, HEAD_DIM // 2, axis=1))


def _rotate(x, cos, sin_signed):
    outs = []
    for j in range(x.shape[1] // LANES):
        xs = x[:, j * LANES:(j + 1) * LANES]
        outs.append(xs * cos + _swap_halves(xs) * sin_signed)
    return outs[0] if len(outs) == 1 else jnp.concatenate(outs, axis=1)


def _repeat_rows(m, n):
    r, w = m.shape
    return jnp.broadcast_to(m[:, None, :], (r, n, w)).reshape(r * n, w)


def _block_rotary(blk_ref, tab_ref, which):
    ca, sa = blk_ref[0, 2 * which:2 * which + 1, :], blk_ref[0, 2 * which + 1:2 * which + 2, :]
    cb, sb, cbs, sbs = (tab_ref[4 * which + n] for n in range(4))
    return ca * cb - sa * sb, sa * cbs + ca * sbs


def _rms_rows(x):
    return x * lax.rsqrt(jnp.mean(x * x, axis=-1, keepdims=True) + EPS)


def _pre_norm(x, gain, shift, scale):
    return _rms_rows(x) * gain * (1.0 + scale) + shift


def _head_norm(x, gain_tiled, blockdiag):
    ms = jnp.dot(_bf(x * x), blockdiag, preferred_element_type=jnp.float32)
    return x * lax.rsqrt(ms + EPS) * gain_tiled


def _project(h, w_in_ref, qg, kg, bd_q, bd_k, cos_a, sin_a, cos_r, sin_r):
    z = jnp.dot(_bf(h), w_in_ref[...], preferred_element_type=jnp.float32)
    q = _rotate(_head_norm(z[:, _Q0:_K0], qg, bd_q), cos_a, sin_a) * (HEAD_DIM ** -0.5)
    k = _rotate(_head_norm(z[:, _K0:_V0], kg, bd_k), cos_a, sin_a)
    v = z[:, _V0:_RQ0]
    rq = _rotate(z[:, _RQ0:_RK0], cos_r, sin_r)
    rk = _rotate(z[:, _RK0:_RV0], cos_r, sin_r) * (RET_DK ** -0.5)
    rv = z[:, _RV0:_RG0]
    rg = z[:, _RG0:IN_W]
    return q, k, v, rq, rk, rv, rg


_NAT_HEADS = (0, 2, 5, 7)
_ROL_HEADS = (1, 3, 4, 6)


def _stack_heads(q, heads):
    lo = _lane_lo((q.shape[0], LANES))
    parts = []
    for j in heads:
        slab = q[:, (j // 2) * LANES:(j // 2 + 1) * LANES]
        parts.append(jnp.where(lo if j % 2 == 0 else ~lo, slab, 0.0))
    return jnp.concatenate(parts, axis=0)


def _sink_column(sinks_ref, heads, rows):
    r = lax.broadcasted_iota(jnp.int32, (len(heads) * rows, 1), 0)
    col = jnp.full((len(heads) * rows, 1), sinks_ref[heads[-1]], jnp.float32)
    for n in range(len(heads) - 2, -1, -1):
        col = jnp.where(r < (n + 1) * rows, sinks_ref[heads[n]], col)
    return col


def _softmax_pv(s, mask, sink, v):
    s = jnp.where(mask, s, NEG_BIG)
    m = jnp.maximum(jnp.max(s, axis=-1, keepdims=True), sink)
    p = jnp.exp(s - m)
    denom = jnp.sum(p, axis=-1, keepdims=True) + jnp.exp(sink - m)
    return _dot(p, v) / denom


def _attention(q, k_all, v_all, mask, sinks_ref):
    rows = q.shape[0]
    k_rol = pltpu.roll(k_all, HEAD_DIM, axis=1)
    v_rol = pltpu.roll(v_all, HEAD_DIM, axis=1)
    mask4 = jnp.concatenate([mask] * 4, axis=0)
    o_nat = _softmax_pv(_dot_nt(_stack_heads(q, _NAT_HEADS), k_all), mask4,
                        _sink_column(sinks_ref, _NAT_HEADS, rows), v_all)
    o_rol = _softmax_pv(_dot_nt(_stack_heads(q, _ROL_HEADS), k_rol), mask4,
                        _sink_column(sinks_ref, _ROL_HEADS, rows), v_rol)
    lo = _lane_lo((rows, LANES))
    blk = lambda o, n: o[n * rows:(n + 1) * rows]
    return jnp.concatenate([
        jnp.where(lo, blk(o_nat, 0), blk(o_rol, 0)),
        jnp.where(lo, blk(o_nat, 1), blk(o_rol, 1)),
        jnp.where(lo, blk(o_rol, 2), blk(o_nat, 2)),
        jnp.where(lo, blk(o_rol, 3), blk(o_nat, 3)),
    ], axis=1)


def _retention(rq, rk, rv, rg, state, decay_ref, qdec, kdec, gpow):
    rows = rq.shape[0]
    lo = _lane_lo((rows, LANES))
    rkd = rk * kdec
    outs, new_state = [], []
    for m in range(RET_HEADS // 2):
        sl = slice(m * LANES, (m + 1) * LANES)
        upd = state[m] * gpow[:, m:m + 1]
        for p in range(2):
            h = 2 * m + p
            half = lo if p == 0 else ~lo
            rqm = jnp.where(half, rq[:, sl], 0.0)
            rv_h = rv[:, h * RET_DV:(h + 1) * RET_DV]
            s = _dot_nt(rqm, rk[:, sl]) * decay_ref[h]
            ret = _dot(s, rv_h) + _dot(rqm, state[m]) * qdec[:, h * RET_DV:(h + 1) * RET_DV]
            upd = upd + _dot_tn(jnp.where(half, rkd[:, sl], 0.0), rv_h)
            rg_h = rg[:, h * RET_DV:(h + 1) * RET_DV]
            outs.append(rg_h * jax.nn.sigmoid(rg_h) * _rms_rows(ret))
        new_state.append(upd)
    return jnp.concatenate(outs, axis=1), new_state


def _route(h2, wrt_hi_ref, wrt_cat_ref, br_col):
    rows = h2.shape[0]
    h_hi = _bf(h2)
    h_lo = _bf(h2 - h_hi.astype(jnp.float32))
    both = lax.dot_general(wrt_cat_ref[...], h_hi, _NT, preferred_element_type=jnp.float32)
    low = lax.dot_general(wrt_hi_ref[...], h_lo, _NT, preferred_element_type=jnp.float32)
    logits = (both[0:N_EXPERTS] + both[LANES:LANES + N_EXPERTS] + low[0:N_EXPERTS]) + br_col
    sub = lax.broadcasted_iota(jnp.int32, logits.shape, 0)
    row8 = lax.broadcasted_iota(jnp.int32, (SUBLANES, rows), 0)
    idx8 = jnp.zeros((SUBLANES, rows), jnp.int32)
    val8 = jnp.zeros((SUBLANES, rows), jnp.float32)
    onehot = jnp.zeros(logits.shape, jnp.float32)
    vals = []
    work = logits
    for k in range(TOP_K):
        mx = jnp.max(work, axis=0, keepdims=True)
        ix = jnp.min(jnp.where(work == mx, sub, N_EXPERTS), axis=0, keepdims=True)
        pick = sub == ix
        work = jnp.where(pick, NEG_BIG, work)
        onehot = jnp.where(pick, 1.0, onehot)
        idx8 = jnp.where(row8 == k, ix, idx8)
        vals.append(mx)
    exps = [jnp.exp(v - vals[0]) for v in vals]
    total = exps[0] + exps[1] + exps[2] + exps[3]
    for k in range(TOP_K):
        val8 = jnp.where(row8 == k, exps[k] / total, val8)
    fill = jnp.zeros((LANES - SUBLANES, rows), jnp.float32)
    idx_out = jnp.concatenate([idx8.astype(jnp.float32), fill], axis=0).T.astype(jnp.int32)
    val_out = jnp.concatenate([val8, fill], axis=0).T
    tok = lax.broadcasted_iota(jnp.int32, (rows, LANES), 0)
    grp = lax.broadcasted_iota(jnp.int32, (rows, LANES), 1)
    counts = _dot(onehot, jnp.where(tok // COUNT_BLOCK == grp, 1.0, 0.0)).astype(jnp.int32)
    return idx_out, val_out, idx8, counts


def _adaln_kernel(c_ref, w_ref, b_ref, o_ref):
    c = c_ref[...]
    o_ref[...] = _dot(c * jax.nn.sigmoid(c), w_ref[...]) + b_ref[...]


def _adaln(c_all, w_ada, b_ada):
    rows, d = c_all.shape
    n = w_ada.shape[1]
    bn = 1536
    return pl.pallas_call(
        _adaln_kernel,
        grid=(n // bn,),
        in_specs=[pl.BlockSpec((rows, d), lambda j: (0, 0)),
                  pl.BlockSpec((d, bn), lambda j: (0, j)),
                  pl.BlockSpec((1, bn), lambda j: (0, j))],
        out_specs=pl.BlockSpec((rows, bn), lambda j: (0, j)),
        out_shape=jax.ShapeDtypeStruct((rows, n), jnp.float32),
        compiler_params=pltpu.CompilerParams(vmem_limit_bytes=VMEM_LIMIT_BYTES),
    )(c_all, w_ada, b_ada.reshape(1, n))


def _prompt_mix_kernel(sinks_ref, x_ref, mod_ref, gmix_ref, gffn_ref, w_in_ref, qg_ref, kg_ref,
                       bdq_ref, bdk_ref, rotblk_ref, rottab_ref,
                       decay_ref, qdec_ref, kdec_ref, gpow_ref, w_out_ref,
                       wrh_ref, wrl_ref, br_ref,
                       x1_ref, h2_ref, topi_ref, topw_ref, topit_ref, cnt_ref, kwin_ref, vwin_ref, rend_ref,
                       kprev, vprev, state, mix):
    step = pl.program_id(0)
    tb = x_ref.shape[0]

    @pl.when(step == 0)
    def _():
        kprev[...] = jnp.zeros_like(kprev)
        vprev[...] = jnp.zeros_like(vprev)
        state[...] = jnp.zeros_like(state)

    x = x_ref[...]
    h = _pre_norm(x, gmix_ref[...], mod_ref[0:1, :], mod_ref[1:2, :])
    q, k, v, rq, rk, rv, rg = _project(
        h, w_in_ref, qg_ref[...], kg_ref[...], bdq_ref[...], bdk_ref[...],
        *_block_rotary(rotblk_ref, rottab_ref, 0), *_block_rotary(rotblk_ref, rottab_ref, 1))

    qi = lax.broadcasted_iota(jnp.int32, (WINDOW, 2 * WINDOW), 0)
    ci = lax.broadcasted_iota(jnp.int32, (WINDOW, 2 * WINDOW), 1)
    band = (ci > qi) & (ci <= qi + WINDOW)
    qdec, kdec, gpow = qdec_ref[...], kdec_ref[...], gpow_ref[...]

    n_sb = tb // WINDOW
    rows = [slice(sb * WINDOW, (sb + 1) * WINDOW) for sb in range(n_sb)]
    k_prev, v_prev = kprev[...], vprev[...]
    for sb, rs in enumerate(rows):
        k_sb, v_sb = k[rs], v[rs]
        mask = band & ((ci >= WINDOW) | (step > 0)) if sb == 0 else band
        att = _attention(q[rs], jnp.concatenate([k_prev, k_sb], axis=0),
                         jnp.concatenate([v_prev, v_sb], axis=0), mask, sinks_ref)
        mix[rs, 0:ATT_W] = _bf(att)
        k_prev, v_prev = k_sb, v_sb
    kprev[...] = k_prev
    vprev[...] = v_prev

    cur_state = [state[0], state[1]]
    for rs in rows:
        yret, cur_state = _retention(rq[rs], rk[rs], rv[rs], rg[rs], cur_state,
                                     decay_ref, qdec, kdec, gpow)
        mix[rs, ATT_W:ATT_W + RET_W] = _bf(yret)
    state[0] = cur_state[0]
    state[1] = cur_state[1]

    x1 = x + mod_ref[2:3, :] * jnp.dot(mix[...], w_out_ref[...],
                                       preferred_element_type=jnp.float32)
    x1_ref[...] = x1
    h2 = _pre_norm(x1, gffn_ref[...], mod_ref[3:4, :], mod_ref[4:5, :])
    h2_ref[...] = h2
    topi, topw, topi_t, counts = _route(h2, wrh_ref, wrl_ref, br_ref[...])
    topi_ref[...] = topi
    topw_ref[...] = topw
    topit_ref[...] = topi_t
    cnt_ref[0] = counts

    @pl.when(step == pl.num_programs(0) - 1)
    def _():
        kwin_ref[...] = kprev[...]
        vwin_ref[...] = vprev[...]
        rend_ref[...] = state[...]


def _const_spec(shape):
    nd = len(shape)
    return pl.BlockSpec(shape, lambda *_: (0,) * nd)


def _prompt_mix(x, mod, sinks, gmix, gffn, w_in, qg, kg, bdq, bdk, tabs, ret_tabs, w_out,
                wrh, wrl, br):
    t, d = x.shape
    tb = PROMPT_BLOCK
    rot_blk, rot_tab = tabs
    decay, qdec, kdec, gpow = ret_tabs
    row_spec = lambda w: pl.BlockSpec((tb, w), lambda i, *_: (i, 0))
    consts = [mod, gmix, gffn, w_in, qg, kg, bdq, bdk]
    consts2 = [rot_tab, decay, qdec, kdec, gpow, w_out, wrh, wrl, br]
    grid_spec = pltpu.PrefetchScalarGridSpec(
        num_scalar_prefetch=1,
        grid=(t // tb,),
        in_specs=([row_spec(d)] + [_const_spec(a.shape) for a in consts]
                  + [pl.BlockSpec((1,) + rot_blk.shape[1:], lambda i, *_: (i, 0, 0))]
                  + [_const_spec(a.shape) for a in consts2]),
        out_specs=[row_spec(d), row_spec(d), row_spec(LANES), row_spec(LANES),
                   pl.BlockSpec((SUBLANES, tb), lambda i, *_: (0, i)),
                   pl.BlockSpec((1, N_EXPERTS, LANES), lambda i, *_: (i, 0, 0)),
                   _const_spec((WINDOW, KV_W)), _const_spec((WINDOW, KV_W)),
                   _const_spec((2, LANES, RET_DV))],
        scratch_shapes=[pltpu.VMEM((WINDOW, KV_W), jnp.float32),
                        pltpu.VMEM((WINDOW, KV_W), jnp.float32),
                        pltpu.VMEM((2, LANES, RET_DV), jnp.float32),
                        pltpu.VMEM((tb, ATT_W + RET_W), jnp.bfloat16)],
    )
    return pl.pallas_call(
        _prompt_mix_kernel,
        grid_spec=grid_spec,
        out_shape=[jax.ShapeDtypeStruct((t, d), jnp.float32),
                   jax.ShapeDtypeStruct((t, d), jnp.float32),
                   jax.ShapeDtypeStruct((t, LANES), jnp.int32),
                   jax.ShapeDtypeStruct((t, LANES), jnp.float32),
                   jax.ShapeDtypeStruct((SUBLANES, t), jnp.int32),
                   jax.ShapeDtypeStruct((t // tb, N_EXPERTS, LANES), jnp.int32),
                   jax.ShapeDtypeStruct((WINDOW, KV_W), jnp.float32),
                   jax.ShapeDtypeStruct((WINDOW, KV_W), jnp.float32),
                   jax.ShapeDtypeStruct((2, LANES, RET_DV), jnp.float32)],
        compiler_params=pltpu.CompilerParams(
            dimension_semantics=("arbitrary",), vmem_limit_bytes=VMEM_LIMIT_BYTES),
    )(sinks, x, *consts, rot_blk, *consts2)


def _per_seq(stacked, rows, ds, s):
    return jnp.concatenate([stacked[g * rows + s * ds:g * rows + (s + 1) * ds]
                            for g in range(stacked.shape[0] // rows)], axis=0)


def _from_per_seq(parts, ds):
    groups = parts[0].shape[0] // ds
    return jnp.concatenate([p[g * ds:(g + 1) * ds] for g in range(groups) for p in parts], axis=0)


def _sample_attention(q, k_new, v_new, ck_ref, cv_ref, sinks_ref, ds):
    rows = q.shape[0]
    nseq, cache_w = ck_ref.shape[0], ck_ref.shape[1]
    tok_c = lax.broadcasted_iota(jnp.int32, (4 * rows, cache_w), 0) % rows
    col_c = lax.broadcasted_iota(jnp.int32, (4 * rows, cache_w), 1)
    delta = tok_c % ds + cache_w - col_c
    mask_c = (delta >= 0) & (delta < WINDOW)
    tok_n = lax.broadcasted_iota(jnp.int32, (4 * rows, rows), 0) % rows
    col_n = lax.broadcasted_iota(jnp.int32, (4 * rows, rows), 1)
    mask_n = (tok_n // ds == col_n // ds) & (col_n % ds <= tok_n % ds)

    outs = []
    for heads, rolled in ((_NAT_HEADS, False), (_ROL_HEADS, True)):
        arrange = (lambda a: pltpu.roll(a, HEAD_DIM, axis=1)) if rolled else (lambda a: a)
        qst = _stack_heads(q, heads)
        s_new = jnp.where(mask_n, _dot_nt(qst, arrange(k_new)), NEG_BIG)
        s_cache = _from_per_seq(
            [_dot_nt(_per_seq(qst, rows, ds, s), arrange(ck_ref[s])) for s in range(nseq)], ds)
        s_cache = jnp.where(mask_c, s_cache, NEG_BIG)
        sink = _sink_column(sinks_ref, heads, rows)
        m = jnp.maximum(jnp.maximum(jnp.max(s_cache, axis=-1, keepdims=True),
                                    jnp.max(s_new, axis=-1, keepdims=True)), sink)
        p_cache = jnp.exp(s_cache - m)
        p_new = jnp.exp(s_new - m)
        denom = (jnp.sum(p_cache, axis=-1, keepdims=True) + jnp.sum(p_new, axis=-1, keepdims=True)
                 + jnp.exp(sink - m))
        o = _from_per_seq(
            [_dot(_per_seq(p_cache, rows, ds, s), arrange(cv_ref[s])) for s in range(nseq)], ds)
        outs.append((o + _dot(p_new, arrange(v_new))) / denom)
    o_nat, o_rol = outs
    lo = _lane_lo((rows, LANES))
    blk = lambda o, n: o[n * rows:(n + 1) * rows]
    return jnp.concatenate([
        jnp.where(lo, blk(o_nat, 0), blk(o_rol, 0)),
        jnp.where(lo, blk(o_nat, 1), blk(o_rol, 1)),
        jnp.where(lo, blk(o_rol, 2), blk(o_nat, 2)),
        jnp.where(lo, blk(o_rol, 3), blk(o_nat, 3)),
    ], axis=1)


def _sample_retention(rq, rk, rv, rg, st_ref, rend_ref, decay_ref, qdec, kdec, gpow, ds):
    rows = rq.shape[0]
    nseq = st_ref.shape[0]
    lo = _lane_lo((rows, LANES))
    rkd = rk * kdec
    outs = []
    for m in range(RET_HEADS // 2):
        sl = slice(m * LANES, (m + 1) * LANES)
        heads = (2 * m, 2 * m + 1)
        rqm = [jnp.where(lo if p == 0 else ~lo, rq[:, sl], 0.0) for p in range(2)]
        rkdm = [jnp.where(lo if p == 0 else ~lo, rkd[:, sl], 0.0) for p in range(2)]
        rv_h = [rv[:, h * RET_DV:(h + 1) * RET_DV] for h in heads]
        q2 = jnp.concatenate(rqm, axis=0)
        k2 = jnp.concatenate(rkdm, axis=0)
        v2 = jnp.concatenate(rv_h, axis=0)
        inter = _from_per_seq(
            [_dot(_per_seq(q2, rows, ds, s), st_ref[s, m]) for s in range(nseq)], ds)
        for s in range(nseq):
            rend_ref[s, m] = (st_ref[s, m] * gpow[:, m:m + 1]
                              + _dot_tn(_per_seq(k2, rows, ds, s), _per_seq(v2, rows, ds, s)))
        for p, h in enumerate(heads):
            intra = _dot(_dot_nt(rqm[p], rk[:, sl]) * decay_ref[h], rv_h[p])
            ret = intra + inter[p * rows:(p + 1) * rows] * qdec[:, h * RET_DV:(h + 1) * RET_DV]
            rg_h = rg[:, h * RET_DV:(h + 1) * RET_DV]
            outs.append(rg_h * jax.nn.sigmoid(rg_h) * _rms_rows(ret))
    return jnp.concatenate(outs, axis=1)


def _sample_mix_kernel(sinks_ref, x_ref, mod_ref, gmix_ref, gffn_ref, w_in_ref, qg_ref, kg_ref,
                       bdq_ref, bdk_ref, cosa_ref, sina_ref, cosr_ref, sinr_ref,
                       decay_ref, qdec_ref, kdec_ref, gpow_ref, w_out_ref,
                       wrh_ref, wrl_ref, br_ref, ck_ref, cv_ref, st_ref,
                       x1_ref, h2_ref, topi_ref, topw_ref, topit_ref, cnt_ref, kwin_ref, vwin_ref, rend_ref,
                       mix):
    nseq, cache_w = ck_ref.shape[0], ck_ref.shape[1]
    ds = x_ref.shape[0] // nseq
    d = x_ref.shape[1]
    x = x_ref[...]
    mod = lambda i: _repeat_rows(mod_ref[:, i * d:(i + 1) * d], ds)
    h = _pre_norm(x, gmix_ref[...], mod(0), mod(1))
    q, k, v, rq, rk, rv, rg = _project(
        h, w_in_ref, qg_ref[...], kg_ref[...], bdq_ref[...], bdk_ref[...],
        cosa_ref[...], sina_ref[...], cosr_ref[...], sinr_ref[...])

    for s in range(nseq):
        rs = slice(s * ds, (s + 1) * ds)
        kwin_ref[s] = jnp.concatenate([ck_ref[s, ds:, :], k[rs]], axis=0)
        vwin_ref[s] = jnp.concatenate([cv_ref[s, ds:, :], v[rs]], axis=0)
    mix[:, 0:ATT_W] = _bf(_sample_attention(q, k, v, ck_ref, cv_ref, sinks_ref, ds))
    mix[:, ATT_W:ATT_W + RET_W] = _bf(_sample_retention(
        rq, rk, rv, rg, st_ref, rend_ref, decay_ref, qdec_ref[...], kdec_ref[...], gpow_ref[...], ds))

    x1 = x + mod(2) * jnp.dot(mix[...], w_out_ref[...], preferred_element_type=jnp.float32)
    x1_ref[...] = x1
    h2 = _pre_norm(x1, gffn_ref[...], mod(3), mod(4))
    h2_ref[...] = h2
    topi, topw, topi_t, counts = _route(h2, wrh_ref, wrl_ref, br_ref[...])
    topi_ref[...] = topi
    topw_ref[...] = topw
    topit_ref[...] = topi_t
    cnt_ref[0] = counts


def _sample_mix(x, mod_seq, sinks, gmix, gffn, w_in, qg, kg, bdq, bdk, tabs, ret_tabs, w_out,
                wrh, wrl, br, cache_k, cache_v, state):
    t, d = x.shape
    nb, cache_w = cache_k.shape[0], cache_k.shape[1]
    ds = t // nb
    g = SAMPLE_SEQS
    tb = g * ds
    cosa, sina, cosr, sinr = tabs
    decay, qdec, kdec, gpow = ret_tabs
    row_spec = lambda w: pl.BlockSpec((tb, w), lambda i, *_: (i, 0))
    seq3 = lambda a: pl.BlockSpec((g,) + a.shape[1:], lambda i, *_: (i,) + (0,) * (a.ndim - 1))
    consts = [gmix, gffn, w_in, qg, kg, bdq, bdk, cosa, sina, cosr, sinr,
              decay, qdec, kdec, gpow, w_out, wrh, wrl, br]
    grid_spec = pltpu.PrefetchScalarGridSpec(
        num_scalar_prefetch=1,
        grid=(nb // g,),
        in_specs=([row_spec(d), pl.BlockSpec((g, mod_seq.shape[1]), lambda i, *_: (i, 0))]
                  + [_const_spec(a.shape) for a in consts]
                  + [seq3(cache_k), seq3(cache_v), seq3(state)]),
        out_specs=[row_spec(d), row_spec(d), row_spec(LANES), row_spec(LANES),
                   pl.BlockSpec((SUBLANES, tb), lambda i, *_: (0, i)),
                   pl.BlockSpec((1, N_EXPERTS, LANES), lambda i, *_: (i, 0, 0)),
                   seq3(cache_k), seq3(cache_v), seq3(state)],
        scratch_shapes=[pltpu.VMEM((tb, ATT_W + RET_W), jnp.bfloat16)],
    )
    return pl.pallas_call(
        _sample_mix_kernel,
        grid_spec=grid_spec,
        out_shape=[jax.ShapeDtypeStruct((t, d), jnp.float32),
                   jax.ShapeDtypeStruct((t, d), jnp.float32),
                   jax.ShapeDtypeStruct((t, LANES), jnp.int32),
                   jax.ShapeDtypeStruct((t, LANES), jnp.float32),
                   jax.ShapeDtypeStruct((SUBLANES, t), jnp.int32),
                   jax.ShapeDtypeStruct((t // tb, N_EXPERTS, LANES), jnp.int32),
                   jax.ShapeDtypeStruct(cache_k.shape, jnp.float32),
                   jax.ShapeDtypeStruct(cache_v.shape, jnp.float32),
                   jax.ShapeDtypeStruct(state.shape, jnp.float32)],
        compiler_params=pltpu.CompilerParams(
            dimension_semantics=("arbitrary",), vmem_limit_bytes=VMEM_LIMIT_BYTES),
    )(sinks, x, mod_seq, *consts, cache_k, cache_v, state)


def _chunk_copies(cnt8, src_ref, src_start, dst_ref, dst_start, sem):
    for bit in range(CHUNK_BITS - 1, SUBLANES.bit_length() - 2, -1):
        size = 1 << bit
        done = (cnt8 >> (bit + 1)) << (bit + 1)

        @pl.when(((cnt8 >> bit) & 1) == 1)
        def _():
            src = pl.multiple_of(src_start + done, SUBLANES)
            dst = pl.multiple_of(dst_start + done, SUBLANES)
            pltpu.make_async_copy(src_ref.at[pl.ds(src, size)],
                                  dst_ref.at[pl.ds(dst, size)], sem).start()


def _wait_rows(ref, rows8, sem, bits=STAGE_BITS):
    for bit in range(bits - 1, SUBLANES.bit_length() - 2, -1):
        size = 1 << bit

        @pl.when(((rows8 >> bit) & 1) == 1)
        def _():
            pltpu.make_async_copy(ref.at[pl.ds(0, size)], ref.at[pl.ds(0, size)], sem).wait()


def _dispatch_kernel(cnt_ref, gstart_ref, cb_ref, blkrows_ref, ends_ref, padcnt_ref,
                     topit_p_ref, topit_s_ref, h2_p_ref, h2_s_ref, cbcol_ref, xs_ref,
                     stage, zeros, sems, zsem, *, nblk_p):
    step = pl.program_id(0)
    nblk = pl.num_programs(0)
    tb = h2_p_ref.shape[0]
    slot = step % STAGE_SLOTS

    tile = zeros.shape[0]
    first_unused = ends_ref[N_EXPERTS - 1] // tile
    n_tiles = xs_ref.shape[0] // tile

    def tail_fill(t):
        return pltpu.make_async_copy(zeros, xs_ref.at[pl.ds(pl.multiple_of(t * tile, tile), tile)],
                                     zsem)

    @pl.when(step == 0)
    def _():
        zeros[...] = jnp.zeros_like(zeros)

        def pad_fill(e, carry):
            _chunk_copies(padcnt_ref[e], zeros, 0, xs_ref, ends_ref[e] - padcnt_ref[e], zsem)
            return carry

        lax.fori_loop(0, N_EXPERTS, pad_fill, 0)
        lax.fori_loop(first_unused, n_tiles, lambda t, c: (tail_fill(t).start(), c)[1], 0)

    @pl.when(step == nblk - 1)
    def _():
        _wait_rows(xs_ref, padcnt_ref[N_EXPERTS], zsem, bits=(N_EXPERTS * tile).bit_length())
        lax.fori_loop(first_unused, n_tiles, lambda t, c: (tail_fill(t).wait(), c)[1], 0)

    is_p = step < nblk_p
    topit = jnp.where(is_p, topit_p_ref[...], topit_s_ref[...])
    h2 = jnp.where(is_p, h2_p_ref[...], h2_s_ref[...])
    cbcol = cbcol_ref[0]

    sub = lax.broadcasted_iota(jnp.int32, (LANES, tb), 0)
    picks = [sub == topit[k:k + 1, :] for k in range(TOP_K)]
    onehot = jnp.zeros((LANES, tb), jnp.float32)
    for pk in picks:
        onehot = jnp.where(pk, 1.0, onehot)
    r = lax.broadcasted_iota(jnp.int32, (tb, tb), 0)
    c = lax.broadcasted_iota(jnp.int32, (tb, tb), 1)
    earlier = jnp.where(r < c, 1.0, 0.0)
    before = _dot(onehot, earlier)
    base = before + cbcol.astype(jnp.float32)
    row = lax.broadcasted_iota(jnp.int32, (stage.shape[1], tb), 0)
    sel = jnp.zeros((stage.shape[1], tb), jnp.float32)
    for pk in picks:
        col = jnp.sum(jnp.where(pk, base, 0.0), axis=0, keepdims=True).astype(jnp.int32)
        sel = jnp.where(row == col, 1.0, sel)

    stage[slot] = _dot(sel, h2)

    def issue(e, carry):
        i = step * N_EXPERTS + e
        _chunk_copies(cnt_ref[i], stage.at[slot], cb_ref[i], xs_ref, gstart_ref[i], sems.at[slot])
        return carry

    lax.fori_loop(0, N_EXPERTS, issue, 0)

    oldest = step - (STAGE_SLOTS - 1)

    @pl.when(oldest >= 0)
    def _():
        _wait_rows(xs_ref, blkrows_ref[jnp.maximum(oldest, 0)], sems.at[(step + 1) % STAGE_SLOTS])

    @pl.when(step == nblk - 1)
    def _():
        for back in range(STAGE_SLOTS - 2, -1, -1):
            @pl.when(step - back >= 0)
            def _():
                blk = jnp.maximum(step - back, 0)
                _wait_rows(xs_ref, blkrows_ref[blk], sems.at[blk % STAGE_SLOTS])


def _dispatch(tables, topit_p, topit_s, h2_p, h2_s, cbcol, n_rows):
    n_p, d = h2_p.shape
    n_s = h2_s.shape[0]
    tb = ROUTE_BLOCK
    nblk_p, nblk_s = n_p // tb, n_s // tb
    pidx = lambda i, *_: jnp.minimum(i, nblk_p - 1)
    sidx = lambda i, *_: jnp.maximum(i - nblk_p, 0)
    grid_spec = pltpu.PrefetchScalarGridSpec(
        num_scalar_prefetch=6,
        grid=(nblk_p + nblk_s,),
        in_specs=[pl.BlockSpec((SUBLANES, tb), lambda i, *_: (0, pidx(i))),
                  pl.BlockSpec((SUBLANES, tb), lambda i, *_: (0, sidx(i))),
                  pl.BlockSpec((tb, d), lambda i, *_: (pidx(i), 0)),
                  pl.BlockSpec((tb, d), lambda i, *_: (sidx(i), 0)),
                  pl.BlockSpec((1, LANES, 1), lambda i, *_: (i, 0, 0))],
        out_specs=pl.BlockSpec(memory_space=pl.ANY),
        scratch_shapes=[pltpu.VMEM((STAGE_SLOTS, STAGE_ROWS, d), jnp.float32),
                        pltpu.VMEM((EXPERT_TILE, d), jnp.float32),
                        pltpu.SemaphoreType.DMA((STAGE_SLOTS,)),
                        pltpu.SemaphoreType.DMA],
    )
    return pl.pallas_call(
        functools.partial(_dispatch_kernel, nblk_p=nblk_p),
        grid_spec=grid_spec,
        out_shape=jax.ShapeDtypeStruct((n_rows, d), jnp.float32),
        compiler_params=pltpu.CompilerParams(
            dimension_semantics=("arbitrary",), vmem_limit_bytes=VMEM_LIMIT_BYTES),
    )(*tables, topit_p, topit_s, h2_p, h2_s, cbcol)


def _expert_mlp(x, wgu_bf, bgu, wd_bf, bd):
    d_ff = wd_bf.shape[0]
    gu = jnp.dot(_bf(x), wgu_bf[...], preferred_element_type=jnp.float32) + bgu
    glu = jnp.minimum(gu[:, :d_ff], SWIGLU_LIMIT)
    lin = jnp.clip(gu[:, d_ff:], -SWIGLU_LIMIT, SWIGLU_LIMIT)
    act = glu * jax.nn.sigmoid(SWIGLU_ALPHA * glu) * (lin + 1.0)
    return jnp.dot(_bf(act), wd_bf[...], preferred_element_type=jnp.float32) + bd


def _experts_kernel(tile_expert_ref, n_used_ref, groups_ref, next_ref,
                    x_ref, wgu_hbm, bgu_ref, wd_hbm, bd_ref, y_ref,
                    wgu_f32, wd_f32, wgu_bf, wd_bf, sems):
    step = pl.program_id(0)
    groups = groups_ref[step]

    def weight_copies(expert):
        return (pltpu.make_async_copy(wgu_hbm.at[expert], wgu_f32, sems.at[0]),
                pltpu.make_async_copy(wd_hbm.at[expert], wd_f32, sems.at[1]))

    @pl.when(groups > 0)
    def _():
        e = tile_expert_ref[step]
        prev = tile_expert_ref[jnp.maximum(step - 1, 0)]

        @pl.when(step == 0)
        def _():
            for cp in weight_copies(e):
                cp.start()

        @pl.when(jnp.logical_or(step == 0, e != prev))
        def _():
            for cp in weight_copies(e):
                cp.wait()
            wgu_bf[...] = _bf(wgu_f32[...])
            wd_bf[...] = _bf(wd_f32[...])
            nxt = next_ref[e]

            @pl.when(nxt >= 0)
            def _():
                for cp in weight_copies(nxt):
                    cp.start()

    tm = x_ref.shape[0]
    for n in range(1, EXPERT_GROUPS + 1):
        @pl.when(groups == n)
        def _():
            rows = n * (tm // EXPERT_GROUPS)
            y_ref[0:rows, :] = _expert_mlp(x_ref[0:rows, :], wgu_bf, bgu_ref[0], wd_bf, bd_ref[0])
            if rows < tm:
                y_ref[rows:, :] = jnp.zeros((tm - rows, y_ref.shape[1]), y_ref.dtype)

    @pl.when(groups == 0)
    def _():
        y_ref[...] = jnp.zeros_like(y_ref)


def _experts(tile_expert, n_used, tile_groups, next_used, x_sorted, w_gate_up, b_gate_up,
             w_down, b_down):
    p = x_sorted.shape[0]
    n_e, d, two_ff = w_gate_up.shape
    d_ff = two_ff // 2
    tm = EXPERT_TILE
    tile = lambda i, te, nu: jnp.minimum(i, nu[0] - 1)
    grid_spec = pltpu.PrefetchScalarGridSpec(
        num_scalar_prefetch=4,
        grid=(p // tm,),
        in_specs=[pl.BlockSpec((tm, d), lambda i, te, nu, *_: (tile(i, te, nu), 0)),
                  pl.BlockSpec(memory_space=pl.ANY),
                  pl.BlockSpec((1, 1, two_ff), lambda i, te, *_: (te[i], 0, 0)),
                  pl.BlockSpec(memory_space=pl.ANY),
                  pl.BlockSpec((1, 1, d), lambda i, te, *_: (te[i], 0, 0))],
        out_specs=pl.BlockSpec((tm, d), lambda i, *_: (i, 0)),
        scratch_shapes=[pltpu.VMEM((d, two_ff), jnp.float32),
                        pltpu.VMEM((d_ff, d), jnp.float32),
                        pltpu.VMEM((d, two_ff), jnp.bfloat16),
                        pltpu.VMEM((d_ff, d), jnp.bfloat16),
                        pltpu.SemaphoreType.DMA((2,))],
    )
    return pl.pallas_call(
        _experts_kernel,
        grid_spec=grid_spec,
        out_shape=jax.ShapeDtypeStruct(x_sorted.shape, jnp.float32),
        compiler_params=pltpu.CompilerParams(
            dimension_semantics=("arbitrary",), vmem_limit_bytes=VMEM_LIMIT_BYTES),
    )(tile_expert, n_used, tile_groups, next_used, x_sorted, w_gate_up,
      b_gate_up.reshape(n_e, 1, two_ff), w_down, b_down.reshape(n_e, 1, d))


def _combine_kernel(cnt_ref, gstart_ref, cb_ref, blkrows_ref,
                    topi_p_ref, topi_s_ref, topw_p_ref, topw_s_ref, x1_p_ref, x1_s_ref,
                    gate_p_ref, gate_s_ref, cbrow_ref, y_ref, out_p_ref, out_s_ref,
                    ybuf, sems, *, nblk_p):
    step = pl.program_id(0)
    nblk = pl.num_programs(0)
    tb = x1_p_ref.shape[0]
    slot = step % STAGE_SLOTS

    def fetch_chunk(blk, into, e):
        i = blk * N_EXPERTS + e
        _chunk_copies(cnt_ref[i], y_ref, gstart_ref[i], ybuf.at[into], cb_ref[i], sems.at[into])

    def fetch(blk, into):
        lax.fori_loop(0, N_EXPERTS, lambda e, c: (fetch_chunk(blk, into, e), c)[1], 0)

    @pl.when(step == 0)
    def _():
        ybuf[...] = jnp.zeros_like(ybuf)
        for ahead in range(STAGE_SLOTS - 1):
            @pl.when(ahead < nblk)
            def _():
                fetch(ahead, ahead)

    for e in range(N_EXPERTS):
        fetch_chunk(step + STAGE_SLOTS - 1, (step + STAGE_SLOTS - 1) % STAGE_SLOTS, e)

    _wait_rows(y_ref, blkrows_ref[step], sems.at[slot])

    is_p = step < nblk_p
    topi = jnp.where(is_p, topi_p_ref[...], topi_s_ref[...])
    topw = jnp.where(is_p, topw_p_ref[...], topw_s_ref[...])
    cbrow = cbrow_ref[0]

    lane = lax.broadcasted_iota(jnp.int32, (tb, LANES), 1)
    picks = []
    onehot = jnp.zeros((tb, LANES), jnp.float32)
    for k in range(TOP_K):
        ix = jnp.sum(jnp.where(lane == k, topi, 0), axis=-1, keepdims=True)
        picks.append(lane == ix)
        onehot = jnp.where(picks[k], 1.0, onehot)
    r = lax.broadcasted_iota(jnp.int32, (tb, tb), 0)
    c = lax.broadcasted_iota(jnp.int32, (tb, tb), 1)
    earlier = jnp.where(c < r, 1.0, 0.0)
    base = _dot(earlier, onehot) + cbrow.astype(jnp.float32)
    cols = [jnp.sum(jnp.where(picks[k], base, 0.0), axis=-1, keepdims=True).astype(jnp.int32)
            for k in range(TOP_K)]
    wks = [jnp.sum(jnp.where(lane == k, topw, 0.0), axis=-1, keepdims=True) for k in range(TOP_K)]
    moe = None
    for c0 in range(0, ybuf.shape[1], COMBINE_SLAB):
        colid = lax.broadcasted_iota(jnp.int32, (tb, COMBINE_SLAB), 1) + c0
        weights = jnp.zeros((tb, COMBINE_SLAB), jnp.float32)
        for k in range(TOP_K):
            weights = jnp.where(colid == cols[k], wks[k], weights)
        part = _dot(weights, ybuf[slot, c0:c0 + COMBINE_SLAB, :])
        moe = part if moe is None else moe + part

    @pl.when(is_p)
    def _():
        out_p_ref[...] = x1_p_ref[...] + gate_p_ref[...] * moe

    @pl.when(jnp.logical_not(is_p))
    def _():
        gate_s = _repeat_rows(gate_s_ref[...], tb // gate_s_ref.shape[0])
        out_s_ref[...] = x1_s_ref[...] + gate_s * moe


def _combine(tables, topi_p, topi_s, topw_p, topw_s, x1_p, x1_s, gate_p, mod_seq, dec_seq, cbrow,
             y_sorted):
    n_p, d = x1_p.shape
    n_s = x1_s.shape[0]
    tb = ROUTE_BLOCK
    nblk_p, nblk_s = n_p // tb, n_s // tb
    pidx = lambda i, *_: jnp.minimum(i, nblk_p - 1)
    sidx = lambda i, *_: jnp.maximum(i - nblk_p, 0)
    prow = lambda w: pl.BlockSpec((tb, w), lambda i, *_: (pidx(i), 0))
    srow = lambda w: pl.BlockSpec((tb, w), lambda i, *_: (sidx(i), 0))
    grid_spec = pltpu.PrefetchScalarGridSpec(
        num_scalar_prefetch=4,
        grid=(nblk_p + nblk_s,),
        in_specs=[prow(LANES), srow(LANES), prow(LANES), srow(LANES), prow(d), srow(d),
                  pl.BlockSpec((1, d), lambda i, *_: (0, 0)),
                  pl.BlockSpec((tb // dec_seq, d), lambda i, *_: (sidx(i), 5)),
                  pl.BlockSpec((1, 1, LANES), lambda i, *_: (i, 0, 0)),
                  pl.BlockSpec(memory_space=pl.ANY)],
        out_specs=[prow(d), srow(d)],
        scratch_shapes=[pltpu.VMEM((STAGE_SLOTS, STAGE_ROWS, d), jnp.float32),
                        pltpu.SemaphoreType.DMA((STAGE_SLOTS,))],
    )
    return pl.pallas_call(
        functools.partial(_combine_kernel, nblk_p=nblk_p),
        grid_spec=grid_spec,
        out_shape=[jax.ShapeDtypeStruct((n_p, d), jnp.float32),
                   jax.ShapeDtypeStruct((n_s, d), jnp.float32)],
        compiler_params=pltpu.CompilerParams(
            dimension_semantics=("arbitrary",), vmem_limit_bytes=VMEM_LIMIT_BYTES),
    )(*tables, topi_p, topi_s, topw_p, topw_s, x1_p, x1_s, gate_p, mod_seq, cbrow, y_sorted)


def _rotary_tables(pos, inv_freq):
    ang = np.asarray(pos, np.float64)[:, None] * inv_freq[None, :]
    cos, sin = np.cos(ang), np.sin(ang)
    cos_t = np.tile(cos, (1, LANES // cos.shape[1]))
    sin_t = np.tile(np.concatenate([-sin, sin], axis=1), (1, LANES // (2 * sin.shape[1])))
    return cos_t.astype(np.float32), sin_t.astype(np.float32)


def _rotary_split_tables(block, nblk, inv_freq):
    half = inv_freq.shape[0]
    freq = np.tile(inv_freq, LANES // half)
    sign = np.tile(np.concatenate([-np.ones(half), np.ones(half)]), LANES // (2 * half))
    base = (np.arange(nblk) * block).astype(np.float64)[:, None] * freq[None, :]
    off = np.arange(block).astype(np.float64)[:, None] * freq[None, :]
    blk = np.stack([np.cos(base), np.sin(base)], axis=1)
    tab = np.stack([np.cos(off), np.sin(off), np.cos(off) * sign, np.sin(off) * sign])
    return blk.astype(np.float32), tab.astype(np.float32)


def _retention_tables(chunk, nseq=1):
    log_gamma = np.log1p(-np.exp2(-5.0 - np.arange(RET_HEADS, dtype=np.float64)))
    idx = np.arange(chunk, dtype=np.float64)
    diff = idx[:, None] - idx[None, :]
    decay = np.where(diff[None] >= 0,
                     np.exp(np.maximum(diff, 0.0)[None] * log_gamma[:, None, None]), 0.0)
    decay = np.stack([np.kron(np.eye(nseq), decay[h]) for h in range(RET_HEADS)])
    q_decay = np.exp((idx + 1.0)[:, None] * log_gamma[None, :])
    k_decay = np.exp((chunk - 1.0 - idx)[:, None] * log_gamma[None, :])
    qdec = np.tile(np.repeat(q_decay, RET_DV, axis=1), (nseq, 1))
    kdec = np.tile(np.repeat(k_decay, RET_DK, axis=1), (nseq, 1))
    g_chunk = np.exp(chunk * log_gamma)
    gpow = np.repeat(g_chunk.reshape(RET_HEADS // 2, 2), RET_DK, axis=1).T
    return tuple(a.astype(np.float32) for a in (decay, qdec, kdec, gpow))


def _block_diag_mean(width):
    head = np.arange(width) // HEAD_DIM
    return jnp.asarray(np.where(head[:, None] == head[None, :], 1.0 / HEAD_DIM, 0.0), jnp.bfloat16)


def kernel(x_prompt, x_sample, cache_k, cache_v, state_ret, c_prompt, c_sample, w_ada, b_ada,
           g_mix, w_in, q_gain, k_gain, sinks, w_out, g_ffn, w_router, b_router, w_gate_up,
           b_gate_up, w_down, b_down):
    depth = w_ada.shape[0]
    batch, seq, d = x_prompt.shape
    dec_batch, dec_seq, _ = x_sample.shape
    cache_w = cache_k.shape[2]
    assert batch == 1 and depth == 1
    assert seq % PROMPT_BLOCK == 0 and PROMPT_BLOCK % WINDOW == 0 and WINDOW == RET_CHUNK
    assert dec_batch % SAMPLE_SEQS == 0 and cache_w == WINDOW
    n_p, n_s = batch * seq, dec_batch * dec_seq
    n_tok = n_p + n_s
    assert n_p % ROUTE_BLOCK == 0 and n_s % ROUTE_BLOCK == 0
    assert ROUTE_BLOCK % COUNT_BLOCK == 0 and (SAMPLE_SEQS * dec_seq) % COUNT_BLOCK == 0
    assert PROMPT_BLOCK % COUNT_BLOCK == 0
    assert STAGE_ROWS % COMBINE_SLAB == 0 and EXPERT_TILE % (EXPERT_GROUPS * SUBLANES) == 0

    l = 0
    f32 = jnp.float32
    rope_freq = 1.0 / (ROPE_THETA ** (np.arange(0, HEAD_DIM, 2, dtype=np.float64) / HEAD_DIM))
    ret_freq = 1.0 / (ROPE_THETA ** np.linspace(0.0, 1.0, RET_DK // 2))
    pos_s = PAST_LEN + np.arange(dec_seq)
    blk_a, tab_a = _rotary_split_tables(PROMPT_BLOCK, seq // PROMPT_BLOCK, rope_freq)
    blk_r, tab_r = _rotary_split_tables(PROMPT_BLOCK, seq // PROMPT_BLOCK, ret_freq)
    tabs_p = (np.concatenate([blk_a, blk_r], axis=1), np.concatenate([tab_a, tab_r], axis=0))
    tabs_s = tuple(np.tile(a, (SAMPLE_SEQS, 1))
                   for a in _rotary_tables(pos_s, rope_freq) + _rotary_tables(pos_s, ret_freq))
    ret_p = _retention_tables(RET_CHUNK)
    ret_s = _retention_tables(dec_seq, SAMPLE_SEQS)

    n_c = batch + dec_batch
    c_rows = -(-n_c // SUBLANES) * SUBLANES
    c_all = jnp.concatenate([c_sample, c_prompt, jnp.zeros((c_rows - n_c, d), f32)], axis=0)
    mod = _adaln(c_all, w_ada[l], b_ada[l])
    mod_p = mod[dec_batch].reshape(6, d)

    w_in_bf = _bf(w_in[l])
    w_out_bf = _bf(w_out[l])
    qg = jnp.tile(q_gain[l], ATT_HEADS).reshape(1, ATT_W)
    kg = jnp.tile(k_gain[l], ATT_KV_HEADS).reshape(1, KV_W)
    bdq, bdk = _block_diag_mean(ATT_W), _block_diag_mean(KV_W)
    gmix = g_mix[l].reshape(1, d)
    gffn = g_ffn[l].reshape(1, d)
    wr = jnp.pad(w_router[l].T, ((0, LANES - N_EXPERTS), (0, 0)))
    wr_hi = _bf(wr)
    wr_cat = jnp.concatenate([wr_hi, _bf(wr - wr_hi.astype(f32))], axis=0)
    br = b_router[l].reshape(N_EXPERTS, 1)
    shared = (sinks[l], gmix, gffn, w_in_bf, qg, kg, bdq, bdk)

    (x1_p, h2_p, topi_p, topw_p, topit_p, cnt_p, kwin_p, vwin_p, rend_p) = _prompt_mix(
        x_prompt.reshape(n_p, d), mod_p, *shared, tabs_p, ret_p, w_out_bf, wr_hi, wr_cat, br)

    (x1_s, h2_s, topi_s, topw_s, topit_s, cnt_s, kwin_s, vwin_s, rend_s) = _sample_mix(
        x_sample.reshape(n_s, d), mod, *shared, tabs_s, ret_s, w_out_bf, wr_hi, wr_cat, br,
        cache_k[l].reshape(dec_batch, cache_w, KV_W), cache_v[l].reshape(dec_batch, cache_w, KV_W),
        state_ret[l].reshape(dec_batch, 2, LANES, RET_DV))

    tb, tm = ROUTE_BLOCK, EXPERT_TILE
    i32 = jnp.int32
    per_group = lambda c, step_rows: jnp.swapaxes(c[:, :, :step_rows // COUNT_BLOCK], 1, 2).reshape(
        -1, N_EXPERTS)
    cnt = jnp.concatenate([per_group(cnt_p, PROMPT_BLOCK), per_group(cnt_s, SAMPLE_SEQS * dec_seq)])
    cnt = cnt.reshape(n_tok // tb, tb // COUNT_BLOCK, N_EXPERTS).sum(axis=1)
    cnt = jnp.pad(cnt, ((0, 0), (0, LANES - N_EXPERTS)))
    cnt8 = -(-cnt // SUBLANES) * SUBLANES
    rs = jnp.cumsum(cnt8, axis=0) - cnt8
    tot = jnp.sum(cnt8, axis=0)[:N_EXPERTS]
    padded = -(-tot // tm) * tm
    ends = jnp.cumsum(padded)
    gstart = (ends - padded)[None, :] + rs[:, :N_EXPERTS]
    cb = jnp.cumsum(cnt8, axis=1) - cnt8
    blk_rows = jnp.sum(cnt8, axis=1).astype(i32)
    max_rows = n_tok * TOP_K + cnt.shape[0] * N_EXPERTS * (SUBLANES - 1)
    n_rows = -(-max_rows // tm) * tm + N_EXPERTS * tm
    tile_start = jnp.arange(n_rows // tm, dtype=i32) * tm
    tile_expert = jnp.minimum(jnp.sum(ends[None, :] <= tile_start[:, None], axis=1),
                              N_EXPERTS - 1).astype(i32)
    n_used = (ends[-1:] // tm).astype(i32)
    eid = jnp.arange(N_EXPERTS, dtype=i32)
    of_tile = lambda v: jnp.sum(jnp.where(tile_expert[:, None] == eid[None, :], v[None, :], 0), axis=1)
    tile_rows = jnp.clip(of_tile(tot) - (tile_start - of_tile(ends - padded)), 0, tm)
    tile_rows = jnp.where(tile_start < ends[-1], tile_rows, 0)
    tile_groups = (-(-tile_rows // (tm // EXPERT_GROUPS))).astype(i32)
    flat = lambda a: a[:, :N_EXPERTS].reshape(-1).astype(i32)
    tables = (flat(cnt8), gstart.reshape(-1).astype(i32), flat(cb), blk_rows)
    ahead = jnp.zeros(((STAGE_SLOTS - 1) * N_EXPERTS,), i32)
    tables_c = tuple(jnp.concatenate([t, ahead]) for t in tables[:3]) + tables[3:]

    pad_cnt = padded - tot
    pad_cnt = jnp.concatenate([pad_cnt, jnp.sum(pad_cnt, keepdims=True)]).astype(i32)
    x_sorted = _dispatch(tables + (ends.astype(i32), pad_cnt), topit_p, topit_s, h2_p, h2_s,
                         cb.astype(i32)[:, :, None], n_rows)
    used = tot > 0
    later = jnp.where(used[None, :] & (eid[None, :] > eid[:, None]), eid[None, :], N_EXPERTS)
    next_used = jnp.min(later, axis=1)
    next_used = jnp.where(next_used == N_EXPERTS, -1, next_used).astype(i32)
    y_sorted = _experts(tile_expert, n_used, tile_groups, next_used, x_sorted, w_gate_up[l], b_gate_up[l], w_down[l],
                        b_down[l])
    y_p, y_s = _combine(tables_c, topi_p, topi_s, topw_p, topw_s, x1_p, x1_s, mod_p[5:6], mod,
                        dec_seq, cb.astype(i32)[:, None, :], y_sorted)

    kv5 = lambda a, n: a.reshape(1, n, cache_w, ATT_KV_HEADS, HEAD_DIM)
    st5 = lambda a, n: a.reshape(1, n, RET_HEADS, RET_DK, RET_DV)
    return (y_p.reshape(batch, seq, d), y_s.reshape(dec_batch, dec_seq, d),
            kv5(kwin_p, batch), kv5(vwin_p, batch), st5(rend_p, batch),
            kv5(kwin_s, dec_batch), kv5(vwin_s, dec_batch), st5(rend_s, dec_batch))
```

```python
import functools

import jax
import jax.numpy as jnp
import numpy as np
from jax import lax
from jax.experimental import pallas as pl
from jax.experimental.pallas import tpu as pltpu

HEAD_DIM = 64
ATT_HEADS = 8
ATT_KV_HEADS = 2
ATT_W = ATT_HEADS * HEAD_DIM
KV_W = ATT_KV_HEADS * HEAD_DIM
WINDOW = 128
PAST_LEN = 16384
ROPE_THETA = 10000.0
RET_HEADS = 4
RET_DK = 64
RET_DV = 128
RET_QK_W = RET_HEADS * RET_DK
RET_W = RET_HEADS * RET_DV
RET_CHUNK = 128
N_EXPERTS = 32
TOP_K = 4
SWIGLU_LIMIT = 7.0
SWIGLU_ALPHA = 1.702
EPS = 1e-6

_Q0, _K0, _V0 = 0, ATT_W, ATT_W + KV_W
_RQ0 = ATT_W + 2 * KV_W
_RK0 = _RQ0 + RET_QK_W
_RV0 = _RK0 + RET_QK_W
_RG0 = _RV0 + RET_W
IN_W = _RG0 + RET_W

LANES = 128
SUBLANES = 8
VMEM_LIMIT_BYTES = 56 * 1024 * 1024
NEG_BIG = -1e30

PROMPT_BLOCK = 512
SAMPLE_SEQS = 16
COUNT_BLOCK = 128
ROUTE_BLOCK = 512
EXPERT_TILE = 512
EXPERT_GROUPS = 4
CHUNK_BITS = ROUTE_BLOCK.bit_length()
STAGE_ROWS = -(-(TOP_K * ROUTE_BLOCK + N_EXPERTS * (SUBLANES - 1)) // LANES) * LANES
STAGE_BITS = STAGE_ROWS.bit_length()
STAGE_SLOTS = 3
COMBINE_SLAB = 768

_NT = (((1,), (1,)), ((), ()))
_TN = (((0,), (0,)), ((), ()))


def _bf(x):
    return x.astype(jnp.bfloat16)


def _dot(a, b):
    return jnp.dot(_bf(a), _bf(b), preferred_element_type=jnp.float32)


def _dot_nt(a, b):
    return lax.dot_general(_bf(a), _bf(b), _NT, preferred_element_type=jnp.float32)


def _dot_tn(a, b):
    return lax.dot_general(_bf(a), _bf(b), _TN, preferred_element_type=jnp.float32)


def _lane_lo(shape):
    lane = lax.broadcasted_iota(jnp.int32, shape, len(shape) - 1)
    return (lane % LANES) < HEAD_DIM


def _swap_halves(x):
    lane = lax.broadcasted_iota(jnp.int32, x.shape, 1)
    first = (lane % HEAD_DIM) < (HEAD_DIM // 2)
    return jnp.where(first, pltpu.roll(x, LANES - HEAD_DIM // 2, axis=1),
                     pltpu.roll(x, HEAD_DIM // 2, axis=1))


def _rotate(x, cos, sin_signed):
    outs = []
    for j in range(x.shape[1] // LANES):
        xs = x[:, j * LANES:(j + 1) * LANES]
        outs.append(xs * cos + _swap_halves(xs) * sin_signed)
    return outs[0] if len(outs) == 1 else jnp.concatenate(outs, axis=1)


def _repeat_rows(m, n):
    r, w = m.shape
    return jnp.broadcast_to(m[:, None, :], (r, n, w)).reshape(r * n, w)


def _block_rotary(blk_ref, tab_ref, which):
    ca, sa = blk_ref[0, 2 * which:2 * which + 1, :], blk_ref[0, 2 * which + 1:2 * which + 2, :]
    cb, sb, cbs, sbs = (tab_ref[4 * which + n] for n in range(4))
    return ca * cb - sa * sb, sa * cbs + ca * sbs


def _rms_rows(x):
    return x * lax.rsqrt(jnp.mean(x * x, axis=-1, keepdims=True) + EPS)


def _pre_norm(x, gain, shift, scale):
    return _rms_rows(x) * gain * (1.0 + scale) + shift


def _head_norm(x, gain_tiled, blockdiag):
    ms = jnp.dot(_bf(x * x), blockdiag, preferred_element_type=jnp.float32)
    return x * lax.rsqrt(ms + EPS) * gain_tiled


def _project(h, w_in_ref, qg, kg, bd_q, bd_k, cos_a, sin_a, cos_r, sin_r):
    z = jnp.dot(_bf(h), w_in_ref[...], preferred_element_type=jnp.float32)
    q = _rotate(_head_norm(z[:, _Q0:_K0], qg, bd_q), cos_a, sin_a) * (HEAD_DIM ** -0.5)
    k = _rotate(_head_norm(z[:, _K0:_V0], kg, bd_k), cos_a, sin_a)
    v = z[:, _V0:_RQ0]
    rq = _rotate(z[:, _RQ0:_RK0], cos_r, sin_r)
    rk = _rotate(z[:, _RK0:_RV0], cos_r, sin_r) * (RET_DK ** -0.5)
    rv = z[:, _RV0:_RG0]
    rg = z[:, _RG0:IN_W]
    return q, k, v, rq, rk, rv, rg


_NAT_HEADS = (0, 2, 5, 7)
_ROL_HEADS = (1, 3, 4, 6)


def _stack_heads(q, heads):
    lo = _lane_lo((q.shape[0], LANES))
    parts = []
    for j in heads:
        slab = q[:, (j // 2) * LANES:(j // 2 + 1) * LANES]
        parts.append(jnp.where(lo if j % 2 == 0 else ~lo, slab, 0.0))
    return jnp.concatenate(parts, axis=0)


def _sink_column(sinks_ref, heads, rows):
    r = lax.broadcasted_iota(jnp.int32, (len(heads) * rows, 1), 0)
    col = jnp.full((len(heads) * rows, 1), sinks_ref[heads[-1]], jnp.float32)
    for n in range(len(heads) - 2, -1, -1):
        col = jnp.where(r < (n + 1) * rows, sinks_ref[heads[n]], col)
    return col


def _softmax_pv(s, mask, sink, v):
    s = jnp.where(mask, s, NEG_BIG)
    m = jnp.maximum(jnp.max(s, axis=-1, keepdims=True), sink)
    p = jnp.exp(s - m)
    denom = jnp.sum(p, axis=-1, keepdims=True) + jnp.exp(sink - m)
    return _dot(p, v) / denom


def _attention(q, k_all, v_all, mask, sinks_ref):
    rows = q.shape[0]
    k_rol = pltpu.roll(k_all, HEAD_DIM, axis=1)
    v_rol = pltpu.roll(v_all, HEAD_DIM, axis=1)
    mask4 = jnp.concatenate([mask] * 4, axis=0)
    o_nat = _softmax_pv(_dot_nt(_stack_heads(q, _NAT_HEADS), k_all), mask4,
                        _sink_column(sinks_ref, _NAT_HEADS, rows), v_all)
    o_rol = _softmax_pv(_dot_nt(_stack_heads(q, _ROL_HEADS), k_rol), mask4,
                        _sink_column(sinks_ref, _ROL_HEADS, rows), v_rol)
    lo = _lane_lo((rows, LANES))
    blk = lambda o, n: o[n * rows:(n + 1) * rows]
    return jnp.concatenate([
        jnp.where(lo, blk(o_nat, 0), blk(o_rol, 0)),
        jnp.where(lo, blk(o_nat, 1), blk(o_rol, 1)),
        jnp.where(lo, blk(o_rol, 2), blk(o_nat, 2)),
        jnp.where(lo, blk(o_rol, 3), blk(o_nat, 3)),
    ], axis=1)


def _retention(rq, rk, rv, rg, state, decay_ref, qdec, kdec, gpow):
    rows = rq.shape[0]
    lo = _lane_lo((rows, LANES))
    rkd = rk * kdec
    outs, new_state = [], []
    for m in range(RET_HEADS // 2):
        sl = slice(m * LANES, (m + 1) * LANES)
        upd = state[m] * gpow[:, m:m + 1]
        for p in range(2):
            h = 2 * m + p
            half = lo if p == 0 else ~lo
            rqm = jnp.where(half, rq[:, sl], 0.0)
            rv_h = rv[:, h * RET_DV:(h + 1) * RET_DV]
            s = _dot_nt(rqm, rk[:, sl]) * decay_ref[h]
            ret = _dot(s, rv_h) + _dot(rqm, state[m]) * qdec[:, h * RET_DV:(h + 1) * RET_DV]
            upd = upd + _dot_tn(jnp.where(half, rkd[:, sl], 0.0), rv_h)
            rg_h = rg[:, h * RET_DV:(h + 1) * RET_DV]
            outs.append(rg_h * jax.nn.sigmoid(rg_h) * _rms_rows(ret))
        new_state.append(upd)
    return jnp.concatenate(outs, axis=1), new_state


def _route(h2, wrt_hi_ref, wrt_cat_ref, br_col):
    rows = h2.shape[0]
    h_hi = _bf(h2)
    h_lo = _bf(h2 - h_hi.astype(jnp.float32))
    both = lax.dot_general(wrt_cat_ref[...], h_hi, _NT, preferred_element_type=jnp.float32)
    low = lax.dot_general(wrt_hi_ref[...], h_lo, _NT, preferred_element_type=jnp.float32)
    logits = (both[0:N_EXPERTS] + both[LANES:LANES + N_EXPERTS] + low[0:N_EXPERTS]) + br_col
    sub = lax.broadcasted_iota(jnp.int32, logits.shape, 0)
    row8 = lax.broadcasted_iota(jnp.int32, (SUBLANES, rows), 0)
    idx8 = jnp.zeros((SUBLANES, rows), jnp.int32)
    val8 = jnp.zeros((SUBLANES, rows), jnp.float32)
    onehot = jnp.zeros(logits.shape, jnp.float32)
    vals = []
    work = logits
    for k in range(TOP_K):
        mx = jnp.max(work, axis=0, keepdims=True)
        ix = jnp.min(jnp.where(work == mx, sub, N_EXPERTS), axis=0, keepdims=True)
        pick = sub == ix
        work = jnp.where(pick, NEG_BIG, work)
        onehot = jnp.where(pick, 1.0, onehot)
        idx8 = jnp.where(row8 == k, ix, idx8)
        vals.append(mx)
    exps = [jnp.exp(v - vals[0]) for v in vals]
    total = exps[0] + exps[1] + exps[2] + exps[3]
    for k in range(TOP_K):
        val8 = jnp.where(row8 == k, exps[k] / total, val8)
    fill = jnp.zeros((LANES - SUBLANES, rows), jnp.float32)
    idx_out = jnp.concatenate([idx8.astype(jnp.float32), fill], axis=0).T.astype(jnp.int32)
    val_out = jnp.concatenate([val8, fill], axis=0).T
    tok = lax.broadcasted_iota(jnp.int32, (rows, LANES), 0)
    grp = lax.broadcasted_iota(jnp.int32, (rows, LANES), 1)
    counts = _dot(onehot, jnp.where(tok // COUNT_BLOCK == grp, 1.0, 0.0)).astype(jnp.int32)
    return idx_out, val_out, idx8, counts


def _adaln_kernel(c_ref, w_ref, b_ref, o_ref):
    c = c_ref[...]
    o_ref[...] = _dot(c * jax.nn.sigmoid(c), w_ref[...]) + b_ref[...]


def _adaln(c_all, w_ada, b_ada):
    rows, d = c_all.shape
    n = w_ada.shape[1]
    bn = 1536
    return pl.pallas_call(
        _adaln_kernel,
        grid=(n // bn,),
        in_specs=[pl.BlockSpec((rows, d), lambda j: (0, 0)),
                  pl.BlockSpec((d, bn), lambda j: (0, j)),
                  pl.BlockSpec((1, bn), lambda j: (0, j))],
        out_specs=pl.BlockSpec((rows, bn), lambda j: (0, j)),
        out_shape=jax.ShapeDtypeStruct((rows, n), jnp.float32),
        compiler_params=pltpu.CompilerParams(vmem_limit_bytes=VMEM_LIMIT_BYTES),
    )(c_all, w_ada, b_ada.reshape(1, n))


def _prompt_mix_kernel(sinks_ref, x_ref, mod_ref, gmix_ref, gffn_ref, w_in_ref, qg_ref, kg_ref,
                       bdq_ref, bdk_ref, rotblk_ref, rottab_ref,
                       decay_ref, qdec_ref, kdec_ref, gpow_ref, w_out_ref,
                       wrh_ref, wrl_ref, br_ref,
                       x1_ref, h2_ref, topi_ref, topw_ref, topit_ref, cnt_ref, kwin_ref, vwin_ref, rend_ref,
                       kprev, vprev, state, mix):
    step = pl.program_id(0)
    tb = x_ref.shape[0]

    @pl.when(step == 0)
    def _():
        kprev[...] = jnp.zeros_like(kprev)
        vprev[...] = jnp.zeros_like(vprev)
        state[...] = jnp.zeros_like(state)

    x = x_ref[...]
    h = _pre_norm(x, gmix_ref[...], mod_ref[0:1, :], mod_ref[1:2, :])
    q, k, v, rq, rk, rv, rg = _project(
        h, w_in_ref, qg_ref[...], kg_ref[...], bdq_ref[...], bdk_ref[...],
        *_block_rotary(rotblk_ref, rottab_ref, 0), *_block_rotary(rotblk_ref, rottab_ref, 1))

    qi = lax.broadcasted_iota(jnp.int32, (WINDOW, 2 * WINDOW), 0)
    ci = lax.broadcasted_iota(jnp.int32, (WINDOW, 2 * WINDOW), 1)
    band = (ci > qi) & (ci <= qi + WINDOW)
    qdec, kdec, gpow = qdec_ref[...], kdec_ref[...], gpow_ref[...]

    n_sb = tb // WINDOW
    rows = [slice(sb * WINDOW, (sb + 1) * WINDOW) for sb in range(n_sb)]
    k_prev, v_prev = kprev[...], vprev[...]
    for sb, rs in enumerate(rows):
        k_sb, v_sb = k[rs], v[rs]
        mask = band & ((ci >= WINDOW) | (step > 0)) if sb == 0 else band
        att = _attention(q[rs], jnp.concatenate([k_prev, k_sb], axis=0),
                         jnp.concatenate([v_prev, v_sb], axis=0), mask, sinks_ref)
        mix[rs, 0:ATT_W] = _bf(att)
        k_prev, v_prev = k_sb, v_sb
    kprev[...] = k_prev
    vprev[...] = v_prev

    cur_state = [state[0], state[1]]
    for rs in rows:
        yret, cur_state = _retention(rq[rs], rk[rs], rv[rs], rg[rs], cur_state,
                                     decay_ref, qdec, kdec, gpow)
        mix[rs, ATT_W:ATT_W + RET_W] = _bf(yret)
    state[0] = cur_state[0]
    state[1] = cur_state[1]

    x1 = x + mod_ref[2:3, :] * jnp.dot(mix[...], w_out_ref[...],
                                       preferred_element_type=jnp.float32)
    x1_ref[...] = x1
    h2 = _pre_norm(x1, gffn_ref[...], mod_ref[3:4, :], mod_ref[4:5, :])
    h2_ref[...] = h2
    topi, topw, topi_t, counts = _route(h2, wrh_ref, wrl_ref, br_ref[...])
    topi_ref[...] = topi
    topw_ref[...] = topw
    topit_ref[...] = topi_t
    cnt_ref[0] = counts

    @pl.when(step == pl.num_programs(0) - 1)
    def _():
        kwin_ref[...] = kprev[...]
        vwin_ref[...] = vprev[...]
        rend_ref[...] = state[...]


def _const_spec(shape):
    nd = len(shape)
    return pl.BlockSpec(shape, lambda *_: (0,) * nd)


def _prompt_mix(x, mod, sinks, gmix, gffn, w_in, qg, kg, bdq, bdk, tabs, ret_tabs, w_out,
                wrh, wrl, br):
    t, d = x.shape
    tb = PROMPT_BLOCK
    rot_blk, rot_tab = tabs
    decay, qdec, kdec, gpow = ret_tabs
    row_spec = lambda w: pl.BlockSpec((tb, w), lambda i, *_: (i, 0))
    consts = [mod, gmix, gffn, w_in, qg, kg, bdq, bdk]
    consts2 = [rot_tab, decay, qdec, kdec, gpow, w_out, wrh, wrl, br]
    grid_spec = pltpu.PrefetchScalarGridSpec(
        num_scalar_prefetch=1,
        grid=(t // tb,),
        in_specs=([row_spec(d)] + [_const_spec(a.shape) for a in consts]
                  + [pl.BlockSpec((1,) + rot_blk.shape[1:], lambda i, *_: (i, 0, 0))]
                  + [_const_spec(a.shape) for a in consts2]),
        out_specs=[row_spec(d), row_spec(d), row_spec(LANES), row_spec(LANES),
                   pl.BlockSpec((SUBLANES, tb), lambda i, *_: (0, i)),
                   pl.BlockSpec((1, N_EXPERTS, LANES), lambda i, *_: (i, 0, 0)),
                   _const_spec((WINDOW, KV_W)), _const_spec((WINDOW, KV_W)),
                   _const_spec((2, LANES, RET_DV))],
        scratch_shapes=[pltpu.VMEM((WINDOW, KV_W), jnp.float32),
                        pltpu.VMEM((WINDOW, KV_W), jnp.float32),
                        pltpu.VMEM((2, LANES, RET_DV), jnp.float32),
                        pltpu.VMEM((tb, ATT_W + RET_W), jnp.bfloat16)],
    )
    return pl.pallas_call(
        _prompt_mix_kernel,
        grid_spec=grid_spec,
        out_shape=[jax.ShapeDtypeStruct((t, d), jnp.float32),
                   jax.ShapeDtypeStruct((t, d), jnp.float32),
                   jax.ShapeDtypeStruct((t, LANES), jnp.int32),
                   jax.ShapeDtypeStruct((t, LANES), jnp.float32),
                   jax.ShapeDtypeStruct((SUBLANES, t), jnp.int32),
                   jax.ShapeDtypeStruct((t // tb, N_EXPERTS, LANES), jnp.int32),
                   jax.ShapeDtypeStruct((WINDOW, KV_W), jnp.float32),
                   jax.ShapeDtypeStruct((WINDOW, KV_W), jnp.float32),
                   jax.ShapeDtypeStruct((2, LANES, RET_DV), jnp.float32)],
        compiler_params=pltpu.CompilerParams(
            dimension_semantics=("arbitrary",), vmem_limit_bytes=VMEM_LIMIT_BYTES),
    )(sinks, x, *consts, rot_blk, *consts2)


def _per_seq(stacked, rows, ds, s):
    return jnp.concatenate([stacked[g * rows + s * ds:g * rows + (s + 1) * ds]
                            for g in range(stacked.shape[0] // rows)], axis=0)


def _from_per_seq(parts, ds):
    groups = parts[0].shape[0] // ds
    return jnp.concatenate([p[g * ds:(g + 1) * ds] for g in range(groups) for p in parts], axis=0)


def _sample_attention(q, k_new, v_new, ck_ref, cv_ref, sinks_ref, ds):
    rows = q.shape[0]
    nseq, cache_w = ck_ref.shape[0], ck_ref.shape[1]
    tok_c = lax.broadcasted_iota(jnp.int32, (4 * rows, cache_w), 0) % rows
    col_c = lax.broadcasted_iota(jnp.int32, (4 * rows, cache_w), 1)
    delta = tok_c % ds + cache_w - col_c
    mask_c = (delta >= 0) & (delta < WINDOW)
    tok_n = lax.broadcasted_iota(jnp.int32, (4 * rows, rows), 0) % rows
    col_n = lax.broadcasted_iota(jnp.int32, (4 * rows, rows), 1)
    mask_n = (tok_n // ds == col_n // ds) & (col_n % ds <= tok_n % ds)

    outs = []
    for heads, rolled in ((_NAT_HEADS, False), (_ROL_HEADS, True)):
        arrange = (lambda a: pltpu.roll(a, HEAD_DIM, axis=1)) if rolled else (lambda a: a)
        qst = _stack_heads(q, heads)
        s_new = jnp.where(mask_n, _dot_nt(qst, arrange(k_new)), NEG_BIG)
        s_cache = _from_per_seq(
            [_dot_nt(_per_seq(qst, rows, ds, s), arrange(ck_ref[s])) for s in range(nseq)], ds)
        s_cache = jnp.where(mask_c, s_cache, NEG_BIG)
        sink = _sink_column(sinks_ref, heads, rows)
        m = jnp.maximum(jnp.maximum(jnp.max(s_cache, axis=-1, keepdims=True),
                                    jnp.max(s_new, axis=-1, keepdims=True)), sink)
        p_cache = jnp.exp(s_cache - m)
        p_new = jnp.exp(s_new - m)
        denom = (jnp.sum(p_cache, axis=-1, keepdims=True) + jnp.sum(p_new, axis=-1, keepdims=True)
                 + jnp.exp(sink - m))
        o = _from_per_seq(
            [_dot(_per_seq(p_cache, rows, ds, s), arrange(cv_ref[s])) for s in range(nseq)], ds)
        outs.append((o + _dot(p_new, arrange(v_new))) / denom)
    o_nat, o_rol = outs
    lo = _lane_lo((rows, LANES))
    blk = lambda o, n: o[n * rows:(n + 1) * rows]
    return jnp.concatenate([
        jnp.where(lo, blk(o_nat, 0), blk(o_rol, 0)),
        jnp.where(lo, blk(o_nat, 1), blk(o_rol, 1)),
        jnp.where(lo, blk(o_rol, 2), blk(o_nat, 2)),
        jnp.where(lo, blk(o_rol, 3), blk(o_nat, 3)),
    ], axis=1)


def _sample_retention(rq, rk, rv, rg, st_ref, rend_ref, decay_ref, qdec, kdec, gpow, ds):
    rows = rq.shape[0]
    nseq = st_ref.shape[0]
    lo = _lane_lo((rows, LANES))
    rkd = rk * kdec
    outs = []
    for m in range(RET_HEADS // 2):
        sl = slice(m * LANES, (m + 1) * LANES)
        heads = (2 * m, 2 * m + 1)
        rqm = [jnp.where(lo if p == 0 else ~lo, rq[:, sl], 0.0) for p in range(2)]
        rkdm = [jnp.where(lo if p == 0 else ~lo, rkd[:, sl], 0.0) for p in range(2)]
        rv_h = [rv[:, h * RET_DV:(h + 1) * RET_DV] for h in heads]
        q2 = jnp.concatenate(rqm, axis=0)
        k2 = jnp.concatenate(rkdm, axis=0)
        v2 = jnp.concatenate(rv_h, axis=0)
        inter = _from_per_seq(
            [_dot(_per_seq(q2, rows, ds, s), st_ref[s, m]) for s in range(nseq)], ds)
        for s in range(nseq):
            rend_ref[s, m] = (st_ref[s, m] * gpow[:, m:m + 1]
                              + _dot_tn(_per_seq(k2, rows, ds, s), _per_seq(v2, rows, ds, s)))
        for p, h in enumerate(heads):
            intra = _dot(_dot_nt(rqm[p], rk[:, sl]) * decay_ref[h], rv_h[p])
            ret = intra + inter[p * rows:(p + 1) * rows] * qdec[:, h * RET_DV:(h + 1) * RET_DV]
            rg_h = rg[:, h * RET_DV:(h + 1) * RET_DV]
            outs.append(rg_h * jax.nn.sigmoid(rg_h) * _rms_rows(ret))
    return jnp.concatenate(outs, axis=1)


def _sample_mix_kernel(sinks_ref, x_ref, mod_ref, gmix_ref, gffn_ref, w_in_ref, qg_ref, kg_ref,
                       bdq_ref, bdk_ref, cosa_ref, sina_ref, cosr_ref, sinr_ref,
                       decay_ref, qdec_ref, kdec_ref, gpow_ref, w_out_ref,
                       wrh_ref, wrl_ref, br_ref, ck_ref, cv_ref, st_ref,
                       x1_ref, h2_ref, topi_ref, topw_ref, topit_ref, cnt_ref, kwin_ref, vwin_ref, rend_ref,
                       mix):
    nseq, cache_w = ck_ref.shape[0], ck_ref.shape[1]
    ds = x_ref.shape[0] // nseq
    d = x_ref.shape[1]
    x = x_ref[...]
    mod = lambda i: _repeat_rows(mod_ref[:, i * d:(i + 1) * d], ds)
    h = _pre_norm(x, gmix_ref[...], mod(0), mod(1))
    q, k, v, rq, rk, rv, rg = _project(
        h, w_in_ref, qg_ref[...], kg_ref[...], bdq_ref[...], bdk_ref[...],
        cosa_ref[...], sina_ref[...], cosr_ref[...], sinr_ref[...])

    for s in range(nseq):
        rs = slice(s * ds, (s + 1) * ds)
        kwin_ref[s] = jnp.concatenate([ck_ref[s, ds:, :], k[rs]], axis=0)
        vwin_ref[s] = jnp.concatenate([cv_ref[s, ds:, :], v[rs]], axis=0)
    mix[:, 0:ATT_W] = _bf(_sample_attention(q, k, v, ck_ref, cv_ref, sinks_ref, ds))
    mix[:, ATT_W:ATT_W + RET_W] = _bf(_sample_retention(
        rq, rk, rv, rg, st_ref, rend_ref, decay_ref, qdec_ref[...], kdec_ref[...], gpow_ref[...], ds))

    x1 = x + mod(2) * jnp.dot(mix[...], w_out_ref[...], preferred_element_type=jnp.float32)
    x1_ref[...] = x1
    h2 = _pre_norm(x1, gffn_ref[...], mod(3), mod(4))
    h2_ref[...] = h2
    topi, topw, topi_t, counts = _route(h2, wrh_ref, wrl_ref, br_ref[...])
    topi_ref[...] = topi
    topw_ref[...] = topw
    topit_ref[...] = topi_t
    cnt_ref[0] = counts


def _sample_mix(x, mod_seq, sinks, gmix, gffn, w_in, qg, kg, bdq, bdk, tabs, ret_tabs, w_out,
                wrh, wrl, br, cache_k, cache_v, state):
    t, d = x.shape
    nb, cache_w = cache_k.shape[0], cache_k.shape[1]
    ds = t // nb
    g = SAMPLE_SEQS
    tb = g * ds
    cosa, sina, cosr, sinr = tabs
    decay, qdec, kdec, gpow = ret_tabs
    row_spec = lambda w: pl.BlockSpec((tb, w), lambda i, *_: (i, 0))
    seq3 = lambda a: pl.BlockSpec((g,) + a.shape[1:], lambda i, *_: (i,) + (0,) * (a.ndim - 1))
    consts = [gmix, gffn, w_in, qg, kg, bdq, bdk, cosa, sina, cosr, sinr,
              decay, qdec, kdec, gpow, w_out, wrh, wrl, br]
    grid_spec = pltpu.PrefetchScalarGridSpec(
        num_scalar_prefetch=1,
        grid=(nb // g,),
        in_specs=([row_spec(d), pl.BlockSpec((g, mod_seq.shape[1]), lambda i, *_: (i, 0))]
                  + [_const_spec(a.shape) for a in consts]
                  + [seq3(cache_k), seq3(cache_v), seq3(state)]),
        out_specs=[row_spec(d), row_spec(d), row_spec(LANES), row_spec(LANES),
                   pl.BlockSpec((SUBLANES, tb), lambda i, *_: (0, i)),
                   pl.BlockSpec((1, N_EXPERTS, LANES), lambda i, *_: (i, 0, 0)),
                   seq3(cache_k), seq3(cache_v), seq3(state)],
        scratch_shapes=[pltpu.VMEM((tb, ATT_W + RET_W), jnp.bfloat16)],
    )
    return pl.pallas_call(
        _sample_mix_kernel,
        grid_spec=grid_spec,
        out_shape=[jax.ShapeDtypeStruct((t, d), jnp.float32),
                   jax.ShapeDtypeStruct((t, d), jnp.float32),
                   jax.ShapeDtypeStruct((t, LANES), jnp.int32),
                   jax.ShapeDtypeStruct((t, LANES), jnp.float32),
                   jax.ShapeDtypeStruct((SUBLANES, t), jnp.int32),
                   jax.ShapeDtypeStruct((t // tb, N_EXPERTS, LANES), jnp.int32),
                   jax.ShapeDtypeStruct(cache_k.shape, jnp.float32),
                   jax.ShapeDtypeStruct(cache_v.shape, jnp.float32),
                   jax.ShapeDtypeStruct(state.shape, jnp.float32)],
        compiler_params=pltpu.CompilerParams(
            dimension_semantics=("arbitrary",), vmem_limit_bytes=VMEM_LIMIT_BYTES),
    )(sinks, x, mod_seq, *consts, cache_k, cache_v, state)


def _chunk_copies(cnt8, src_ref, src_start, dst_ref, dst_start, sem):
    for bit in range(CHUNK_BITS - 1, SUBLANES.bit_length() - 2, -1):
        size = 1 << bit
        done = (cnt8 >> (bit + 1)) << (bit + 1)

        @pl.when(((cnt8 >> bit) & 1) == 1)
        def _():
            src = pl.multiple_of(src_start + done, SUBLANES)
            dst = pl.multiple_of(dst_start + done, SUBLANES)
            pltpu.make_async_copy(src_ref.at[pl.ds(src, size)],
                                  dst_ref.at[pl.ds(dst, size)], sem).start()


def _wait_rows(ref, rows8, sem, bits=STAGE_BITS):
    for bit in range(bits - 1, SUBLANES.bit_length() - 2, -1):
        size = 1 << bit

        @pl.when(((rows8 >> bit) & 1) == 1)
        def _():
            pltpu.make_async_copy(ref.at[pl.ds(0, size)], ref.at[pl.ds(0, size)], sem).wait()


def _dispatch_kernel(cnt_ref, gstart_ref, cb_ref, blkrows_ref, ends_ref, padcnt_ref,
                     topit_p_ref, topit_s_ref, h2_p_ref, h2_s_ref, cbcol_ref, xs_ref,
                     stage, zeros, sems, zsem, *, nblk_p):
    step = pl.program_id(0)
    nblk = pl.num_programs(0)
    tb = h2_p_ref.shape[0]
    slot = step % STAGE_SLOTS

    tile = zeros.shape[0]
    first_unused = ends_ref[N_EXPERTS - 1] // tile
    n_tiles = xs_ref.shape[0] // tile

    def tail_fill(t):
        return pltpu.make_async_copy(zeros, xs_ref.at[pl.ds(pl.multiple_of(t * tile, tile), tile)],
                                     zsem)

    @pl.when(step == 0)
    def _():
        zeros[...] = jnp.zeros_like(zeros)

        def pad_fill(e, carry):
            _chunk_copies(padcnt_ref[e], zeros, 0, xs_ref, ends_ref[e] - padcnt_ref[e], zsem)
            return carry

        lax.fori_loop(0, N_EXPERTS, pad_fill, 0)
        lax.fori_loop(first_unused, n_tiles, lambda t, c: (tail_fill(t).start(), c)[1], 0)

    @pl.when(step == nblk - 1)
    def _():
        _wait_rows(xs_ref, padcnt_ref[N_EXPERTS], zsem, bits=(N_EXPERTS * tile).bit_length())
        lax.fori_loop(first_unused, n_tiles, lambda t, c: (tail_fill(t).wait(), c)[1], 0)

    def issue_chunk(blk, e):
        i = (blk + 1) * N_EXPERTS + e
        _chunk_copies(cnt_ref[i], stage.at[blk % STAGE_SLOTS], cb_ref[i], xs_ref, gstart_ref[i],
                      sems.at[blk % STAGE_SLOTS])

    for e in range(N_EXPERTS):
        issue_chunk(step - 1, e)

    is_p = step < nblk_p
    topit = jnp.where(is_p, topit_p_ref[...], topit_s_ref[...])
    h2 = jnp.where(is_p, h2_p_ref[...], h2_s_ref[...])
    cbcol = cbcol_ref[0]

    sub = lax.broadcasted_iota(jnp.int32, (LANES, tb), 0)
    picks = [sub == topit[k:k + 1, :] for k in range(TOP_K)]
    onehot = jnp.zeros((LANES, tb), jnp.float32)
    for pk in picks:
        onehot = jnp.where(pk, 1.0, onehot)
    r = lax.broadcasted_iota(jnp.int32, (tb, tb), 0)
    c = lax.broadcasted_iota(jnp.int32, (tb, tb), 1)
    earlier = jnp.where(r < c, 1.0, 0.0)
    before = _dot(onehot, earlier)
    base = before + cbcol.astype(jnp.float32)
    row = lax.broadcasted_iota(jnp.int32, (stage.shape[1], tb), 0)
    sel = jnp.zeros((stage.shape[1], tb), jnp.float32)
    for pk in picks:
        col = jnp.sum(jnp.where(pk, base, 0.0), axis=0, keepdims=True).astype(jnp.int32)
        sel = jnp.where(row == col, 1.0, sel)

    stage[slot] = _dot(sel, h2)

    @pl.when(step == nblk - 1)
    def _():
        lax.fori_loop(0, N_EXPERTS, lambda e, c: (issue_chunk(step, e), c)[1], 0)

    oldest = step - (STAGE_SLOTS - 1)

    @pl.when(oldest >= 0)
    def _():
        _wait_rows(xs_ref, blkrows_ref[jnp.maximum(oldest, 0)], sems.at[(step + 1) % STAGE_SLOTS])

    @pl.when(step == nblk - 1)
    def _():
        for back in range(STAGE_SLOTS - 2, -1, -1):
            @pl.when(step - back >= 0)
            def _():
                blk = jnp.maximum(step - back, 0)
                _wait_rows(xs_ref, blkrows_ref[blk], sems.at[blk % STAGE_SLOTS])


def _dispatch(tables, topit_p, topit_s, h2_p, h2_s, cbcol, n_rows):
    n_p, d = h2_p.shape
    n_s = h2_s.shape[0]
    tb = ROUTE_BLOCK
    nblk_p, nblk_s = n_p // tb, n_s // tb
    pidx = lambda i, *_: jnp.minimum(i, nblk_p - 1)
    sidx = lambda i, *_: jnp.maximum(i - nblk_p, 0)
    grid_spec = pltpu.PrefetchScalarGridSpec(
        num_scalar_prefetch=6,
        grid=(nblk_p + nblk_s,),
        in_specs=[pl.BlockSpec((SUBLANES, tb), lambda i, *_: (0, pidx(i))),
                  pl.BlockSpec((SUBLANES, tb), lambda i, *_: (0, sidx(i))),
                  pl.BlockSpec((tb, d), lambda i, *_: (pidx(i), 0)),
                  pl.BlockSpec((tb, d), lambda i, *_: (sidx(i), 0)),
                  pl.BlockSpec((1, LANES, 1), lambda i, *_: (i, 0, 0))],
        out_specs=pl.BlockSpec(memory_space=pl.ANY),
        scratch_shapes=[pltpu.VMEM((STAGE_SLOTS, STAGE_ROWS, d), jnp.float32),
                        pltpu.VMEM((EXPERT_TILE, d), jnp.float32),
                        pltpu.SemaphoreType.DMA((STAGE_SLOTS,)),
                        pltpu.SemaphoreType.DMA],
    )
    return pl.pallas_call(
        functools.partial(_dispatch_kernel, nblk_p=nblk_p),
        grid_spec=grid_spec,
        out_shape=jax.ShapeDtypeStruct((n_rows, d), jnp.float32),
        compiler_params=pltpu.CompilerParams(
            dimension_semantics=("arbitrary",), vmem_limit_bytes=VMEM_LIMIT_BYTES),
    )(*tables, topit_p, topit_s, h2_p, h2_s, cbcol)


def _expert_mlp(x, wgu_bf, bgu, wd_bf, bd):
    d_ff = wd_bf.shape[0]
    gu = jnp.dot(_bf(x), wgu_bf[...], preferred_element_type=jnp.float32) + bgu
    glu = jnp.minimum(gu[:, :d_ff], SWIGLU_LIMIT)
    lin = jnp.clip(gu[:, d_ff:], -SWIGLU_LIMIT, SWIGLU_LIMIT)
    act = glu * jax.nn.sigmoid(SWIGLU_ALPHA * glu) * (lin + 1.0)
    return jnp.dot(_bf(act), wd_bf[...], preferred_element_type=jnp.float32) + bd


def _experts_kernel(tile_expert_ref, n_used_ref, groups_ref, next_ref,
                    x_ref, wgu_hbm, bgu_ref, wd_hbm, bd_ref, y_ref,
                    wgu_f32, wd_f32, wgu_bf, wd_bf, sems):
    step = pl.program_id(0)
    groups = groups_ref[step]

    def weight_copies(expert):
        return (pltpu.make_async_copy(wgu_hbm.at[expert], wgu_f32, sems.at[0]),
                pltpu.make_async_copy(wd_hbm.at[expert], wd_f32, sems.at[1]))

    @pl.when(groups > 0)
    def _():
        e = tile_expert_ref[step]
        prev = tile_expert_ref[jnp.maximum(step - 1, 0)]

        @pl.when(step == 0)
        def _():
            for cp in weight_copies(e):
                cp.start()

        @pl.when(jnp.logical_or(step == 0, e != prev))
        def _():
            for cp in weight_copies(e):
                cp.wait()
            wgu_bf[...] = _bf(wgu_f32[...])
            wd_bf[...] = _bf(wd_f32[...])
            nxt = next_ref[e]

            @pl.when(nxt >= 0)
            def _():
                for cp in weight_copies(nxt):
                    cp.start()

    tm = x_ref.shape[0]
    for n in range(1, EXPERT_GROUPS + 1):
        @pl.when(groups == n)
        def _():
            rows = n * (tm // EXPERT_GROUPS)
            y_ref[0:rows, :] = _expert_mlp(x_ref[0:rows, :], wgu_bf, bgu_ref[0], wd_bf, bd_ref[0])
            if rows < tm:
                y_ref[rows:, :] = jnp.zeros((tm - rows, y_ref.shape[1]), y_ref.dtype)

    @pl.when(groups == 0)
    def _():
        y_ref[...] = jnp.zeros_like(y_ref)


def _experts(tile_expert, n_used, tile_groups, next_used, x_sorted, w_gate_up, b_gate_up,
             w_down, b_down):
    p = x_sorted.shape[0]
    n_e, d, two_ff = w_gate_up.shape
    d_ff = two_ff // 2
    tm = EXPERT_TILE
    tile = lambda i, te, nu: jnp.minimum(i, nu[0] - 1)
    grid_spec = pltpu.PrefetchScalarGridSpec(
        num_scalar_prefetch=4,
        grid=(p // tm,),
        in_specs=[pl.BlockSpec((tm, d), lambda i, te, nu, *_: (tile(i, te, nu), 0)),
                  pl.BlockSpec(memory_space=pl.ANY),
                  pl.BlockSpec((1, 1, two_ff), lambda i, te, *_: (te[i], 0, 0)),
                  pl.BlockSpec(memory_space=pl.ANY),
                  pl.BlockSpec((1, 1, d), lambda i, te, *_: (te[i], 0, 0))],
        out_specs=pl.BlockSpec((tm, d), lambda i, *_: (i, 0)),
        scratch_shapes=[pltpu.VMEM((d, two_ff), jnp.float32),
                        pltpu.VMEM((d_ff, d), jnp.float32),
                        pltpu.VMEM((d, two_ff), jnp.bfloat16),
                        pltpu.VMEM((d_ff, d), jnp.bfloat16),
                        pltpu.SemaphoreType.DMA((2,))],
    )
    return pl.pallas_call(
        _experts_kernel,
        grid_spec=grid_spec,
        out_shape=jax.ShapeDtypeStruct(x_sorted.shape, jnp.float32),
        compiler_params=pltpu.CompilerParams(
            dimension_semantics=("arbitrary",), vmem_limit_bytes=VMEM_LIMIT_BYTES),
    )(tile_expert, n_used, tile_groups, next_used, x_sorted, w_gate_up,
      b_gate_up.reshape(n_e, 1, two_ff), w_down, b_down.reshape(n_e, 1, d))


def _combine_kernel(cnt_ref, gstart_ref, cb_ref, blkrows_ref,
                    topi_p_ref, topi_s_ref, topw_p_ref, topw_s_ref, x1_p_ref, x1_s_ref,
                    gate_p_ref, gate_s_ref, cbrow_ref, y_ref, out_p_ref, out_s_ref,
                    ybuf, sems, *, nblk_p):
    step = pl.program_id(0)
    nblk = pl.num_programs(0)
    tb = x1_p_ref.shape[0]
    slot = step % STAGE_SLOTS

    def fetch_chunk(blk, into, e):
        i = blk * N_EXPERTS + e
        _chunk_copies(cnt_ref[i], y_ref, gstart_ref[i], ybuf.at[into], cb_ref[i], sems.at[into])

    def fetch(blk, into):
        lax.fori_loop(0, N_EXPERTS, lambda e, c: (fetch_chunk(blk, into, e), c)[1], 0)

    @pl.when(step == 0)
    def _():
        ybuf[...] = jnp.zeros_like(ybuf)
        for ahead in range(STAGE_SLOTS - 1):
            @pl.when(ahead < nblk)
            def _():
                fetch(ahead, ahead)

    for e in range(N_EXPERTS):
        fetch_chunk(step + STAGE_SLOTS - 1, (step + STAGE_SLOTS - 1) % STAGE_SLOTS, e)

    _wait_rows(y_ref, blkrows_ref[step], sems.at[slot])

    is_p = step < nblk_p
    topi = jnp.where(is_p, topi_p_ref[...], topi_s_ref[...])
    topw = jnp.where(is_p, topw_p_ref[...], topw_s_ref[...])
    cbrow = cbrow_ref[0]

    lane = lax.broadcasted_iota(jnp.int32, (tb, LANES), 1)
    picks = []
    onehot = jnp.zeros((tb, LANES), jnp.float32)
    for k in range(TOP_K):
        ix = jnp.sum(jnp.where(lane == k, topi, 0), axis=-1, keepdims=True)
        picks.append(lane == ix)
        onehot = jnp.where(picks[k], 1.0, onehot)
    r = lax.broadcasted_iota(jnp.int32, (tb, tb), 0)
    c = lax.broadcasted_iota(jnp.int32, (tb, tb), 1)
    earlier = jnp.where(c < r, 1.0, 0.0)
    base = _dot(earlier, onehot) + cbrow.astype(jnp.float32)
    cols = [jnp.sum(jnp.where(picks[k], base, 0.0), axis=-1, keepdims=True).astype(jnp.int32)
            for k in range(TOP_K)]
    wks = [jnp.sum(jnp.where(lane == k, topw, 0.0), axis=-1, keepdims=True) for k in range(TOP_K)]
    moe = None
    for c0 in range(0, ybuf.shape[1], COMBINE_SLAB):
        colid = lax.broadcasted_iota(jnp.int32, (tb, COMBINE_SLAB), 1) + c0
        weights = jnp.zeros((tb, COMBINE_SLAB), jnp.float32)
        for k in range(TOP_K):
            weights = jnp.where(colid == cols[k], wks[k], weights)
        part = _dot(weights, ybuf[slot, c0:c0 + COMBINE_SLAB, :])
        moe = part if moe is None else moe + part

    @pl.when(is_p)
    def _():
        out_p_ref[...] = x1_p_ref[...] + gate_p_ref[...] * moe

    @pl.when(jnp.logical_not(is_p))
    def _():
        gate_s = _repeat_rows(gate_s_ref[...], tb // gate_s_ref.shape[0])
        out_s_ref[...] = x1_s_ref[...] + gate_s * moe


def _combine(tables, topi_p, topi_s, topw_p, topw_s, x1_p, x1_s, gate_p, mod_seq, dec_seq, cbrow,
             y_sorted):
    n_p, d = x1_p.shape
    n_s = x1_s.shape[0]
    tb = ROUTE_BLOCK
    nblk_p, nblk_s = n_p // tb, n_s // tb
    pidx = lambda i, *_: jnp.minimum(i, nblk_p - 1)
    sidx = lambda i, *_: jnp.maximum(i - nblk_p, 0)
    prow = lambda w: pl.BlockSpec((tb, w), lambda i, *_: (pidx(i), 0))
    srow = lambda w: pl.BlockSpec((tb, w), lambda i, *_: (sidx(i), 0))
    grid_spec = pltpu.PrefetchScalarGridSpec(
        num_scalar_prefetch=4,
        grid=(nblk_p + nblk_s,),
        in_specs=[prow(LANES), srow(LANES), prow(LANES), srow(LANES), prow(d), srow(d),
                  pl.BlockSpec((1, d), lambda i, *_: (0, 0)),
                  pl.BlockSpec((tb // dec_seq, d), lambda i, *_: (sidx(i), 5)),
                  pl.BlockSpec((1, 1, LANES), lambda i, *_: (i, 0, 0)),
                  pl.BlockSpec(memory_space=pl.ANY)],
        out_specs=[prow(d), srow(d)],
        scratch_shapes=[pltpu.VMEM((STAGE_SLOTS, STAGE_ROWS, d), jnp.float32),
                        pltpu.SemaphoreType.DMA((STAGE_SLOTS,))],
    )
    return pl.pallas_call(
        functools.partial(_combine_kernel, nblk_p=nblk_p),
        grid_spec=grid_spec,
        out_shape=[jax.ShapeDtypeStruct((n_p, d), jnp.float32),
                   jax.ShapeDtypeStruct((n_s, d), jnp.float32)],
        compiler_params=pltpu.CompilerParams(
            dimension_semantics=("arbitrary",), vmem_limit_bytes=VMEM_LIMIT_BYTES),
    )(*tables, topi_p, topi_s, topw_p, topw_s, x1_p, x1_s, gate_p, mod_seq, cbrow, y_sorted)


def _rotary_tables(pos, inv_freq):
    ang = np.asarray(pos, np.float64)[:, None] * inv_freq[None, :]
    cos, sin = np.cos(ang), np.sin(ang)
    cos_t = np.tile(cos, (1, LANES // cos.shape[1]))
    sin_t = np.tile(np.concatenate([-sin, sin], axis=1), (1, LANES // (2 * sin.shape[1])))
    return cos_t.astype(np.float32), sin_t.astype(np.float32)


def _rotary_split_tables(block, nblk, inv_freq):
    half = inv_freq.shape[0]
    freq = np.tile(inv_freq, LANES // half)
    sign = np.tile(np.concatenate([-np.ones(half), np.ones(half)]), LANES // (2 * half))
    base = (np.arange(nblk) * block).astype(np.float64)[:, None] * freq[None, :]
    off = np.arange(block).astype(np.float64)[:, None] * freq[None, :]
    blk = np.stack([np.cos(base), np.sin(base)], axis=1)
    tab = np.stack([np.cos(off), np.sin(off), np.cos(off) * sign, np.sin(off) * sign])
    return blk.astype(np.float32), tab.astype(np.float32)


def _retention_tables(chunk, nseq=1):
    log_gamma = np.log1p(-np.exp2(-5.0 - np.arange(RET_HEADS, dtype=np.float64)))
    idx = np.arange(chunk, dtype=np.float64)
    diff = idx[:, None] - idx[None, :]
    decay = np.where(diff[None] >= 0,
                     np.exp(np.maximum(diff, 0.0)[None] * log_gamma[:, None, None]), 0.0)
    decay = np.stack([np.kron(np.eye(nseq), decay[h]) for h in range(RET_HEADS)])
    q_decay = np.exp((idx + 1.0)[:, None] * log_gamma[None, :])
    k_decay = np.exp((chunk - 1.0 - idx)[:, None] * log_gamma[None, :])
    qdec = np.tile(np.repeat(q_decay, RET_DV, axis=1), (nseq, 1))
    kdec = np.tile(np.repeat(k_decay, RET_DK, axis=1), (nseq, 1))
    g_chunk = np.exp(chunk * log_gamma)
    gpow = np.repeat(g_chunk.reshape(RET_HEADS // 2, 2), RET_DK, axis=1).T
    return tuple(a.astype(np.float32) for a in (decay, qdec, kdec, gpow))


def _block_diag_mean(width):
    head = np.arange(width) // HEAD_DIM
    return jnp.asarray(np.where(head[:, None] == head[None, :], 1.0 / HEAD_DIM, 0.0), jnp.bfloat16)


def kernel(x_prompt, x_sample, cache_k, cache_v, state_ret, c_prompt, c_sample, w_ada, b_ada,
           g_mix, w_in, q_gain, k_gain, sinks, w_out, g_ffn, w_router, b_router, w_gate_up,
           b_gate_up, w_down, b_down):
    depth = w_ada.shape[0]
    batch, seq, d = x_prompt.shape
    dec_batch, dec_seq, _ = x_sample.shape
    cache_w = cache_k.shape[2]
    assert batch == 1 and depth == 1
    assert seq % PROMPT_BLOCK == 0 and PROMPT_BLOCK % WINDOW == 0 and WINDOW == RET_CHUNK
    assert dec_batch % SAMPLE_SEQS == 0 and cache_w == WINDOW
    n_p, n_s = batch * seq, dec_batch * dec_seq
    n_tok = n_p + n_s
    assert n_p % ROUTE_BLOCK == 0 and n_s % ROUTE_BLOCK == 0
    assert ROUTE_BLOCK % COUNT_BLOCK == 0 and (SAMPLE_SEQS * dec_seq) % COUNT_BLOCK == 0
    assert PROMPT_BLOCK % COUNT_BLOCK == 0
    assert STAGE_ROWS % COMBINE_SLAB == 0 and EXPERT_TILE % (EXPERT_GROUPS * SUBLANES) == 0

    l = 0
    f32 = jnp.float32
    rope_freq = 1.0 / (ROPE_THETA ** (np.arange(0, HEAD_DIM, 2, dtype=np.float64) / HEAD_DIM))
    ret_freq = 1.0 / (ROPE_THETA ** np.linspace(0.0, 1.0, RET_DK // 2))
    pos_s = PAST_LEN + np.arange(dec_seq)
    blk_a, tab_a = _rotary_split_tables(PROMPT_BLOCK, seq // PROMPT_BLOCK, rope_freq)
    blk_r, tab_r = _rotary_split_tables(PROMPT_BLOCK, seq // PROMPT_BLOCK, ret_freq)
    tabs_p = (np.concatenate([blk_a, blk_r], axis=1), np.concatenate([tab_a, tab_r], axis=0))
    tabs_s = tuple(np.tile(a, (SAMPLE_SEQS, 1))
                   for a in _rotary_tables(pos_s, rope_freq) + _rotary_tables(pos_s, ret_freq))
    ret_p = _retention_tables(RET_CHUNK)
    ret_s = _retention_tables(dec_seq, SAMPLE_SEQS)

    n_c = batch + dec_batch
    c_rows = -(-n_c // SUBLANES) * SUBLANES
    c_all = jnp.concatenate([c_sample, c_prompt, jnp.zeros((c_rows - n_c, d), f32)], axis=0)
    mod = _adaln(c_all, w_ada[l], b_ada[l])
    mod_p = mod[dec_batch].reshape(6, d)

    w_in_bf = _bf(w_in[l])
    w_out_bf = _bf(w_out[l])
    qg = jnp.tile(q_gain[l], ATT_HEADS).reshape(1, ATT_W)
    kg = jnp.tile(k_gain[l], ATT_KV_HEADS).reshape(1, KV_W)
    bdq, bdk = _block_diag_mean(ATT_W), _block_diag_mean(KV_W)
    gmix = g_mix[l].reshape(1, d)
    gffn = g_ffn[l].reshape(1, d)
    wr = jnp.pad(w_router[l].T, ((0, LANES - N_EXPERTS), (0, 0)))
    wr_hi = _bf(wr)
    wr_cat = jnp.concatenate([wr_hi, _bf(wr - wr_hi.astype(f32))], axis=0)
    br = b_router[l].reshape(N_EXPERTS, 1)
    shared = (sinks[l], gmix, gffn, w_in_bf, qg, kg, bdq, bdk)

    (x1_p, h2_p, topi_p, topw_p, topit_p, cnt_p, kwin_p, vwin_p, rend_p) = _prompt_mix(
        x_prompt.reshape(n_p, d), mod_p, *shared, tabs_p, ret_p, w_out_bf, wr_hi, wr_cat, br)

    (x1_s, h2_s, topi_s, topw_s, topit_s, cnt_s, kwin_s, vwin_s, rend_s) = _sample_mix(
        x_sample.reshape(n_s, d), mod, *shared, tabs_s, ret_s, w_out_bf, wr_hi, wr_cat, br,
        cache_k[l].reshape(dec_batch, cache_w, KV_W), cache_v[l].reshape(dec_batch, cache_w, KV_W),
        state_ret[l].reshape(dec_batch, 2, LANES, RET_DV))

    tb, tm = ROUTE_BLOCK, EXPERT_TILE
    i32 = jnp.int32
    per_group = lambda c, step_rows: jnp.swapaxes(c[:, :, :step_rows // COUNT_BLOCK], 1, 2).reshape(
        -1, N_EXPERTS)
    cnt = jnp.concatenate([per_group(cnt_p, PROMPT_BLOCK), per_group(cnt_s, SAMPLE_SEQS * dec_seq)])
    cnt = cnt.reshape(n_tok // tb, tb // COUNT_BLOCK, N_EXPERTS).sum(axis=1)
    cnt = jnp.pad(cnt, ((0, 0), (0, LANES - N_EXPERTS)))
    cnt8 = -(-cnt // SUBLANES) * SUBLANES
    rs = jnp.cumsum(cnt8, axis=0) - cnt8
    tot = jnp.sum(cnt8, axis=0)[:N_EXPERTS]
    padded = -(-tot // tm) * tm
    ends = jnp.cumsum(padded)
    gstart = (ends - padded)[None, :] + rs[:, :N_EXPERTS]
    cb = jnp.cumsum(cnt8, axis=1) - cnt8
    blk_rows = jnp.sum(cnt8, axis=1).astype(i32)
    max_rows = n_tok * TOP_K + cnt.shape[0] * N_EXPERTS * (SUBLANES - 1)
    n_rows = -(-max_rows // tm) * tm + N_EXPERTS * tm
    tile_start = jnp.arange(n_rows // tm, dtype=i32) * tm
    tile_expert = jnp.minimum(jnp.sum(ends[None, :] <= tile_start[:, None], axis=1),
                              N_EXPERTS - 1).astype(i32)
    n_used = (ends[-1:] // tm).astype(i32)
    eid = jnp.arange(N_EXPERTS, dtype=i32)
    of_tile = lambda v: jnp.sum(jnp.where(tile_expert[:, None] == eid[None, :], v[None, :], 0), axis=1)
    tile_rows = jnp.clip(of_tile(tot) - (tile_start - of_tile(ends - padded)), 0, tm)
    tile_rows = jnp.where(tile_start < ends[-1], tile_rows, 0)
    tile_groups = (-(-tile_rows // (tm // EXPERT_GROUPS))).astype(i32)
    flat = lambda a: a[:, :N_EXPERTS].reshape(-1).astype(i32)
    tables = (flat(cnt8), gstart.reshape(-1).astype(i32), flat(cb), blk_rows)
    ahead = jnp.zeros(((STAGE_SLOTS - 1) * N_EXPERTS,), i32)
    tables_c = tuple(jnp.concatenate([t, ahead]) for t in tables[:3]) + tables[3:]

    pad_cnt = padded - tot
    pad_cnt = jnp.concatenate([pad_cnt, jnp.sum(pad_cnt, keepdims=True)]).astype(i32)
    behind = jnp.zeros((N_EXPERTS,), i32)
    tables_d = tuple(jnp.concatenate([behind, t]) for t in tables[:3]) + tables[3:]
    x_sorted = _dispatch(tables_d + (ends.astype(i32), pad_cnt), topit_p, topit_s, h2_p, h2_s,
                         cb.astype(i32)[:, :, None], n_rows)
    used = tot > 0
    later = jnp.where(used[None, :] & (eid[None, :] > eid[:, None]), eid[None, :], N_EXPERTS)
    next_used = jnp.min(later, axis=1)
    next_used = jnp.where(next_used == N_EXPERTS, -1, next_used).astype(i32)
    y_sorted = _experts(tile_expert, n_used, tile_groups, next_used, x_sorted, w_gate_up[l], b_gate_up[l], w_down[l],
                        b_down[l])
    y_p, y_s = _combine(tables_c, topi_p, topi_s, topw_p, topw_s, x1_p, x1_s, mod_p[5:6], mod,
                        dec_seq, cb.astype(i32)[:, None, :], y_sorted)

    kv5 = lambda a, n: a.reshape(1, n, cache_w, ATT_KV_HEADS, HEAD_DIM)
    st5 = lambda a, n: a.reshape(1, n, RET_HEADS, RET_DK, RET_DV)
    return (y_p.reshape(batch, seq, d), y_s.reshape(dec_batch, dec_seq, d),
            kv5(kwin_p, batch), kv5(vwin_p, batch), st5(rend_p, batch),
            kv5(kwin_s, dec_batch), kv5(vwin_s, dec_batch), st5(rend_s, dec_batch))
```

```python
import functools

import jax
import jax.numpy as jnp
import numpy as np
from jax import lax
from jax.experimental import pallas as pl
from jax.experimental.pallas import tpu as pltpu

HEAD_DIM = 64
ATT_HEADS = 8
ATT_KV_HEADS = 2
ATT_W = ATT_HEADS * HEAD_DIM
KV_W = ATT_KV_HEADS * HEAD_DIM
WINDOW = 128
PAST_LEN = 16384
ROPE_THETA = 10000.0
RET_HEADS = 4
RET_DK = 64
RET_DV = 128
RET_QK_W = RET_HEADS * RET_DK
RET_W = RET_HEADS * RET_DV
RET_CHUNK = 128
N_EXPERTS = 32
TOP_K = 4
SWIGLU_LIMIT = 7.0
SWIGLU_ALPHA = 1.702
EPS = 1e-6

_Q0, _K0, _V0 = 0, ATT_W, ATT_W + KV_W
_RQ0 = ATT_W + 2 * KV_W
_RK0 = _RQ0 + RET_QK_W
_RV0 = _RK0 + RET_QK_W
_RG0 = _RV0 + RET_W
IN_W = _RG0 + RET_W

LANES = 128
SUBLANES = 8
VMEM_LIMIT_BYTES = 56 * 1024 * 1024
NEG_BIG = -1e30

PROMPT_BLOCK = 512
SAMPLE_SEQS = 16
COUNT_BLOCK = 128
ROUTE_BLOCK = 512
EXPERT_TILE = 512
EXPERT_GROUPS = 4
CHUNK_BITS = ROUTE_BLOCK.bit_length()
STAGE_ROWS = -(-(TOP_K * ROUTE_BLOCK + N_EXPERTS * (SUBLANES - 1)) // LANES) * LANES
STAGE_BITS = STAGE_ROWS.bit_length()
STAGE_SLOTS = 3
COMBINE_SLAB = 768

_NT = (((1,), (1,)), ((), ()))
_TN = (((0,), (0,)), ((), ()))


def _bf(x):
    return x.astype(jnp.bfloat16)


def _dot(a, b):
    return jnp.dot(_bf(a), _bf(b), preferred_element_type=jnp.float32)


def _dot_nt(a, b):
    return lax.dot_general(_bf(a), _bf(b), _NT, preferred_element_type=jnp.float32)


def _dot_tn(a, b):
    return lax.dot_general(_bf(a), _bf(b), _TN, preferred_element_type=jnp.float32)


def _lane_lo(shape):
    lane = lax.broadcasted_iota(jnp.int32, shape, len(shape) - 1)
    return (lane % LANES) < HEAD_DIM


def _swap_halves(x):
    lane = lax.broadcasted_iota(jnp.int32, x.shape, 1)
    first = (lane % HEAD_DIM) < (HEAD_DIM // 2)
    return jnp.where(first, pltpu.roll(x, LANES - HEAD_DIM // 2, axis=1),
                     pltpu.roll(x, HEAD_DIM // 2, axis=1))


def _rotate(x, cos, sin_signed):
    outs = []
    for j in range(x.shape[1] // LANES):
        xs = x[:, j * LANES:(j + 1) * LANES]
        outs.append(xs * cos + _swap_halves(xs) * sin_signed)
    return outs[0] if len(outs) == 1 else jnp.concatenate(outs, axis=1)


def _repeat_rows(m, n):
    r, w = m.shape
    return jnp.broadcast_to(m[:, None, :], (r, n, w)).reshape(r * n, w)


def _block_rotary(blk_ref, tab_ref, which):
    ca, sa = blk_ref[0, 2 * which:2 * which + 1, :], blk_ref[0, 2 * which + 1:2 * which + 2, :]
    cb, sb, cbs, sbs = (tab_ref[4 * which + n] for n in range(4))
    return ca * cb - sa * sb, sa * cbs + ca * sbs


def _rms_rows(x):
    return x * lax.rsqrt(jnp.mean(x * x, axis=-1, keepdims=True) + EPS)


def _pre_norm(x, gain, shift, scale):
    return _rms_rows(x) * gain * (1.0 + scale) + shift


def _head_norm(x, gain_tiled, blockdiag):
    ms = jnp.dot(_bf(x * x), blockdiag, preferred_element_type=jnp.float32)
    return x * lax.rsqrt(ms + EPS) * gain_tiled


def _project(h, w_in_ref, qg, kg, bd_q, bd_k, cos_a, sin_a, cos_r, sin_r):
    z = jnp.dot(_bf(h), w_in_ref[...], preferred_element_type=jnp.float32)
    q = _rotate(_head_norm(z[:, _Q0:_K0], qg, bd_q), cos_a, sin_a) * (HEAD_DIM ** -0.5)
    k = _rotate(_head_norm(z[:, _K0:_V0], kg, bd_k), cos_a, sin_a)
    v = z[:, _V0:_RQ0]
    rq = _rotate(z[:, _RQ0:_RK0], cos_r, sin_r)
    rk = _rotate(z[:, _RK0:_RV0], cos_r, sin_r) * (RET_DK ** -0.5)
    rv = z[:, _RV0:_RG0]
    rg = z[:, _RG0:IN_W]
    return q, k, v, rq, rk, rv, rg


_NAT_HEADS = (0, 2, 5, 7)
_ROL_HEADS = (1, 3, 4, 6)


def _stack_heads(q, heads):
    lo = _lane_lo((q.shape[0], LANES))
    parts = []
    for j in heads:
        slab = q[:, (j // 2) * LANES:(j // 2 + 1) * LANES]
        parts.append(jnp.where(lo if j % 2 == 0 else ~lo, slab, 0.0))
    return jnp.concatenate(parts, axis=0)


def _sink_column(sinks_ref, heads, rows):
    r = lax.broadcasted_iota(jnp.int32, (len(heads) * rows, 1), 0)
    col = jnp.full((len(heads) * rows, 1), sinks_ref[heads[-1]], jnp.float32)
    for n in range(len(heads) - 2, -1, -1):
        col = jnp.where(r < (n + 1) * rows, sinks_ref[heads[n]], col)
    return col


def _softmax_pv(s, mask, sink, v):
    s = jnp.where(mask, s, NEG_BIG)
    m = jnp.maximum(jnp.max(s, axis=-1, keepdims=True), sink)
    p = jnp.exp(s - m)
    denom = jnp.sum(p, axis=-1, keepdims=True) + jnp.exp(sink - m)
    return _dot(p, v) / denom


def _attention(q, k_all, v_all, mask, sinks_ref):
    rows = q.shape[0]
    k_rol = pltpu.roll(k_all, HEAD_DIM, axis=1)
    v_rol = pltpu.roll(v_all, HEAD_DIM, axis=1)
    mask4 = jnp.concatenate([mask] * 4, axis=0)
    o_nat = _softmax_pv(_dot_nt(_stack_heads(q, _NAT_HEADS), k_all), mask4,
                        _sink_column(sinks_ref, _NAT_HEADS, rows), v_all)
    o_rol = _softmax_pv(_dot_nt(_stack_heads(q, _ROL_HEADS), k_rol), mask4,
                        _sink_column(sinks_ref, _ROL_HEADS, rows), v_rol)
    lo = _lane_lo((rows, LANES))
    blk = lambda o, n: o[n * rows:(n + 1) * rows]
    return jnp.concatenate([
        jnp.where(lo, blk(o_nat, 0), blk(o_rol, 0)),
        jnp.where(lo, blk(o_nat, 1), blk(o_rol, 1)),
        jnp.where(lo, blk(o_rol, 2), blk(o_nat, 2)),
        jnp.where(lo, blk(o_rol, 3), blk(o_nat, 3)),
    ], axis=1)


def _retention(rq, rk, rv, rg, state, decay_ref, qdec, kdec, gpow):
    rows = rq.shape[0]
    lo = _lane_lo((rows, LANES))
    rkd = rk * kdec
    outs, new_state = [], []
    for m in range(RET_HEADS // 2):
        sl = slice(m * LANES, (m + 1) * LANES)
        upd = state[m] * gpow[:, m:m + 1]
        for p in range(2):
            h = 2 * m + p
            half = lo if p == 0 else ~lo
            rqm = jnp.where(half, rq[:, sl], 0.0)
            rv_h = rv[:, h * RET_DV:(h + 1) * RET_DV]
            s = _dot_nt(rqm, rk[:, sl]) * decay_ref[h]
            ret = _dot(s, rv_h) + _dot(rqm, state[m]) * qdec[:, h * RET_DV:(h + 1) * RET_DV]
            upd = upd + _dot_tn(jnp.where(half, rkd[:, sl], 0.0), rv_h)
            rg_h = rg[:, h * RET_DV:(h + 1) * RET_DV]
            outs.append(rg_h * jax.nn.sigmoid(rg_h) * _rms_rows(ret))
        new_state.append(upd)
    return jnp.concatenate(outs, axis=1), new_state


def _route(h2, wrt_hi_ref, wrt_cat_ref, br_col):
    rows = h2.shape[0]
    h_hi = _bf(h2)
    h_lo = _bf(h2 - h_hi.astype(jnp.float32))
    both = lax.dot_general(wrt_cat_ref[...], h_hi, _NT, preferred_element_type=jnp.float32)
    low = lax.dot_general(wrt_hi_ref[...], h_lo, _NT, preferred_element_type=jnp.float32)
    logits = (both[0:N_EXPERTS] + both[LANES:LANES + N_EXPERTS] + low[0:N_EXPERTS]) + br_col
    sub = lax.broadcasted_iota(jnp.int32, logits.shape, 0)
    row8 = lax.broadcasted_iota(jnp.int32, (SUBLANES, rows), 0)
    idx8 = jnp.zeros((SUBLANES, rows), jnp.int32)
    val8 = jnp.zeros((SUBLANES, rows), jnp.float32)
    onehot = jnp.zeros(logits.shape, jnp.float32)
    vals = []
    work = logits
    for k in range(TOP_K):
        mx = jnp.max(work, axis=0, keepdims=True)
        ix = jnp.min(jnp.where(work == mx, sub, N_EXPERTS), axis=0, keepdims=True)
        pick = sub == ix
        work = jnp.where(pick, NEG_BIG, work)
        onehot = jnp.where(pick, 1.0, onehot)
        idx8 = jnp.where(row8 == k, ix, idx8)
        vals.append(mx)
    exps = [jnp.exp(v - vals[0]) for v in vals]
    total = exps[0] + exps[1] + exps[2] + exps[3]
    for k in range(TOP_K):
        val8 = jnp.where(row8 == k, exps[k] / total, val8)
    fill = jnp.zeros((LANES - SUBLANES, rows), jnp.float32)
    idx_out = jnp.concatenate([idx8.astype(jnp.float32), fill], axis=0).T.astype(jnp.int32)
    val_out = jnp.concatenate([val8, fill], axis=0).T
    tok = lax.broadcasted_iota(jnp.int32, (rows, LANES), 0)
    grp = lax.broadcasted_iota(jnp.int32, (rows, LANES), 1)
    counts = _dot(onehot, jnp.where(tok // COUNT_BLOCK == grp, 1.0, 0.0)).astype(jnp.int32)
    return idx_out, val_out, idx8, counts


def _adaln_kernel(c_ref, w_ref, b_ref, o_ref):
    c = c_ref[...]
    o_ref[...] = _dot(c * jax.nn.sigmoid(c), w_ref[...]) + b_ref[...]


def _adaln(c_all, w_ada, b_ada):
    rows, d = c_all.shape
    n = w_ada.shape[1]
    bn = 1536
    return pl.pallas_call(
        _adaln_kernel,
        grid=(n // bn,),
        in_specs=[pl.BlockSpec((rows, d), lambda j: (0, 0)),
                  pl.BlockSpec((d, bn), lambda j: (0, j)),
                  pl.BlockSpec((1, bn), lambda j: (0, j))],
        out_specs=pl.BlockSpec((rows, bn), lambda j: (0, j)),
        out_shape=jax.ShapeDtypeStruct((rows, n), jnp.float32),
        compiler_params=pltpu.CompilerParams(vmem_limit_bytes=VMEM_LIMIT_BYTES),
    )(c_all, w_ada, b_ada.reshape(1, n))


def _prompt_mix_kernel(sinks_ref, x_ref, mod_ref, gmix_ref, gffn_ref, w_in_ref, qg_ref, kg_ref,
                       bdq_ref, bdk_ref, rotblk_ref, rottab_ref,
                       decay_ref, qdec_ref, kdec_ref, gpow_ref, w_out_ref,
                       wrh_ref, wrl_ref, br_ref,
                       x1_ref, h2_ref, topi_ref, topw_ref, topit_ref, cnt_ref, kwin_ref, vwin_ref, rend_ref,
                       kprev, vprev, state, mix):
    step = pl.program_id(0)
    tb = x_ref.shape[0]

    @pl.when(step == 0)
    def _():
        kprev[...] = jnp.zeros_like(kprev)
        vprev[...] = jnp.zeros_like(vprev)
        state[...] = jnp.zeros_like(state)

    x = x_ref[...]
    h = _pre_norm(x, gmix_ref[...], mod_ref[0:1, :], mod_ref[1:2, :])
    q, k, v, rq, rk, rv, rg = _project(
        h, w_in_ref, qg_ref[...], kg_ref[...], bdq_ref[...], bdk_ref[...],
        *_block_rotary(rotblk_ref, rottab_ref, 0), *_block_rotary(rotblk_ref, rottab_ref, 1))

    qi = lax.broadcasted_iota(jnp.int32, (WINDOW, 2 * WINDOW), 0)
    ci = lax.broadcasted_iota(jnp.int32, (WINDOW, 2 * WINDOW), 1)
    band = (ci > qi) & (ci <= qi + WINDOW)
    qdec, kdec, gpow = qdec_ref[...], kdec_ref[...], gpow_ref[...]

    n_sb = tb // WINDOW
    rows = [slice(sb * WINDOW, (sb + 1) * WINDOW) for sb in range(n_sb)]
    k_prev, v_prev = kprev[...], vprev[...]
    for sb, rs in enumerate(rows):
        k_sb, v_sb = k[rs], v[rs]
        mask = band & ((ci >= WINDOW) | (step > 0)) if sb == 0 else band
        att = _attention(q[rs], jnp.concatenate([k_prev, k_sb], axis=0),
                         jnp.concatenate([v_prev, v_sb], axis=0), mask, sinks_ref)
        mix[rs, 0:ATT_W] = _bf(att)
        k_prev, v_prev = k_sb, v_sb
    kprev[...] = k_prev
    vprev[...] = v_prev

    cur_state = [state[0], state[1]]
    for rs in rows:
        yret, cur_state = _retention(rq[rs], rk[rs], rv[rs], rg[rs], cur_state,
                                     decay_ref, qdec, kdec, gpow)
        mix[rs, ATT_W:ATT_W + RET_W] = _bf(yret)
    state[0] = cur_state[0]
    state[1] = cur_state[1]

    x1 = x + mod_ref[2:3, :] * jnp.dot(mix[...], w_out_ref[...],
                                       preferred_element_type=jnp.float32)
    x1_ref[...] = x1
    h2 = _pre_norm(x1, gffn_ref[...], mod_ref[3:4, :], mod_ref[4:5, :])
    h2_ref[...] = h2
    topi, topw, topi_t, counts = _route(h2, wrh_ref, wrl_ref, br_ref[...])
    topi_ref[...] = topi
    topw_ref[...] = topw
    topit_ref[...] = topi_t
    cnt_ref[0] = counts

    @pl.when(step == pl.num_programs(0) - 1)
    def _():
        kwin_ref[...] = kprev[...]
        vwin_ref[...] = vprev[...]
        rend_ref[...] = state[...]


def _const_spec(shape):
    nd = len(shape)
    return pl.BlockSpec(shape, lambda *_: (0,) * nd)


def _prompt_mix(x, mod, sinks, gmix, gffn, w_in, qg, kg, bdq, bdk, tabs, ret_tabs, w_out,
                wrh, wrl, br):
    t, d = x.shape
    tb = PROMPT_BLOCK
    rot_blk, rot_tab = tabs
    decay, qdec, kdec, gpow = ret_tabs
    row_spec = lambda w: pl.BlockSpec((tb, w), lambda i, *_: (i, 0))
    consts = [mod, gmix, gffn, w_in, qg, kg, bdq, bdk]
    consts2 = [rot_tab, decay, qdec, kdec, gpow, w_out, wrh, wrl, br]
    grid_spec = pltpu.PrefetchScalarGridSpec(
        num_scalar_prefetch=1,
        grid=(t // tb,),
        in_specs=([row_spec(d)] + [_const_spec(a.shape) for a in consts]
                  + [pl.BlockSpec((1,) + rot_blk.shape[1:], lambda i, *_: (i, 0, 0))]
                  + [_const_spec(a.shape) for a in consts2]),
        out_specs=[row_spec(d), row_spec(d), row_spec(LANES), row_spec(LANES),
                   pl.BlockSpec((SUBLANES, tb), lambda i, *_: (0, i)),
                   pl.BlockSpec((1, N_EXPERTS, LANES), lambda i, *_: (i, 0, 0)),
                   _const_spec((WINDOW, KV_W)), _const_spec((WINDOW, KV_W)),
                   _const_spec((2, LANES, RET_DV))],
        scratch_shapes=[pltpu.VMEM((WINDOW, KV_W), jnp.float32),
                        pltpu.VMEM((WINDOW, KV_W), jnp.float32),
                        pltpu.VMEM((2, LANES, RET_DV), jnp.float32),
                        pltpu.VMEM((tb, ATT_W + RET_W), jnp.bfloat16)],
    )
    return pl.pallas_call(
        _prompt_mix_kernel,
        grid_spec=grid_spec,
        out_shape=[jax.ShapeDtypeStruct((t, d), jnp.float32),
                   jax.ShapeDtypeStruct((t, d), jnp.float32),
                   jax.ShapeDtypeStruct((t, LANES), jnp.int32),
                   jax.ShapeDtypeStruct((t, LANES), jnp.float32),
                   jax.ShapeDtypeStruct((SUBLANES, t), jnp.int32),
                   jax.ShapeDtypeStruct((t // tb, N_EXPERTS, LANES), jnp.int32),
                   jax.ShapeDtypeStruct((WINDOW, KV_W), jnp.float32),
                   jax.ShapeDtypeStruct((WINDOW, KV_W), jnp.float32),
                   jax.ShapeDtypeStruct((2, LANES, RET_DV), jnp.float32)],
        compiler_params=pltpu.CompilerParams(
            dimension_semantics=("arbitrary",), vmem_limit_bytes=VMEM_LIMIT_BYTES),
    )(sinks, x, *consts, rot_blk, *consts2)


def _per_seq(stacked, rows, ds, s):
    return jnp.concatenate([stacked[g * rows + s * ds:g * rows + (s + 1) * ds]
                            for g in range(stacked.shape[0] // rows)], axis=0)


def _from_per_seq(parts, ds):
    groups = parts[0].shape[0] // ds
    return jnp.concatenate([p[g * ds:(g + 1) * ds] for g in range(groups) for p in parts], axis=0)


def _sample_attention(q, k_new, v_new, ck_ref, cv_ref, sinks_ref, ds):
    rows = q.shape[0]
    nseq, cache_w = ck_ref.shape[0], ck_ref.shape[1]
    tok_c = lax.broadcasted_iota(jnp.int32, (4 * rows, cache_w), 0) % rows
    col_c = lax.broadcasted_iota(jnp.int32, (4 * rows, cache_w), 1)
    delta = tok_c % ds + cache_w - col_c
    mask_c = (delta >= 0) & (delta < WINDOW)
    tok_n = lax.broadcasted_iota(jnp.int32, (4 * rows, rows), 0) % rows
    col_n = lax.broadcasted_iota(jnp.int32, (4 * rows, rows), 1)
    mask_n = (tok_n // ds == col_n // ds) & (col_n % ds <= tok_n % ds)

    outs = []
    for heads, rolled in ((_NAT_HEADS, False), (_ROL_HEADS, True)):
        arrange = (lambda a: pltpu.roll(a, HEAD_DIM, axis=1)) if rolled else (lambda a: a)
        qst = _stack_heads(q, heads)
        s_new = jnp.where(mask_n, _dot_nt(qst, arrange(k_new)), NEG_BIG)
        s_cache = _from_per_seq(
            [_dot_nt(_per_seq(qst, rows, ds, s), arrange(ck_ref[s])) for s in range(nseq)], ds)
        s_cache = jnp.where(mask_c, s_cache, NEG_BIG)
        sink = _sink_column(sinks_ref, heads, rows)
        m = jnp.maximum(jnp.maximum(jnp.max(s_cache, axis=-1, keepdims=True),
                                    jnp.max(s_new, axis=-1, keepdims=True)), sink)
        p_cache = jnp.exp(s_cache - m)
        p_new = jnp.exp(s_new - m)
        denom = (jnp.sum(p_cache, axis=-1, keepdims=True) + jnp.sum(p_new, axis=-1, keepdims=True)
                 + jnp.exp(sink - m))
        o = _from_per_seq(
            [_dot(_per_seq(p_cache, rows, ds, s), arrange(cv_ref[s])) for s in range(nseq)], ds)
        outs.append((o + _dot(p_new, arrange(v_new))) / denom)
    o_nat, o_rol = outs
    lo = _lane_lo((rows, LANES))
    blk = lambda o, n: o[n * rows:(n + 1) * rows]
    return jnp.concatenate([
        jnp.where(lo, blk(o_nat, 0), blk(o_rol, 0)),
        jnp.where(lo, blk(o_nat, 1), blk(o_rol, 1)),
        jnp.where(lo, blk(o_rol, 2), blk(o_nat, 2)),
        jnp.where(lo, blk(o_rol, 3), blk(o_nat, 3)),
    ], axis=1)


def _sample_retention(rq, rk, rv, rg, st_ref, rend_ref, decay_ref, qdec, kdec, gpow, ds):
    rows = rq.shape[0]
    nseq = st_ref.shape[0]
    lo = _lane_lo((rows, LANES))
    rkd = rk * kdec
    outs = []
    for m in range(RET_HEADS // 2):
        sl = slice(m * LANES, (m + 1) * LANES)
        heads = (2 * m, 2 * m + 1)
        rqm = [jnp.where(lo if p == 0 else ~lo, rq[:, sl], 0.0) for p in range(2)]
        rkdm = [jnp.where(lo if p == 0 else ~lo, rkd[:, sl], 0.0) for p in range(2)]
        rv_h = [rv[:, h * RET_DV:(h + 1) * RET_DV] for h in heads]
        q2 = jnp.concatenate(rqm, axis=0)
        k2 = jnp.concatenate(rkdm, axis=0)
        v2 = jnp.concatenate(rv_h, axis=0)
        inter = _from_per_seq(
            [_dot(_per_seq(q2, rows, ds, s), st_ref[s, m]) for s in range(nseq)], ds)
        for s in range(nseq):
            rend_ref[s, m] = (st_ref[s, m] * gpow[:, m:m + 1]
                              + _dot_tn(_per_seq(k2, rows, ds, s), _per_seq(v2, rows, ds, s)))
        for p, h in enumerate(heads):
            intra = _dot(_dot_nt(rqm[p], rk[:, sl]) * decay_ref[h], rv_h[p])
            ret = intra + inter[p * rows:(p + 1) * rows] * qdec[:, h * RET_DV:(h + 1) * RET_DV]
            rg_h = rg[:, h * RET_DV:(h + 1) * RET_DV]
            outs.append(rg_h * jax.nn.sigmoid(rg_h) * _rms_rows(ret))
    return jnp.concatenate(outs, axis=1)


def _sample_mix_kernel(sinks_ref, x_ref, mod_ref, gmix_ref, gffn_ref, w_in_ref, qg_ref, kg_ref,
                       bdq_ref, bdk_ref, cosa_ref, sina_ref, cosr_ref, sinr_ref,
                       decay_ref, qdec_ref, kdec_ref, gpow_ref, w_out_ref,
                       wrh_ref, wrl_ref, br_ref, ck_ref, cv_ref, st_ref,
                       x1_ref, h2_ref, topi_ref, topw_ref, topit_ref, cnt_ref, kwin_ref, vwin_ref, rend_ref,
                       mix):
    nseq, cache_w = ck_ref.shape[0], ck_ref.shape[1]
    ds = x_ref.shape[0] // nseq
    d = x_ref.shape[1]
    x = x_ref[...]
    mod = lambda i: _repeat_rows(mod_ref[:, i * d:(i + 1) * d], ds)
    h = _pre_norm(x, gmix_ref[...], mod(0), mod(1))
    q, k, v, rq, rk, rv, rg = _project(
        h, w_in_ref, qg_ref[...], kg_ref[...], bdq_ref[...], bdk_ref[...],
        cosa_ref[...], sina_ref[...], cosr_ref[...], sinr_ref[...])

    for s in range(nseq):
        rs = slice(s * ds, (s + 1) * ds)
        kwin_ref[s] = jnp.concatenate([ck_ref[s, ds:, :], k[rs]], axis=0)
        vwin_ref[s] = jnp.concatenate([cv_ref[s, ds:, :], v[rs]], axis=0)
    mix[:, 0:ATT_W] = _bf(_sample_attention(q, k, v, ck_ref, cv_ref, sinks_ref, ds))
    mix[:, ATT_W:ATT_W + RET_W] = _bf(_sample_retention(
        rq, rk, rv, rg, st_ref, rend_ref, decay_ref, qdec_ref[...], kdec_ref[...], gpow_ref[...], ds))

    x1 = x + mod(2) * jnp.dot(mix[...], w_out_ref[...], preferred_element_type=jnp.float32)
    x1_ref[...] = x1
    h2 = _pre_norm(x1, gffn_ref[...], mod(3), mod(4))
    h2_ref[...] = h2
    topi, topw, topi_t, counts = _route(h2, wrh_ref, wrl_ref, br_ref[...])
    topi_ref[...] = topi
    topw_ref[...] = topw
    topit_ref[...] = topi_t
    cnt_ref[0] = counts


def _sample_mix(x, mod_seq, sinks, gmix, gffn, w_in, qg, kg, bdq, bdk, tabs, ret_tabs, w_out,
                wrh, wrl, br, cache_k, cache_v, state):
    t, d = x.shape
    nb, cache_w = cache_k.shape[0], cache_k.shape[1]
    ds = t // nb
    g = SAMPLE_SEQS
    tb = g * ds
    cosa, sina, cosr, sinr = tabs
    decay, qdec, kdec, gpow = ret_tabs
    row_spec = lambda w: pl.BlockSpec((tb, w), lambda i, *_: (i, 0))
    seq3 = lambda a: pl.BlockSpec((g,) + a.shape[1:], lambda i, *_: (i,) + (0,) * (a.ndim - 1))
    consts = [gmix, gffn, w_in, qg, kg, bdq, bdk, cosa, sina, cosr, sinr,
              decay, qdec, kdec, gpow, w_out, wrh, wrl, br]
    grid_spec = pltpu.PrefetchScalarGridSpec(
        num_scalar_prefetch=1,
        grid=(nb // g,),
        in_specs=([row_spec(d), pl.BlockSpec((g, mod_seq.shape[1]), lambda i, *_: (i, 0))]
                  + [_const_spec(a.shape) for a in consts]
                  + [seq3(cache_k), seq3(cache_v), seq3(state)]),
        out_specs=[row_spec(d), row_spec(d), row_spec(LANES), row_spec(LANES),
                   pl.BlockSpec((SUBLANES, tb), lambda i, *_: (0, i)),
                   pl.BlockSpec((1, N_EXPERTS, LANES), lambda i, *_: (i, 0, 0)),
                   seq3(cache_k), seq3(cache_v), seq3(state)],
        scratch_shapes=[pltpu.VMEM((tb, ATT_W + RET_W), jnp.bfloat16)],
    )
    return pl.pallas_call(
        _sample_mix_kernel,
        grid_spec=grid_spec,
        out_shape=[jax.ShapeDtypeStruct((t, d), jnp.float32),
                   jax.ShapeDtypeStruct((t, d), jnp.float32),
                   jax.ShapeDtypeStruct((t, LANES), jnp.int32),
                   jax.ShapeDtypeStruct((t, LANES), jnp.float32),
                   jax.ShapeDtypeStruct((SUBLANES, t), jnp.int32),
                   jax.ShapeDtypeStruct((t // tb, N_EXPERTS, LANES), jnp.int32),
                   jax.ShapeDtypeStruct(cache_k.shape, jnp.float32),
                   jax.ShapeDtypeStruct(cache_v.shape, jnp.float32),
                   jax.ShapeDtypeStruct(state.shape, jnp.float32)],
        compiler_params=pltpu.CompilerParams(
            dimension_semantics=("arbitrary",), vmem_limit_bytes=VMEM_LIMIT_BYTES),
    )(sinks, x, mod_seq, *consts, cache_k, cache_v, state)


def _chunk_copies(cnt8, src_ref, src_start, dst_ref, dst_start, sem):
    for bit in range(CHUNK_BITS - 1, SUBLANES.bit_length() - 2, -1):
        size = 1 << bit
        done = (cnt8 >> (bit + 1)) << (bit + 1)

        @pl.when(((cnt8 >> bit) & 1) == 1)
        def _():
            src = pl.multiple_of(src_start + done, SUBLANES)
            dst = pl.multiple_of(dst_start + done, SUBLANES)
            pltpu.make_async_copy(src_ref.at[pl.ds(src, size)],
                                  dst_ref.at[pl.ds(dst, size)], sem).start()


def _wait_rows(ref, rows8, sem, bits=STAGE_BITS):
    for bit in range(bits - 1, SUBLANES.bit_length() - 2, -1):
        size = 1 << bit

        @pl.when(((rows8 >> bit) & 1) == 1)
        def _():
            pltpu.make_async_copy(ref.at[pl.ds(0, size)], ref.at[pl.ds(0, size)], sem).wait()


def _dispatch_kernel(cnt_ref, gstart_ref, cb_ref, blkrows_ref, ends_ref, padcnt_ref,
                     topit_p_ref, topit_s_ref, h2_p_ref, h2_s_ref, cbcol_ref, xs_ref,
                     stage, zeros, sems, zsem, *, nblk_p):
    step = pl.program_id(0)
    nblk = pl.num_programs(0)
    tb = h2_p_ref.shape[0]
    slot = step % STAGE_SLOTS

    tile = zeros.shape[0]
    first_unused = ends_ref[N_EXPERTS - 1] // tile
    n_tiles = xs_ref.shape[0] // tile

    def tail_fill(t):
        return pltpu.make_async_copy(zeros, xs_ref.at[pl.ds(pl.multiple_of(t * tile, tile), tile)],
                                     zsem)

    @pl.when(step == 0)
    def _():
        zeros[...] = jnp.zeros_like(zeros)

        def pad_fill(e, carry):
            _chunk_copies(padcnt_ref[e], zeros, 0, xs_ref, ends_ref[e] - padcnt_ref[e], zsem)
            return carry

        lax.fori_loop(0, N_EXPERTS, pad_fill, 0)
        lax.fori_loop(first_unused, n_tiles, lambda t, c: (tail_fill(t).start(), c)[1], 0)

    @pl.when(step == nblk - 1)
    def _():
        _wait_rows(xs_ref, padcnt_ref[N_EXPERTS], zsem, bits=(N_EXPERTS * tile).bit_length())
        lax.fori_loop(first_unused, n_tiles, lambda t, c: (tail_fill(t).wait(), c)[1], 0)

    def issue_chunk(blk, e):
        i = (blk + 1) * N_EXPERTS + e
        _chunk_copies(cnt_ref[i], stage.at[blk % STAGE_SLOTS], cb_ref[i], xs_ref, gstart_ref[i],
                      sems.at[blk % STAGE_SLOTS])

    for e in range(N_EXPERTS):
        issue_chunk(step - 1, e)

    is_p = step < nblk_p
    topit = jnp.where(is_p, topit_p_ref[...], topit_s_ref[...])
    h2 = jnp.where(is_p, h2_p_ref[...], h2_s_ref[...])
    cbcol = cbcol_ref[0]

    sub = lax.broadcasted_iota(jnp.int32, (LANES, tb), 0)
    picks = [sub == topit[k:k + 1, :] for k in range(TOP_K)]
    onehot = jnp.zeros((LANES, tb), jnp.float32)
    for pk in picks:
        onehot = jnp.where(pk, 1.0, onehot)
    r = lax.broadcasted_iota(jnp.int32, (tb, tb), 0)
    c = lax.broadcasted_iota(jnp.int32, (tb, tb), 1)
    earlier = jnp.where(r < c, 1.0, 0.0)
    before = _dot(onehot, earlier)
    base = before + cbcol.astype(jnp.float32)
    row = lax.broadcasted_iota(jnp.int32, (stage.shape[1], tb), 0)
    sel = jnp.zeros((stage.shape[1], tb), jnp.float32)
    for pk in picks:
        col = jnp.sum(jnp.where(pk, base, 0.0), axis=0, keepdims=True).astype(jnp.int32)
        sel = jnp.where(row == col, 1.0, sel)

    stage[slot] = _dot(sel, h2)

    @pl.when(step == nblk - 1)
    def _():
        lax.fori_loop(0, N_EXPERTS, lambda e, c: (issue_chunk(step, e), c)[1], 0)

    oldest = step - (STAGE_SLOTS - 1)

    @pl.when(oldest >= 0)
    def _():
        _wait_rows(xs_ref, blkrows_ref[jnp.maximum(oldest, 0)], sems.at[(step + 1) % STAGE_SLOTS])

    @pl.when(step == nblk - 1)
    def _():
        for back in range(STAGE_SLOTS - 2, -1, -1):
            @pl.when(step - back >= 0)
            def _():
                blk = jnp.maximum(step - back, 0)
                _wait_rows(xs_ref, blkrows_ref[blk], sems.at[blk % STAGE_SLOTS])


def _dispatch(tables, topit_p, topit_s, h2_p, h2_s, cbcol, n_rows):
    n_p, d = h2_p.shape
    n_s = h2_s.shape[0]
    tb = ROUTE_BLOCK
    nblk_p, nblk_s = n_p // tb, n_s // tb
    pidx = lambda i, *_: jnp.minimum(i, nblk_p - 1)
    sidx = lambda i, *_: jnp.maximum(i - nblk_p, 0)
    grid_spec = pltpu.PrefetchScalarGridSpec(
        num_scalar_prefetch=6,
        grid=(nblk_p + nblk_s,),
        in_specs=[pl.BlockSpec((SUBLANES, tb), lambda i, *_: (0, pidx(i))),
                  pl.BlockSpec((SUBLANES, tb), lambda i, *_: (0, sidx(i))),
                  pl.BlockSpec((tb, d), lambda i, *_: (pidx(i), 0)),
                  pl.BlockSpec((tb, d), lambda i, *_: (sidx(i), 0)),
                  pl.BlockSpec((1, LANES, 1), lambda i, *_: (i, 0, 0))],
        out_specs=pl.BlockSpec(memory_space=pl.ANY),
        scratch_shapes=[pltpu.VMEM((STAGE_SLOTS, STAGE_ROWS, d), jnp.float32),
                        pltpu.VMEM((EXPERT_TILE, d), jnp.float32),
                        pltpu.SemaphoreType.DMA((STAGE_SLOTS,)),
                        pltpu.SemaphoreType.DMA],
    )
    return pl.pallas_call(
        functools.partial(_dispatch_kernel, nblk_p=nblk_p),
        grid_spec=grid_spec,
        out_shape=jax.ShapeDtypeStruct((n_rows, d), jnp.float32),
        compiler_params=pltpu.CompilerParams(
            dimension_semantics=("arbitrary",), vmem_limit_bytes=VMEM_LIMIT_BYTES),
    )(*tables, topit_p, topit_s, h2_p, h2_s, cbcol)


def _expert_mlp(x, wgu_bf, bgu, wd_bf, bd):
    d_ff = wd_bf.shape[0]
    gu = jnp.dot(_bf(x), wgu_bf[...], preferred_element_type=jnp.float32) + bgu
    glu = jnp.minimum(gu[:, :d_ff], SWIGLU_LIMIT)
    lin = jnp.clip(gu[:, d_ff:], -SWIGLU_LIMIT, SWIGLU_LIMIT)
    act = glu * jax.nn.sigmoid(SWIGLU_ALPHA * glu) * (lin + 1.0)
    return jnp.dot(_bf(act), wd_bf[...], preferred_element_type=jnp.float32) + bd


def _experts_kernel(tile_expert_ref, n_used_ref, groups_ref, next_ref,
                    x_ref, wgu_hbm, bgu_ref, wd_hbm, bd_ref, y_ref,
                    wgu_f32, wd_f32, wgu_bf, wd_bf, sems):
    step = pl.program_id(0)
    groups = groups_ref[step]

    def weight_copies(expert):
        return (pltpu.make_async_copy(wgu_hbm.at[expert], wgu_f32, sems.at[0]),
                pltpu.make_async_copy(wd_hbm.at[expert], wd_f32, sems.at[1]))

    @pl.when(groups > 0)
    def _():
        e = tile_expert_ref[step]
        prev = tile_expert_ref[jnp.maximum(step - 1, 0)]

        @pl.when(step == 0)
        def _():
            for cp in weight_copies(e):
                cp.start()

        @pl.when(jnp.logical_or(step == 0, e != prev))
        def _():
            for cp in weight_copies(e):
                cp.wait()
            wgu_bf[...] = _bf(wgu_f32[...])
            wd_bf[...] = _bf(wd_f32[...])
            nxt = next_ref[e]

            @pl.when(nxt >= 0)
            def _():
                for cp in weight_copies(nxt):
                    cp.start()

    tm = x_ref.shape[0]
    for n in range(1, EXPERT_GROUPS + 1):
        @pl.when(groups == n)
        def _():
            rows = n * (tm // EXPERT_GROUPS)
            y_ref[0:rows, :] = _expert_mlp(x_ref[0:rows, :], wgu_bf, bgu_ref[0], wd_bf, bd_ref[0])
            if rows < tm:
                y_ref[rows:, :] = jnp.zeros((tm - rows, y_ref.shape[1]), y_ref.dtype)

    @pl.when(groups == 0)
    def _():
        y_ref[...] = jnp.zeros_like(y_ref)


def _experts(tile_expert, n_used, tile_groups, next_used, x_sorted, w_gate_up, b_gate_up,
             w_down, b_down):
    p = x_sorted.shape[0]
    n_e, d, two_ff = w_gate_up.shape
    d_ff = two_ff // 2
    tm = EXPERT_TILE
    tile = lambda i, te, nu: jnp.minimum(i, nu[0] - 1)
    grid_spec = pltpu.PrefetchScalarGridSpec(
        num_scalar_prefetch=4,
        grid=(p // tm,),
        in_specs=[pl.BlockSpec((tm, d), lambda i, te, nu, *_: (tile(i, te, nu), 0)),
                  pl.BlockSpec(memory_space=pl.ANY),
                  pl.BlockSpec((1, 1, two_ff), lambda i, te, *_: (te[i], 0, 0)),
                  pl.BlockSpec(memory_space=pl.ANY),
                  pl.BlockSpec((1, 1, d), lambda i, te, *_: (te[i], 0, 0))],
        out_specs=pl.BlockSpec((tm, d), lambda i, *_: (i, 0)),
        scratch_shapes=[pltpu.VMEM((d, two_ff), jnp.float32),
                        pltpu.VMEM((d_ff, d), jnp.float32),
                        pltpu.VMEM((d, two_ff), jnp.bfloat16),
                        pltpu.VMEM((d_ff, d), jnp.bfloat16),
                        pltpu.SemaphoreType.DMA((2,))],
    )
    return pl.pallas_call(
        _experts_kernel,
        grid_spec=grid_spec,
        out_shape=jax.ShapeDtypeStruct(x_sorted.shape, jnp.float32),
        compiler_params=pltpu.CompilerParams(
            dimension_semantics=("arbitrary",), vmem_limit_bytes=VMEM_LIMIT_BYTES),
    )(tile_expert, n_used, tile_groups, next_used, x_sorted, w_gate_up,
      b_gate_up.reshape(n_e, 1, two_ff), w_down, b_down.reshape(n_e, 1, d))


def _combine_kernel(cnt_ref, gstart_ref, cb_ref, blkrows_ref,
                    topi_p_ref, topi_s_ref, topw_p_ref, topw_s_ref, x1_p_ref, x1_s_ref,
                    gate_p_ref, gate_s_ref, cbrow_ref, y_ref, out_p_ref, out_s_ref,
                    ybuf, sems, *, nblk_p):
    step = pl.program_id(0)
    nblk = pl.num_programs(0)
    tb = x1_p_ref.shape[0]
    slot = step % STAGE_SLOTS

    def fetch_chunk(blk, into, e):
        i = blk * N_EXPERTS + e
        _chunk_copies(cnt_ref[i], y_ref, gstart_ref[i], ybuf.at[into], cb_ref[i], sems.at[into])

    def fetch(blk, into):
        lax.fori_loop(0, N_EXPERTS, lambda e, c: (fetch_chunk(blk, into, e), c)[1], 0)

    @pl.when(step == 0)
    def _():
        ybuf[...] = jnp.zeros_like(ybuf)
        for ahead in range(STAGE_SLOTS - 1):
            @pl.when(ahead < nblk)
            def _():
                fetch(ahead, ahead)

    _wait_rows(y_ref, blkrows_ref[step], sems.at[slot])

    for e in range(N_EXPERTS):
        fetch_chunk(step + STAGE_SLOTS - 1, (step + STAGE_SLOTS - 1) % STAGE_SLOTS, e)

    is_p = step < nblk_p
    topi = jnp.where(is_p, topi_p_ref[...], topi_s_ref[...])
    topw = jnp.where(is_p, topw_p_ref[...], topw_s_ref[...])
    cbrow = cbrow_ref[0]

    lane = lax.broadcasted_iota(jnp.int32, (tb, LANES), 1)
    picks = []
    onehot = jnp.zeros((tb, LANES), jnp.float32)
    for k in range(TOP_K):
        ix = jnp.sum(jnp.where(lane == k, topi, 0), axis=-1, keepdims=True)
        picks.append(lane == ix)
        onehot = jnp.where(picks[k], 1.0, onehot)
    r = lax.broadcasted_iota(jnp.int32, (tb, tb), 0)
    c = lax.broadcasted_iota(jnp.int32, (tb, tb), 1)
    earlier = jnp.where(c < r, 1.0, 0.0)
    base = _dot(earlier, onehot) + cbrow.astype(jnp.float32)
    cols = [jnp.sum(jnp.where(picks[k], base, 0.0), axis=-1, keepdims=True).astype(jnp.int32)
            for k in range(TOP_K)]
    wks = [jnp.sum(jnp.where(lane == k, topw, 0.0), axis=-1, keepdims=True) for k in range(TOP_K)]
    moe = None
    for c0 in range(0, ybuf.shape[1], COMBINE_SLAB):
        colid = lax.broadcasted_iota(jnp.int32, (tb, COMBINE_SLAB), 1) + c0
        weights = jnp.zeros((tb, COMBINE_SLAB), jnp.float32)
        for k in range(TOP_K):
            weights = jnp.where(colid == cols[k], wks[k], weights)
        part = _dot(weights, ybuf[slot, c0:c0 + COMBINE_SLAB, :])
        moe = part if moe is None else moe + part

    @pl.when(is_p)
    def _():
        out_p_ref[...] = x1_p_ref[...] + gate_p_ref[...] * moe

    @pl.when(jnp.logical_not(is_p))
    def _():
        gate_s = _repeat_rows(gate_s_ref[...], tb // gate_s_ref.shape[0])
        out_s_ref[...] = x1_s_ref[...] + gate_s * moe


def _combine(tables, topi_p, topi_s, topw_p, topw_s, x1_p, x1_s, gate_p, mod_seq, dec_seq, cbrow,
             y_sorted):
    n_p, d = x1_p.shape
    n_s = x1_s.shape[0]
    tb = ROUTE_BLOCK
    nblk_p, nblk_s = n_p // tb, n_s // tb
    pidx = lambda i, *_: jnp.minimum(i, nblk_p - 1)
    sidx = lambda i, *_: jnp.maximum(i - nblk_p, 0)
    prow = lambda w: pl.BlockSpec((tb, w), lambda i, *_: (pidx(i), 0))
    srow = lambda w: pl.BlockSpec((tb, w), lambda i, *_: (sidx(i), 0))
    grid_spec = pltpu.PrefetchScalarGridSpec(
        num_scalar_prefetch=4,
        grid=(nblk_p + nblk_s,),
        in_specs=[prow(LANES), srow(LANES), prow(LANES), srow(LANES), prow(d), srow(d),
                  pl.BlockSpec((1, d), lambda i, *_: (0, 0)),
                  pl.BlockSpec((tb // dec_seq, d), lambda i, *_: (sidx(i), 5)),
                  pl.BlockSpec((1, 1, LANES), lambda i, *_: (i, 0, 0)),
                  pl.BlockSpec(memory_space=pl.ANY)],
        out_specs=[prow(d), srow(d)],
        scratch_shapes=[pltpu.VMEM((STAGE_SLOTS, STAGE_ROWS, d), jnp.float32),
                        pltpu.SemaphoreType.DMA((STAGE_SLOTS,))],
    )
    return pl.pallas_call(
        functools.partial(_combine_kernel, nblk_p=nblk_p),
        grid_spec=grid_spec,
        out_shape=[jax.ShapeDtypeStruct((n_p, d), jnp.float32),
                   jax.ShapeDtypeStruct((n_s, d), jnp.float32)],
        compiler_params=pltpu.CompilerParams(
            dimension_semantics=("arbitrary",), vmem_limit_bytes=VMEM_LIMIT_BYTES),
    )(*tables, topi_p, topi_s, topw_p, topw_s, x1_p, x1_s, gate_p, mod_seq, cbrow, y_sorted)


def _rotary_tables(pos, inv_freq):
    ang = np.asarray(pos, np.float64)[:, None] * inv_freq[None, :]
    cos, sin = np.cos(ang), np.sin(ang)
    cos_t = np.tile(cos, (1, LANES // cos.shape[1]))
    sin_t = np.tile(np.concatenate([-sin, sin], axis=1), (1, LANES // (2 * sin.shape[1])))
    return cos_t.astype(np.float32), sin_t.astype(np.float32)


def _rotary_split_tables(block, nblk, inv_freq):
    half = inv_freq.shape[0]
    freq = np.tile(inv_freq, LANES // half)
    sign = np.tile(np.concatenate([-np.ones(half), np.ones(half)]), LANES // (2 * half))
    base = (np.arange(nblk) * block).astype(np.float64)[:, None] * freq[None, :]
    off = np.arange(block).astype(np.float64)[:, None] * freq[None, :]
    blk = np.stack([np.cos(base), np.sin(base)], axis=1)
    tab = np.stack([np.cos(off), np.sin(off), np.cos(off) * sign, np.sin(off) * sign])
    return blk.astype(np.float32), tab.astype(np.float32)


def _retention_tables(chunk, nseq=1):
    log_gamma = np.log1p(-np.exp2(-5.0 - np.arange(RET_HEADS, dtype=np.float64)))
    idx = np.arange(chunk, dtype=np.float64)
    diff = idx[:, None] - idx[None, :]
    decay = np.where(diff[None] >= 0,
                     np.exp(np.maximum(diff, 0.0)[None] * log_gamma[:, None, None]), 0.0)
    decay = np.stack([np.kron(np.eye(nseq), decay[h]) for h in range(RET_HEADS)])
    q_decay = np.exp((idx + 1.0)[:, None] * log_gamma[None, :])
    k_decay = np.exp((chunk - 1.0 - idx)[:, None] * log_gamma[None, :])
    qdec = np.tile(np.repeat(q_decay, RET_DV, axis=1), (nseq, 1))
    kdec = np.tile(np.repeat(k_decay, RET_DK, axis=1), (nseq, 1))
    g_chunk = np.exp(chunk * log_gamma)
    gpow = np.repeat(g_chunk.reshape(RET_HEADS // 2, 2), RET_DK, axis=1).T
    return tuple(a.astype(np.float32) for a in (decay, qdec, kdec, gpow))


def _block_diag_mean(width):
    head = np.arange(width) // HEAD_DIM
    return jnp.asarray(np.where(head[:, None] == head[None, :], 1.0 / HEAD_DIM, 0.0), jnp.bfloat16)


def kernel(x_prompt, x_sample, cache_k, cache_v, state_ret, c_prompt, c_sample, w_ada, b_ada,
           g_mix, w_in, q_gain, k_gain, sinks, w_out, g_ffn, w_router, b_router, w_gate_up,
           b_gate_up, w_down, b_down):
    depth = w_ada.shape[0]
    batch, seq, d = x_prompt.shape
    dec_batch, dec_seq, _ = x_sample.shape
    cache_w = cache_k.shape[2]
    assert batch == 1 and depth == 1
    assert seq % PROMPT_BLOCK == 0 and PROMPT_BLOCK % WINDOW == 0 and WINDOW == RET_CHUNK
    assert dec_batch % SAMPLE_SEQS == 0 and cache_w == WINDOW
    n_p, n_s = batch * seq, dec_batch * dec_seq
    n_tok = n_p + n_s
    assert n_p % ROUTE_BLOCK == 0 and n_s % ROUTE_BLOCK == 0
    assert ROUTE_BLOCK % COUNT_BLOCK == 0 and (SAMPLE_SEQS * dec_seq) % COUNT_BLOCK == 0
    assert PROMPT_BLOCK % COUNT_BLOCK == 0
    assert STAGE_ROWS % COMBINE_SLAB == 0 and EXPERT_TILE % (EXPERT_GROUPS * SUBLANES) == 0

    l = 0
    f32 = jnp.float32
    rope_freq = 1.0 / (ROPE_THETA ** (np.arange(0, HEAD_DIM, 2, dtype=np.float64) / HEAD_DIM))
    ret_freq = 1.0 / (ROPE_THETA ** np.linspace(0.0, 1.0, RET_DK // 2))
    pos_s = PAST_LEN + np.arange(dec_seq)
    blk_a, tab_a = _rotary_split_tables(PROMPT_BLOCK, seq // PROMPT_BLOCK, rope_freq)
    blk_r, tab_r = _rotary_split_tables(PROMPT_BLOCK, seq // PROMPT_BLOCK, ret_freq)
    tabs_p = (np.concatenate([blk_a, blk_r], axis=1), np.concatenate([tab_a, tab_r], axis=0))
    tabs_s = tuple(np.tile(a, (SAMPLE_SEQS, 1))
                   for a in _rotary_tables(pos_s, rope_freq) + _rotary_tables(pos_s, ret_freq))
    ret_p = _retention_tables(RET_CHUNK)
    ret_s = _retention_tables(dec_seq, SAMPLE_SEQS)

    n_c = batch + dec_batch
    c_rows = -(-n_c // SUBLANES) * SUBLANES
    c_all = jnp.concatenate([c_sample, c_prompt, jnp.zeros((c_rows - n_c, d), f32)], axis=0)
    mod = _adaln(c_all, w_ada[l], b_ada[l])
    mod_p = mod[dec_batch].reshape(6, d)

    w_in_bf = _bf(w_in[l])
    w_out_bf = _bf(w_out[l])
    qg = jnp.tile(q_gain[l], ATT_HEADS).reshape(1, ATT_W)
    kg = jnp.tile(k_gain[l], ATT_KV_HEADS).reshape(1, KV_W)
    bdq, bdk = _block_diag_mean(ATT_W), _block_diag_mean(KV_W)
    gmix = g_mix[l].reshape(1, d)
    gffn = g_ffn[l].reshape(1, d)
    wr = jnp.pad(w_router[l].T, ((0, LANES - N_EXPERTS), (0, 0)))
    wr_hi = _bf(wr)
    wr_cat = jnp.concatenate([wr_hi, _bf(wr - wr_hi.astype(f32))], axis=0)
    br = b_router[l].reshape(N_EXPERTS, 1)
    shared = (sinks[l], gmix, gffn, w_in_bf, qg, kg, bdq, bdk)

    (x1_p, h2_p, topi_p, topw_p, topit_p, cnt_p, kwin_p, vwin_p, rend_p) = _prompt_mix(
        x_prompt.reshape(n_p, d), mod_p, *shared, tabs_p, ret_p, w_out_bf, wr_hi, wr_cat, br)

    (x1_s, h2_s, topi_s, topw_s, topit_s, cnt_s, kwin_s, vwin_s, rend_s) = _sample_mix(
        x_sample.reshape(n_s, d), mod, *shared, tabs_s, ret_s, w_out_bf, wr_hi, wr_cat, br,
        cache_k[l].reshape(dec_batch, cache_w, KV_W), cache_v[l].reshape(dec_batch, cache_w, KV_W),
        state_ret[l].reshape(dec_batch, 2, LANES, RET_DV))

    tb, tm = ROUTE_BLOCK, EXPERT_TILE
    i32 = jnp.int32
    per_group = lambda c, step_rows: jnp.swapaxes(c[:, :, :step_rows // COUNT_BLOCK], 1, 2).reshape(
        -1, N_EXPERTS)
    cnt = jnp.concatenate([per_group(cnt_p, PROMPT_BLOCK), per_group(cnt_s, SAMPLE_SEQS * dec_seq)])
    cnt = cnt.reshape(n_tok // tb, tb // COUNT_BLOCK, N_EXPERTS).sum(axis=1)
    cnt = jnp.pad(cnt, ((0, 0), (0, LANES - N_EXPERTS)))
    cnt8 = -(-cnt // SUBLANES) * SUBLANES
    rs = jnp.cumsum(cnt8, axis=0) - cnt8
    tot = jnp.sum(cnt8, axis=0)[:N_EXPERTS]
    padded = -(-tot // tm) * tm
    ends = jnp.cumsum(padded)
    gstart = (ends - padded)[None, :] + rs[:, :N_EXPERTS]
    cb = jnp.cumsum(cnt8, axis=1) - cnt8
    blk_rows = jnp.sum(cnt8, axis=1).astype(i32)
    max_rows = n_tok * TOP_K + cnt.shape[0] * N_EXPERTS * (SUBLANES - 1)
    n_rows = -(-max_rows // tm) * tm + N_EXPERTS * tm
    tile_start = jnp.arange(n_rows // tm, dtype=i32) * tm
    tile_expert = jnp.minimum(jnp.sum(ends[None, :] <= tile_start[:, None], axis=1),
                              N_EXPERTS - 1).astype(i32)
    n_used = (ends[-1:] // tm).astype(i32)
    eid = jnp.arange(N_EXPERTS, dtype=i32)
    of_tile = lambda v: jnp.sum(jnp.where(tile_expert[:, None] == eid[None, :], v[None, :], 0), axis=1)
    tile_rows = jnp.clip(of_tile(tot) - (tile_start - of_tile(ends - padded)), 0, tm)
    tile_rows = jnp.where(tile_start < ends[-1], tile_rows, 0)
    tile_groups = (-(-tile_rows // (tm // EXPERT_GROUPS))).astype(i32)
    flat = lambda a: a[:, :N_EXPERTS].reshape(-1).astype(i32)
    tables = (flat(cnt8), gstart.reshape(-1).astype(i32), flat(cb), blk_rows)
    ahead = jnp.zeros(((STAGE_SLOTS - 1) * N_EXPERTS,), i32)
    tables_c = tuple(jnp.concatenate([t, ahead]) for t in tables[:3]) + tables[3:]

    pad_cnt = padded - tot
    pad_cnt = jnp.concatenate([pad_cnt, jnp.sum(pad_cnt, keepdims=True)]).astype(i32)
    behind = jnp.zeros((N_EXPERTS,), i32)
    tables_d = tuple(jnp.concatenate([behind, t]) for t in tables[:3]) + tables[3:]
    x_sorted = _dispatch(tables_d + (ends.astype(i32), pad_cnt), topit_p, topit_s, h2_p, h2_s,
                         cb.astype(i32)[:, :, None], n_rows)
    used = tot > 0
    later = jnp.where(used[None, :] & (eid[None, :] > eid[:, None]), eid[None, :], N_EXPERTS)
    next_used = jnp.min(later, axis=1)
    next_used = jnp.where(next_used == N_EXPERTS, -1, next_used).astype(i32)
    y_sorted = _experts(tile_expert, n_used, tile_groups, next_used, x_sorted, w_gate_up[l], b_gate_up[l], w_down[l],
                        b_down[l])
    y_p, y_s = _combine(tables_c, topi_p, topi_s, topw_p, topw_s, x1_p, x1_s, mod_p[5:6], mod,
                        dec_seq, cb.astype(i32)[:, None, :], y_sorted)

    kv5 = lambda a, n: a.reshape(1, n, cache_w, ATT_KV_HEADS, HEAD_DIM)
    st5 = lambda a, n: a.reshape(1, n, RET_HEADS, RET_DK, RET_DV)
    return (y_p.reshape(batch, seq, d), y_s.reshape(dec_batch, dec_seq, d),
            kv5(kwin_p, batch), kv5(vwin_p, batch), st5(rend_p, batch),
            kv5(kwin_s, dec_batch), kv5(vwin_s, dec_batch), st5(rend_s, dec_batch))
```

```python
import functools

import jax
import jax.numpy as jnp
import numpy as np
from jax import lax
from jax.experimental import pallas as pl
from jax.experimental.pallas import tpu as pltpu

HEAD_DIM = 64
ATT_HEADS = 8
ATT_KV_HEADS = 2
ATT_W = ATT_HEADS * HEAD_DIM
KV_W = ATT_KV_HEADS * HEAD_DIM
WINDOW = 128
PAST_LEN = 16384
ROPE_THETA = 10000.0
RET_HEADS = 4
RET_DK = 64
RET_DV = 128
RET_QK_W = RET_HEADS * RET_DK
RET_W = RET_HEADS * RET_DV
RET_CHUNK = 128
N_EXPERTS = 32
TOP_K = 4
SWIGLU_LIMIT = 7.0
SWIGLU_ALPHA = 1.702
EPS = 1e-6

_Q0, _K0, _V0 = 0, ATT_W, ATT_W + KV_W
_RQ0 = ATT_W + 2 * KV_W
_RK0 = _RQ0 + RET_QK_W
_RV0 = _RK0 + RET_QK_W
_RG0 = _RV0 + RET_W
IN_W = _RG0 + RET_W

LANES = 128
SUBLANES = 8
VMEM_LIMIT_BYTES = 56 * 1024 * 1024
NEG_BIG = float("-inf")

PROMPT_BLOCK = 512
SAMPLE_SEQS = 16
COUNT_BLOCK = 128
ROUTE_BLOCK = 512
EXPERT_TILE = 512
EXPERT_GROUPS = 4
CHUNK_BITS = ROUTE_BLOCK.bit_length()
STAGE_ROWS = -(-(TOP_K * ROUTE_BLOCK + N_EXPERTS * (SUBLANES - 1)) // LANES) * LANES
STAGE_BITS = STAGE_ROWS.bit_length()
STAGE_SLOTS = 3
COMBINE_SLAB = 768

_NT = (((1,), (1,)), ((), ()))
_TN = (((0,), (0,)), ((), ()))


def _bf(x):
    return x.astype(jnp.bfloat16)


def _dot(a, b):
    return jnp.dot(_bf(a), _bf(b), preferred_element_type=jnp.float32)


def _dot_nt(a, b):
    return lax.dot_general(_bf(a), _bf(b), _NT, preferred_element_type=jnp.float32)


def _dot_tn(a, b):
    return lax.dot_general(_bf(a), _bf(b), _TN, preferred_element_type=jnp.float32)


def _lane_lo(shape):
    lane = lax.broadcasted_iota(jnp.int32, shape, len(shape) - 1)
    return (lane % LANES) < HEAD_DIM


def _swap_halves(x):
    lane = lax.broadcasted_iota(jnp.int32, x.shape, 1)
    first = (lane % HEAD_DIM) < (HEAD_DIM // 2)
    return jnp.where(first, pltpu.roll(x, LANES - HEAD_DIM // 2, axis=1),
                     pltpu.roll(x, HEAD_DIM // 2, axis=1))


def _rotate(x, cos, sin_signed):
    outs = []
    for j in range(x.shape[1] // LANES):
        xs = x[:, j * LANES:(j + 1) * LANES]
        outs.append(xs * cos + _swap_halves(xs) * sin_signed)
    return outs[0] if len(outs) == 1 else jnp.concatenate(outs, axis=1)


def _repeat_rows(m, n):
    r, w = m.shape
    return jnp.broadcast_to(m[:, None, :], (r, n, w)).reshape(r * n, w)


def _block_rotary(blk_ref, tab_ref, which):
    ca, sa = blk_ref[0, 2 * which:2 * which + 1, :], blk_ref[0, 2 * which + 1:2 * which + 2, :]
    cb, sb, cbs, sbs = (tab_ref[4 * which + n] for n in range(4))
    return ca * cb - sa * sb, sa * cbs + ca * sbs


def _rms_rows(x):
    return x * lax.rsqrt(jnp.mean(x * x, axis=-1, keepdims=True) + EPS)


def _pre_norm(x, gain, shift, scale):
    return _rms_rows(x) * gain * (1.0 + scale) + shift


def _head_norm(x, gain_tiled, blockdiag):
    ms = jnp.dot(_bf(x * x), blockdiag, preferred_element_type=jnp.float32)
    return x * lax.rsqrt(ms + EPS) * gain_tiled


def _project(h, w_in_ref, qg, kg, bd_q, bd_k, cos_a, sin_a, cos_r, sin_r):
    z = jnp.dot(_bf(h), w_in_ref[...], preferred_element_type=jnp.float32)
    q = _rotate(_head_norm(z[:, _Q0:_K0], qg, bd_q), cos_a, sin_a) * (HEAD_DIM ** -0.5)
    k = _rotate(_head_norm(z[:, _K0:_V0], kg, bd_k), cos_a, sin_a)
    v = z[:, _V0:_RQ0]
    rq = _rotate(z[:, _RQ0:_RK0], cos_r, sin_r)
    rk = _rotate(z[:, _RK0:_RV0], cos_r, sin_r) * (RET_DK ** -0.5)
    rv = z[:, _RV0:_RG0]
    rg = z[:, _RG0:IN_W]
    return q, k, v, rq, rk, rv, rg


_NAT_HEADS = (0, 2, 5, 7)
_ROL_HEADS = (1, 3, 4, 6)


def _stack_heads(q, heads):
    lo = _lane_lo((q.shape[0], LANES))
    parts = []
    for j in heads:
        slab = q[:, (j // 2) * LANES:(j // 2 + 1) * LANES]
        parts.append(jnp.where(lo if j % 2 == 0 else ~lo, slab, 0.0))
    return jnp.concatenate(parts, axis=0)


def _sink_column(sinks_ref, heads, rows):
    r = lax.broadcasted_iota(jnp.int32, (len(heads) * rows, 1), 0)
    col = jnp.full((len(heads) * rows, 1), sinks_ref[heads[-1]], jnp.float32)
    for n in range(len(heads) - 2, -1, -1):
        col = jnp.where(r < (n + 1) * rows, sinks_ref[heads[n]], col)
    return col


def _softmax_pv(s, mask, sink, v):
    s = jnp.where(mask, s, NEG_BIG)
    m = jnp.maximum(jnp.max(s, axis=-1, keepdims=True), sink)
    p = jnp.exp(s - m)
    denom = jnp.sum(p, axis=-1, keepdims=True) + jnp.exp(sink - m)
    return _dot(p, v) / denom


def _attention(q, k_all, v_all, mask, sinks_ref):
    rows = q.shape[0]
    k_rol = pltpu.roll(k_all, HEAD_DIM, axis=1)
    v_rol = pltpu.roll(v_all, HEAD_DIM, axis=1)
    mask4 = jnp.concatenate([mask] * 4, axis=0)
    o_nat = _softmax_pv(_dot_nt(_stack_heads(q, _NAT_HEADS), k_all), mask4,
                        _sink_column(sinks_ref, _NAT_HEADS, rows), v_all)
    o_rol = _softmax_pv(_dot_nt(_stack_heads(q, _ROL_HEADS), k_rol), mask4,
                        _sink_column(sinks_ref, _ROL_HEADS, rows), v_rol)
    lo = _lane_lo((rows, LANES))
    blk = lambda o, n: o[n * rows:(n + 1) * rows]
    return jnp.concatenate([
        jnp.where(lo, blk(o_nat, 0), blk(o_rol, 0)),
        jnp.where(lo, blk(o_nat, 1), blk(o_rol, 1)),
        jnp.where(lo, blk(o_rol, 2), blk(o_nat, 2)),
        jnp.where(lo, blk(o_rol, 3), blk(o_nat, 3)),
    ], axis=1)


def _retention(rq, rk, rv, rg, state, decay_ref, qdec, kdec, gpow):
    rows = rq.shape[0]
    lo = _lane_lo((rows, LANES))
    rkd = rk * kdec
    outs, new_state = [], []
    for m in range(RET_HEADS // 2):
        sl = slice(m * LANES, (m + 1) * LANES)
        upd = state[m] * gpow[:, m:m + 1]
        for p in range(2):
            h = 2 * m + p
            half = lo if p == 0 else ~lo
            rqm = jnp.where(half, rq[:, sl], 0.0)
            rv_h = rv[:, h * RET_DV:(h + 1) * RET_DV]
            s = _dot_nt(rqm, rk[:, sl]) * decay_ref[h]
            ret = _dot(s, rv_h) + _dot(rqm, state[m]) * qdec[:, h * RET_DV:(h + 1) * RET_DV]
            upd = upd + _dot_tn(jnp.where(half, rkd[:, sl], 0.0), rv_h)
            rg_h = rg[:, h * RET_DV:(h + 1) * RET_DV]
            outs.append(rg_h * jax.nn.sigmoid(rg_h) * _rms_rows(ret))
        new_state.append(upd)
    return jnp.concatenate(outs, axis=1), new_state


def _route(h2, wrt_hi_ref, wrt_cat_ref, br_col):
    rows = h2.shape[0]
    h_hi = _bf(h2)
    h_lo = _bf(h2 - h_hi.astype(jnp.float32))
    both = lax.dot_general(wrt_cat_ref[...], h_hi, _NT, preferred_element_type=jnp.float32)
    low = lax.dot_general(wrt_hi_ref[...], h_lo, _NT, preferred_element_type=jnp.float32)
    logits = (both[0:N_EXPERTS] + both[LANES:LANES + N_EXPERTS] + low[0:N_EXPERTS]) + br_col
    sub = lax.broadcasted_iota(jnp.int32, logits.shape, 0)
    row8 = lax.broadcasted_iota(jnp.int32, (SUBLANES, rows), 0)
    idx8 = jnp.zeros((SUBLANES, rows), jnp.int32)
    val8 = jnp.zeros((SUBLANES, rows), jnp.float32)
    onehot = jnp.zeros(logits.shape, jnp.float32)
    vals = []
    work = logits
    for k in range(TOP_K):
        mx = jnp.max(work, axis=0, keepdims=True)
        ix = jnp.min(jnp.where(work == mx, sub, N_EXPERTS), axis=0, keepdims=True)
        pick = sub == ix
        work = jnp.where(pick, NEG_BIG, work)
        onehot = jnp.where(pick, 1.0, onehot)
        idx8 = jnp.where(row8 == k, ix, idx8)
        vals.append(mx)
    exps = [jnp.exp(v - vals[0]) for v in vals]
    total = exps[0] + exps[1] + exps[2] + exps[3]
    for k in range(TOP_K):
        val8 = jnp.where(row8 == k, exps[k] / total, val8)
    fill = jnp.zeros((LANES - SUBLANES, rows), jnp.float32)
    idx_out = jnp.concatenate([idx8.astype(jnp.float32), fill], axis=0).T.astype(jnp.int32)
    val_out = jnp.concatenate([val8, fill], axis=0).T
    tok = lax.broadcasted_iota(jnp.int32, (rows, LANES), 0)
    grp = lax.broadcasted_iota(jnp.int32, (rows, LANES), 1)
    counts = _dot(onehot, jnp.where(tok // COUNT_BLOCK == grp, 1.0, 0.0)).astype(jnp.int32)
    return idx_out, val_out, idx8, counts


def _adaln_kernel(c_ref, w_ref, b_ref, o_ref):
    c = c_ref[...]
    o_ref[...] = _dot(c * jax.nn.sigmoid(c), w_ref[...]) + b_ref[...]


def _adaln(c_all, w_ada, b_ada):
    rows, d = c_all.shape
    n = w_ada.shape[1]
    bn = 1536
    return pl.pallas_call(
        _adaln_kernel,
        grid=(n // bn,),
        in_specs=[pl.BlockSpec((rows, d), lambda j: (0, 0)),
                  pl.BlockSpec((d, bn), lambda j: (0, j)),
                  pl.BlockSpec((1, bn), lambda j: (0, j))],
        out_specs=pl.BlockSpec((rows, bn), lambda j: (0, j)),
        out_shape=jax.ShapeDtypeStruct((rows, n), jnp.float32),
        compiler_params=pltpu.CompilerParams(vmem_limit_bytes=VMEM_LIMIT_BYTES),
    )(c_all, w_ada, b_ada.reshape(1, n))


def _prompt_mix_kernel(sinks_ref, x_ref, mod_ref, gmix_ref, gffn_ref, w_in_ref, qg_ref, kg_ref,
                       bdq_ref, bdk_ref, rotblk_ref, rottab_ref,
                       decay_ref, qdec_ref, kdec_ref, gpow_ref, w_out_ref,
                       wrh_ref, wrl_ref, br_ref,
                       x1_ref, h2_ref, topi_ref, topw_ref, topit_ref, cnt_ref, kwin_ref, vwin_ref, rend_ref,
                       kprev, vprev, state, mix):
    step = pl.program_id(0)
    tb = x_ref.shape[0]

    @pl.when(step == 0)
    def _():
        kprev[...] = jnp.zeros_like(kprev)
        vprev[...] = jnp.zeros_like(vprev)
        state[...] = jnp.zeros_like(state)

    x = x_ref[...]
    h = _pre_norm(x, gmix_ref[...], mod_ref[0:1, :], mod_ref[1:2, :])
    q, k, v, rq, rk, rv, rg = _project(
        h, w_in_ref, qg_ref[...], kg_ref[...], bdq_ref[...], bdk_ref[...],
        *_block_rotary(rotblk_ref, rottab_ref, 0), *_block_rotary(rotblk_ref, rottab_ref, 1))

    qi = lax.broadcasted_iota(jnp.int32, (WINDOW, 2 * WINDOW), 0)
    ci = lax.broadcasted_iota(jnp.int32, (WINDOW, 2 * WINDOW), 1)
    band = (ci > qi) & (ci <= qi + WINDOW)
    qdec, kdec, gpow = qdec_ref[...], kdec_ref[...], gpow_ref[...]

    n_sb = tb // WINDOW
    rows = [slice(sb * WINDOW, (sb + 1) * WINDOW) for sb in range(n_sb)]
    k_prev, v_prev = kprev[...], vprev[...]
    for sb, rs in enumerate(rows):
        k_sb, v_sb = k[rs], v[rs]
        mask = band & ((ci >= WINDOW) | (step > 0)) if sb == 0 else band
        att = _attention(q[rs], jnp.concatenate([k_prev, k_sb], axis=0),
                         jnp.concatenate([v_prev, v_sb], axis=0), mask, sinks_ref)
        mix[rs, 0:ATT_W] = _bf(att)
        k_prev, v_prev = k_sb, v_sb
    kprev[...] = k_prev
    vprev[...] = v_prev

    cur_state = [state[0], state[1]]
    for rs in rows:
        yret, cur_state = _retention(rq[rs], rk[rs], rv[rs], rg[rs], cur_state,
                                     decay_ref, qdec, kdec, gpow)
        mix[rs, ATT_W:ATT_W + RET_W] = _bf(yret)
    state[0] = cur_state[0]
    state[1] = cur_state[1]

    x1 = x + mod_ref[2:3, :] * jnp.dot(mix[...], w_out_ref[...],
                                       preferred_element_type=jnp.float32)
    x1_ref[...] = x1
    h2 = _pre_norm(x1, gffn_ref[...], mod_ref[3:4, :], mod_ref[4:5, :])
    h2_ref[...] = h2
    topi, topw, topi_t, counts = _route(h2, wrh_ref, wrl_ref, br_ref[...])
    topi_ref[...] = topi
    topw_ref[...] = topw
    topit_ref[...] = topi_t
    cnt_ref[0] = counts

    @pl.when(step == pl.num_programs(0) - 1)
    def _():
        kwin_ref[...] = kprev[...]
        vwin_ref[...] = vprev[...]
        rend_ref[...] = state[...]


def _const_spec(shape):
    nd = len(shape)
    return pl.BlockSpec(shape, lambda *_: (0,) * nd)


def _prompt_mix(x, mod, sinks, gmix, gffn, w_in, qg, kg, bdq, bdk, tabs, ret_tabs, w_out,
                wrh, wrl, br):
    t, d = x.shape
    tb = PROMPT_BLOCK
    rot_blk, rot_tab = tabs
    decay, qdec, kdec, gpow = ret_tabs
    row_spec = lambda w: pl.BlockSpec((tb, w), lambda i, *_: (i, 0))
    consts = [mod, gmix, gffn, w_in, qg, kg, bdq, bdk]
    consts2 = [rot_tab, decay, qdec, kdec, gpow, w_out, wrh, wrl, br]
    grid_spec = pltpu.PrefetchScalarGridSpec(
        num_scalar_prefetch=1,
        grid=(t // tb,),
        in_specs=([row_spec(d)] + [_const_spec(a.shape) for a in consts]
                  + [pl.BlockSpec((1,) + rot_blk.shape[1:], lambda i, *_: (i, 0, 0))]
                  + [_const_spec(a.shape) for a in consts2]),
        out_specs=[row_spec(d), row_spec(d), row_spec(LANES), row_spec(LANES),
                   pl.BlockSpec((SUBLANES, tb), lambda i, *_: (0, i)),
                   pl.BlockSpec((1, N_EXPERTS, LANES), lambda i, *_: (i, 0, 0)),
                   _const_spec((WINDOW, KV_W)), _const_spec((WINDOW, KV_W)),
                   _const_spec((2, LANES, RET_DV))],
        scratch_shapes=[pltpu.VMEM((WINDOW, KV_W), jnp.float32),
                        pltpu.VMEM((WINDOW, KV_W), jnp.float32),
                        pltpu.VMEM((2, LANES, RET_DV), jnp.float32),
                        pltpu.VMEM((tb, ATT_W + RET_W), jnp.bfloat16)],
    )
    return pl.pallas_call(
        _prompt_mix_kernel,
        grid_spec=grid_spec,
        out_shape=[jax.ShapeDtypeStruct((t, d), jnp.float32),
                   jax.ShapeDtypeStruct((t, d), jnp.float32),
                   jax.ShapeDtypeStruct((t, LANES), jnp.int32),
                   jax.ShapeDtypeStruct((t, LANES), jnp.float32),
                   jax.ShapeDtypeStruct((SUBLANES, t), jnp.int32),
                   jax.ShapeDtypeStruct((t // tb, N_EXPERTS, LANES), jnp.int32),
                   jax.ShapeDtypeStruct((WINDOW, KV_W), jnp.float32),
                   jax.ShapeDtypeStruct((WINDOW, KV_W), jnp.float32),
                   jax.ShapeDtypeStruct((2, LANES, RET_DV), jnp.float32)],
        compiler_params=pltpu.CompilerParams(
            dimension_semantics=("arbitrary",), vmem_limit_bytes=VMEM_LIMIT_BYTES),
    )(sinks, x, *consts, rot_blk, *consts2)


def _per_seq(stacked, rows, ds, s):
    return jnp.concatenate([stacked[g * rows + s * ds:g * rows + (s + 1) * ds]
                            for g in range(stacked.shape[0] // rows)], axis=0)


def _from_per_seq(parts, ds):
    groups = parts[0].shape[0] // ds
    return jnp.concatenate([p[g * ds:(g + 1) * ds] for g in range(groups) for p in parts], axis=0)


def _sample_attention(q, k_new, v_new, ck_ref, cv_ref, sinks_ref, ds):
    rows = q.shape[0]
    nseq, cache_w = ck_ref.shape[0], ck_ref.shape[1]
    tok_c = lax.broadcasted_iota(jnp.int32, (4 * rows, cache_w), 0) % rows
    col_c = lax.broadcasted_iota(jnp.int32, (4 * rows, cache_w), 1)
    delta = tok_c % ds + cache_w - col_c
    mask_c = (delta >= 0) & (delta < WINDOW)
    tok_n = lax.broadcasted_iota(jnp.int32, (4 * rows, rows), 0) % rows
    col_n = lax.broadcasted_iota(jnp.int32, (4 * rows, rows), 1)
    mask_n = (tok_n // ds == col_n // ds) & (col_n % ds <= tok_n % ds)

    outs = []
    for heads, rolled in ((_NAT_HEADS, False), (_ROL_HEADS, True)):
        arrange = (lambda a: pltpu.roll(a, HEAD_DIM, axis=1)) if rolled else (lambda a: a)
        qst = _stack_heads(q, heads)
        s_new = jnp.where(mask_n, _dot_nt(qst, arrange(k_new)), NEG_BIG)
        s_cache = _from_per_seq(
            [_dot_nt(_per_seq(qst, rows, ds, s), arrange(ck_ref[s])) for s in range(nseq)], ds)
        s_cache = jnp.where(mask_c, s_cache, NEG_BIG)
        sink = _sink_column(sinks_ref, heads, rows)
        m = jnp.maximum(jnp.maximum(jnp.max(s_cache, axis=-1, keepdims=True),
                                    jnp.max(s_new, axis=-1, keepdims=True)), sink)
        p_cache = jnp.exp(s_cache - m)
        p_new = jnp.exp(s_new - m)
        denom = (jnp.sum(p_cache, axis=-1, keepdims=True) + jnp.sum(p_new, axis=-1, keepdims=True)
                 + jnp.exp(sink - m))
        o = _from_per_seq(
            [_dot(_per_seq(p_cache, rows, ds, s), arrange(cv_ref[s])) for s in range(nseq)], ds)
        outs.append((o + _dot(p_new, arrange(v_new))) / denom)
    o_nat, o_rol = outs
    lo = _lane_lo((rows, LANES))
    blk = lambda o, n: o[n * rows:(n + 1) * rows]
    return jnp.concatenate([
        jnp.where(lo, blk(o_nat, 0), blk(o_rol, 0)),
        jnp.where(lo, blk(o_nat, 1), blk(o_rol, 1)),
        jnp.where(lo, blk(o_rol, 2), blk(o_nat, 2)),
        jnp.where(lo, blk(o_rol, 3), blk(o_nat, 3)),
    ], axis=1)


def _sample_retention(rq, rk, rv, rg, st_ref, rend_ref, decay_ref, qdec, kdec, gpow, ds):
    rows = rq.shape[0]
    nseq = st_ref.shape[0]
    lo = _lane_lo((rows, LANES))
    rkd = rk * kdec
    outs = []
    for m in range(RET_HEADS // 2):
        sl = slice(m * LANES, (m + 1) * LANES)
        heads = (2 * m, 2 * m + 1)
        rqm = [jnp.where(lo if p == 0 else ~lo, rq[:, sl], 0.0) for p in range(2)]
        rkdm = [jnp.where(lo if p == 0 else ~lo, rkd[:, sl], 0.0) for p in range(2)]
        rv_h = [rv[:, h * RET_DV:(h + 1) * RET_DV] for h in heads]
        q2 = jnp.concatenate(rqm, axis=0)
        k2 = jnp.concatenate(rkdm, axis=0)
        v2 = jnp.concatenate(rv_h, axis=0)
        inter = _from_per_seq(
            [_dot(_per_seq(q2, rows, ds, s), st_ref[s, m]) for s in range(nseq)], ds)
        for s in range(nseq):
            rend_ref[s, m] = (st_ref[s, m] * gpow[:, m:m + 1]
                              + _dot_tn(_per_seq(k2, rows, ds, s), _per_seq(v2, rows, ds, s)))
        for p, h in enumerate(heads):
            intra = _dot(_dot_nt(rqm[p], rk[:, sl]) * decay_ref[h], rv_h[p])
            ret = intra + inter[p * rows:(p + 1) * rows] * qdec[:, h * RET_DV:(h + 1) * RET_DV]
            rg_h = rg[:, h * RET_DV:(h + 1) * RET_DV]
            outs.append(rg_h * jax.nn.sigmoid(rg_h) * _rms_rows(ret))
    return jnp.concatenate(outs, axis=1)


def _sample_mix_kernel(sinks_ref, x_ref, mod_ref, gmix_ref, gffn_ref, w_in_ref, qg_ref, kg_ref,
                       bdq_ref, bdk_ref, cosa_ref, sina_ref, cosr_ref, sinr_ref,
                       decay_ref, qdec_ref, kdec_ref, gpow_ref, w_out_ref,
                       wrh_ref, wrl_ref, br_ref, ck_ref, cv_ref, st_ref,
                       x1_ref, h2_ref, topi_ref, topw_ref, topit_ref, cnt_ref, kwin_ref, vwin_ref, rend_ref,
                       mix):
    nseq, cache_w = ck_ref.shape[0], ck_ref.shape[1]
    ds = x_ref.shape[0] // nseq
    d = x_ref.shape[1]
    x = x_ref[...]
    mod = lambda i: _repeat_rows(mod_ref[:, i * d:(i + 1) * d], ds)
    h = _pre_norm(x, gmix_ref[...], mod(0), mod(1))
    q, k, v, rq, rk, rv, rg = _project(
        h, w_in_ref, qg_ref[...], kg_ref[...], bdq_ref[...], bdk_ref[...],
        cosa_ref[...], sina_ref[...], cosr_ref[...], sinr_ref[...])

    for s in range(nseq):
        rs = slice(s * ds, (s + 1) * ds)
        kwin_ref[s] = jnp.concatenate([ck_ref[s, ds:, :], k[rs]], axis=0)
        vwin_ref[s] = jnp.concatenate([cv_ref[s, ds:, :], v[rs]], axis=0)
    mix[:, 0:ATT_W] = _bf(_sample_attention(q, k, v, ck_ref, cv_ref, sinks_ref, ds))
    mix[:, ATT_W:ATT_W + RET_W] = _bf(_sample_retention(
        rq, rk, rv, rg, st_ref, rend_ref, decay_ref, qdec_ref[...], kdec_ref[...], gpow_ref[...], ds))

    x1 = x + mod(2) * jnp.dot(mix[...], w_out_ref[...], preferred_element_type=jnp.float32)
    x1_ref[...] = x1
    h2 = _pre_norm(x1, gffn_ref[...], mod(3), mod(4))
    h2_ref[...] = h2
    topi, topw, topi_t, counts = _route(h2, wrh_ref, wrl_ref, br_ref[...])
    topi_ref[...] = topi
    topw_ref[...] = topw
    topit_ref[...] = topi_t
    cnt_ref[0] = counts


def _sample_mix(x, mod_seq, sinks, gmix, gffn, w_in, qg, kg, bdq, bdk, tabs, ret_tabs, w_out,
                wrh, wrl, br, cache_k, cache_v, state):
    t, d = x.shape
    nb, cache_w = cache_k.shape[0], cache_k.shape[1]
    ds = t // nb
    g = SAMPLE_SEQS
    tb = g * ds
    cosa, sina, cosr, sinr = tabs
    decay, qdec, kdec, gpow = ret_tabs
    row_spec = lambda w: pl.BlockSpec((tb, w), lambda i, *_: (i, 0))
    seq3 = lambda a: pl.BlockSpec((g,) + a.shape[1:], lambda i, *_: (i,) + (0,) * (a.ndim - 1))
    consts = [gmix, gffn, w_in, qg, kg, bdq, bdk, cosa, sina, cosr, sinr,
              decay, qdec, kdec, gpow, w_out, wrh, wrl, br]
    grid_spec = pltpu.PrefetchScalarGridSpec(
        num_scalar_prefetch=1,
        grid=(nb // g,),
        in_specs=([row_spec(d), pl.BlockSpec((g, mod_seq.shape[1]), lambda i, *_: (i, 0))]
                  + [_const_spec(a.shape) for a in consts]
                  + [seq3(cache_k), seq3(cache_v), seq3(state)]),
        out_specs=[row_spec(d), row_spec(d), row_spec(LANES), row_spec(LANES),
                   pl.BlockSpec((SUBLANES, tb), lambda i, *_: (0, i)),
                   pl.BlockSpec((1, N_EXPERTS, LANES), lambda i, *_: (i, 0, 0)),
                   seq3(cache_k), seq3(cache_v), seq3(state)],
        scratch_shapes=[pltpu.VMEM((tb, ATT_W + RET_W), jnp.bfloat16)],
    )
    return pl.pallas_call(
        _sample_mix_kernel,
        grid_spec=grid_spec,
        out_shape=[jax.ShapeDtypeStruct((t, d), jnp.float32),
                   jax.ShapeDtypeStruct((t, d), jnp.float32),
                   jax.ShapeDtypeStruct((t, LANES), jnp.int32),
                   jax.ShapeDtypeStruct((t, LANES), jnp.float32),
                   jax.ShapeDtypeStruct((SUBLANES, t), jnp.int32),
                   jax.ShapeDtypeStruct((t // tb, N_EXPERTS, LANES), jnp.int32),
                   jax.ShapeDtypeStruct(cache_k.shape, jnp.float32),
                   jax.ShapeDtypeStruct(cache_v.shape, jnp.float32),
                   jax.ShapeDtypeStruct(state.shape, jnp.float32)],
        compiler_params=pltpu.CompilerParams(
            dimension_semantics=("arbitrary",), vmem_limit_bytes=VMEM_LIMIT_BYTES),
    )(sinks, x, mod_seq, *consts, cache_k, cache_v, state)


def _strict_triangle(n, upper):
    tri = np.triu(np.ones((n, n)), 1) if upper else np.tril(np.ones((n, n)), -1)
    return jnp.asarray(tri, jnp.bfloat16)


def _chunk_copies(cnt8, src_ref, src_start, dst_ref, dst_start, sem):
    for bit in range(CHUNK_BITS - 1, SUBLANES.bit_length() - 2, -1):
        size = 1 << bit
        done = (cnt8 >> (bit + 1)) << (bit + 1)

        @pl.when(((cnt8 >> bit) & 1) == 1)
        def _():
            src = pl.multiple_of(src_start + done, SUBLANES)
            dst = pl.multiple_of(dst_start + done, SUBLANES)
            pltpu.make_async_copy(src_ref.at[pl.ds(src, size)],
                                  dst_ref.at[pl.ds(dst, size)], sem).start()


def _wait_rows(ref, rows8, sem, bits=STAGE_BITS):
    for bit in range(bits - 1, SUBLANES.bit_length() - 2, -1):
        size = 1 << bit

        @pl.when(((rows8 >> bit) & 1) == 1)
        def _():
            pltpu.make_async_copy(ref.at[pl.ds(0, size)], ref.at[pl.ds(0, size)], sem).wait()


def _dispatch_kernel(cnt_ref, gstart_ref, cb_ref, blkrows_ref, ends_ref, padcnt_ref,
                     topit_p_ref, topit_s_ref, h2_p_ref, h2_s_ref, cbcol_ref, earlier_ref, xs_ref,
                     stage, zeros, sems, zsem, *, nblk_p):
    step = pl.program_id(0)
    nblk = pl.num_programs(0)
    tb = h2_p_ref.shape[0]
    slot = step % STAGE_SLOTS

    tile = zeros.shape[0]
    first_unused = ends_ref[N_EXPERTS - 1] // tile
    n_tiles = xs_ref.shape[0] // tile

    def tail_fill(t):
        return pltpu.make_async_copy(zeros, xs_ref.at[pl.ds(pl.multiple_of(t * tile, tile), tile)],
                                     zsem)

    @pl.when(step == 0)
    def _():
        zeros[...] = jnp.zeros_like(zeros)

        def pad_fill(e, carry):
            _chunk_copies(padcnt_ref[e], zeros, 0, xs_ref, ends_ref[e] - padcnt_ref[e], zsem)
            return carry

        lax.fori_loop(0, N_EXPERTS, pad_fill, 0)
        lax.fori_loop(first_unused, n_tiles, lambda t, c: (tail_fill(t).start(), c)[1], 0)

    @pl.when(step == nblk - 1)
    def _():
        _wait_rows(xs_ref, padcnt_ref[N_EXPERTS], zsem, bits=(N_EXPERTS * tile).bit_length())
        lax.fori_loop(first_unused, n_tiles, lambda t, c: (tail_fill(t).wait(), c)[1], 0)

    def issue_chunk(blk, e):
        i = (blk + 1) * N_EXPERTS + e
        _chunk_copies(cnt_ref[i], stage.at[blk % STAGE_SLOTS], cb_ref[i], xs_ref, gstart_ref[i],
                      sems.at[blk % STAGE_SLOTS])

    for e in range(N_EXPERTS):
        issue_chunk(step - 1, e)

    is_p = step < nblk_p
    topit = jnp.where(is_p, topit_p_ref[...], topit_s_ref[...])
    h2 = jnp.where(is_p, h2_p_ref[...], h2_s_ref[...])
    cbcol = cbcol_ref[0]

    sub = lax.broadcasted_iota(jnp.int32, (LANES, tb), 0)
    picks = [sub == topit[k:k + 1, :] for k in range(TOP_K)]
    onehot = jnp.zeros((LANES, tb), jnp.float32)
    for pk in picks:
        onehot = jnp.where(pk, 1.0, onehot)
    before = jnp.dot(_bf(onehot), earlier_ref[...], preferred_element_type=jnp.float32)
    base = before + cbcol.astype(jnp.float32)
    row = lax.broadcasted_iota(jnp.int32, (stage.shape[1], tb), 0)
    sel = jnp.zeros((stage.shape[1], tb), jnp.float32)
    for pk in picks:
        col = jnp.sum(jnp.where(pk, base, 0.0), axis=0, keepdims=True).astype(jnp.int32)
        sel = jnp.where(row == col, 1.0, sel)

    stage[slot] = _dot(sel, h2)

    @pl.when(step == nblk - 1)
    def _():
        lax.fori_loop(0, N_EXPERTS, lambda e, c: (issue_chunk(step, e), c)[1], 0)

    oldest = step - (STAGE_SLOTS - 1)

    @pl.when(oldest >= 0)
    def _():
        _wait_rows(xs_ref, blkrows_ref[jnp.maximum(oldest, 0)], sems.at[(step + 1) % STAGE_SLOTS])

    @pl.when(step == nblk - 1)
    def _():
        for back in range(STAGE_SLOTS - 2, -1, -1):
            @pl.when(step - back >= 0)
            def _():
                blk = jnp.maximum(step - back, 0)
                _wait_rows(xs_ref, blkrows_ref[blk], sems.at[blk % STAGE_SLOTS])


def _dispatch(tables, topit_p, topit_s, h2_p, h2_s, cbcol, n_rows):
    n_p, d = h2_p.shape
    n_s = h2_s.shape[0]
    tb = ROUTE_BLOCK
    nblk_p, nblk_s = n_p // tb, n_s // tb
    pidx = lambda i, *_: jnp.minimum(i, nblk_p - 1)
    sidx = lambda i, *_: jnp.maximum(i - nblk_p, 0)
    grid_spec = pltpu.PrefetchScalarGridSpec(
        num_scalar_prefetch=6,
        grid=(nblk_p + nblk_s,),
        in_specs=[pl.BlockSpec((SUBLANES, tb), lambda i, *_: (0, pidx(i))),
                  pl.BlockSpec((SUBLANES, tb), lambda i, *_: (0, sidx(i))),
                  pl.BlockSpec((tb, d), lambda i, *_: (pidx(i), 0)),
                  pl.BlockSpec((tb, d), lambda i, *_: (sidx(i), 0)),
                  pl.BlockSpec((1, LANES, 1), lambda i, *_: (i, 0, 0)),
                  pl.BlockSpec((tb, tb), lambda i, *_: (0, 0))],
        out_specs=pl.BlockSpec(memory_space=pl.ANY),
        scratch_shapes=[pltpu.VMEM((STAGE_SLOTS, STAGE_ROWS, d), jnp.float32),
                        pltpu.VMEM((EXPERT_TILE, d), jnp.float32),
                        pltpu.SemaphoreType.DMA((STAGE_SLOTS,)),
                        pltpu.SemaphoreType.DMA],
    )
    return pl.pallas_call(
        functools.partial(_dispatch_kernel, nblk_p=nblk_p),
        grid_spec=grid_spec,
        out_shape=jax.ShapeDtypeStruct((n_rows, d), jnp.float32),
        compiler_params=pltpu.CompilerParams(
            dimension_semantics=("arbitrary",), vmem_limit_bytes=VMEM_LIMIT_BYTES),
    )(*tables, topit_p, topit_s, h2_p, h2_s, cbcol, _strict_triangle(tb, upper=True))


def _expert_mlp(x, wgu_bf, bgu, wd_bf, bd):
    d_ff = wd_bf.shape[0]
    gu = jnp.dot(_bf(x), wgu_bf[...], preferred_element_type=jnp.float32) + bgu
    glu = jnp.minimum(gu[:, :d_ff], SWIGLU_LIMIT)
    lin = jnp.clip(gu[:, d_ff:], -SWIGLU_LIMIT, SWIGLU_LIMIT)
    act = glu * jax.nn.sigmoid(SWIGLU_ALPHA * glu) * (lin + 1.0)
    return jnp.dot(_bf(act), wd_bf[...], preferred_element_type=jnp.float32) + bd


def _experts_kernel(tile_expert_ref, n_used_ref, groups_ref, next_ref, slot_ref,
                    x_ref, wgu_hbm, bgu_ref, wd_hbm, bd_ref, y_ref,
                    wgu_f32, wd_f32, wgu_bf, wd_bf, sems):
    step = pl.program_id(0)
    groups = groups_ref[step]

    def weight_copies(expert, slot):
        return (pltpu.make_async_copy(wgu_hbm.at[expert], wgu_f32.at[slot], sems.at[slot, 0]),
                pltpu.make_async_copy(wd_hbm.at[expert], wd_f32.at[slot], sems.at[slot, 1]))

    @pl.when(groups > 0)
    def _():
        e = tile_expert_ref[step]
        prev = tile_expert_ref[jnp.maximum(step - 1, 0)]
        slot = slot_ref[e]

        @pl.when(step == 0)
        def _():
            for cp in weight_copies(e, slot):
                cp.start()

        @pl.when(jnp.logical_or(step == 0, e != prev))
        def _():
            for cp in weight_copies(e, slot):
                cp.wait()
            nxt = next_ref[e]

            @pl.when(nxt >= 0)
            def _():
                for cp in weight_copies(nxt, 1 - slot):
                    cp.start()

            wgu_bf[...] = _bf(wgu_f32[slot])
            wd_bf[...] = _bf(wd_f32[slot])

    tm = x_ref.shape[0]
    for n in range(1, EXPERT_GROUPS + 1):
        @pl.when(groups == n)
        def _():
            rows = n * (tm // EXPERT_GROUPS)
            y_ref[0:rows, :] = _expert_mlp(x_ref[0:rows, :], wgu_bf, bgu_ref[0], wd_bf, bd_ref[0])
            if rows < tm:
                y_ref[rows:, :] = jnp.zeros((tm - rows, y_ref.shape[1]), y_ref.dtype)

    @pl.when(groups == 0)
    def _():
        y_ref[...] = jnp.zeros_like(y_ref)


def _experts(tile_expert, n_used, tile_groups, next_used, slot_of, x_sorted, w_gate_up, b_gate_up,
             w_down, b_down):
    p = x_sorted.shape[0]
    n_e, d, two_ff = w_gate_up.shape
    d_ff = two_ff // 2
    tm = EXPERT_TILE
    tile = lambda i, te, nu: jnp.minimum(i, nu[0] - 1)
    grid_spec = pltpu.PrefetchScalarGridSpec(
        num_scalar_prefetch=5,
        grid=(p // tm,),
        in_specs=[pl.BlockSpec((tm, d), lambda i, te, nu, *_: (tile(i, te, nu), 0)),
                  pl.BlockSpec(memory_space=pl.ANY),
                  pl.BlockSpec((1, 1, two_ff), lambda i, te, *_: (te[i], 0, 0)),
                  pl.BlockSpec(memory_space=pl.ANY),
                  pl.BlockSpec((1, 1, d), lambda i, te, *_: (te[i], 0, 0))],
        out_specs=pl.BlockSpec((tm, d), lambda i, *_: (i, 0)),
        scratch_shapes=[pltpu.VMEM((2, d, two_ff), jnp.float32),
                        pltpu.VMEM((2, d_ff, d), jnp.float32),
                        pltpu.VMEM((d, two_ff), jnp.bfloat16),
                        pltpu.VMEM((d_ff, d), jnp.bfloat16),
                        pltpu.SemaphoreType.DMA((2, 2))],
    )
    return pl.pallas_call(
        _experts_kernel,
        grid_spec=grid_spec,
        out_shape=jax.ShapeDtypeStruct(x_sorted.shape, jnp.float32),
        compiler_params=pltpu.CompilerParams(
            dimension_semantics=("arbitrary",), vmem_limit_bytes=VMEM_LIMIT_BYTES),
    )(tile_expert, n_used, tile_groups, next_used, slot_of, x_sorted, w_gate_up,
      b_gate_up.reshape(n_e, 1, two_ff), w_down, b_down.reshape(n_e, 1, d))


def _combine_kernel(cnt_ref, gstart_ref, cb_ref, blkrows_ref,
                    topi_p_ref, topi_s_ref, topw_p_ref, topw_s_ref, x1_p_ref, x1_s_ref,
                    gate_p_ref, gate_s_ref, cbrow_ref, earlier_ref, y_ref, out_p_ref, out_s_ref,
                    ybuf, sems, *, nblk_p):
    step = pl.program_id(0)
    nblk = pl.num_programs(0)
    tb = x1_p_ref.shape[0]
    slot = step % STAGE_SLOTS

    def fetch_chunk(blk, into, e):
        i = blk * N_EXPERTS + e
        _chunk_copies(cnt_ref[i], y_ref, gstart_ref[i], ybuf.at[into], cb_ref[i], sems.at[into])

    def fetch(blk, into):
        lax.fori_loop(0, N_EXPERTS, lambda e, c: (fetch_chunk(blk, into, e), c)[1], 0)

    @pl.when(step == 0)
    def _():
        ybuf[...] = jnp.zeros_like(ybuf)
        for ahead in range(STAGE_SLOTS - 1):
            @pl.when(ahead < nblk)
            def _():
                fetch(ahead, ahead)

    _wait_rows(y_ref, blkrows_ref[step], sems.at[slot])

    for e in range(N_EXPERTS):
        fetch_chunk(step + STAGE_SLOTS - 1, (step + STAGE_SLOTS - 1) % STAGE_SLOTS, e)

    is_p = step < nblk_p
    topi = jnp.where(is_p, topi_p_ref[...], topi_s_ref[...])
    topw = jnp.where(is_p, topw_p_ref[...], topw_s_ref[...])
    cbrow = cbrow_ref[0]

    lane = lax.broadcasted_iota(jnp.int32, (tb, LANES), 1)
    picks = []
    onehot = jnp.zeros((tb, LANES), jnp.float32)
    for k in range(TOP_K):
        ix = jnp.sum(jnp.where(lane == k, topi, 0), axis=-1, keepdims=True)
        picks.append(lane == ix)
        onehot = jnp.where(picks[k], 1.0, onehot)
    base = (jnp.dot(earlier_ref[...], _bf(onehot), preferred_element_type=jnp.float32)
            + cbrow.astype(jnp.float32))
    cols = [jnp.sum(jnp.where(picks[k], base, 0.0), axis=-1, keepdims=True).astype(jnp.int32)
            for k in range(TOP_K)]
    wks = [jnp.sum(jnp.where(lane == k, topw, 0.0), axis=-1, keepdims=True) for k in range(TOP_K)]
    moe = None
    for c0 in range(0, ybuf.shape[1], COMBINE_SLAB):
        colid = lax.broadcasted_iota(jnp.int32, (tb, COMBINE_SLAB), 1) + c0
        weights = jnp.zeros((tb, COMBINE_SLAB), jnp.float32)
        for k in range(TOP_K):
            weights = jnp.where(colid == cols[k], wks[k], weights)
        part = _dot(weights, ybuf[slot, c0:c0 + COMBINE_SLAB, :])
        moe = part if moe is None else moe + part

    @pl.when(is_p)
    def _():
        out_p_ref[...] = x1_p_ref[...] + gate_p_ref[...] * moe

    @pl.when(jnp.logical_not(is_p))
    def _():
        gate_s = _repeat_rows(gate_s_ref[...], tb // gate_s_ref.shape[0])
        out_s_ref[...] = x1_s_ref[...] + gate_s * moe


def _combine(tables, topi_p, topi_s, topw_p, topw_s, x1_p, x1_s, gate_p, mod_seq, dec_seq, cbrow,
             y_sorted):
    n_p, d = x1_p.shape
    n_s = x1_s.shape[0]
    tb = ROUTE_BLOCK
    nblk_p, nblk_s = n_p // tb, n_s // tb
    pidx = lambda i, *_: jnp.minimum(i, nblk_p - 1)
    sidx = lambda i, *_: jnp.maximum(i - nblk_p, 0)
    prow = lambda w: pl.BlockSpec((tb, w), lambda i, *_: (pidx(i), 0))
    srow = lambda w: pl.BlockSpec((tb, w), lambda i, *_: (sidx(i), 0))
    grid_spec = pltpu.PrefetchScalarGridSpec(
        num_scalar_prefetch=4,
        grid=(nblk_p + nblk_s,),
        in_specs=[prow(LANES), srow(LANES), prow(LANES), srow(LANES), prow(d), srow(d),
                  pl.BlockSpec((1, d), lambda i, *_: (0, 0)),
                  pl.BlockSpec((tb // dec_seq, d), lambda i, *_: (sidx(i), 5)),
                  pl.BlockSpec((1, 1, LANES), lambda i, *_: (i, 0, 0)),
                  pl.BlockSpec((tb, tb), lambda i, *_: (0, 0)),
                  pl.BlockSpec(memory_space=pl.ANY)],
        out_specs=[prow(d), srow(d)],
        scratch_shapes=[pltpu.VMEM((STAGE_SLOTS, STAGE_ROWS, d), jnp.float32),
                        pltpu.SemaphoreType.DMA((STAGE_SLOTS,))],
    )
    return pl.pallas_call(
        functools.partial(_combine_kernel, nblk_p=nblk_p),
        grid_spec=grid_spec,
        out_shape=[jax.ShapeDtypeStruct((n_p, d), jnp.float32),
                   jax.ShapeDtypeStruct((n_s, d), jnp.float32)],
        compiler_params=pltpu.CompilerParams(
            dimension_semantics=("arbitrary",), vmem_limit_bytes=VMEM_LIMIT_BYTES),
    )(*tables, topi_p, topi_s, topw_p, topw_s, x1_p, x1_s, gate_p, mod_seq, cbrow,
      _strict_triangle(tb, upper=False), y_sorted)


def _rotary_tables(pos, inv_freq):
    ang = np.asarray(pos, np.float64)[:, None] * inv_freq[None, :]
    cos, sin = np.cos(ang), np.sin(ang)
    cos_t = np.tile(cos, (1, LANES // cos.shape[1]))
    sin_t = np.tile(np.concatenate([-sin, sin], axis=1), (1, LANES // (2 * sin.shape[1])))
    return cos_t.astype(np.float32), sin_t.astype(np.float32)


def _rotary_split_tables(block, nblk, inv_freq):
    half = inv_freq.shape[0]
    freq = np.tile(inv_freq, LANES // half)
    sign = np.tile(np.concatenate([-np.ones(half), np.ones(half)]), LANES // (2 * half))
    base = (np.arange(nblk) * block).astype(np.float64)[:, None] * freq[None, :]
    off = np.arange(block).astype(np.float64)[:, None] * freq[None, :]
    blk = np.stack([np.cos(base), np.sin(base)], axis=1)
    tab = np.stack([np.cos(off), np.sin(off), np.cos(off) * sign, np.sin(off) * sign])
    return blk.astype(np.float32), tab.astype(np.float32)


def _retention_tables(chunk, nseq=1):
    log_gamma = np.log1p(-np.exp2(-5.0 - np.arange(RET_HEADS, dtype=np.float64)))
    idx = np.arange(chunk, dtype=np.float64)
    diff = idx[:, None] - idx[None, :]
    decay = np.where(diff[None] >= 0,
                     np.exp(np.maximum(diff, 0.0)[None] * log_gamma[:, None, None]), 0.0)
    decay = np.stack([np.kron(np.eye(nseq), decay[h]) for h in range(RET_HEADS)])
    q_decay = np.exp((idx + 1.0)[:, None] * log_gamma[None, :])
    k_decay = np.exp((chunk - 1.0 - idx)[:, None] * log_gamma[None, :])
    qdec = np.tile(np.repeat(q_decay, RET_DV, axis=1), (nseq, 1))
    kdec = np.tile(np.repeat(k_decay, RET_DK, axis=1), (nseq, 1))
    g_chunk = np.exp(chunk * log_gamma)
    gpow = np.repeat(g_chunk.reshape(RET_HEADS // 2, 2), RET_DK, axis=1).T
    return tuple(a.astype(np.float32) for a in (decay, qdec, kdec, gpow))


def _block_diag_mean(width):
    head = np.arange(width) // HEAD_DIM
    return jnp.asarray(np.where(head[:, None] == head[None, :], 1.0 / HEAD_DIM, 0.0), jnp.bfloat16)


def kernel(x_prompt, x_sample, cache_k, cache_v, state_ret, c_prompt, c_sample, w_ada, b_ada,
           g_mix, w_in, q_gain, k_gain, sinks, w_out, g_ffn, w_router, b_router, w_gate_up,
           b_gate_up, w_down, b_down):
    depth = w_ada.shape[0]
    batch, seq, d = x_prompt.shape
    dec_batch, dec_seq, _ = x_sample.shape
    cache_w = cache_k.shape[2]
    assert batch == 1 and depth == 1
    assert seq % PROMPT_BLOCK == 0 and PROMPT_BLOCK % WINDOW == 0 and WINDOW == RET_CHUNK
    assert dec_batch % SAMPLE_SEQS == 0 and cache_w == WINDOW
    n_p, n_s = batch * seq, dec_batch * dec_seq
    n_tok = n_p + n_s
    assert n_p % ROUTE_BLOCK == 0 and n_s % ROUTE_BLOCK == 0
    assert ROUTE_BLOCK % COUNT_BLOCK == 0 and (SAMPLE_SEQS * dec_seq) % COUNT_BLOCK == 0
    assert PROMPT_BLOCK % COUNT_BLOCK == 0
    assert STAGE_ROWS % COMBINE_SLAB == 0 and EXPERT_TILE % (EXPERT_GROUPS * SUBLANES) == 0

    l = 0
    f32 = jnp.float32
    rope_freq = 1.0 / (ROPE_THETA ** (np.arange(0, HEAD_DIM, 2, dtype=np.float64) / HEAD_DIM))
    ret_freq = 1.0 / (ROPE_THETA ** np.linspace(0.0, 1.0, RET_DK // 2))
    pos_s = PAST_LEN + np.arange(dec_seq)
    blk_a, tab_a = _rotary_split_tables(PROMPT_BLOCK, seq // PROMPT_BLOCK, rope_freq)
    blk_r, tab_r = _rotary_split_tables(PROMPT_BLOCK, seq // PROMPT_BLOCK, ret_freq)
    tabs_p = (np.concatenate([blk_a, blk_r], axis=1), np.concatenate([tab_a, tab_r], axis=0))
    tabs_s = tuple(np.tile(a, (SAMPLE_SEQS, 1))
                   for a in _rotary_tables(pos_s, rope_freq) + _rotary_tables(pos_s, ret_freq))
    ret_p = _retention_tables(RET_CHUNK)
    ret_s = _retention_tables(dec_seq, SAMPLE_SEQS)

    n_c = batch + dec_batch
    c_rows = -(-n_c // SUBLANES) * SUBLANES
    c_all = jnp.concatenate([c_sample, c_prompt, jnp.zeros((c_rows - n_c, d), f32)], axis=0)
    mod = _adaln(c_all, w_ada[l], b_ada[l])
    mod_p = mod[dec_batch].reshape(6, d)

    w_in_bf = _bf(w_in[l])
    w_out_bf = _bf(w_out[l])
    qg = jnp.tile(q_gain[l], ATT_HEADS).reshape(1, ATT_W)
    kg = jnp.tile(k_gain[l], ATT_KV_HEADS).reshape(1, KV_W)
    bdq, bdk = _block_diag_mean(ATT_W), _block_diag_mean(KV_W)
    gmix = g_mix[l].reshape(1, d)
    gffn = g_ffn[l].reshape(1, d)
    wr = jnp.pad(w_router[l].T, ((0, LANES - N_EXPERTS), (0, 0)))
    wr_hi = _bf(wr)
    wr_cat = jnp.concatenate([wr_hi, _bf(wr - wr_hi.astype(f32))], axis=0)
    br = b_router[l].reshape(N_EXPERTS, 1)
    shared = (sinks[l], gmix, gffn, w_in_bf, qg, kg, bdq, bdk)

    (x1_p, h2_p, topi_p, topw_p, topit_p, cnt_p, kwin_p, vwin_p, rend_p) = _prompt_mix(
        x_prompt.reshape(n_p, d), mod_p, *shared, tabs_p, ret_p, w_out_bf, wr_hi, wr_cat, br)

    (x1_s, h2_s, topi_s, topw_s, topit_s, cnt_s, kwin_s, vwin_s, rend_s) = _sample_mix(
        x_sample.reshape(n_s, d), mod, *shared, tabs_s, ret_s, w_out_bf, wr_hi, wr_cat, br,
        cache_k[l].reshape(dec_batch, cache_w, KV_W), cache_v[l].reshape(dec_batch, cache_w, KV_W),
        state_ret[l].reshape(dec_batch, 2, LANES, RET_DV))

    tb, tm = ROUTE_BLOCK, EXPERT_TILE
    i32 = jnp.int32
    per_group = lambda c, step_rows: jnp.swapaxes(c[:, :, :step_rows // COUNT_BLOCK], 1, 2).reshape(
        -1, N_EXPERTS)
    cnt = jnp.concatenate([per_group(cnt_p, PROMPT_BLOCK), per_group(cnt_s, SAMPLE_SEQS * dec_seq)])
    cnt = cnt.reshape(n_tok // tb, tb // COUNT_BLOCK, N_EXPERTS).sum(axis=1)
    cnt = jnp.pad(cnt, ((0, 0), (0, LANES - N_EXPERTS)))
    cnt8 = -(-cnt // SUBLANES) * SUBLANES
    rs = jnp.cumsum(cnt8, axis=0) - cnt8
    tot = jnp.sum(cnt8, axis=0)[:N_EXPERTS]
    padded = -(-tot // tm) * tm
    ends = jnp.cumsum(padded)
    gstart = (ends - padded)[None, :] + rs[:, :N_EXPERTS]
    cb = jnp.cumsum(cnt8, axis=1) - cnt8
    blk_rows = jnp.sum(cnt8, axis=1).astype(i32)
    max_rows = n_tok * TOP_K + cnt.shape[0] * N_EXPERTS * (SUBLANES - 1)
    n_rows = -(-max_rows // tm) * tm + N_EXPERTS * tm
    tile_start = jnp.arange(n_rows // tm, dtype=i32) * tm
    tile_expert = jnp.minimum(jnp.sum(ends[None, :] <= tile_start[:, None], axis=1),
                              N_EXPERTS - 1).astype(i32)
    n_used = (ends[-1:] // tm).astype(i32)
    eid = jnp.arange(N_EXPERTS, dtype=i32)
    of_tile = lambda v: jnp.sum(jnp.where(tile_expert[:, None] == eid[None, :], v[None, :], 0), axis=1)
    tile_rows = jnp.clip(of_tile(tot) - (tile_start - of_tile(ends - padded)), 0, tm)
    tile_rows = jnp.where(tile_start < ends[-1], tile_rows, 0)
    tile_groups = (-(-tile_rows // (tm // EXPERT_GROUPS))).astype(i32)
    flat = lambda a: a[:, :N_EXPERTS].reshape(-1).astype(i32)
    tables = (flat(cnt8), gstart.reshape(-1).astype(i32), flat(cb), blk_rows)
    ahead = jnp.zeros(((STAGE_SLOTS - 1) * N_EXPERTS,), i32)
    tables_c = tuple(jnp.concatenate([t, ahead]) for t in tables[:3]) + tables[3:]

    pad_cnt = padded - tot
    pad_cnt = jnp.concatenate([pad_cnt, jnp.sum(pad_cnt, keepdims=True)]).astype(i32)
    behind = jnp.zeros((N_EXPERTS,), i32)
    tables_d = tuple(jnp.concatenate([behind, t]) for t in tables[:3]) + tables[3:]
    x_sorted = _dispatch(tables_d + (ends.astype(i32), pad_cnt), topit_p, topit_s, h2_p, h2_s,
                         cb.astype(i32)[:, :, None], n_rows)
    used = tot > 0
    later = jnp.where(used[None, :] & (eid[None, :] > eid[:, None]), eid[None, :], N_EXPERTS)
    next_used = jnp.min(later, axis=1)
    next_used = jnp.where(next_used == N_EXPERTS, -1, next_used).astype(i32)
    slot_of = ((jnp.cumsum(used.astype(i32)) - 1) % 2).astype(i32)
    y_sorted = _experts(tile_expert, n_used, tile_groups, next_used, slot_of, x_sorted, w_gate_up[l], b_gate_up[l], w_down[l],
                        b_down[l])
    y_p, y_s = _combine(tables_c, topi_p, topi_s, topw_p, topw_s, x1_p, x1_s, mod_p[5:6], mod,
                        dec_seq, cb.astype(i32)[:, None, :], y_sorted)

    kv5 = lambda a, n: a.reshape(1, n, cache_w, ATT_KV_HEADS, HEAD_DIM)
    st5 = lambda a, n: a.reshape(1, n, RET_HEADS, RET_DK, RET_DV)
    return (y_p.reshape(batch, seq, d), y_s.reshape(dec_batch, dec_seq, d),
            kv5(kwin_p, batch), kv5(vwin_p, batch), st5(rend_p, batch),
            kv5(kwin_s, dec_batch), kv5(vwin_s, dec_batch), st5(rend_s, dec_batch))
```

```python
import functools

import jax
import jax.numpy as jnp
import numpy as np
from jax import lax
from jax.experimental import pallas as pl
from jax.experimental.pallas import tpu as pltpu

HEAD_DIM = 64
ATT_HEADS = 8
ATT_KV_HEADS = 2
ATT_W = ATT_HEADS * HEAD_DIM
KV_W = ATT_KV_HEADS * HEAD_DIM
WINDOW = 128
PAST_LEN = 16384
ROPE_THETA = 10000.0
RET_HEADS = 4
RET_DK = 64
RET_DV = 128
RET_QK_W = RET_HEADS * RET_DK
RET_W = RET_HEADS * RET_DV
RET_CHUNK = 128
N_EXPERTS = 32
TOP_K = 4
SWIGLU_LIMIT = 7.0
SWIGLU_ALPHA = 1.702
EPS = 1e-6

_Q0, _K0, _V0 = 0, ATT_W, ATT_W + KV_W
_RQ0 = ATT_W + 2 * KV_W
_RK0 = _RQ0 + RET_QK_W
_RV0 = _RK0 + RET_QK_W
_RG0 = _RV0 + RET_W
IN_W = _RG0 + RET_W

LANES = 128
SUBLANES = 8
VMEM_LIMIT_BYTES = 56 * 1024 * 1024
NEG_BIG = float("-inf")

PROMPT_BLOCK = 512
SAMPLE_SEQS = 16
COUNT_BLOCK = 128
ROUTE_BLOCK = 512
EXPERT_TILE = 512
EXPERT_FF_SLICES = 2
EXPERT_GROUPS = 4
CHUNK_BITS = ROUTE_BLOCK.bit_length()
STAGE_ROWS = -(-(TOP_K * ROUTE_BLOCK + N_EXPERTS * (SUBLANES - 1)) // LANES) * LANES
STAGE_BITS = STAGE_ROWS.bit_length()
STAGE_SLOTS = 3
COMBINE_SLAB = 768

_NT = (((1,), (1,)), ((), ()))
_TN = (((0,), (0,)), ((), ()))


def _bf(x):
    return x.astype(jnp.bfloat16)


def _dot(a, b):
    return jnp.dot(_bf(a), _bf(b), preferred_element_type=jnp.float32)


def _dot_nt(a, b):
    return lax.dot_general(_bf(a), _bf(b), _NT, preferred_element_type=jnp.float32)


def _dot_tn(a, b):
    return lax.dot_general(_bf(a), _bf(b), _TN, preferred_element_type=jnp.float32)


def _lane_lo(shape):
    lane = lax.broadcasted_iota(jnp.int32, shape, len(shape) - 1)
    return (lane % LANES) < HEAD_DIM


def _swap_halves(x):
    lane = lax.broadcasted_iota(jnp.int32, x.shape, 1)
    first = (lane % HEAD_DIM) < (HEAD_DIM // 2)
    return jnp.where(first, pltpu.roll(x, LANES - HEAD_DIM // 2, axis=1),
                     pltpu.roll(x, HEAD_DIM // 2, axis=1))


def _rotate(x, cos, sin_signed):
    outs = []
    for j in range(x.shape[1] // LANES):
        xs = x[:, j * LANES:(j + 1) * LANES]
        outs.append(xs * cos + _swap_halves(xs) * sin_signed)
    return outs[0] if len(outs) == 1 else jnp.concatenate(outs, axis=1)


def _repeat_rows(m, n):
    r, w = m.shape
    return jnp.broadcast_to(m[:, None, :], (r, n, w)).reshape(r * n, w)


def _block_rotary(blk_ref, tab_ref, which):
    ca, sa = blk_ref[0, 2 * which:2 * which + 1, :], blk_ref[0, 2 * which + 1:2 * which + 2, :]
    cb, sb, cbs, sbs = (tab_ref[4 * which + n] for n in range(4))
    return ca * cb - sa * sb, sa * cbs + ca * sbs


def _rms_rows(x):
    return x * lax.rsqrt(jnp.mean(x * x, axis=-1, keepdims=True) + EPS)


def _pre_norm(x, gain, shift, scale):
    return _rms_rows(x) * gain * (1.0 + scale) + shift


def _head_norm(x, gain_tiled, blockdiag):
    ms = jnp.dot(_bf(x * x), blockdiag, preferred_element_type=jnp.float32)
    return x * lax.rsqrt(ms + EPS) * gain_tiled


def _project(h, w_in_ref, qg, kg, bd_q, bd_k, cos_a, sin_a, cos_r, sin_r):
    z = jnp.dot(_bf(h), w_in_ref[...], preferred_element_type=jnp.float32)
    q = _rotate(_head_norm(z[:, _Q0:_K0], qg, bd_q), cos_a, sin_a) * (HEAD_DIM ** -0.5)
    k = _rotate(_head_norm(z[:, _K0:_V0], kg, bd_k), cos_a, sin_a)
    v = z[:, _V0:_RQ0]
    rq = _rotate(z[:, _RQ0:_RK0], cos_r, sin_r)
    rk = _rotate(z[:, _RK0:_RV0], cos_r, sin_r) * (RET_DK ** -0.5)
    rv = z[:, _RV0:_RG0]
    rg = z[:, _RG0:IN_W]
    return q, k, v, rq, rk, rv, rg


_NAT_HEADS = (0, 2, 5, 7)
_ROL_HEADS = (1, 3, 4, 6)


def _stack_heads(q, heads):
    lo = _lane_lo((q.shape[0], LANES))
    parts = []
    for j in heads:
        slab = q[:, (j // 2) * LANES:(j // 2 + 1) * LANES]
        parts.append(jnp.where(lo if j % 2 == 0 else ~lo, slab, 0.0))
    return jnp.concatenate(parts, axis=0)


def _sink_column(sinks_ref, heads, rows):
    r = lax.broadcasted_iota(jnp.int32, (len(heads) * rows, 1), 0)
    col = jnp.full((len(heads) * rows, 1), sinks_ref[heads[-1]], jnp.float32)
    for n in range(len(heads) - 2, -1, -1):
        col = jnp.where(r < (n + 1) * rows, sinks_ref[heads[n]], col)
    return col


def _softmax_pv(s, mask, sink, v):
    s = jnp.where(mask, s, NEG_BIG)
    m = jnp.maximum(jnp.max(s, axis=-1, keepdims=True), sink)
    p = jnp.exp(s - m)
    denom = jnp.sum(p, axis=-1, keepdims=True) + jnp.exp(sink - m)
    return _dot(p, v) / denom


def _attention(q, k_all, v_all, mask, sinks_ref):
    rows = q.shape[0]
    k_rol = pltpu.roll(k_all, HEAD_DIM, axis=1)
    v_rol = pltpu.roll(v_all, HEAD_DIM, axis=1)
    mask4 = jnp.concatenate([mask] * 4, axis=0)
    o_nat = _softmax_pv(_dot_nt(_stack_heads(q, _NAT_HEADS), k_all), mask4,
                        _sink_column(sinks_ref, _NAT_HEADS, rows), v_all)
    o_rol = _softmax_pv(_dot_nt(_stack_heads(q, _ROL_HEADS), k_rol), mask4,
                        _sink_column(sinks_ref, _ROL_HEADS, rows), v_rol)
    lo = _lane_lo((rows, LANES))
    blk = lambda o, n: o[n * rows:(n + 1) * rows]
    return jnp.concatenate([
        jnp.where(lo, blk(o_nat, 0), blk(o_rol, 0)),
        jnp.where(lo, blk(o_nat, 1), blk(o_rol, 1)),
        jnp.where(lo, blk(o_rol, 2), blk(o_nat, 2)),
        jnp.where(lo, blk(o_rol, 3), blk(o_nat, 3)),
    ], axis=1)


def _retention(rq, rk, rv, rg, state, decay_ref, qdec, kdec, gpow):
    rows = rq.shape[0]
    lo = _lane_lo((rows, LANES))
    rkd = rk * kdec
    outs, new_state = [], []
    for m in range(RET_HEADS // 2):
        sl = slice(m * LANES, (m + 1) * LANES)
        upd = state[m] * gpow[:, m:m + 1]
        for p in range(2):
            h = 2 * m + p
            half = lo if p == 0 else ~lo
            rqm = jnp.where(half, rq[:, sl], 0.0)
            rv_h = rv[:, h * RET_DV:(h + 1) * RET_DV]
            s = _dot_nt(rqm, rk[:, sl]) * decay_ref[h]
            ret = _dot(s, rv_h) + _dot(rqm, state[m]) * qdec[:, h * RET_DV:(h + 1) * RET_DV]
            upd = upd + _dot_tn(jnp.where(half, rkd[:, sl], 0.0), rv_h)
            rg_h = rg[:, h * RET_DV:(h + 1) * RET_DV]
            outs.append(rg_h * jax.nn.sigmoid(rg_h) * _rms_rows(ret))
        new_state.append(upd)
    return jnp.concatenate(outs, axis=1), new_state


def _route(h2, wrt_hi_ref, wrt_cat_ref, br_col):
    rows = h2.shape[0]
    h_hi = _bf(h2)
    h_lo = _bf(h2 - h_hi.astype(jnp.float32))
    both = lax.dot_general(wrt_cat_ref[...], h_hi, _NT, preferred_element_type=jnp.float32)
    low = lax.dot_general(wrt_hi_ref[...], h_lo, _NT, preferred_element_type=jnp.float32)
    logits = (both[0:N_EXPERTS] + both[LANES:LANES + N_EXPERTS] + low[0:N_EXPERTS]) + br_col
    sub = lax.broadcasted_iota(jnp.int32, logits.shape, 0)
    row8 = lax.broadcasted_iota(jnp.int32, (SUBLANES, rows), 0)
    idx8 = jnp.zeros((SUBLANES, rows), jnp.int32)
    val8 = jnp.zeros((SUBLANES, rows), jnp.float32)
    onehot = jnp.zeros(logits.shape, jnp.float32)
    vals = []
    work = logits
    for k in range(TOP_K):
        mx = jnp.max(work, axis=0, keepdims=True)
        ix = jnp.min(jnp.where(work == mx, sub, N_EXPERTS), axis=0, keepdims=True)
        pick = sub == ix
        work = jnp.where(pick, NEG_BIG, work)
        onehot = jnp.where(pick, 1.0, onehot)
        idx8 = jnp.where(row8 == k, ix, idx8)
        vals.append(mx)
    exps = [jnp.exp(v - vals[0]) for v in vals]
    total = exps[0] + exps[1] + exps[2] + exps[3]
    for k in range(TOP_K):
        val8 = jnp.where(row8 == k, exps[k] / total, val8)
    fill = jnp.zeros((LANES - SUBLANES, rows), jnp.float32)
    idx_out = jnp.concatenate([idx8.astype(jnp.float32), fill], axis=0).T.astype(jnp.int32)
    val_out = jnp.concatenate([val8, fill], axis=0).T
    tok = lax.broadcasted_iota(jnp.int32, (rows, LANES), 0)
    grp = lax.broadcasted_iota(jnp.int32, (rows, LANES), 1)
    counts = _dot(onehot, jnp.where(tok // COUNT_BLOCK == grp, 1.0, 0.0)).astype(jnp.int32)
    return idx_out, val_out, idx8, counts


def _adaln_kernel(c_ref, w_ref, b_ref, o_ref):
    c = c_ref[...]
    o_ref[...] = _dot(c * jax.nn.sigmoid(c), w_ref[...]) + b_ref[...]


def _adaln(c_all, w_ada, b_ada):
    rows, d = c_all.shape
    n = w_ada.shape[1]
    bn = 1536
    return pl.pallas_call(
        _adaln_kernel,
        grid=(n // bn,),
        in_specs=[pl.BlockSpec((rows, d), lambda j: (0, 0)),
                  pl.BlockSpec((d, bn), lambda j: (0, j)),
                  pl.BlockSpec((1, bn), lambda j: (0, j))],
        out_specs=pl.BlockSpec((rows, bn), lambda j: (0, j)),
        out_shape=jax.ShapeDtypeStruct((rows, n), jnp.float32),
        compiler_params=pltpu.CompilerParams(vmem_limit_bytes=VMEM_LIMIT_BYTES),
    )(c_all, w_ada, b_ada.reshape(1, n))


def _prompt_mix_kernel(sinks_ref, x_ref, mod_ref, gmix_ref, gffn_ref, w_in_ref, qg_ref, kg_ref,
                       bdq_ref, bdk_ref, rotblk_ref, rottab_ref,
                       decay_ref, qdec_ref, kdec_ref, gpow_ref, w_out_ref,
                       wrh_ref, wrl_ref, br_ref,
                       x1_ref, h2_ref, topi_ref, topw_ref, topit_ref, cnt_ref, kwin_ref, vwin_ref, rend_ref,
                       kprev, vprev, state, mix):
    step = pl.program_id(0)
    tb = x_ref.shape[0]

    @pl.when(step == 0)
    def _():
        kprev[...] = jnp.zeros_like(kprev)
        vprev[...] = jnp.zeros_like(vprev)
        state[...] = jnp.zeros_like(state)

    x = x_ref[...]
    h = _pre_norm(x, gmix_ref[...], mod_ref[0:1, :], mod_ref[1:2, :])
    q, k, v, rq, rk, rv, rg = _project(
        h, w_in_ref, qg_ref[...], kg_ref[...], bdq_ref[...], bdk_ref[...],
        *_block_rotary(rotblk_ref, rottab_ref, 0), *_block_rotary(rotblk_ref, rottab_ref, 1))

    qi = lax.broadcasted_iota(jnp.int32, (WINDOW, 2 * WINDOW), 0)
    ci = lax.broadcasted_iota(jnp.int32, (WINDOW, 2 * WINDOW), 1)
    band = (ci > qi) & (ci <= qi + WINDOW)
    qdec, kdec, gpow = qdec_ref[...], kdec_ref[...], gpow_ref[...]

    n_sb = tb // WINDOW
    rows = [slice(sb * WINDOW, (sb + 1) * WINDOW) for sb in range(n_sb)]
    k_prev, v_prev = kprev[...], vprev[...]
    for sb, rs in enumerate(rows):
        k_sb, v_sb = k[rs], v[rs]
        mask = band & ((ci >= WINDOW) | (step > 0)) if sb == 0 else band
        att = _attention(q[rs], jnp.concatenate([k_prev, k_sb], axis=0),
                         jnp.concatenate([v_prev, v_sb], axis=0), mask, sinks_ref)
        mix[rs, 0:ATT_W] = _bf(att)
        k_prev, v_prev = k_sb, v_sb
    kprev[...] = k_prev
    vprev[...] = v_prev

    cur_state = [state[0], state[1]]
    for rs in rows:
        yret, cur_state = _retention(rq[rs], rk[rs], rv[rs], rg[rs], cur_state,
                                     decay_ref, qdec, kdec, gpow)
        mix[rs, ATT_W:ATT_W + RET_W] = _bf(yret)
    state[0] = cur_state[0]
    state[1] = cur_state[1]

    x1 = x + mod_ref[2:3, :] * jnp.dot(mix[...], w_out_ref[...],
                                       preferred_element_type=jnp.float32)
    x1_ref[...] = x1
    h2 = _pre_norm(x1, gffn_ref[...], mod_ref[3:4, :], mod_ref[4:5, :])
    h2_ref[...] = h2
    topi, topw, topi_t, counts = _route(h2, wrh_ref, wrl_ref, br_ref[...])
    topi_ref[...] = topi
    topw_ref[...] = topw
    topit_ref[...] = topi_t
    cnt_ref[0] = counts

    @pl.when(step == pl.num_programs(0) - 1)
    def _():
        kwin_ref[...] = kprev[...]
        vwin_ref[...] = vprev[...]
        rend_ref[...] = state[...]


def _const_spec(shape):
    nd = len(shape)
    return pl.BlockSpec(shape, lambda *_: (0,) * nd)


def _prompt_mix(x, mod, sinks, gmix, gffn, w_in, qg, kg, bdq, bdk, tabs, ret_tabs, w_out,
                wrh, wrl, br):
    t, d = x.shape
    tb = PROMPT_BLOCK
    rot_blk, rot_tab = tabs
    decay, qdec, kdec, gpow = ret_tabs
    row_spec = lambda w: pl.BlockSpec((tb, w), lambda i, *_: (i, 0))
    consts = [mod, gmix, gffn, w_in, qg, kg, bdq, bdk]
    consts2 = [rot_tab, decay, qdec, kdec, gpow, w_out, wrh, wrl, br]
    grid_spec = pltpu.PrefetchScalarGridSpec(
        num_scalar_prefetch=1,
        grid=(t // tb,),
        in_specs=([row_spec(d)] + [_const_spec(a.shape) for a in consts]
                  + [pl.BlockSpec((1,) + rot_blk.shape[1:], lambda i, *_: (i, 0, 0))]
                  + [_const_spec(a.shape) for a in consts2]),
        out_specs=[row_spec(d), row_spec(d), row_spec(LANES), row_spec(LANES),
                   pl.BlockSpec((SUBLANES, tb), lambda i, *_: (0, i)),
                   pl.BlockSpec((1, N_EXPERTS, LANES), lambda i, *_: (i, 0, 0)),
                   _const_spec((WINDOW, KV_W)), _const_spec((WINDOW, KV_W)),
                   _const_spec((2, LANES, RET_DV))],
        scratch_shapes=[pltpu.VMEM((WINDOW, KV_W), jnp.float32),
                        pltpu.VMEM((WINDOW, KV_W), jnp.float32),
                        pltpu.VMEM((2, LANES, RET_DV), jnp.float32),
                        pltpu.VMEM((tb, ATT_W + RET_W), jnp.bfloat16)],
    )
    return pl.pallas_call(
        _prompt_mix_kernel,
        grid_spec=grid_spec,
        out_shape=[jax.ShapeDtypeStruct((t, d), jnp.float32),
                   jax.ShapeDtypeStruct((t, d), jnp.float32),
                   jax.ShapeDtypeStruct((t, LANES), jnp.int32),
                   jax.ShapeDtypeStruct((t, LANES), jnp.float32),
                   jax.ShapeDtypeStruct((SUBLANES, t), jnp.int32),
                   jax.ShapeDtypeStruct((t // tb, N_EXPERTS, LANES), jnp.int32),
                   jax.ShapeDtypeStruct((WINDOW, KV_W), jnp.float32),
                   jax.ShapeDtypeStruct((WINDOW, KV_W), jnp.float32),
                   jax.ShapeDtypeStruct((2, LANES, RET_DV), jnp.float32)],
        compiler_params=pltpu.CompilerParams(
            dimension_semantics=("arbitrary",), vmem_limit_bytes=VMEM_LIMIT_BYTES),
    )(sinks, x, *consts, rot_blk, *consts2)


def _per_seq(stacked, rows, ds, s):
    return jnp.concatenate([stacked[g * rows + s * ds:g * rows + (s + 1) * ds]
                            for g in range(stacked.shape[0] // rows)], axis=0)


def _from_per_seq(parts, ds):
    groups = parts[0].shape[0] // ds
    return jnp.concatenate([p[g * ds:(g + 1) * ds] for g in range(groups) for p in parts], axis=0)


def _sample_attention(q, k_new, v_new, ck_ref, cv_ref, sinks_ref, ds):
    rows = q.shape[0]
    nseq, cache_w = ck_ref.shape[0], ck_ref.shape[1]
    tok_c = lax.broadcasted_iota(jnp.int32, (4 * rows, cache_w), 0) % rows
    col_c = lax.broadcasted_iota(jnp.int32, (4 * rows, cache_w), 1)
    delta = tok_c % ds + cache_w - col_c
    mask_c = (delta >= 0) & (delta < WINDOW)
    tok_n = lax.broadcasted_iota(jnp.int32, (4 * rows, rows), 0) % rows
    col_n = lax.broadcasted_iota(jnp.int32, (4 * rows, rows), 1)
    mask_n = (tok_n // ds == col_n // ds) & (col_n % ds <= tok_n % ds)

    outs = []
    for heads, rolled in ((_NAT_HEADS, False), (_ROL_HEADS, True)):
        arrange = (lambda a: pltpu.roll(a, HEAD_DIM, axis=1)) if rolled else (lambda a: a)
        qst = _stack_heads(q, heads)
        s_new = jnp.where(mask_n, _dot_nt(qst, arrange(k_new)), NEG_BIG)
        s_cache = _from_per_seq(
            [_dot_nt(_per_seq(qst, rows, ds, s), arrange(ck_ref[s])) for s in range(nseq)], ds)
        s_cache = jnp.where(mask_c, s_cache, NEG_BIG)
        sink = _sink_column(sinks_ref, heads, rows)
        m = jnp.maximum(jnp.maximum(jnp.max(s_cache, axis=-1, keepdims=True),
                                    jnp.max(s_new, axis=-1, keepdims=True)), sink)
        p_cache = jnp.exp(s_cache - m)
        p_new = jnp.exp(s_new - m)
        denom = (jnp.sum(p_cache, axis=-1, keepdims=True) + jnp.sum(p_new, axis=-1, keepdims=True)
                 + jnp.exp(sink - m))
        o = _from_per_seq(
            [_dot(_per_seq(p_cache, rows, ds, s), arrange(cv_ref[s])) for s in range(nseq)], ds)
        outs.append((o + _dot(p_new, arrange(v_new))) / denom)
    o_nat, o_rol = outs
    lo = _lane_lo((rows, LANES))
    blk = lambda o, n: o[n * rows:(n + 1) * rows]
    return jnp.concatenate([
        jnp.where(lo, blk(o_nat, 0), blk(o_rol, 0)),
        jnp.where(lo, blk(o_nat, 1), blk(o_rol, 1)),
        jnp.where(lo, blk(o_rol, 2), blk(o_nat, 2)),
        jnp.where(lo, blk(o_rol, 3), blk(o_nat, 3)),
    ], axis=1)


def _sample_retention(rq, rk, rv, rg, st_ref, rend_ref, decay_ref, qdec, kdec, gpow, ds):
    rows = rq.shape[0]
    nseq = st_ref.shape[0]
    lo = _lane_lo((rows, LANES))
    rkd = rk * kdec
    outs = []
    for m in range(RET_HEADS // 2):
        sl = slice(m * LANES, (m + 1) * LANES)
        heads = (2 * m, 2 * m + 1)
        rqm = [jnp.where(lo if p == 0 else ~lo, rq[:, sl], 0.0) for p in range(2)]
        rkdm = [jnp.where(lo if p == 0 else ~lo, rkd[:, sl], 0.0) for p in range(2)]
        rv_h = [rv[:, h * RET_DV:(h + 1) * RET_DV] for h in heads]
        q2 = jnp.concatenate(rqm, axis=0)
        k2 = jnp.concatenate(rkdm, axis=0)
        v2 = jnp.concatenate(rv_h, axis=0)
        inter = _from_per_seq(
            [_dot(_per_seq(q2, rows, ds, s), st_ref[s, m]) for s in range(nseq)], ds)
        for s in range(nseq):
            rend_ref[s, m] = (st_ref[s, m] * gpow[:, m:m + 1]
                              + _dot_tn(_per_seq(k2, rows, ds, s), _per_seq(v2, rows, ds, s)))
        for p, h in enumerate(heads):
            intra = _dot(_dot_nt(rqm[p], rk[:, sl]) * decay_ref[h], rv_h[p])
            ret = intra + inter[p * rows:(p + 1) * rows] * qdec[:, h * RET_DV:(h + 1) * RET_DV]
            rg_h = rg[:, h * RET_DV:(h + 1) * RET_DV]
            outs.append(rg_h * jax.nn.sigmoid(rg_h) * _rms_rows(ret))
    return jnp.concatenate(outs, axis=1)


def _sample_mix_kernel(sinks_ref, x_ref, mod_ref, gmix_ref, gffn_ref, w_in_ref, qg_ref, kg_ref,
                       bdq_ref, bdk_ref, cosa_ref, sina_ref, cosr_ref, sinr_ref,
                       decay_ref, qdec_ref, kdec_ref, gpow_ref, w_out_ref,
                       wrh_ref, wrl_ref, br_ref, ck_ref, cv_ref, st_ref,
                       x1_ref, h2_ref, topi_ref, topw_ref, topit_ref, cnt_ref, kwin_ref, vwin_ref, rend_ref,
                       mix):
    nseq, cache_w = ck_ref.shape[0], ck_ref.shape[1]
    ds = x_ref.shape[0] // nseq
    d = x_ref.shape[1]
    x = x_ref[...]
    mod = lambda i: _repeat_rows(mod_ref[:, i * d:(i + 1) * d], ds)
    h = _pre_norm(x, gmix_ref[...], mod(0), mod(1))
    q, k, v, rq, rk, rv, rg = _project(
        h, w_in_ref, qg_ref[...], kg_ref[...], bdq_ref[...], bdk_ref[...],
        cosa_ref[...], sina_ref[...], cosr_ref[...], sinr_ref[...])

    for s in range(nseq):
        rs = slice(s * ds, (s + 1) * ds)
        kwin_ref[s] = jnp.concatenate([ck_ref[s, ds:, :], k[rs]], axis=0)
        vwin_ref[s] = jnp.concatenate([cv_ref[s, ds:, :], v[rs]], axis=0)
    mix[:, 0:ATT_W] = _bf(_sample_attention(q, k, v, ck_ref, cv_ref, sinks_ref, ds))
    mix[:, ATT_W:ATT_W + RET_W] = _bf(_sample_retention(
        rq, rk, rv, rg, st_ref, rend_ref, decay_ref, qdec_ref[...], kdec_ref[...], gpow_ref[...], ds))

    x1 = x + mod(2) * jnp.dot(mix[...], w_out_ref[...], preferred_element_type=jnp.float32)
    x1_ref[...] = x1
    h2 = _pre_norm(x1, gffn_ref[...], mod(3), mod(4))
    h2_ref[...] = h2
    topi, topw, topi_t, counts = _route(h2, wrh_ref, wrl_ref, br_ref[...])
    topi_ref[...] = topi
    topw_ref[...] = topw
    topit_ref[...] = topi_t
    cnt_ref[0] = counts


def _sample_mix(x, mod_seq, sinks, gmix, gffn, w_in, qg, kg, bdq, bdk, tabs, ret_tabs, w_out,
                wrh, wrl, br, cache_k, cache_v, state):
    t, d = x.shape
    nb, cache_w = cache_k.shape[0], cache_k.shape[1]
    ds = t // nb
    g = SAMPLE_SEQS
    tb = g * ds
    cosa, sina, cosr, sinr = tabs
    decay, qdec, kdec, gpow = ret_tabs
    row_spec = lambda w: pl.BlockSpec((tb, w), lambda i, *_: (i, 0))
    seq3 = lambda a: pl.BlockSpec((g,) + a.shape[1:], lambda i, *_: (i,) + (0,) * (a.ndim - 1))
    consts = [gmix, gffn, w_in, qg, kg, bdq, bdk, cosa, sina, cosr, sinr,
              decay, qdec, kdec, gpow, w_out, wrh, wrl, br]
    grid_spec = pltpu.PrefetchScalarGridSpec(
        num_scalar_prefetch=1,
        grid=(nb // g,),
        in_specs=([row_spec(d), pl.BlockSpec((g, mod_seq.shape[1]), lambda i, *_: (i, 0))]
                  + [_const_spec(a.shape) for a in consts]
                  + [seq3(cache_k), seq3(cache_v), seq3(state)]),
        out_specs=[row_spec(d), row_spec(d), row_spec(LANES), row_spec(LANES),
                   pl.BlockSpec((SUBLANES, tb), lambda i, *_: (0, i)),
                   pl.BlockSpec((1, N_EXPERTS, LANES), lambda i, *_: (i, 0, 0)),
                   seq3(cache_k), seq3(cache_v), seq3(state)],
        scratch_shapes=[pltpu.VMEM((tb, ATT_W + RET_W), jnp.bfloat16)],
    )
    return pl.pallas_call(
        _sample_mix_kernel,
        grid_spec=grid_spec,
        out_shape=[jax.ShapeDtypeStruct((t, d), jnp.float32),
                   jax.ShapeDtypeStruct((t, d), jnp.float32),
                   jax.ShapeDtypeStruct((t, LANES), jnp.int32),
                   jax.ShapeDtypeStruct((t, LANES), jnp.float32),
                   jax.ShapeDtypeStruct((SUBLANES, t), jnp.int32),
                   jax.ShapeDtypeStruct((t // tb, N_EXPERTS, LANES), jnp.int32),
                   jax.ShapeDtypeStruct(cache_k.shape, jnp.float32),
                   jax.ShapeDtypeStruct(cache_v.shape, jnp.float32),
                   jax.ShapeDtypeStruct(state.shape, jnp.float32)],
        compiler_params=pltpu.CompilerParams(
            dimension_semantics=("arbitrary",), vmem_limit_bytes=VMEM_LIMIT_BYTES),
    )(sinks, x, mod_seq, *consts, cache_k, cache_v, state)


def _chunk_copies(cnt8, src_ref, src_start, dst_ref, dst_start, sem):
    for bit in range(CHUNK_BITS - 1, SUBLANES.bit_length() - 2, -1):
        size = 1 << bit
        done = (cnt8 >> (bit + 1)) << (bit + 1)

        @pl.when(((cnt8 >> bit) & 1) == 1)
        def _():
            src = pl.multiple_of(src_start + done, SUBLANES)
            dst = pl.multiple_of(dst_start + done, SUBLANES)
            pltpu.make_async_copy(src_ref.at[pl.ds(src, size)],
                                  dst_ref.at[pl.ds(dst, size)], sem).start()


def _wait_rows(ref, rows8, sem, bits=STAGE_BITS):
    for bit in range(bits - 1, SUBLANES.bit_length() - 2, -1):
        size = 1 << bit

        @pl.when(((rows8 >> bit) & 1) == 1)
        def _():
            pltpu.make_async_copy(ref.at[pl.ds(0, size)], ref.at[pl.ds(0, size)], sem).wait()


def _dispatch_kernel(cnt_ref, gstart_ref, cb_ref, blkrows_ref, ends_ref, padcnt_ref,
                     topit_p_ref, topit_s_ref, h2_p_ref, h2_s_ref, cbcol_ref, xs_ref,
                     stage, zeros, sems, zsem, *, nblk_p):
    step = pl.program_id(0)
    nblk = pl.num_programs(0)
    tb = h2_p_ref.shape[0]
    slot = step % STAGE_SLOTS

    tile = zeros.shape[0]
    first_unused = ends_ref[N_EXPERTS - 1] // tile
    n_tiles = xs_ref.shape[0] // tile

    def tail_fill(t):
        return pltpu.make_async_copy(zeros, xs_ref.at[pl.ds(pl.multiple_of(t * tile, tile), tile)],
                                     zsem)

    @pl.when(step == 0)
    def _():
        zeros[...] = jnp.zeros_like(zeros)

        def pad_fill(e, carry):
            _chunk_copies(padcnt_ref[e], zeros, 0, xs_ref, ends_ref[e] - padcnt_ref[e], zsem)
            return carry

        lax.fori_loop(0, N_EXPERTS, pad_fill, 0)
        lax.fori_loop(first_unused, n_tiles, lambda t, c: (tail_fill(t).start(), c)[1], 0)

    @pl.when(step == nblk - 1)
    def _():
        _wait_rows(xs_ref, padcnt_ref[N_EXPERTS], zsem, bits=(N_EXPERTS * tile).bit_length())
        lax.fori_loop(first_unused, n_tiles, lambda t, c: (tail_fill(t).wait(), c)[1], 0)

    def issue_chunk(blk, e):
        i = (blk + 1) * N_EXPERTS + e
        _chunk_copies(cnt_ref[i], stage.at[blk % STAGE_SLOTS], cb_ref[i], xs_ref, gstart_ref[i],
                      sems.at[blk % STAGE_SLOTS])

    for e in range(N_EXPERTS):
        issue_chunk(step - 1, e)

    is_p = step < nblk_p
    topit = jnp.where(is_p, topit_p_ref[...], topit_s_ref[...])
    h2 = jnp.where(is_p, h2_p_ref[...], h2_s_ref[...])
    cbcol = cbcol_ref[0]

    sub = lax.broadcasted_iota(jnp.int32, (LANES, tb), 0)
    picks = [sub == topit[k:k + 1, :] for k in range(TOP_K)]
    onehot = jnp.zeros((LANES, tb), jnp.float32)
    for pk in picks:
        onehot = jnp.where(pk, 1.0, onehot)
    r = lax.broadcasted_iota(jnp.int32, (tb, tb), 0)
    c = lax.broadcasted_iota(jnp.int32, (tb, tb), 1)
    earlier = jnp.where(r < c, 1.0, 0.0)
    before = _dot(onehot, earlier)
    base = before + cbcol.astype(jnp.float32)
    row = lax.broadcasted_iota(jnp.int32, (stage.shape[1], tb), 0)
    sel = jnp.zeros((stage.shape[1], tb), jnp.float32)
    for pk in picks:
        col = jnp.sum(jnp.where(pk, base, 0.0), axis=0, keepdims=True).astype(jnp.int32)
        sel = jnp.where(row == col, 1.0, sel)

    stage[slot] = _dot(sel, h2)

    @pl.when(step == nblk - 1)
    def _():
        lax.fori_loop(0, N_EXPERTS, lambda e, c: (issue_chunk(step, e), c)[1], 0)

    oldest = step - (STAGE_SLOTS - 1)

    @pl.when(oldest >= 0)
    def _():
        _wait_rows(xs_ref, blkrows_ref[jnp.maximum(oldest, 0)], sems.at[(step + 1) % STAGE_SLOTS])

    @pl.when(step == nblk - 1)
    def _():
        for back in range(STAGE_SLOTS - 2, -1, -1):
            @pl.when(step - back >= 0)
            def _():
                blk = jnp.maximum(step - back, 0)
                _wait_rows(xs_ref, blkrows_ref[blk], sems.at[blk % STAGE_SLOTS])


def _dispatch(tables, topit_p, topit_s, h2_p, h2_s, cbcol, n_rows):
    n_p, d = h2_p.shape
    n_s = h2_s.shape[0]
    tb = ROUTE_BLOCK
    nblk_p, nblk_s = n_p // tb, n_s // tb
    pidx = lambda i, *_: jnp.minimum(i, nblk_p - 1)
    sidx = lambda i, *_: jnp.maximum(i - nblk_p, 0)
    grid_spec = pltpu.PrefetchScalarGridSpec(
        num_scalar_prefetch=6,
        grid=(nblk_p + nblk_s,),
        in_specs=[pl.BlockSpec((SUBLANES, tb), lambda i, *_: (0, pidx(i))),
                  pl.BlockSpec((SUBLANES, tb), lambda i, *_: (0, sidx(i))),
                  pl.BlockSpec((tb, d), lambda i, *_: (pidx(i), 0)),
                  pl.BlockSpec((tb, d), lambda i, *_: (sidx(i), 0)),
                  pl.BlockSpec((1, LANES, 1), lambda i, *_: (i, 0, 0))],
        out_specs=pl.BlockSpec(memory_space=pl.ANY),
        scratch_shapes=[pltpu.VMEM((STAGE_SLOTS, STAGE_ROWS, d), jnp.float32),
                        pltpu.VMEM((EXPERT_TILE, d), jnp.float32),
                        pltpu.SemaphoreType.DMA((STAGE_SLOTS,)),
                        pltpu.SemaphoreType.DMA],
    )
    return pl.pallas_call(
        functools.partial(_dispatch_kernel, nblk_p=nblk_p),
        grid_spec=grid_spec,
        out_shape=jax.ShapeDtypeStruct((n_rows, d), jnp.float32),
        compiler_params=pltpu.CompilerParams(
            dimension_semantics=("arbitrary",), vmem_limit_bytes=VMEM_LIMIT_BYTES),
    )(*tables, topit_p, topit_s, h2_p, h2_s, cbcol)


def _expert_mlp(x, wgu_bf, bgu, wd_bf, bd):
    d_ff = wd_bf.shape[0]
    xb = _bf(x)
    w = d_ff // EXPERT_FF_SLICES
    y = None
    for j in range(EXPERT_FF_SLICES):
        gate = jnp.dot(xb, wgu_bf[:, j * w:(j + 1) * w],
                       preferred_element_type=jnp.float32) + bgu[:, j * w:(j + 1) * w]
        up = jnp.dot(xb, wgu_bf[:, d_ff + j * w:d_ff + (j + 1) * w],
                     preferred_element_type=jnp.float32) + bgu[:, d_ff + j * w:d_ff + (j + 1) * w]
        glu = jnp.minimum(gate, SWIGLU_LIMIT)
        lin = jnp.clip(up, -SWIGLU_LIMIT, SWIGLU_LIMIT)
        act = glu * jax.nn.sigmoid(SWIGLU_ALPHA * glu) * (lin + 1.0)
        part = jnp.dot(_bf(act), wd_bf[j * w:(j + 1) * w, :], preferred_element_type=jnp.float32)
        y = part if y is None else y + part
    return y + bd


def _experts_kernel(tile_expert_ref, n_used_ref, groups_ref, next_ref, slot_ref,
                    x_ref, wgu_hbm, bgu_ref, wd_hbm, bd_ref, y_ref,
                    wgu_f32, wd_f32, wgu_bf, wd_bf, sems):
    step = pl.program_id(0)
    groups = groups_ref[step]

    def weight_copies(expert, slot):
        return (pltpu.make_async_copy(wgu_hbm.at[expert], wgu_f32.at[slot], sems.at[slot, 0]),
                pltpu.make_async_copy(wd_hbm.at[expert], wd_f32.at[slot], sems.at[slot, 1]))

    @pl.when(groups > 0)
    def _():
        e = tile_expert_ref[step]
        prev = tile_expert_ref[jnp.maximum(step - 1, 0)]
        slot = slot_ref[e]

        @pl.when(step == 0)
        def _():
            for cp in weight_copies(e, slot):
                cp.start()

        @pl.when(jnp.logical_or(step == 0, e != prev))
        def _():
            for cp in weight_copies(e, slot):
                cp.wait()
            nxt = next_ref[e]

            @pl.when(nxt >= 0)
            def _():
                for cp in weight_copies(nxt, 1 - slot):
                    cp.start()

            wgu_bf[...] = _bf(wgu_f32[slot])
            wd_bf[...] = _bf(wd_f32[slot])

    tm = x_ref.shape[0]
    for n in range(1, EXPERT_GROUPS + 1):
        @pl.when(groups == n)
        def _():
            rows = n * (tm // EXPERT_GROUPS)
            y_ref[0:rows, :] = _expert_mlp(x_ref[0:rows, :], wgu_bf, bgu_ref[0], wd_bf, bd_ref[0])
            if rows < tm:
                y_ref[rows:, :] = jnp.zeros((tm - rows, y_ref.shape[1]), y_ref.dtype)

    @pl.when(groups == 0)
    def _():
        y_ref[...] = jnp.zeros_like(y_ref)


def _experts(tile_expert, n_used, tile_groups, next_used, slot_of, x_sorted, w_gate_up, b_gate_up,
             w_down, b_down):
    p = x_sorted.shape[0]
    n_e, d, two_ff = w_gate_up.shape
    d_ff = two_ff // 2
    tm = EXPERT_TILE
    tile = lambda i, te, nu: jnp.minimum(i, nu[0] - 1)
    grid_spec = pltpu.PrefetchScalarGridSpec(
        num_scalar_prefetch=5,
        grid=(p // tm,),
        in_specs=[pl.BlockSpec((tm, d), lambda i, te, nu, *_: (tile(i, te, nu), 0)),
                  pl.BlockSpec(memory_space=pl.ANY),
                  pl.BlockSpec((1, 1, two_ff), lambda i, te, *_: (te[i], 0, 0)),
                  pl.BlockSpec(memory_space=pl.ANY),
                  pl.BlockSpec((1, 1, d), lambda i, te, *_: (te[i], 0, 0))],
        out_specs=pl.BlockSpec((tm, d), lambda i, *_: (i, 0)),
        scratch_shapes=[pltpu.VMEM((2, d, two_ff), jnp.float32),
                        pltpu.VMEM((2, d_ff, d), jnp.float32),
                        pltpu.VMEM((d, two_ff), jnp.bfloat16),
                        pltpu.VMEM((d_ff, d), jnp.bfloat16),
                        pltpu.SemaphoreType.DMA((2, 2))],
    )
    return pl.pallas_call(
        _experts_kernel,
        grid_spec=grid_spec,
        out_shape=jax.ShapeDtypeStruct(x_sorted.shape, jnp.float32),
        compiler_params=pltpu.CompilerParams(
            dimension_semantics=("arbitrary",), vmem_limit_bytes=VMEM_LIMIT_BYTES),
    )(tile_expert, n_used, tile_groups, next_used, slot_of, x_sorted, w_gate_up,
      b_gate_up.reshape(n_e, 1, two_ff), w_down, b_down.reshape(n_e, 1, d))


def _combine_kernel(cnt_ref, gstart_ref, cb_ref, blkrows_ref,
                    topi_p_ref, topi_s_ref, topw_p_ref, topw_s_ref, x1_p_ref, x1_s_ref,
                    gate_p_ref, gate_s_ref, cbrow_ref, y_ref, out_p_ref, out_s_ref,
                    ybuf, sems, *, nblk_p):
    step = pl.program_id(0)
    nblk = pl.num_programs(0)
    tb = x1_p_ref.shape[0]
    slot = step % STAGE_SLOTS

    def fetch_chunk(blk, into, e):
        i = blk * N_EXPERTS + e
        _chunk_copies(cnt_ref[i], y_ref, gstart_ref[i], ybuf.at[into], cb_ref[i], sems.at[into])

    def fetch(blk, into):
        lax.fori_loop(0, N_EXPERTS, lambda e, c: (fetch_chunk(blk, into, e), c)[1], 0)

    @pl.when(step == 0)
    def _():
        ybuf[...] = jnp.zeros_like(ybuf)
        for ahead in range(STAGE_SLOTS - 1):
            @pl.when(ahead < nblk)
            def _():
                fetch(ahead, ahead)

    _wait_rows(y_ref, blkrows_ref[step], sems.at[slot])

    for e in range(N_EXPERTS):
        fetch_chunk(step + STAGE_SLOTS - 1, (step + STAGE_SLOTS - 1) % STAGE_SLOTS, e)

    is_p = step < nblk_p
    topi = jnp.where(is_p, topi_p_ref[...], topi_s_ref[...])
    topw = jnp.where(is_p, topw_p_ref[...], topw_s_ref[...])
    cbrow = cbrow_ref[0]

    lane = lax.broadcasted_iota(jnp.int32, (tb, LANES), 1)
    picks = []
    onehot = jnp.zeros((tb, LANES), jnp.float32)
    for k in range(TOP_K):
        ix = jnp.sum(jnp.where(lane == k, topi, 0), axis=-1, keepdims=True)
        picks.append(lane == ix)
        onehot = jnp.where(picks[k], 1.0, onehot)
    r = lax.broadcasted_iota(jnp.int32, (tb, tb), 0)
    c = lax.broadcasted_iota(jnp.int32, (tb, tb), 1)
    earlier = jnp.where(c < r, 1.0, 0.0)
    base = _dot(earlier, onehot) + cbrow.astype(jnp.float32)
    cols = [jnp.sum(jnp.where(picks[k], base, 0.0), axis=-1, keepdims=True).astype(jnp.int32)
            for k in range(TOP_K)]
    wks = [jnp.sum(jnp.where(lane == k, topw, 0.0), axis=-1, keepdims=True) for k in range(TOP_K)]
    moe = None
    for c0 in range(0, ybuf.shape[1], COMBINE_SLAB):
        colid = lax.broadcasted_iota(jnp.int32, (tb, COMBINE_SLAB), 1) + c0
        weights = jnp.zeros((tb, COMBINE_SLAB), jnp.float32)
        for k in range(TOP_K):
            weights = jnp.where(colid == cols[k], wks[k], weights)
        part = _dot(weights, ybuf[slot, c0:c0 + COMBINE_SLAB, :])
        moe = part if moe is None else moe + part

    @pl.when(is_p)
    def _():
        out_p_ref[...] = x1_p_ref[...] + gate_p_ref[...] * moe

    @pl.when(jnp.logical_not(is_p))
    def _():
        gate_s = _repeat_rows(gate_s_ref[...], tb // gate_s_ref.shape[0])
        out_s_ref[...] = x1_s_ref[...] + gate_s * moe


def _combine(tables, topi_p, topi_s, topw_p, topw_s, x1_p, x1_s, gate_p, mod_seq, dec_seq, cbrow,
             y_sorted):
    n_p, d = x1_p.shape
    n_s = x1_s.shape[0]
    tb = ROUTE_BLOCK
    nblk_p, nblk_s = n_p // tb, n_s // tb
    pidx = lambda i, *_: jnp.minimum(i, nblk_p - 1)
    sidx = lambda i, *_: jnp.maximum(i - nblk_p, 0)
    prow = lambda w: pl.BlockSpec((tb, w), lambda i, *_: (pidx(i), 0))
    srow = lambda w: pl.BlockSpec((tb, w), lambda i, *_: (sidx(i), 0))
    grid_spec = pltpu.PrefetchScalarGridSpec(
        num_scalar_prefetch=4,
        grid=(nblk_p + nblk_s,),
        in_specs=[prow(LANES), srow(LANES), prow(LANES), srow(LANES), prow(d), srow(d),
                  pl.BlockSpec((1, d), lambda i, *_: (0, 0)),
                  pl.BlockSpec((tb // dec_seq, d), lambda i, *_: (sidx(i), 5)),
                  pl.BlockSpec((1, 1, LANES), lambda i, *_: (i, 0, 0)),
                  pl.BlockSpec(memory_space=pl.ANY)],
        out_specs=[prow(d), srow(d)],
        scratch_shapes=[pltpu.VMEM((STAGE_SLOTS, STAGE_ROWS, d), jnp.float32),
                        pltpu.SemaphoreType.DMA((STAGE_SLOTS,))],
    )
    return pl.pallas_call(
        functools.partial(_combine_kernel, nblk_p=nblk_p),
        grid_spec=grid_spec,
        out_shape=[jax.ShapeDtypeStruct((n_p, d), jnp.float32),
                   jax.ShapeDtypeStruct((n_s, d), jnp.float32)],
        compiler_params=pltpu.CompilerParams(
            dimension_semantics=("arbitrary",), vmem_limit_bytes=VMEM_LIMIT_BYTES),
    )(*tables, topi_p, topi_s, topw_p, topw_s, x1_p, x1_s, gate_p, mod_seq, cbrow, y_sorted)


def _rotary_tables(pos, inv_freq):
    ang = np.asarray(pos, np.float64)[:, None] * inv_freq[None, :]
    cos, sin = np.cos(ang), np.sin(ang)
    cos_t = np.tile(cos, (1, LANES // cos.shape[1]))
    sin_t = np.tile(np.concatenate([-sin, sin], axis=1), (1, LANES // (2 * sin.shape[1])))
    return cos_t.astype(np.float32), sin_t.astype(np.float32)


def _rotary_split_tables(block, nblk, inv_freq):
    half = inv_freq.shape[0]
    freq = np.tile(inv_freq, LANES // half)
    sign = np.tile(np.concatenate([-np.ones(half), np.ones(half)]), LANES // (2 * half))
    base = (np.arange(nblk) * block).astype(np.float64)[:, None] * freq[None, :]
    off = np.arange(block).astype(np.float64)[:, None] * freq[None, :]
    blk = np.stack([np.cos(base), np.sin(base)], axis=1)
    tab = np.stack([np.cos(off), np.sin(off), np.cos(off) * sign, np.sin(off) * sign])
    return blk.astype(np.float32), tab.astype(np.float32)


def _retention_tables(chunk, nseq=1):
    log_gamma = np.log1p(-np.exp2(-5.0 - np.arange(RET_HEADS, dtype=np.float64)))
    idx = np.arange(chunk, dtype=np.float64)
    diff = idx[:, None] - idx[None, :]
    decay = np.where(diff[None] >= 0,
                     np.exp(np.maximum(diff, 0.0)[None] * log_gamma[:, None, None]), 0.0)
    decay = np.stack([np.kron(np.eye(nseq), decay[h]) for h in range(RET_HEADS)])
    q_decay = np.exp((idx + 1.0)[:, None] * log_gamma[None, :])
    k_decay = np.exp((chunk - 1.0 - idx)[:, None] * log_gamma[None, :])
    qdec = np.tile(np.repeat(q_decay, RET_DV, axis=1), (nseq, 1))
    kdec = np.tile(np.repeat(k_decay, RET_DK, axis=1), (nseq, 1))
    g_chunk = np.exp(chunk * log_gamma)
    gpow = np.repeat(g_chunk.reshape(RET_HEADS // 2, 2), RET_DK, axis=1).T
    return tuple(a.astype(np.float32) for a in (decay, qdec, kdec, gpow))


def _block_diag_mean(width):
    head = np.arange(width) // HEAD_DIM
    return jnp.asarray(np.where(head[:, None] == head[None, :], 1.0 / HEAD_DIM, 0.0), jnp.bfloat16)


def kernel(x_prompt, x_sample, cache_k, cache_v, state_ret, c_prompt, c_sample, w_ada, b_ada,
           g_mix, w_in, q_gain, k_gain, sinks, w_out, g_ffn, w_router, b_router, w_gate_up,
           b_gate_up, w_down, b_down):
    depth = w_ada.shape[0]
    batch, seq, d = x_prompt.shape
    dec_batch, dec_seq, _ = x_sample.shape
    cache_w = cache_k.shape[2]
    assert batch == 1 and depth == 1
    assert seq % PROMPT_BLOCK == 0 and PROMPT_BLOCK % WINDOW == 0 and WINDOW == RET_CHUNK
    assert dec_batch % SAMPLE_SEQS == 0 and cache_w == WINDOW
    n_p, n_s = batch * seq, dec_batch * dec_seq
    n_tok = n_p + n_s
    assert n_p % ROUTE_BLOCK == 0 and n_s % ROUTE_BLOCK == 0
    assert ROUTE_BLOCK % COUNT_BLOCK == 0 and (SAMPLE_SEQS * dec_seq) % COUNT_BLOCK == 0
    assert PROMPT_BLOCK % COUNT_BLOCK == 0
    assert STAGE_ROWS % COMBINE_SLAB == 0 and EXPERT_TILE % (EXPERT_GROUPS * SUBLANES) == 0

    l = 0
    f32 = jnp.float32
    rope_freq = 1.0 / (ROPE_THETA ** (np.arange(0, HEAD_DIM, 2, dtype=np.float64) / HEAD_DIM))
    ret_freq = 1.0 / (ROPE_THETA ** np.linspace(0.0, 1.0, RET_DK // 2))
    pos_s = PAST_LEN + np.arange(dec_seq)
    blk_a, tab_a = _rotary_split_tables(PROMPT_BLOCK, seq // PROMPT_BLOCK, rope_freq)
    blk_r, tab_r = _rotary_split_tables(PROMPT_BLOCK, seq // PROMPT_BLOCK, ret_freq)
    tabs_p = (np.concatenate([blk_a, blk_r], axis=1), np.concatenate([tab_a, tab_r], axis=0))
    tabs_s = tuple(np.tile(a, (SAMPLE_SEQS, 1))
                   for a in _rotary_tables(pos_s, rope_freq) + _rotary_tables(pos_s, ret_freq))
    ret_p = _retention_tables(RET_CHUNK)
    ret_s = _retention_tables(dec_seq, SAMPLE_SEQS)

    n_c = batch + dec_batch
    c_rows = -(-n_c // SUBLANES) * SUBLANES
    c_all = jnp.concatenate([c_sample, c_prompt, jnp.zeros((c_rows - n_c, d), f32)], axis=0)
    mod = _adaln(c_all, w_ada[l], b_ada[l])
    mod_p = mod[dec_batch].reshape(6, d)

    w_in_bf = _bf(w_in[l])
    w_out_bf = _bf(w_out[l])
    qg = jnp.tile(q_gain[l], ATT_HEADS).reshape(1, ATT_W)
    kg = jnp.tile(k_gain[l], ATT_KV_HEADS).reshape(1, KV_W)
    bdq, bdk = _block_diag_mean(ATT_W), _block_diag_mean(KV_W)
    gmix = g_mix[l].reshape(1, d)
    gffn = g_ffn[l].reshape(1, d)
    wr = jnp.pad(w_router[l].T, ((0, LANES - N_EXPERTS), (0, 0)))
    wr_hi = _bf(wr)
    wr_cat = jnp.concatenate([wr_hi, _bf(wr - wr_hi.astype(f32))], axis=0)
    br = b_router[l].reshape(N_EXPERTS, 1)
    shared = (sinks[l], gmix, gffn, w_in_bf, qg, kg, bdq, bdk)

    (x1_p, h2_p, topi_p, topw_p, topit_p, cnt_p, kwin_p, vwin_p, rend_p) = _prompt_mix(
        x_prompt.reshape(n_p, d), mod_p, *shared, tabs_p, ret_p, w_out_bf, wr_hi, wr_cat, br)

    (x1_s, h2_s, topi_s, topw_s, topit_s, cnt_s, kwin_s, vwin_s, rend_s) = _sample_mix(
        x_sample.reshape(n_s, d), mod, *shared, tabs_s, ret_s, w_out_bf, wr_hi, wr_cat, br,
        cache_k[l].reshape(dec_batch, cache_w, KV_W), cache_v[l].reshape(dec_batch, cache_w, KV_W),
        state_ret[l].reshape(dec_batch, 2, LANES, RET_DV))

    tb, tm = ROUTE_BLOCK, EXPERT_TILE
    i32 = jnp.int32
    per_group = lambda c, step_rows: jnp.swapaxes(c[:, :, :step_rows // COUNT_BLOCK], 1, 2).reshape(
        -1, N_EXPERTS)
    cnt = jnp.concatenate([per_group(cnt_p, PROMPT_BLOCK), per_group(cnt_s, SAMPLE_SEQS * dec_seq)])
    cnt = cnt.reshape(n_tok // tb, tb // COUNT_BLOCK, N_EXPERTS).sum(axis=1)
    cnt = jnp.pad(cnt, ((0, 0), (0, LANES - N_EXPERTS)))
    cnt8 = -(-cnt // SUBLANES) * SUBLANES
    rs = jnp.cumsum(cnt8, axis=0) - cnt8
    tot = jnp.sum(cnt8, axis=0)[:N_EXPERTS]
    padded = -(-tot // tm) * tm
    ends = jnp.cumsum(padded)
    gstart = (ends - padded)[None, :] + rs[:, :N_EXPERTS]
    cb = jnp.cumsum(cnt8, axis=1) - cnt8
    blk_rows = jnp.sum(cnt8, axis=1).astype(i32)
    max_rows = n_tok * TOP_K + cnt.shape[0] * N_EXPERTS * (SUBLANES - 1)
    n_rows = -(-max_rows // tm) * tm + N_EXPERTS * tm
    tile_start = jnp.arange(n_rows // tm, dtype=i32) * tm
    tile_expert = jnp.minimum(jnp.sum(ends[None, :] <= tile_start[:, None], axis=1),
                              N_EXPERTS - 1).astype(i32)
    n_used = (ends[-1:] // tm).astype(i32)
    eid = jnp.arange(N_EXPERTS, dtype=i32)
    of_tile = lambda v: jnp.sum(jnp.where(tile_expert[:, None] == eid[None, :], v[None, :], 0), axis=1)
    tile_rows = jnp.clip(of_tile(tot) - (tile_start - of_tile(ends - padded)), 0, tm)
    tile_rows = jnp.where(tile_start < ends[-1], tile_rows, 0)
    tile_groups = (-(-tile_rows // (tm // EXPERT_GROUPS))).astype(i32)
    flat = lambda a: a[:, :N_EXPERTS].reshape(-1).astype(i32)
    tables = (flat(cnt8), gstart.reshape(-1).astype(i32), flat(cb), blk_rows)
    ahead = jnp.zeros(((STAGE_SLOTS - 1) * N_EXPERTS,), i32)
    tables_c = tuple(jnp.concatenate([t, ahead]) for t in tables[:3]) + tables[3:]

    pad_cnt = padded - tot
    pad_cnt = jnp.concatenate([pad_cnt, jnp.sum(pad_cnt, keepdims=True)]).astype(i32)
    behind = jnp.zeros((N_EXPERTS,), i32)
    tables_d = tuple(jnp.concatenate([behind, t]) for t in tables[:3]) + tables[3:]
    x_sorted = _dispatch(tables_d + (ends.astype(i32), pad_cnt), topit_p, topit_s, h2_p, h2_s,
                         cb.astype(i32)[:, :, None], n_rows)
    used = tot > 0
    later = jnp.where(used[None, :] & (eid[None, :] > eid[:, None]), eid[None, :], N_EXPERTS)
    next_used = jnp.min(later, axis=1)
    next_used = jnp.where(next_used == N_EXPERTS, -1, next_used).astype(i32)
    slot_of = ((jnp.cumsum(used.astype(i32)) - 1) % 2).astype(i32)
    y_sorted = _experts(tile_expert, n_used, tile_groups, next_used, slot_of, x_sorted, w_gate_up[l], b_gate_up[l], w_down[l],
                        b_down[l])
    y_p, y_s = _combine(tables_c, topi_p, topi_s, topw_p, topw_s, x1_p, x1_s, mod_p[5:6], mod,
                        dec_seq, cb.astype(i32)[:, None, :], y_sorted)

    kv5 = lambda a, n: a.reshape(1, n, cache_w, ATT_KV_HEADS, HEAD_DIM)
    st5 = lambda a, n: a.reshape(1, n, RET_HEADS, RET_DK, RET_DV)
    return (y_p.reshape(batch, seq, d), y_s.reshape(dec_batch, dec_seq, d),
            kv5(kwin_p, batch), kv5(vwin_p, batch), st5(rend_p, batch),
            kv5(kwin_s, dec_batch), kv5(vwin_s, dec_batch), st5(rend_s, dec_batch))
```
